```python
import jax, jax.numpy as jnp
from jax import lax
import numpy as np

D_MODEL = 1024
BATCH = 8
SEQ = 8192
DEPTH = 2

MIX_WIDTH = D_MODEL
HEAD_DIM = 64
ATTN_HEADS = 8
ATTN_KV_HEADS = 2
ATTN_REP = ATTN_HEADS // ATTN_KV_HEADS
ATTN_WIDTH = ATTN_HEADS * HEAD_DIM
KV_WIDTH = ATTN_KV_HEADS * HEAD_DIM
WINDOW = 128
BLOCK = 128
ATTN_SCALE = HEAD_DIM ** -0.5
GMLP_HEADS = 4
GMLP_WIDTH = GMLP_HEADS * HEAD_DIM
CHUNK = 128
POOL_GROUPS = 4
POOL_GROUP_DIM = 64
POOL_WIDTH = POOL_GROUPS * POOL_GROUP_DIM
POOL_WINDOWS = (2, 4, 8, 16)
IN_WIDTH = ATTN_WIDTH + 2 * KV_WIDTH + 2 * GMLP_WIDTH + POOL_WIDTH
D_FF = 2816
EPS = 1e-6

kernel_name = "hybrid_macaron_swa_gmlp_pool"


def rms_norm(x, g):
    xf = x.astype(jnp.float32)
    y = xf * lax.rsqrt(jnp.mean(xf * xf, axis=-1, keepdims=True) + EPS)
    return (y * g.astype(jnp.float32)).astype(x.dtype)


def swiglu(h, w_gate, w_up, w_down):
    return (jax.nn.silu(h @ w_gate) * (h @ w_up)) @ w_down


def sliding_window_attention(q, k, v, sinks):
    b, s = q.shape[0], q.shape[1]
    nb = s // BLOCK
    qb = q.reshape(b, nb, BLOCK, ATTN_KV_HEADS, ATTN_REP, HEAD_DIM)
    pad = ((0, 0), (BLOCK, 0), (0, 0), (0, 0))
    kp = jnp.pad(k, pad).reshape(b, nb + 1, BLOCK, ATTN_KV_HEADS, HEAD_DIM)
    vp = jnp.pad(v, pad).reshape(b, nb + 1, BLOCK, ATTN_KV_HEADS, HEAD_DIM)
    kb = jnp.concatenate([kp[:, :-1], kp[:, 1:]], axis=2)
    vb = jnp.concatenate([vp[:, :-1], vp[:, 1:]], axis=2)
    scores = jnp.einsum('bnqgrd,bnkgd->bngrqk', qb, kb).astype(jnp.float32) * ATTN_SCALE
    qi = jnp.arange(BLOCK)[:, None]
    kj = jnp.arange(2 * BLOCK)[None, :]
    rel = qi + BLOCK - kj
    band = (rel >= 0) & (rel < WINDOW)
    not_pad = (jnp.arange(nb)[:, None, None] > 0) | (kj >= BLOCK)[None]
    valid = band[None] & not_pad
    scores = jnp.where(valid[None, :, None, None], scores, -jnp.inf)
    sink = sinks.astype(jnp.float32).reshape(ATTN_KV_HEADS, ATTN_REP)[None, None, :, :, None, None]
    m = jnp.maximum(jnp.max(scores, axis=-1, keepdims=True), sink)
    p = jnp.exp(scores - m)
    p = p / (jnp.sum(p, axis=-1, keepdims=True) + jnp.exp(sink - m))
    o = jnp.einsum('bngrqk,bnkgd->bnqgrd', p.astype(v.dtype), vb)
    return o.reshape(b, s, ATTN_WIDTH)


def chunked_spatial_gating(u, v, v_norm, w_s, bias):
    b, s = u.shape[0], u.shape[1]
    nc = s // CHUNK
    v = rms_norm(v, v_norm)
    vc = v.reshape(b, nc, CHUNK, GMLP_HEADS, HEAD_DIM)
    causal = jnp.tril(jnp.ones((CHUNK, CHUNK), dtype=bool))
    w = jnp.where(causal[None], w_s, jnp.zeros((), w_s.dtype))
    f = jnp.einsum('hij,bcjhd->bcihd', w, vc) + bias.T[None, None, :, :, None]
    return u * f.reshape(b, s, GMLP_WIDTH)


def multiscale_pool(p_in, pool_w, pool_scale):
    s = p_in.shape[1]
    pf = p_in.astype(jnp.float32)
    cs = jnp.pad(jnp.cumsum(pf, axis=1), ((0, 0), (1, 0), (0, 0)))
    pos1 = jnp.arange(1, s + 1, dtype=jnp.float32)
    outs = []
    for g, w in enumerate(POOL_WINDOWS):
        sl = slice(g * POOL_GROUP_DIM, (g + 1) * POOL_GROUP_DIM)
        csg = cs[..., sl]
        lagged = jnp.pad(csg, ((0, 0), (w, 0), (0, 0)))[:, 1:s + 1]
        count = jnp.minimum(pos1, float(w))[None, :, None]
        diff = (csg[:, 1:] - lagged) / count - pf[..., sl]
        outs.append(diff.astype(p_in.dtype) @ pool_w[g])
    return jnp.concatenate(outs, axis=-1) * pool_scale


def _fwd_setup_inputs(seed: int = 0) -> dict:
    key = jax.random.key(seed)
    ks = jax.random.split(key, 24)
    f32 = jnp.float32

    def nrm(k, shape, scale):
        return jax.random.normal(k, shape, f32) * scale

    def gain(k, shape):
        return 1.0 + 0.02 * jax.random.normal(k, shape, f32)

    return {
        "x": jax.random.normal(ks[0], (BATCH, SEQ, D_MODEL), f32),
        "ffn1_norm": gain(ks[1], (DEPTH, D_MODEL)),
        "ffn1_w_gate": nrm(ks[2], (DEPTH, D_MODEL, D_FF), D_MODEL ** -0.5),
        "ffn1_w_up": nrm(ks[3], (DEPTH, D_MODEL, D_FF), D_MODEL ** -0.5),
        "ffn1_w_down": nrm(ks[4], (DEPTH, D_FF, D_MODEL), D_FF ** -0.5),
        "mix_norm": gain(ks[5], (DEPTH, D_MODEL)),
        "w_in": nrm(ks[6], (DEPTH, D_MODEL, IN_WIDTH), D_MODEL ** -0.5),
        "attn_sinks": nrm(ks[7], (DEPTH, ATTN_HEADS), 0.5),
        "gmlp_v_norm": gain(ks[8], (DEPTH, GMLP_WIDTH)),
        "gmlp_w_s": nrm(ks[9], (DEPTH, GMLP_HEADS, CHUNK, CHUNK), CHUNK ** -0.5),
        "gmlp_b": gain(ks[10], (DEPTH, GMLP_HEADS, CHUNK)),
        "pool_w": nrm(ks[11], (DEPTH, POOL_GROUPS, POOL_GROUP_DIM, POOL_GROUP_DIM), POOL_GROUP_DIM ** -0.5),
        "pool_scale": gain(ks[12], (DEPTH, POOL_WIDTH)),
        "w_out": nrm(ks[13], (DEPTH, MIX_WIDTH, D_MODEL), MIX_WIDTH ** -0.5),
        "ffn2_norm": gain(ks[14], (DEPTH, D_MODEL)),
        "ffn2_w_gate": nrm(ks[15], (DEPTH, D_MODEL, D_FF), D_MODEL ** -0.5),
        "ffn2_w_up": nrm(ks[16], (DEPTH, D_MODEL, D_FF), D_MODEL ** -0.5),
        "ffn2_w_down": nrm(ks[17], (DEPTH, D_FF, D_MODEL), D_FF ** -0.5),
        "final_norm": gain(ks[18], (D_MODEL,)),
    }


def _fwd_reference(x, ffn1_norm, ffn1_w_gate, ffn1_w_up, ffn1_w_down, mix_norm, w_in,
              attn_sinks, gmlp_v_norm, gmlp_w_s, gmlp_b, pool_w, pool_scale, w_out,
              ffn2_norm, ffn2_w_gate, ffn2_w_up, ffn2_w_down, final_norm):
    b, s = x.shape[0], x.shape[1]
    o_k = ATTN_WIDTH
    o_v = o_k + KV_WIDTH
    o_u = o_v + KV_WIDTH
    o_g = o_u + GMLP_WIDTH
    o_p = o_g + GMLP_WIDTH
    for l in range(DEPTH):
        h = rms_norm(x, ffn1_norm[l])
        x = x + 0.5 * swiglu(h, ffn1_w_gate[l], ffn1_w_up[l], ffn1_w_down[l])
        h = rms_norm(x, mix_norm[l])
        z = h @ w_in[l]
        q = z[..., :o_k].reshape(b, s, ATTN_HEADS, HEAD_DIM)
        k = z[..., o_k:o_v].reshape(b, s, ATTN_KV_HEADS, HEAD_DIM)
        v = z[..., o_v:o_u].reshape(b, s, ATTN_KV_HEADS, HEAD_DIM)
        g_u = jax.nn.gelu(z[..., o_u:o_g])
        g_v = jax.nn.gelu(z[..., o_g:o_p])
        p_in = z[..., o_p:]
        y_attn = sliding_window_attention(q, k, v, attn_sinks[l])
        y_gmlp = chunked_spatial_gating(g_u, g_v, gmlp_v_norm[l], gmlp_w_s[l], gmlp_b[l])
        y_pool = multiscale_pool(p_in, pool_w[l], pool_scale[l])
        y = jnp.concatenate([y_attn, y_gmlp, y_pool], axis=-1)
        x = x + y @ w_out[l]
        h = rms_norm(x, ffn2_norm[l])
        x = x + 0.5 * swiglu(h, ffn2_w_gate[l], ffn2_w_up[l], ffn2_w_down[l])
    return rms_norm(x, final_norm)


import jax as _jax
import jax.numpy as _jnp

TWIN_FORMAT = 'train_step'
FWD_PARAMS = ['x', 'ffn1_norm', 'ffn1_w_gate', 'ffn1_w_up', 'ffn1_w_down', 'mix_norm', 'w_in', 'attn_sinks', 'gmlp_v_norm', 'gmlp_w_s', 'gmlp_b', 'pool_w', 'pool_scale', 'w_out', 'ffn2_norm', 'ffn2_w_gate', 'ffn2_w_up', 'ffn2_w_down', 'final_norm']
TWIN_WEIGHTS = ['ffn1_norm', 'ffn1_w_gate', 'ffn1_w_up', 'ffn1_w_down', 'mix_norm', 'w_in', 'attn_sinks', 'gmlp_v_norm', 'gmlp_w_s', 'gmlp_b', 'pool_w', 'pool_scale', 'w_out', 'ffn2_norm', 'ffn2_w_gate', 'ffn2_w_up', 'ffn2_w_down', 'final_norm']
TWIN_DIFF_INPUT = 'x'
TWIN_INPUTS = ['x', 'ffn1_norm', 'ffn1_w_gate', 'ffn1_w_up', 'ffn1_w_down', 'mix_norm', 'w_in', 'attn_sinks', 'gmlp_v_norm', 'gmlp_w_s', 'gmlp_b', 'pool_w', 'pool_scale', 'w_out', 'ffn2_norm', 'ffn2_w_gate', 'ffn2_w_up', 'ffn2_w_down', 'final_norm', 'loss_target', 'm_ffn1_norm', 'm_ffn1_w_gate', 'm_ffn1_w_up', 'm_ffn1_w_down', 'm_mix_norm', 'm_w_in', 'm_attn_sinks', 'm_gmlp_v_norm', 'm_gmlp_w_s', 'm_gmlp_b', 'm_pool_w', 'm_pool_scale', 'm_w_out', 'm_ffn2_norm', 'm_ffn2_w_gate', 'm_ffn2_w_up', 'm_ffn2_w_down', 'm_final_norm', 'v_ffn1_norm', 'v_ffn1_w_gate', 'v_ffn1_w_up', 'v_ffn1_w_down', 'v_mix_norm', 'v_w_in', 'v_attn_sinks', 'v_gmlp_v_norm', 'v_gmlp_w_s', 'v_gmlp_b', 'v_pool_w', 'v_pool_scale', 'v_w_out', 'v_ffn2_norm', 'v_ffn2_w_gate', 'v_ffn2_w_up', 'v_ffn2_w_down', 'v_final_norm']
TWIN_OUTPUTS = ['loss', 'grad_x', 'grad_ffn1_norm', 'grad_ffn1_w_gate', 'grad_ffn1_w_up', 'grad_ffn1_w_down', 'grad_mix_norm', 'grad_w_in', 'grad_attn_sinks', 'grad_gmlp_v_norm', 'grad_gmlp_w_s', 'grad_gmlp_b', 'grad_pool_w', 'grad_pool_scale', 'grad_w_out', 'grad_ffn2_norm', 'grad_ffn2_w_gate', 'grad_ffn2_w_up', 'grad_ffn2_w_down', 'grad_final_norm', 'delta_ffn1_norm', 'delta_ffn1_w_gate', 'delta_ffn1_w_up', 'delta_ffn1_w_down', 'delta_mix_norm', 'delta_w_in', 'delta_attn_sinks', 'delta_gmlp_v_norm', 'delta_gmlp_w_s', 'delta_gmlp_b', 'delta_pool_w', 'delta_pool_scale', 'delta_w_out', 'delta_ffn2_norm', 'delta_ffn2_w_gate', 'delta_ffn2_w_up', 'delta_ffn2_w_down', 'delta_final_norm', 'new_m_ffn1_norm', 'new_m_ffn1_w_gate', 'new_m_ffn1_w_up', 'new_m_ffn1_w_down', 'new_m_mix_norm', 'new_m_w_in', 'new_m_attn_sinks', 'new_m_gmlp_v_norm', 'new_m_gmlp_w_s', 'new_m_gmlp_b', 'new_m_pool_w', 'new_m_pool_scale', 'new_m_w_out', 'new_m_ffn2_norm', 'new_m_ffn2_w_gate', 'new_m_ffn2_w_up', 'new_m_ffn2_w_down', 'new_m_final_norm', 'new_v_ffn1_norm', 'new_v_ffn1_w_gate', 'new_v_ffn1_w_up', 'new_v_ffn1_w_down', 'new_v_mix_norm', 'new_v_w_in', 'new_v_attn_sinks', 'new_v_gmlp_v_norm', 'new_v_gmlp_w_s', 'new_v_gmlp_b', 'new_v_pool_w', 'new_v_pool_scale', 'new_v_w_out', 'new_v_ffn2_norm', 'new_v_ffn2_w_gate', 'new_v_ffn2_w_up', 'new_v_ffn2_w_down', 'new_v_final_norm']
TWIN_LEAF_KINDS = {'loss': 'loss', 'grad_x': 'grad_x', 'grad_ffn1_norm': 'grad_w', 'grad_ffn1_w_gate': 'grad_w', 'grad_ffn1_w_up': 'grad_w', 'grad_ffn1_w_down': 'grad_w', 'grad_mix_norm': 'grad_w', 'grad_w_in': 'grad_w', 'grad_attn_sinks': 'grad_w', 'grad_gmlp_v_norm': 'grad_w', 'grad_gmlp_w_s': 'grad_w', 'grad_gmlp_b': 'grad_w', 'grad_pool_w': 'grad_w', 'grad_pool_scale': 'grad_w', 'grad_w_out': 'grad_w', 'grad_ffn2_norm': 'grad_w', 'grad_ffn2_w_gate': 'grad_w', 'grad_ffn2_w_up': 'grad_w', 'grad_ffn2_w_down': 'grad_w', 'grad_final_norm': 'grad_w', 'delta_ffn1_norm': 'delta_w', 'delta_ffn1_w_gate': 'delta_w', 'delta_ffn1_w_up': 'delta_w', 'delta_ffn1_w_down': 'delta_w', 'delta_mix_norm': 'delta_w', 'delta_w_in': 'delta_w', 'delta_attn_sinks': 'delta_w', 'delta_gmlp_v_norm': 'delta_w', 'delta_gmlp_w_s': 'delta_w', 'delta_gmlp_b': 'delta_w', 'delta_pool_w': 'delta_w', 'delta_pool_scale': 'delta_w', 'delta_w_out': 'delta_w', 'delta_ffn2_norm': 'delta_w', 'delta_ffn2_w_gate': 'delta_w', 'delta_ffn2_w_up': 'delta_w', 'delta_ffn2_w_down': 'delta_w', 'delta_final_norm': 'delta_w', 'new_m_ffn1_norm': 'new_m', 'new_m_ffn1_w_gate': 'new_m', 'new_m_ffn1_w_up': 'new_m', 'new_m_ffn1_w_down': 'new_m', 'new_m_mix_norm': 'new_m', 'new_m_w_in': 'new_m', 'new_m_attn_sinks': 'new_m', 'new_m_gmlp_v_norm': 'new_m', 'new_m_gmlp_w_s': 'new_m', 'new_m_gmlp_b': 'new_m', 'new_m_pool_w': 'new_m', 'new_m_pool_scale': 'new_m', 'new_m_w_out': 'new_m', 'new_m_ffn2_norm': 'new_m', 'new_m_ffn2_w_gate': 'new_m', 'new_m_ffn2_w_up': 'new_m', 'new_m_ffn2_w_down': 'new_m', 'new_m_final_norm': 'new_m', 'new_v_ffn1_norm': 'new_v', 'new_v_ffn1_w_gate': 'new_v', 'new_v_ffn1_w_up': 'new_v', 'new_v_ffn1_w_down': 'new_v', 'new_v_mix_norm': 'new_v', 'new_v_w_in': 'new_v', 'new_v_attn_sinks': 'new_v', 'new_v_gmlp_v_norm': 'new_v', 'new_v_gmlp_w_s': 'new_v', 'new_v_gmlp_b': 'new_v', 'new_v_pool_w': 'new_v', 'new_v_pool_scale': 'new_v', 'new_v_w_out': 'new_v', 'new_v_ffn2_norm': 'new_v', 'new_v_ffn2_w_gate': 'new_v', 'new_v_ffn2_w_up': 'new_v', 'new_v_ffn2_w_down': 'new_v', 'new_v_final_norm': 'new_v'}


def _forward(args):
    return _fwd_reference(*[args[k] for k in FWD_PARAMS])


def _output_shape():
    def fwd():
        inp = _fwd_setup_inputs(0)
        return _fwd_reference(*[inp[k] for k in FWD_PARAMS])
    out = _jax.eval_shape(fwd)
    return out.shape, out.dtype

N_MICROBATCH = 1
ADAM_LR = 0.001
ADAM_B1 = 0.9
ADAM_B2 = 0.999
ADAM_EPS = 1e-08
ADAM_WD = 0.01
ADAM_STEP = 10
PER_EXAMPLE_BATCH_AXIS = {'x': 0, 'loss_target': 0}
SHARED_INPUTS = []
_WEIGHT_DTYPES = {'ffn1_norm': _jnp.float32, 'ffn1_w_gate': _jnp.float32, 'ffn1_w_up': _jnp.float32, 'ffn1_w_down': _jnp.float32, 'mix_norm': _jnp.float32, 'w_in': _jnp.float32, 'attn_sinks': _jnp.float32, 'gmlp_v_norm': _jnp.float32, 'gmlp_w_s': _jnp.float32, 'gmlp_b': _jnp.float32, 'pool_w': _jnp.float32, 'pool_scale': _jnp.float32, 'w_out': _jnp.float32, 'ffn2_norm': _jnp.float32, 'ffn2_w_gate': _jnp.float32, 'ffn2_w_up': _jnp.float32, 'ffn2_w_down': _jnp.float32, 'final_norm': _jnp.float32}
MOMENT_SCALE = {'ffn1_norm': 1.089468e-01, 'ffn1_w_gate': 4.502922e-02, 'ffn1_w_up': 4.358874e-02, 'ffn1_w_down': 7.230178e-02, 'mix_norm': 1.448616e-01, 'w_in': 1.162854e-01, 'attn_sinks': 3.225738e-02, 'gmlp_v_norm': 9.809684e-02, 'gmlp_w_s': 6.795036e-02, 'gmlp_b': 1.007595e-01, 'pool_w': 1.850987e-01, 'pool_scale': 1.856733e-01, 'w_out': 1.291427e-01, 'ffn2_norm': 8.457305e-02, 'ffn2_w_gate': 3.636189e-02, 'ffn2_w_up': 3.538951e-02, 'ffn2_w_down': 5.863777e-02, 'final_norm': 6.407716e+01}


def _to_microbatches(a, axis):
    t = _jnp.moveaxis(a, axis, 0)
    t = t.reshape((N_MICROBATCH, t.shape[0] // N_MICROBATCH) + t.shape[1:])
    return _jnp.moveaxis(t, 1, axis + 1)


def setup_inputs(seed: int = 0) -> dict:
    inp = _fwd_setup_inputs(seed)
    key = _jax.random.fold_in(_jax.random.key(seed), 7919)
    shape, _ = _output_shape()
    out = dict(inp)
    out["loss_target"] = _jax.random.normal(_jax.random.fold_in(key, 0), shape, _jnp.float32)
    for i, name in enumerate(TWIN_WEIGHTS):
        w = inp[name].astype(_jnp.float32)
        if MOMENT_SCALE is None:
            s = _jnp.sqrt(_jnp.mean(_jnp.square(w)) + 1e-30)
        else:
            s = MOMENT_SCALE[name]
        km, kv = _jax.random.split(_jax.random.fold_in(key, i + 1))
        out[name] = w
        out["m_" + name] = s * _jax.random.normal(km, w.shape, _jnp.float32)
        out["v_" + name] = (s * s) * _jax.random.uniform(kv, w.shape, _jnp.float32, 0.5, 1.5)
    if N_MICROBATCH > 1:
        for name, axis in PER_EXAMPLE_BATCH_AXIS.items():
            out[name] = _to_microbatches(out[name], axis)
    return {'x': out['x'], 'ffn1_norm': out['ffn1_norm'], 'ffn1_w_gate': out['ffn1_w_gate'], 'ffn1_w_up': out['ffn1_w_up'], 'ffn1_w_down': out['ffn1_w_down'], 'mix_norm': out['mix_norm'], 'w_in': out['w_in'], 'attn_sinks': out['attn_sinks'], 'gmlp_v_norm': out['gmlp_v_norm'], 'gmlp_w_s': out['gmlp_w_s'], 'gmlp_b': out['gmlp_b'], 'pool_w': out['pool_w'], 'pool_scale': out['pool_scale'], 'w_out': out['w_out'], 'ffn2_norm': out['ffn2_norm'], 'ffn2_w_gate': out['ffn2_w_gate'], 'ffn2_w_up': out['ffn2_w_up'], 'ffn2_w_down': out['ffn2_w_down'], 'final_norm': out['final_norm'], 'loss_target': out['loss_target'], 'm_ffn1_norm': out['m_ffn1_norm'], 'm_ffn1_w_gate': out['m_ffn1_w_gate'], 'm_ffn1_w_up': out['m_ffn1_w_up'], 'm_ffn1_w_down': out['m_ffn1_w_down'], 'm_mix_norm': out['m_mix_norm'], 'm_w_in': out['m_w_in'], 'm_attn_sinks': out['m_attn_sinks'], 'm_gmlp_v_norm': out['m_gmlp_v_norm'], 'm_gmlp_w_s': out['m_gmlp_w_s'], 'm_gmlp_b': out['m_gmlp_b'], 'm_pool_w': out['m_pool_w'], 'm_pool_scale': out['m_pool_scale'], 'm_w_out': out['m_w_out'], 'm_ffn2_norm': out['m_ffn2_norm'], 'm_ffn2_w_gate': out['m_ffn2_w_gate'], 'm_ffn2_w_up': out['m_ffn2_w_up'], 'm_ffn2_w_down': out['m_ffn2_w_down'], 'm_final_norm': out['m_final_norm'], 'v_ffn1_norm': out['v_ffn1_norm'], 'v_ffn1_w_gate': out['v_ffn1_w_gate'], 'v_ffn1_w_up': out['v_ffn1_w_up'], 'v_ffn1_w_down': out['v_ffn1_w_down'], 'v_mix_norm': out['v_mix_norm'], 'v_w_in': out['v_w_in'], 'v_attn_sinks': out['v_attn_sinks'], 'v_gmlp_v_norm': out['v_gmlp_v_norm'], 'v_gmlp_w_s': out['v_gmlp_w_s'], 'v_gmlp_b': out['v_gmlp_b'], 'v_pool_w': out['v_pool_w'], 'v_pool_scale': out['v_pool_scale'], 'v_w_out': out['v_w_out'], 'v_ffn2_norm': out['v_ffn2_norm'], 'v_ffn2_w_gate': out['v_ffn2_w_gate'], 'v_ffn2_w_up': out['v_ffn2_w_up'], 'v_ffn2_w_down': out['v_ffn2_w_down'], 'v_final_norm': out['v_final_norm']}


def _loss(weights, diff, rest, loss_target):
    with _jax.named_scope("forward"):
        args = {**rest, TWIN_DIFF_INPUT: diff, **{k: w.astype(_WEIGHT_DTYPES[k]) for k, w in weights.items()}}
        y = _forward(args)
    with _jax.named_scope("loss_head"):
        err = _jnp.square(y.astype(_jnp.float32) - loss_target)
        return 0.5 * _jnp.sum(_jnp.mean(err, axis=-1)) if err.ndim else 0.5 * err


def _adamw(w, g, m, v):
    m = ADAM_B1 * m + (1.0 - ADAM_B1) * g
    v = ADAM_B2 * v + (1.0 - ADAM_B2) * _jnp.square(g)
    m_hat = m / (1.0 - ADAM_B1 ** ADAM_STEP)
    v_hat = v / (1.0 - ADAM_B2 ** ADAM_STEP)
    delta = -ADAM_LR * (m_hat / (_jnp.sqrt(v_hat) + ADAM_EPS) + ADAM_WD * w)
    return delta, m, v


def reference(x, ffn1_norm, ffn1_w_gate, ffn1_w_up, ffn1_w_down, mix_norm, w_in, attn_sinks, gmlp_v_norm, gmlp_w_s, gmlp_b, pool_w, pool_scale, w_out, ffn2_norm, ffn2_w_gate, ffn2_w_up, ffn2_w_down, final_norm, loss_target, m_ffn1_norm, m_ffn1_w_gate, m_ffn1_w_up, m_ffn1_w_down, m_mix_norm, m_w_in, m_attn_sinks, m_gmlp_v_norm, m_gmlp_w_s, m_gmlp_b, m_pool_w, m_pool_scale, m_w_out, m_ffn2_norm, m_ffn2_w_gate, m_ffn2_w_up, m_ffn2_w_down, m_final_norm, v_ffn1_norm, v_ffn1_w_gate, v_ffn1_w_up, v_ffn1_w_down, v_mix_norm, v_w_in, v_attn_sinks, v_gmlp_v_norm, v_gmlp_w_s, v_gmlp_b, v_pool_w, v_pool_scale, v_w_out, v_ffn2_norm, v_ffn2_w_gate, v_ffn2_w_up, v_ffn2_w_down, v_final_norm):
    given = dict(x=x, ffn1_norm=ffn1_norm, ffn1_w_gate=ffn1_w_gate, ffn1_w_up=ffn1_w_up, ffn1_w_down=ffn1_w_down, mix_norm=mix_norm, w_in=w_in, attn_sinks=attn_sinks, gmlp_v_norm=gmlp_v_norm, gmlp_w_s=gmlp_w_s, gmlp_b=gmlp_b, pool_w=pool_w, pool_scale=pool_scale, w_out=w_out, ffn2_norm=ffn2_norm, ffn2_w_gate=ffn2_w_gate, ffn2_w_up=ffn2_w_up, ffn2_w_down=ffn2_w_down, final_norm=final_norm, loss_target=loss_target, m_ffn1_norm=m_ffn1_norm, m_ffn1_w_gate=m_ffn1_w_gate, m_ffn1_w_up=m_ffn1_w_up, m_ffn1_w_down=m_ffn1_w_down, m_mix_norm=m_mix_norm, m_w_in=m_w_in, m_attn_sinks=m_attn_sinks, m_gmlp_v_norm=m_gmlp_v_norm, m_gmlp_w_s=m_gmlp_w_s, m_gmlp_b=m_gmlp_b, m_pool_w=m_pool_w, m_pool_scale=m_pool_scale, m_w_out=m_w_out, m_ffn2_norm=m_ffn2_norm, m_ffn2_w_gate=m_ffn2_w_gate, m_ffn2_w_up=m_ffn2_w_up, m_ffn2_w_down=m_ffn2_w_down, m_final_norm=m_final_norm, v_ffn1_norm=v_ffn1_norm, v_ffn1_w_gate=v_ffn1_w_gate, v_ffn1_w_up=v_ffn1_w_up, v_ffn1_w_down=v_ffn1_w_down, v_mix_norm=v_mix_norm, v_w_in=v_w_in, v_attn_sinks=v_attn_sinks, v_gmlp_v_norm=v_gmlp_v_norm, v_gmlp_w_s=v_gmlp_w_s, v_gmlp_b=v_gmlp_b, v_pool_w=v_pool_w, v_pool_scale=v_pool_scale, v_w_out=v_w_out, v_ffn2_norm=v_ffn2_norm, v_ffn2_w_gate=v_ffn2_w_gate, v_ffn2_w_up=v_ffn2_w_up, v_ffn2_w_down=v_ffn2_w_down, v_final_norm=v_final_norm)
    weights = {n: given[n] for n in TWIN_WEIGHTS}
    shared = {n: given[n] for n in SHARED_INPUTS}
    per_example = {n: given[n] for n in ['x']}
    grad_fn = _jax.value_and_grad(_loss, argnums=(0, 1))

    def one_microbatch(ex, loss_target):
        ex = dict(ex)
        diff = ex.pop(TWIN_DIFF_INPUT)
        return grad_fn(weights, diff, {**shared, **ex}, loss_target)

    if N_MICROBATCH == 1:
        loss, (grad_w, grad_x) = one_microbatch(per_example, given["loss_target"])
    else:
        def body(carry, xs):
            loss_sum, grad_sum = carry
            l_k, (gw_k, gx_k) = one_microbatch(xs[0], xs[1])
            with _jax.named_scope("update"):
                return (loss_sum + l_k, _jax.tree.map(_jnp.add, grad_sum, gw_k)), gx_k

        init = (_jnp.zeros((), _jnp.float32), _jax.tree.map(_jnp.zeros_like, weights))
        (loss, grad_w), grad_x = _jax.lax.scan(body, init, (per_example, given["loss_target"]))
    with _jax.named_scope("update"):
        delta_w, new_m, new_v = {}, {}, {}
        for n in TWIN_WEIGHTS:
            delta_w[n], new_m[n], new_v[n] = _adamw(weights[n], grad_w[n], given["m_" + n], given["v_" + n])
    return (loss, grad_x, *[grad_w[n] for n in TWIN_WEIGHTS], *[delta_w[n] for n in TWIN_WEIGHTS],
            *[new_m[n] for n in TWIN_WEIGHTS], *[new_v[n] for n in TWIN_WEIGHTS])
```

```python
import functools
import math

import jax
import jax.numpy as jnp
from jax import lax
from jax.experimental import pallas as pl
from jax.experimental.pallas import tpu as pltpu

F32 = jnp.float32
BF16 = jnp.bfloat16
MESH = pl.DeviceIdType.MESH

D = 1024
FF = 2816
INW = 1536
N_DEV = 8
DEPTH = 2
BLK = 128
HD = 64
N_HEADS = 8
N_KV = 2
REP = 4
ATTN_SCALE = HD ** -0.5
POOL_WINDOWS = (2, 4, 8, 16)
EPS = 1e-6
NEG = -1e30
FC = 256
GELU_C0 = math.sqrt(2.0 / math.pi)
GELU_C1 = 0.044715

ADAM_LR = 0.001
ADAM_B1 = 0.9
ADAM_B2 = 0.999
ADAM_EPS = 1e-08
ADAM_WD = 0.01
ADAM_STEP = 10

VMEM_LIMIT = 56 * 1024 * 1024

O_K, O_V, O_U, O_G, O_P = 512, 640, 768, 1024, 1280


def _call(body, **kw):
    return pl.pallas_call(body, **kw)


def _params(sem=None, vmem=VMEM_LIMIT):
    return pltpu.CompilerParams(dimension_semantics=sem, vmem_limit_bytes=vmem)


def _nn(a, b):
    return lax.dot_general(a, b, (((1,), (0,)), ((), ())), preferred_element_type=F32)


def _nt(a, b):
    return lax.dot_general(a, b, (((1,), (1,)), ((), ())), preferred_element_type=F32)


def _tn(a, b):
    return lax.dot_general(a, b, (((0,), (0,)), ((), ())), preferred_element_type=F32)


def _sigmoid(a):
    return 1.0 / (1.0 + jnp.exp(-a))


def _gelu(x):
    t = jnp.tanh(GELU_C0 * (x + GELU_C1 * x * x * x))
    return 0.5 * x * (1.0 + t), t


def _gelu_grad(x, t):
    return 0.5 * (1.0 + t) + 0.5 * x * (1.0 - t * t) * (GELU_C0 * (1.0 + 3.0 * GELU_C1 * x * x))


def _rms(x):
    r = lax.rsqrt(jnp.mean(x * x, axis=-1, keepdims=True) + EPS)
    return x * r, r


def _rms_bwd(dy, xh, r, g):
    dg = jnp.sum(dy * xh, axis=0, keepdims=True)
    dxh = dy * g
    dx = r * (dxh - xh * jnp.mean(dxh * xh, axis=-1, keepdims=True))
    return dx, dg


def _wspec(rows, m):
    return pl.BlockSpec((None, rows, D), lambda i, m=m: (m, 0, 0), pipeline_mode=pl.Buffered(1))


def _rowspec(tm, cols):
    return pl.BlockSpec((tm, cols), lambda i: (i, 0))


def _fixspec(rows, cols):
    return pl.BlockSpec((rows, cols), lambda i: (0, 0))


def _ffn_fwd(x, gain, w352, mg, name):
    s = x.shape[0]
    tm = min(512, s)

    def body(x_ref, g_ref, wg_ref, wu_ref, wd_ref, xo_ref, a_ref, b_ref, acc_ref):
        xt = x_ref[...]
        xh, _ = _rms(xt)
        h = (xh * g_ref[...]).astype(BF16)
        for c in range(FF // FC):
            sl = slice(c * FC, (c + 1) * FC)
            a = _nt(h, wg_ref[sl, :])
            b = _nt(h, wu_ref[sl, :])
            a_ref[:, sl] = a.astype(BF16)
            b_ref[:, sl] = b.astype(BF16)
            hid = ((a * _sigmoid(a)) * b).astype(BF16)
            y = _nn(hid, wd_ref[sl, :])
            if c == 0:
                acc_ref[...] = y
            else:
                acc_ref[...] += y
        xo_ref[...] = xt + 0.5 * acc_ref[...]

    return _call(
        body, name=name, grid=(s // tm,),
        in_specs=[_rowspec(tm, D), _fixspec(1, D), _wspec(FF, mg), _wspec(FF, mg + 1), _wspec(FF, mg + 2)],
        out_specs=[_rowspec(tm, D), _rowspec(tm, FF), _rowspec(tm, FF)],
        out_shape=[jax.ShapeDtypeStruct((s, D), F32), jax.ShapeDtypeStruct((s, FF), BF16),
                   jax.ShapeDtypeStruct((s, FF), BF16)],
        scratch_shapes=[pltpu.VMEM((tm, D), F32)],
        compiler_params=_params(("arbitrary",)),
    )(x, gain, w352, w352, w352)


def _ffn_bwd(x, gain, dy, a, b, w352, mg, name):
    s = x.shape[0]
    tm = min(256, s)

    def body(x_ref, g_ref, dy_ref, a_ref, b_ref, wg_ref, wu_ref, wd_ref,
             dx_ref, da_ref, db_ref, hid_ref, h_ref, dyb_ref, dg_ref, acc_ref):
        i = pl.program_id(0)
        xt = x_ref[...]
        g = g_ref[...]
        xh, r = _rms(xt)
        h_ref[...] = (xh * g).astype(BF16)
        dyt = dy_ref[...]
        dyb = (0.5 * dyt).astype(BF16)
        dyb_ref[...] = dyb
        for c in range(FF // FC):
            sl = slice(c * FC, (c + 1) * FC)
            dhid = _nt(dyb, wd_ref[sl, :])
            av = a_ref[:, sl].astype(F32)
            bv = b_ref[:, sl].astype(F32)
            sig = _sigmoid(av)
            sa = av * sig
            da = ((dhid * bv) * (sig * (1.0 + av * (1.0 - sig)))).astype(BF16)
            db = (dhid * sa).astype(BF16)
            da_ref[:, sl] = da
            db_ref[:, sl] = db
            hid_ref[:, sl] = (sa * bv).astype(BF16)
            t = _nn(da, wg_ref[sl, :]) + _nn(db, wu_ref[sl, :])
            if c == 0:
                acc_ref[...] = t
            else:
                acc_ref[...] += t
        dxn, dg = _rms_bwd(acc_ref[...], xh, r, g)
        dx_ref[...] = dyt + dxn

        @pl.when(i == 0)
        def _():
            dg_ref[...] = jnp.zeros_like(dg_ref)

        dg_ref[0:1, :] += dg

    return _call(
        body, name=name, grid=(s // tm,),
        in_specs=[_rowspec(tm, D), _fixspec(1, D), _rowspec(tm, D), _rowspec(tm, FF), _rowspec(tm, FF),
                  _wspec(FF, mg), _wspec(FF, mg + 1), _wspec(FF, mg + 2)],
        out_specs=[_rowspec(tm, D), _rowspec(tm, FF), _rowspec(tm, FF), _rowspec(tm, FF),
                   _rowspec(tm, D), _rowspec(tm, D), _fixspec(8, D)],
        out_shape=[jax.ShapeDtypeStruct((s, D), F32), jax.ShapeDtypeStruct((s, FF), BF16),
                   jax.ShapeDtypeStruct((s, FF), BF16), jax.ShapeDtypeStruct((s, FF), BF16),
                   jax.ShapeDtypeStruct((s, D), BF16), jax.ShapeDtypeStruct((s, D), BF16),
                   jax.ShapeDtypeStruct((8, D), F32)],
        scratch_shapes=[pltpu.VMEM((tm, D), F32)],
        compiler_params=_params(("arbitrary",)),
    )(x, gain, dy, a, b, w352, w352, w352)


def _wgrad(a, b, g, n_slabs, m, name):
    s, mm = a.shape
    tk = min(512, s)
    mb = mm // 2 if mm == FF else mm
    chunks = []
    left = mb
    while left > 0:
        cs = min(512, left)
        chunks.append(cs)
        left -= cs

    def body(*refs):
        a_ref, b_ref = refs[0], refs[1]
        out_ref = refs[-1]
        k = pl.program_id(1)

        @pl.when(k == 0)
        def _():
            out_ref[...] = jnp.zeros_like(out_ref)

        bt = b_ref[...]
        off = 0
        for cs in chunks:
            out_ref[off:off + cs, :] += _tn(a_ref[:, off:off + cs], bt)
            off += cs

    in_specs = [pl.BlockSpec((tk, mb), lambda i, k: (k, i)), pl.BlockSpec((tk, D), lambda i, k: (k, 0))]
    args = [a, b]
    aliases = {}
    if g is not None:
        in_specs.append(pl.BlockSpec(memory_space=pl.ANY))
        args.append(g)
        aliases = {2: 0}
    return _call(
        body, name=name, grid=(mm // mb, s // tk),
        in_specs=in_specs,
        out_specs=pl.BlockSpec((None, mb, D), lambda i, k, m=m: (m, i, 0)),
        out_shape=jax.ShapeDtypeStruct((n_slabs, mm, D), F32),
        input_output_aliases=aliases,
        compiler_params=_params(("arbitrary", "arbitrary")),
    )(*args)


def _mixin_fwd(x, gain, w192, l, name):
    s = x.shape[0]
    tm = min(512, s)

    def body(x_ref, g_ref, w_ref, z_ref, h_ref):
        xh, _ = _rms(x_ref[...])
        h = (xh * g_ref[...]).astype(BF16)
        h_ref[...] = h
        z_ref[...] = _nt(h, w_ref[...])

    return _call(
        body, name=name, grid=(s // tm,),
        in_specs=[_rowspec(tm, D), _fixspec(1, D), _wspec(INW, l)],
        out_specs=[_rowspec(tm, INW), _rowspec(tm, D)],
        out_shape=[jax.ShapeDtypeStruct((s, INW), F32), jax.ShapeDtypeStruct((s, D), BF16)],
        compiler_params=_params(("arbitrary",)),
    )(x, gain, w192)


def _mixin_bwd(x, gain, dz, dx_in, w192, l, name):
    s = x.shape[0]
    tm = min(256, s)

    def body(x_ref, g_ref, dz_ref, dxi_ref, w_ref, dx_ref, dzb_ref, dg_ref):
        i = pl.program_id(0)
        g = g_ref[...]
        xh, r = _rms(x_ref[...])
        dzb = dz_ref[...].astype(BF16)
        dzb_ref[...] = dzb
        dh = _nn(dzb, w_ref[...])
        dxn, dg = _rms_bwd(dh, xh, r, g)
        dx_ref[...] = dxi_ref[...] + dxn

        @pl.when(i == 0)
        def _():
            dg_ref[...] = jnp.zeros_like(dg_ref)

        dg_ref[0:1, :] += dg

    return _call(
        body, name=name, grid=(s // tm,),
        in_specs=[_rowspec(tm, D), _fixspec(1, D), _rowspec(tm, INW), _rowspec(tm, D), _wspec(INW, l)],
        out_specs=[_rowspec(tm, D), _rowspec(tm, INW), _fixspec(8, D)],
        out_shape=[jax.ShapeDtypeStruct((s, D), F32), jax.ShapeDtypeStruct((s, INW), BF16),
                   jax.ShapeDtypeStruct((8, D), F32)],
        compiler_params=_params(("arbitrary",)),
    )(x, gain, dz, dx_in, w192)


def _mixout_fwd(x, y, w128, l, name):
    s = x.shape[0]
    tm = min(512, s)

    def body(x_ref, y_ref, w_ref, xo_ref):
        xo_ref[...] = x_ref[...] + _nn(y_ref[...], w_ref[...])

    return _call(
        body, name=name, grid=(s // tm,),
        in_specs=[_rowspec(tm, D), _rowspec(tm, D), _wspec(D, l)],
        out_specs=_rowspec(tm, D),
        out_shape=jax.ShapeDtypeStruct((s, D), F32),
        compiler_params=_params(("arbitrary",)),
    )(x, y, w128)


def _mixout_bwd(dx, w128, l, name):
    s = dx.shape[0]
    tm = min(512, s)

    def body(dx_ref, w_ref, dy_ref, dxb_ref):
        dxb = dx_ref[...].astype(BF16)
        dxb_ref[...] = dxb
        dy_ref[...] = _nt(dxb, w_ref[...])

    return _call(
        body, name=name, grid=(s // tm,),
        in_specs=[_rowspec(tm, D), _wspec(D, l)],
        out_specs=[_rowspec(tm, D), _rowspec(tm, D)],
        out_shape=[jax.ShapeDtypeStruct((s, D), F32), jax.ShapeDtypeStruct((s, D), BF16)],
        compiler_params=_params(("arbitrary",)),
    )(dx, w128)


def _loss_head(x, gain, tgt, name):
    s = x.shape[0]
    tm = min(512, s)

    def body(x_ref, g_ref, t_ref, dx_ref, loss_ref, dg_ref):
        i = pl.program_id(0)
        g = g_ref[...]
        xh, r = _rms(x_ref[...])
        err = xh * g - t_ref[...]
        tok = jnp.mean(err * err, axis=-1, keepdims=True)
        lp = 0.5 * jnp.sum(tok, axis=0, keepdims=True)
        dxn, dg = _rms_bwd(err * (1.0 / D), xh, r, g)
        dx_ref[...] = dxn

        @pl.when(i == 0)
        def _():
            dg_ref[...] = jnp.zeros_like(dg_ref)
            loss_ref[...] = jnp.zeros_like(loss_ref)

        dg_ref[0:1, :] += dg
        loss_ref[0:1, :] += lp + jnp.zeros((1, 128), F32)

    return _call(
        body, name=name, grid=(s // tm,),
        in_specs=[_rowspec(tm, D), _fixspec(1, D), _rowspec(tm, D)],
        out_specs=[_rowspec(tm, D), _fixspec(8, 128), _fixspec(8, D)],
        out_shape=[jax.ShapeDtypeStruct((s, D), F32), jax.ShapeDtypeStruct((8, 128), F32),
                   jax.ShapeDtypeStruct((8, D), F32)],
        compiler_params=_params(("arbitrary",)),
    )(x, gain, tgt)


def _attn_mask(n):
    row = lax.broadcasted_iota(jnp.int32, (BLK, 2 * BLK), 0)
    col = lax.broadcasted_iota(jnp.int32, (BLK, 2 * BLK), 1)
    lo = jnp.where(n > 0, 0, BLK)
    return (col > row) & (col <= row + BLK) & (col >= lo)


def _tril_mask():
    row = lax.broadcasted_iota(jnp.int32, (BLK, BLK), 0)
    col = lax.broadcasted_iota(jnp.int32, (BLK, BLK), 1)
    return col <= row


def _lane_group_select(vals):
    lane = lax.broadcasted_iota(jnp.int32, vals[0].shape, 1)
    return jnp.where(lane < 64, vals[0], jnp.where(lane < 128, vals[1], jnp.where(lane < 192, vals[2], vals[3])))


def _pool_count(n):
    row = lax.broadcasted_iota(jnp.int32, (BLK, 256), 0)
    pos1 = (n * BLK + row + 1).astype(F32)
    wl = _lane_group_select([jnp.full((BLK, 256), float(w), F32) for w in POOL_WINDOWS])
    return jnp.minimum(pos1, wl)


def _window_sums(e, forward):
    tot = e.shape[0]
    lv = e
    out = []
    for sh in (1, 2, 4, 8):
        lv = lv + pltpu.roll(lv, sh if forward else tot - sh, 0)
        out.append(lv)
    return _lane_group_select(out)


def _mix_common(n, zc_ref, zkvp_ref, zpp_ref, vn_ref, ws_ref, bt_ref, pw_ref):
    u, tu = _gelu(zc_ref[:, O_U:O_G])
    gv, tv = _gelu(zc_ref[:, O_G:O_P])
    xh, rv = _rms(gv)
    vnb = (xh * vn_ref[...]).astype(BF16)
    tril = _tril_mask()
    wm = [jnp.where(tril, ws_ref[h], 0.0).astype(BF16) for h in range(4)]
    f = jnp.concatenate(
        [_nn(wm[h], vnb[:, h * HD:(h + 1) * HD]) + bt_ref[:, h:h + 1] for h in range(4)], axis=1)
    p = zc_ref[:, O_P:INW]
    pm = jnp.where(n > 0, 1.0, 0.0)
    e = jnp.concatenate([zpp_ref[...] * pm, p], axis=0)
    cnt = _pool_count(n)
    diff = (_window_sums(e, True)[BLK:, :] / cnt - p).astype(BF16)
    pwb = [pw_ref[g].astype(BF16) for g in range(4)]
    pout = jnp.concatenate([_nn(diff[:, g * HD:(g + 1) * HD], pwb[g]) for g in range(4)], axis=1)
    return dict(u=u, tu=tu, tv=tv, xh=xh, rv=rv, vnb=vnb, wm=wm, f=f, cnt=cnt, diff=diff, pwb=pwb, pout=pout,
                tril=tril)


def _kv_window(zc_ref, zkvp_ref, g):
    kk = jnp.concatenate([zkvp_ref[:, g * HD:(g + 1) * HD],
                          zc_ref[:, O_K + g * HD:O_K + (g + 1) * HD]], axis=0).astype(BF16)
    vv = jnp.concatenate([zkvp_ref[:, BLK + g * HD:BLK + (g + 1) * HD],
                          zc_ref[:, O_V + g * HD:O_V + (g + 1) * HD]], axis=0).astype(BF16)
    return kk, vv


def _mix_fwd(z, sinks, vnorm, ws, bias_t, pw, pscale, name):
    s = z.shape[0]
    nb = s // BLK

    def body(sink_ref, zc_ref, zkvp_ref, zpp_ref, vn_ref, ws_ref, bt_ref, pw_ref, ps_ref, y_ref, lse_ref):
        n = pl.program_id(0)
        valid = _attn_mask(n)
        lse_ref[...] = jnp.zeros_like(lse_ref)
        for g in range(N_KV):
            kk, vv = _kv_window(zc_ref, zkvp_ref, g)
            for rr in range(REP):
                h = g * REP + rr
                qh = zc_ref[:, h * HD:(h + 1) * HD].astype(BF16)
                sc = jnp.where(valid, _nt(qh, kk) * ATTN_SCALE, NEG)
                sink = sink_ref[h]
                m = jnp.maximum(jnp.max(sc, axis=-1, keepdims=True), sink)
                ex = jnp.exp(sc - m)
                den = jnp.sum(ex, axis=-1, keepdims=True) + jnp.exp(sink - m)
                pr = (ex / den).astype(BF16)
                y_ref[:, h * HD:(h + 1) * HD] = _nn(pr, vv).astype(BF16)
                lse_ref[:, h:h + 1] = m + jnp.log(den)
        c = _mix_common(n, zc_ref, zkvp_ref, zpp_ref, vn_ref, ws_ref, bt_ref, pw_ref)
        y_ref[:, 512:768] = (c["u"] * c["f"]).astype(BF16)
        y_ref[:, 768:1024] = (c["pout"] * ps_ref[...]).astype(BF16)

    prev = lambda n: jnp.maximum(n - 1, 0)
    return _call(
        body, name=name, grid=(nb,),
        in_specs=[pl.BlockSpec(memory_space=pltpu.SMEM),
                  pl.BlockSpec((BLK, INW), lambda n: (n, 0)),
                  pl.BlockSpec((BLK, 256), lambda n: (prev(n), 2)),
                  pl.BlockSpec((BLK, 256), lambda n: (prev(n), 5)),
                  _fixspec(1, 256), pl.BlockSpec((4, BLK, BLK), lambda n: (0, 0, 0)), _fixspec(BLK, 128),
                  pl.BlockSpec((4, HD, HD), lambda n: (0, 0, 0)), _fixspec(1, 256)],
        out_specs=[pl.BlockSpec((BLK, D), lambda n: (n, 0)), pl.BlockSpec((BLK, 128), lambda n: (n, 0))],
        out_shape=[jax.ShapeDtypeStruct((s, D), BF16), jax.ShapeDtypeStruct((s, 128), F32)],
        compiler_params=_params(("arbitrary",)),
    )(sinks, z, z, z, vnorm, ws, bias_t, pw, pscale)


def _mix_bwd(z, dy, lse, sinks, vnorm, ws, bias_t, pw, pscale, name):
    s = z.shape[0]
    nb = s // BLK

    def body(sink_ref, zc_ref, zkvp_ref, zpp_ref, dy_ref, lse_ref, vn_ref, ws_ref, bt_ref, pw_ref, ps_ref,
             dz_ref, dsink_ref, dvn_ref, dws_ref, dbt_ref, dpw_ref, dps_ref, carry_ref, ddc_ref):
        n = pl.program_id(0)

        @pl.when(n == 0)
        def _():
            carry_ref[...] = jnp.zeros_like(carry_ref)
            ddc_ref[...] = jnp.zeros_like(ddc_ref)
            dsink_ref[...] = jnp.zeros_like(dsink_ref)
            dvn_ref[...] = jnp.zeros_like(dvn_ref)
            dws_ref[...] = jnp.zeros_like(dws_ref)
            dbt_ref[...] = jnp.zeros_like(dbt_ref)
            dpw_ref[...] = jnp.zeros_like(dpw_ref)
            dps_ref[...] = jnp.zeros_like(dps_ref)

        def finish_prev(dd_cur):
            dz_ref[...] = carry_ref[...]
            rs = _window_sums(jnp.concatenate([ddc_ref[...], dd_cur], axis=0), False)
            dz_ref[:, O_P:INW] += rs[:BLK, :]

        @pl.when(n < nb)
        def _():
            valid = _attn_mask(n)
            dq = []
            dk_prev, dk_cur, dv_prev, dv_cur = [], [], [], []
            for g in range(N_KV):
                kk, vv = _kv_window(zc_ref, zkvp_ref, g)
                dkk = jnp.zeros((2 * BLK, HD), F32)
                dvv = jnp.zeros((2 * BLK, HD), F32)
                for rr in range(REP):
                    h = g * REP + rr
                    qh = zc_ref[:, h * HD:(h + 1) * HD].astype(BF16)
                    sc = jnp.where(valid, _nt(qh, kk) * ATTN_SCALE, NEG)
                    lse_h = lse_ref[:, h:h + 1]
                    pr = jnp.exp(sc - lse_h)
                    do = dy_ref[:, h * HD:(h + 1) * HD].astype(BF16)
                    dp = _nt(do, vv)
                    delta = jnp.sum(pr * dp, axis=-1, keepdims=True)
                    ds = ((pr * (dp - delta)) * ATTN_SCALE).astype(BF16)
                    psink = jnp.exp(sink_ref[h] - lse_h)
                    dsink_ref[h:h + 1, :] += (jnp.zeros((1, 128), F32)
                                              - jnp.sum(psink * delta, axis=0, keepdims=True))
                    dq.append(_nn(ds, kk))
                    dkk = dkk + _tn(ds, qh)
                    dvv = dvv + _tn(pr.astype(BF16), do)
                dk_prev.append(dkk[:BLK]); dk_cur.append(dkk[BLK:])
                dv_prev.append(dvv[:BLK]); dv_cur.append(dvv[BLK:])
            c = _mix_common(n, zc_ref, zkvp_ref, zpp_ref, vn_ref, ws_ref, bt_ref, pw_ref)
            dyg = dy_ref[:, 512:768]
            du = dyg * c["f"]
            df = dyg * c["u"]
            dzu = du * _gelu_grad(zc_ref[:, O_U:O_G], c["tu"])
            dvn_parts = []
            for h in range(4):
                dfh = df[:, h * HD:(h + 1) * HD]
                dfb = dfh.astype(BF16)
                dbt_ref[:, h:h + 1] += jnp.sum(dfh, axis=1, keepdims=True)
                dws_ref[h] += jnp.where(c["tril"], _nt(dfb, c["vnb"][:, h * HD:(h + 1) * HD]), 0.0)
                dvn_parts.append(_tn(c["wm"][h], dfb))
            dvn = jnp.concatenate(dvn_parts, axis=1)
            dgv, dvg = _rms_bwd(dvn, c["xh"], c["rv"], vn_ref[...])
            dvn_ref[0:1, :] += dvg
            dzv = dgv * _gelu_grad(zc_ref[:, O_G:O_P], c["tv"])
            dyp = dy_ref[:, 768:1024]
            dps_ref[0:1, :] += jnp.sum(dyp * c["pout"], axis=0, keepdims=True)
            dout = (dyp * ps_ref[...]).astype(BF16)
            ddiff_parts = []
            for g in range(4):
                dog = dout[:, g * HD:(g + 1) * HD]
                dpw_ref[g] += _tn(c["diff"][:, g * HD:(g + 1) * HD], dog)
                ddiff_parts.append(_nt(dog, c["pwb"][g]))
            ddiff = jnp.concatenate(ddiff_parts, axis=1)
            dd = ddiff / c["cnt"]
            finish_prev(dd)
            dz_ref[:, O_K:O_V] += jnp.concatenate(dk_prev, axis=1)
            dz_ref[:, O_V:O_U] += jnp.concatenate(dv_prev, axis=1)
            for h in range(N_HEADS):
                carry_ref[:, h * HD:(h + 1) * HD] = dq[h]
            carry_ref[:, O_K:O_V] = jnp.concatenate(dk_cur, axis=1)
            carry_ref[:, O_V:O_U] = jnp.concatenate(dv_cur, axis=1)
            carry_ref[:, O_U:O_G] = dzu
            carry_ref[:, O_G:O_P] = dzv
            carry_ref[:, O_P:INW] = -ddiff
            ddc_ref[...] = dd

        @pl.when(n == nb)
        def _():
            finish_prev(jnp.zeros((BLK, 256), F32))

    cur = lambda n: jnp.minimum(n, nb - 1)
    prev = lambda n: jnp.maximum(jnp.minimum(n, nb - 1) - 1, 0)
    fix3 = lambda a, b, c: pl.BlockSpec((a, b, c), lambda n: (0, 0, 0))
    return _call(
        body, name=name, grid=(nb + 1,),
        in_specs=[pl.BlockSpec(memory_space=pltpu.SMEM),
                  pl.BlockSpec((BLK, INW), lambda n: (cur(n), 0)),
                  pl.BlockSpec((BLK, 256), lambda n: (prev(n), 2)),
                  pl.BlockSpec((BLK, 256), lambda n: (prev(n), 5)),
                  pl.BlockSpec((BLK, D), lambda n: (cur(n), 0)),
                  pl.BlockSpec((BLK, 128), lambda n: (cur(n), 0)),
                  _fixspec(1, 256), fix3(4, BLK, BLK), _fixspec(BLK, 128), fix3(4, HD, HD), _fixspec(1, 256)],
        out_specs=[pl.BlockSpec((BLK, INW), lambda n: (jnp.maximum(n - 1, 0), 0)),
                   _fixspec(8, 128), _fixspec(8, 256), fix3(4, BLK, BLK), _fixspec(BLK, 128),
                   fix3(4, HD, HD), _fixspec(8, 256)],
        out_shape=[jax.ShapeDtypeStruct((s, INW), F32), jax.ShapeDtypeStruct((8, 128), F32),
                   jax.ShapeDtypeStruct((8, 256), F32), jax.ShapeDtypeStruct((4, BLK, BLK), F32),
                   jax.ShapeDtypeStruct((BLK, 128), F32), jax.ShapeDtypeStruct((4, HD, HD), F32),
                   jax.ShapeDtypeStruct((8, 256), F32)],
        scratch_shapes=[pltpu.VMEM((BLK, INW), F32), pltpu.VMEM((BLK, 256), F32)],
        compiler_params=_params(("arbitrary",)),
    )(sinks, z, z, z, dy, lse, vnorm, ws, bias_t, pw, pscale)


def _position():
    x, y, c = lax.axis_index("x"), lax.axis_index("y"), lax.axis_index("c")
    return x, y, c


def _all_gather(srcs, name):
    ng = len(srcs)

    def body(*refs):
        src, dst = refs[:ng], refs[ng:2 * ng]
        send_sems, recv_sems, local_sems = refs[2 * ng:]
        x, y, c = _position()
        me, sibling = (x, y, c), (x, y, 1 - c)
        chips = [(1 - x, y), (x, 1 - y), (1 - x, 1 - y)]

        def slot(pos):
            return 4 * pos[0] + 2 * pos[1] + pos[2]

        def copy(gi, k, block, to, from_src=False):
            rows = dst[gi].at[:, slot(block)]
            return pltpu.make_async_remote_copy(
                src_ref=src[gi] if from_src else rows, dst_ref=rows,
                send_sem=send_sems.at[gi, k], recv_sem=recv_sems.at[gi, k],
                device_id=to, device_id_type=MESH)

        mine = [pltpu.make_async_copy(src[gi], dst[gi].at[:, slot(me)], local_sems.at[gi]) for gi in range(ng)]
        for cp in mine:
            cp.start()
        first = []
        for gi in range(ng):
            first.append(copy(gi, 0, me, sibling, True))
            first += [copy(gi, 1 + j, me, (*chip, c), True) for j, chip in enumerate(chips)]
        for cp in first:
            cp.start()
        passed = []
        for j, chip in enumerate(chips):
            for gi in range(ng):
                copy(gi, 1 + j, (*chip, c), me).wait_recv()
                fw = copy(gi, 4 + j, (*chip, c), sibling)
                fw.start()
                passed.append(fw)
        for gi in range(ng):
            copy(gi, 0, sibling, me).wait_recv()
        for j, chip in enumerate(chips):
            for gi in range(ng):
                copy(gi, 4 + j, (*chip, 1 - c), me).wait_recv()
        for cp in first + passed:
            cp.wait_send()
        for cp in mine:
            cp.wait()

    any_spec = pl.BlockSpec(memory_space=pl.ANY)
    outs = _call(
        body, name=name,
        in_specs=[any_spec] * ng, out_specs=[any_spec] * ng,
        out_shape=[jax.ShapeDtypeStruct((a.shape[0], N_DEV) + a.shape[1:], a.dtype) for a in srcs],
        scratch_shapes=[pltpu.SemaphoreType.DMA((ng, 7)), pltpu.SemaphoreType.DMA((ng, 7)),
                        pltpu.SemaphoreType.DMA((ng,))],
    )(*srcs)
    return outs


def _sibling_exchange(g5s, name):
    ng = len(g5s)

    def body(*refs):
        src, dst = refs[:ng], refs[ng:2 * ng]
        send_sems, recv_sems = refs[2 * ng:]
        x, y, c = _position()
        cps = [pltpu.make_async_remote_copy(
            src_ref=src[gi].at[:, :, 1 - c], dst_ref=dst[gi],
            send_sem=send_sems.at[gi], recv_sem=recv_sems.at[gi],
            device_id=(x, y, 1 - c), device_id_type=MESH) for gi in range(ng)]
        for cp in cps:
            cp.start()
        for cp in cps:
            cp.wait()

    any_spec = pl.BlockSpec(memory_space=pl.ANY)
    return _call(
        body, name=name,
        in_specs=[any_spec] * ng, out_specs=[any_spec] * ng,
        out_shape=[jax.ShapeDtypeStruct((a.shape[0], 4) + a.shape[3:], a.dtype) for a in g5s],
        scratch_shapes=[pltpu.SemaphoreType.DMA((ng,)), pltpu.SemaphoreType.DMA((ng,))],
    )(*g5s)


def _chip_exchange(sbs, name):
    ng = len(sbs)

    def body(*refs):
        src, dst = refs[:ng], refs[ng:2 * ng]
        send_sems, recv_sems = refs[2 * ng:]
        x, y, c = _position()
        jme = 2 * x + y
        chips = [(1 - x, y), (x, 1 - y), (1 - x, 1 - y)]
        cps = []
        for k, chip in enumerate(chips):
            for gi in range(ng):
                cps.append(pltpu.make_async_remote_copy(
                    src_ref=src[gi].at[:, 2 * chip[0] + chip[1]], dst_ref=dst[gi].at[:, jme],
                    send_sem=send_sems.at[gi, k], recv_sem=recv_sems.at[gi, k],
                    device_id=(*chip, c), device_id_type=MESH))
        for cp in cps:
            cp.start()
        for cp in cps:
            cp.wait()

    any_spec = pl.BlockSpec(memory_space=pl.ANY)
    return _call(
        body, name=name,
        in_specs=[any_spec] * ng, out_specs=[any_spec] * ng,
        out_shape=[jax.ShapeDtypeStruct(a.shape, a.dtype) for a in sbs],
        scratch_shapes=[pltpu.SemaphoreType.DMA((ng, 3)), pltpu.SemaphoreType.DMA((ng, 3))],
    )(*sbs)


def _core_sum(ids, g5, r1, name):
    n, _, _, rows, _ = g5.shape

    def body(ids_ref, g_ref, r_ref, sb_ref, own_ref):
        j = pl.program_id(2)
        t = g_ref[...] + r_ref[...]
        sb_ref[...] = t.astype(BF16)

        @pl.when(j == ids_ref[1])
        def _():
            own_ref[...] = t

    grid_spec = pltpu.PrefetchScalarGridSpec(
        num_scalar_prefetch=1, grid=(n, 1, 4),
        in_specs=[pl.BlockSpec((None, None, None, rows, D), lambda i, t, j, ids: (i, j, ids[0], t, 0)),
                  pl.BlockSpec((None, None, rows, D), lambda i, t, j, ids: (i, j, t, 0))],
        out_specs=[pl.BlockSpec((None, None, rows, D), lambda i, t, j, ids: (i, j, t, 0)),
                   pl.BlockSpec((None, rows, D), lambda i, t, j, ids: (i, t, 0))])
    return _call(
        body, name=name, grid_spec=grid_spec,
        out_shape=[jax.ShapeDtypeStruct((n, 4, rows, D), BF16), jax.ShapeDtypeStruct((n, rows, D), F32)],
        compiler_params=_params(("arbitrary", "arbitrary", "arbitrary")),
    )(ids, g5, r1)


def _chip_sum(others, own, r2, name):
    n, rows, _ = own.shape

    def body(oth_ref, own_ref, r0_ref, r1_ref, r2_ref, out_ref):
        out_ref[...] = ((own_ref[...] + r0_ref[...].astype(F32)) + r1_ref[...].astype(F32)) \
            + r2_ref[...].astype(F32)

    def rspec(k):
        return pl.BlockSpec((None, None, rows, D), lambda i, oth, k=k: (i, oth[k], 0, 0))

    grid_spec = pltpu.PrefetchScalarGridSpec(
        num_scalar_prefetch=1, grid=(n,),
        in_specs=[pl.BlockSpec((None, rows, D), lambda i, oth: (i, 0, 0)), rspec(0), rspec(1), rspec(2)],
        out_specs=pl.BlockSpec((None, rows, D), lambda i, oth: (i, 0, 0)))
    return _call(
        body, name=name, grid_spec=grid_spec,
        out_shape=jax.ShapeDtypeStruct((n, rows, D), F32),
        compiler_params=_params(("arbitrary",)),
    )(others, own, r2, r2, r2)


def _adam_math(w, g, m, v):
    m = ADAM_B1 * m + (1.0 - ADAM_B1) * g
    v = ADAM_B2 * v + (1.0 - ADAM_B2) * (g * g)
    m_hat = m / (1.0 - ADAM_B1 ** ADAM_STEP)
    v_hat = v / (1.0 - ADAM_B2 ** ADAM_STEP)
    delta = -ADAM_LR * (m_hat / (jnp.sqrt(v_hat) + ADAM_EPS) + ADAM_WD * w)
    return delta, m, v


def _adamw(w, g, m, v, name):
    shape = w.shape
    c = shape[-1]
    r = w.size // c
    rb = max(d for d in range(8, min(r, 512) + 1, 8) if r % d == 0)

    def body(w_ref, g_ref, m_ref, v_ref, d_ref, mo_ref, vo_ref):
        d_ref[...], mo_ref[...], vo_ref[...] = _adam_math(w_ref[...], g_ref[...], m_ref[...], v_ref[...])

    spec = _rowspec(rb, c)
    outs = _call(
        body, name=name, grid=(r // rb,),
        in_specs=[spec] * 4, out_specs=[spec] * 3,
        out_shape=[jax.ShapeDtypeStruct((r, c), F32)] * 3,
        compiler_params=_params(("arbitrary",)),
    )(*[t.reshape(r, c) for t in (w, g, m, v)])
    return [o.reshape(shape) for o in outs]


def _adamw_small(parts, w, m, v, name):
    def body(p_ref, w_ref, m_ref, v_ref, g_ref, d_ref, mo_ref, vo_ref):
        g = p_ref[0]
        for dev in range(1, N_DEV):
            g = g + p_ref[dev]
        g_ref[...] = g
        d_ref[...], mo_ref[...], vo_ref[...] = _adam_math(w_ref[...], g, m_ref[...], v_ref[...])

    return _call(
        body, name=name,
        out_shape=[jax.ShapeDtypeStruct(w.shape, F32)] * 4,
        compiler_params=_params(),
    )(parts, w, m, v)


SMALL = ["ffn1_norm", "mix_norm", "attn_sinks", "gmlp_v_norm", "gmlp_w_s", "gmlp_b", "pool_w", "pool_scale",
         "ffn2_norm", "final_norm"]


def _pack_small(arrs, extra=None):
    pieces = []
    for a in arrs:
        f = a.reshape(-1)
        pieces.append(jnp.pad(f, (0, (-f.shape[0]) % 128)))
    if extra is not None:
        pieces.append(jnp.pad(extra.reshape(-1), (0, 127)))
    else:
        pieces.append(jnp.zeros((128,), F32))
    flat = jnp.concatenate(pieces)
    flat = jnp.pad(flat, (0, (-flat.shape[0]) % 1024))
    return flat.reshape(-1, 128)


def _unpack_small(packed, like):
    flat = packed.reshape(-1)
    out, off = [], 0
    for a in like:
        out.append(flat[off:off + a.size].reshape(a.shape))
        off += a.size + (-a.size) % 128
    return out, flat[off]


def kernel(x, ffn1_norm, ffn1_w_gate, ffn1_w_up, ffn1_w_down, mix_norm, w_in, attn_sinks, gmlp_v_norm, gmlp_w_s, gmlp_b, pool_w, pool_scale, w_out, ffn2_norm, ffn2_w_gate, ffn2_w_up, ffn2_w_down, final_norm, loss_target, m_ffn1_norm, m_ffn1_w_gate, m_ffn1_w_up, m_ffn1_w_down, m_mix_norm, m_w_in, m_attn_sinks, m_gmlp_v_norm, m_gmlp_w_s, m_gmlp_b, m_pool_w, m_pool_scale, m_w_out, m_ffn2_norm, m_ffn2_w_gate, m_ffn2_w_up, m_ffn2_w_down, m_final_norm, v_ffn1_norm, v_ffn1_w_gate, v_ffn1_w_up, v_ffn1_w_down, v_mix_norm, v_w_in, v_attn_sinks, v_gmlp_v_norm, v_gmlp_w_s, v_gmlp_b, v_pool_w, v_pool_scale, v_w_out, v_ffn2_norm, v_ffn2_w_gate, v_ffn2_w_up, v_ffn2_w_down, v_final_norm):
    s = x.shape[1]
    xi, yi, ci = _position()

    t = lambda a: jnp.swapaxes(a, -1, -2)
    loc352 = jnp.stack([mat for l in range(DEPTH) for mat in (
        t(ffn1_w_gate[l]), t(ffn1_w_up[l]), ffn1_w_down[l], t(ffn2_w_gate[l]), t(ffn2_w_up[l]), ffn2_w_down[l])])
    loc192 = t(w_in)
    loc128 = w_out
    w352, w192, w128 = _all_gather([loc352.astype(BF16), loc192.astype(BF16), loc128.astype(BF16)], "gather_weights")
    w352 = w352.reshape(6 * DEPTH, FF, D)
    w192 = w192.reshape(DEPTH, INW, D)
    w128 = w128.reshape(DEPTH, D, D)

    row = lambda a: a.reshape(1, -1)
    xc = x.reshape(s, D)
    saved = []
    for l in range(DEPTH):
        x0 = xc
        x1, a1, b1 = _ffn_fwd(x0, row(ffn1_norm[l]), w352, 6 * l, f"ffn1_fwd_{l}")
        z, hmix = _mixin_fwd(x1, row(mix_norm[l]), w192, l, f"mixin_fwd_{l}")
        bias_t = jnp.pad(t(gmlp_b[l]), ((0, 0), (0, 124)))
        mixp = (attn_sinks[l], row(gmlp_v_norm[l]), gmlp_w_s[l], bias_t, pool_w[l], row(pool_scale[l]))
        y, lse = _mix_fwd(z, *mixp, f"mix_fwd_{l}")
        x2 = _mixout_fwd(x1, y, w128, l, f"mixout_fwd_{l}")
        x3, a2, b2 = _ffn_fwd(x2, row(ffn2_norm[l]), w352, 6 * l + 3, f"ffn2_fwd_{l}")
        saved.append((x0, a1, b1, x1, z, hmix, mixp, y, lse, x2, a2, b2))
        xc = x3
    dx, loss_part, d_final = _loss_head(xc, row(final_norm), loss_target.reshape(s, D), "loss_head")

    g352 = g192 = g128 = None
    small = {}
    for l in reversed(range(DEPTH)):
        x0, a1, b1, x1, z, hmix, mixp, y, lse, x2, a2, b2 = saved[l]
        dx, da, db, hid, h, dyb, dg = _ffn_bwd(x2, row(ffn2_norm[l]), dx, a2, b2, w352, 6 * l + 3, f"ffn2_bwd_{l}")
        small[("ffn2_norm", l)] = dg[0]
        g352 = _wgrad(da, h, g352, 6 * DEPTH, 6 * l + 3, f"wgrad_gate2_{l}")
        g352 = _wgrad(db, h, g352, 6 * DEPTH, 6 * l + 4, f"wgrad_up2_{l}")
        g352 = _wgrad(hid, dyb, g352, 6 * DEPTH, 6 * l + 5, f"wgrad_down2_{l}")
        dymix, dxb = _mixout_bwd(dx, w128, l, f"mixout_bwd_{l}")
        g128 = _wgrad(y, dxb, g128, DEPTH, l, f"wgrad_out_{l}")
        dz, dsink, dvn, dws, dbt, dpw, dps = _mix_bwd(z, dymix, lse, *mixp, f"mix_bwd_{l}")
        small[("attn_sinks", l)] = dsink[:, 0]
        small[("gmlp_v_norm", l)] = dvn[0]
        small[("gmlp_w_s", l)] = dws
        small[("gmlp_b", l)] = t(dbt[:, :4])
        small[("pool_w", l)] = dpw
        small[("pool_scale", l)] = dps[0]
        dx, dzb, dg = _mixin_bwd(x1, row(mix_norm[l]), dz, dx, w192, l, f"mixin_bwd_{l}")
        small[("mix_norm", l)] = dg[0]
        g192 = _wgrad(dzb, hmix, g192, DEPTH, l, f"wgrad_in_{l}")
        dx, da, db, hid, h, dyb, dg = _ffn_bwd(x0, row(ffn1_norm[l]), dx, a1, b1, w352, 6 * l, f"ffn1_bwd_{l}")
        small[("ffn1_norm", l)] = dg[0]
        g352 = _wgrad(da, h, g352, 6 * DEPTH, 6 * l, f"wgrad_gate1_{l}")
        g352 = _wgrad(db, h, g352, 6 * DEPTH, 6 * l + 1, f"wgrad_up1_{l}")
        g352 = _wgrad(hid, dyb, g352, 6 * DEPTH, 6 * l + 2, f"wgrad_down1_{l}")
    grad_x = dx.reshape(1, s, D)

    g5s = [g352.reshape(6 * DEPTH, 4, 2, FF // N_DEV, D), g192.reshape(DEPTH, 4, 2, INW // N_DEV, D),
           g128.reshape(DEPTH, 4, 2, D // N_DEV, D)]
    r1s = _sibling_exchange(g5s, "reduce_sibling")
    ids = jnp.stack([ci, 2 * xi + yi]).astype(jnp.int32)
    sums = [_core_sum(ids, g5, r1, f"core_sum_{i}") for i, (g5, r1) in enumerate(zip(g5s, r1s))]
    r2s = _chip_exchange([sb for sb, _ in sums], "reduce_chips")
    jme = 2 * xi + yi
    others = jnp.stack([k + (k >= jme).astype(jnp.int32) for k in range(3)]).astype(jnp.int32)
    gr352, gr192, gr128 = [_chip_sum(others, own, r2, f"chip_sum_{i}")
                           for i, ((_, own), r2) in enumerate(zip(sums, r2s))]

    grads = {}
    for k, nm in enumerate(["ffn1_w_gate", "ffn1_w_up", "ffn1_w_down", "ffn2_w_gate", "ffn2_w_up", "ffn2_w_down"]):
        blk = jnp.stack([gr352[6 * l + k] for l in range(DEPTH)])
        grads[nm] = blk if nm.endswith("down") else t(blk)
    grads["w_in"] = t(gr192)
    grads["w_out"] = gr128

    small_w = dict(ffn1_norm=ffn1_norm, mix_norm=mix_norm, attn_sinks=attn_sinks, gmlp_v_norm=gmlp_v_norm,
                   gmlp_w_s=gmlp_w_s, gmlp_b=gmlp_b, pool_w=pool_w, pool_scale=pool_scale, ffn2_norm=ffn2_norm,
                   final_norm=final_norm)
    small_m = dict(ffn1_norm=m_ffn1_norm, mix_norm=m_mix_norm, attn_sinks=m_attn_sinks, gmlp_v_norm=m_gmlp_v_norm,
                   gmlp_w_s=m_gmlp_w_s, gmlp_b=m_gmlp_b, pool_w=m_pool_w, pool_scale=m_pool_scale,
                   ffn2_norm=m_ffn2_norm, final_norm=m_final_norm)
    small_v = dict(ffn1_norm=v_ffn1_norm, mix_norm=v_mix_norm, attn_sinks=v_attn_sinks, gmlp_v_norm=v_gmlp_v_norm,
                   gmlp_w_s=v_gmlp_w_s, gmlp_b=v_gmlp_b, pool_w=v_pool_w, pool_scale=v_pool_scale,
                   ffn2_norm=v_ffn2_norm, final_norm=v_final_norm)
    part = [d_final[0] if nm == "final_norm" else jnp.stack([small[(nm, l)] for l in range(DEPTH)]) for nm in SMALL]
    packed = _pack_small(part, loss_part[0, 0])
    (gathered,) = _all_gather([packed[None]], "gather_small")
    sg, sd, sm, sv = _adamw_small(gathered[0], _pack_small([small_w[nm] for nm in SMALL]),
                                  _pack_small([small_m[nm] for nm in SMALL]),
                                  _pack_small([small_v[nm] for nm in SMALL]), "adamw_small")
    like = [small_w[nm] for nm in SMALL]
    sg_l, loss = _unpack_small(sg, like)
    sd_l, _ = _unpack_small(sd, like)
    sm_l, _ = _unpack_small(sm, like)
    sv_l, _ = _unpack_small(sv, like)
    deltas, new_m, new_v = {}, {}, {}
    for i, nm in enumerate(SMALL):
        grads[nm], deltas[nm], new_m[nm], new_v[nm] = sg_l[i], sd_l[i], sm_l[i], sv_l[i]

    big_w = dict(ffn1_w_gate=ffn1_w_gate, ffn1_w_up=ffn1_w_up, ffn1_w_down=ffn1_w_down, w_in=w_in, w_out=w_out,
                 ffn2_w_gate=ffn2_w_gate, ffn2_w_up=ffn2_w_up, ffn2_w_down=ffn2_w_down)
    big_m = dict(ffn1_w_gate=m_ffn1_w_gate, ffn1_w_up=m_ffn1_w_up, ffn1_w_down=m_ffn1_w_down, w_in=m_w_in,
                 w_out=m_w_out, ffn2_w_gate=m_ffn2_w_gate, ffn2_w_up=m_ffn2_w_up, ffn2_w_down=m_ffn2_w_down)
    big_v = dict(ffn1_w_gate=v_ffn1_w_gate, ffn1_w_up=v_ffn1_w_up, ffn1_w_down=v_ffn1_w_down, w_in=v_w_in,
                 w_out=v_w_out, ffn2_w_gate=v_ffn2_w_gate, ffn2_w_up=v_ffn2_w_up, ffn2_w_down=v_ffn2_w_down)
    for nm in big_w:
        deltas[nm], new_m[nm], new_v[nm] = _adamw(big_w[nm], grads[nm], big_m[nm], big_v[nm], f"adamw_{nm}")

    order = ["ffn1_norm", "ffn1_w_gate", "ffn1_w_up", "ffn1_w_down", "mix_norm", "w_in", "attn_sinks", "gmlp_v_norm",
             "gmlp_w_s", "gmlp_b", "pool_w", "pool_scale", "w_out", "ffn2_norm", "ffn2_w_gate", "ffn2_w_up",
             "ffn2_w_down", "final_norm"]
    return (loss, grad_x, *[grads[n] for n in order], *[deltas[n] for n in order],
            *[new_m[n] for n in order], *[new_v[n] for n in order])
```

```python
import functools
import math

import jax
import jax.numpy as jnp
from jax import lax
from jax.experimental import pallas as pl
from jax.experimental.pallas import tpu as pltpu

F32 = jnp.float32
BF16 = jnp.bfloat16
MESH = pl.DeviceIdType.MESH

D = 1024
FF = 2816
INW = 1536
N_DEV = 8
DEPTH = 2
BLK = 128
HD = 64
N_HEADS = 8
N_KV = 2
REP = 4
ATTN_SCALE = HD ** -0.5
POOL_WINDOWS = (2, 4, 8, 16)
EPS = 1e-6
NEG = -1e30
FC = 256
GELU_C0 = math.sqrt(2.0 / math.pi)
GELU_C1 = 0.044715

ADAM_LR = 0.001
ADAM_B1 = 0.9
ADAM_B2 = 0.999
ADAM_EPS = 1e-08
ADAM_WD = 0.01
ADAM_STEP = 10

VMEM_LIMIT = 56 * 1024 * 1024

O_K, O_V, O_U, O_G, O_P = 512, 640, 768, 1024, 1280


def _call(body, **kw):
    return pl.pallas_call(body, **kw)


def _params(sem=None, vmem=VMEM_LIMIT):
    return pltpu.CompilerParams(dimension_semantics=sem, vmem_limit_bytes=vmem)


def _nn(a, b):
    return lax.dot_general(a, b, (((1,), (0,)), ((), ())), preferred_element_type=F32)


def _nt(a, b):
    return lax.dot_general(a, b, (((1,), (1,)), ((), ())), preferred_element_type=F32)


def _tn(a, b):
    return lax.dot_general(a, b, (((0,), (0,)), ((), ())), preferred_element_type=F32)


def _sigmoid(a):
    return 1.0 / (1.0 + jnp.exp(-a))


def _gelu(x):
    t = jnp.tanh(GELU_C0 * (x + GELU_C1 * x * x * x))
    return 0.5 * x * (1.0 + t), t


def _gelu_grad(x, t):
    return 0.5 * (1.0 + t) + 0.5 * x * (1.0 - t * t) * (GELU_C0 * (1.0 + 3.0 * GELU_C1 * x * x))


def _rms(x):
    r = lax.rsqrt(jnp.mean(x * x, axis=-1, keepdims=True) + EPS)
    return x * r, r


def _rms_bwd(dy, xh, r, g):
    dg = jnp.sum(dy * xh, axis=0, keepdims=True)
    dxh = dy * g
    dx = r * (dxh - xh * jnp.mean(dxh * xh, axis=-1, keepdims=True))
    return dx, dg


def _wspec(rows, m):
    return pl.BlockSpec((None, rows, D), lambda i, m=m: (m, 0, 0), pipeline_mode=pl.Buffered(1))


def _rowspec(tm, cols):
    return pl.BlockSpec((tm, cols), lambda i: (i, 0))


def _fixspec(rows, cols):
    return pl.BlockSpec((rows, cols), lambda i: (0, 0))


def _ffn_fwd(x, gain, w352, mg, name):
    s = x.shape[0]
    tm = min(512, s)

    def body(x_ref, g_ref, wg_ref, wu_ref, wd_ref, xo_ref, p1_ref, p2_ref, hid_ref):
        xt = x_ref[...]
        xh, _ = _rms(xt)
        h = (xh * g_ref[...]).astype(BF16)
        for c in range(FF // FC):
            sl = slice(c * FC, (c + 1) * FC)
            a = _nt(h, wg_ref[sl, :])
            b = _nt(h, wu_ref[sl, :])
            sig = 0.5 * jnp.tanh(0.5 * a) + 0.5
            sa = a * sig
            p1_ref[:, sl] = (b * (sig + sa * (1.0 - sig))).astype(BF16)
            p2_ref[:, sl] = sa.astype(BF16)
            hid_ref[:, sl] = (sa * b).astype(BF16)
        xo_ref[...] = xt + 0.5 * _nn(hid_ref[...], wd_ref[...])

    act = jax.ShapeDtypeStruct((s, FF), BF16)
    return _call(
        body, name=name, grid=(s // tm,),
        in_specs=[_rowspec(tm, D), _fixspec(1, D), _wspec(FF, mg), _wspec(FF, mg + 1), _wspec(FF, mg + 2)],
        out_specs=[_rowspec(tm, D), _rowspec(tm, FF), _rowspec(tm, FF), _rowspec(tm, FF)],
        out_shape=[jax.ShapeDtypeStruct((s, D), F32), act, act, act],
        compiler_params=_params(("arbitrary",)),
    )(x, gain, w352, w352, w352)


def _ffn_bwd(x, gain, dy, p1, p2, w352, mg, name):
    s = x.shape[0]
    tm = min(256, s)

    def body(x_ref, g_ref, dy_ref, p1_ref, p2_ref, wg_ref, wu_ref, wd_ref,
             dx_ref, da_ref, db_ref, h_ref, dyb_ref, dg_ref):
        i = pl.program_id(0)
        xt = x_ref[...]
        g = g_ref[...]
        xh, r = _rms(xt)
        h_ref[...] = (xh * g).astype(BF16)
        dyt = dy_ref[...]
        dyb = (0.5 * dyt).astype(BF16)
        dyb_ref[...] = dyb
        for c in range(FF // FC):
            sl = slice(c * FC, (c + 1) * FC)
            dhid = _nt(dyb, wd_ref[sl, :])
            da_ref[:, sl] = (dhid * p1_ref[:, sl].astype(F32)).astype(BF16)
            db_ref[:, sl] = (dhid * p2_ref[:, sl].astype(F32)).astype(BF16)
        dh = _nn(da_ref[...], wg_ref[...]) + _nn(db_ref[...], wu_ref[...])
        dxn, dg = _rms_bwd(dh, xh, r, g)
        dx_ref[...] = dyt + dxn

        @pl.when(i == 0)
        def _():
            dg_ref[...] = jnp.zeros_like(dg_ref)

        dg_ref[0:1, :] += dg

    act = jax.ShapeDtypeStruct((s, FF), BF16)
    tok = jax.ShapeDtypeStruct((s, D), BF16)
    return _call(
        body, name=name, grid=(s // tm,),
        in_specs=[_rowspec(tm, D), _fixspec(1, D), _rowspec(tm, D), _rowspec(tm, FF), _rowspec(tm, FF),
                  _wspec(FF, mg), _wspec(FF, mg + 1), _wspec(FF, mg + 2)],
        out_specs=[_rowspec(tm, D), _rowspec(tm, FF), _rowspec(tm, FF),
                   _rowspec(tm, D), _rowspec(tm, D), _fixspec(8, D)],
        out_shape=[jax.ShapeDtypeStruct((s, D), F32), act, act, tok, tok, jax.ShapeDtypeStruct((8, D), F32)],
        compiler_params=_params(("arbitrary",)),
    )(x, gain, dy, p1, p2, w352, w352, w352)


def _wgrad(a, b, g, n_slabs, m, name):
    s, mm = a.shape
    mb = 256

    def body(*refs):
        refs[-1][...] = _tn(refs[0][...], refs[1][...])

    in_specs = [pl.BlockSpec((s, mb), lambda i: (0, i)),
                pl.BlockSpec((s, D), lambda i: (0, 0), pipeline_mode=pl.Buffered(1))]
    args = [a, b]
    aliases = {}
    if g is not None:
        in_specs.append(pl.BlockSpec(memory_space=pl.ANY))
        args.append(g)
        aliases = {2: 0}
    return _call(
        body, name=name, grid=(mm // mb,),
        in_specs=in_specs,
        out_specs=pl.BlockSpec((None, mb, D), lambda i, m=m: (m, i, 0)),
        out_shape=jax.ShapeDtypeStruct((n_slabs, mm, D), F32),
        input_output_aliases=aliases,
        compiler_params=_params(("arbitrary",)),
    )(*args)


def _mixin_fwd(x, gain, w192, l, name):
    s = x.shape[0]
    tm = min(512, s)

    def body(x_ref, g_ref, w_ref, z_ref, h_ref):
        xh, _ = _rms(x_ref[...])
        h = (xh * g_ref[...]).astype(BF16)
        h_ref[...] = h
        z_ref[...] = _nt(h, w_ref[...])

    return _call(
        body, name=name, grid=(s // tm,),
        in_specs=[_rowspec(tm, D), _fixspec(1, D), _wspec(INW, l)],
        out_specs=[_rowspec(tm, INW), _rowspec(tm, D)],
        out_shape=[jax.ShapeDtypeStruct((s, INW), F32), jax.ShapeDtypeStruct((s, D), BF16)],
        compiler_params=_params(("arbitrary",)),
    )(x, gain, w192)


def _mixin_bwd(x, gain, dz, dx_in, w192, l, name):
    s = x.shape[0]
    tm = min(256, s)

    def body(x_ref, g_ref, dz_ref, dxi_ref, w_ref, dx_ref, dzb_ref, dg_ref):
        i = pl.program_id(0)
        g = g_ref[...]
        xh, r = _rms(x_ref[...])
        dzb = dz_ref[...].astype(BF16)
        dzb_ref[...] = dzb
        dh = _nn(dzb, w_ref[...])
        dxn, dg = _rms_bwd(dh, xh, r, g)
        dx_ref[...] = dxi_ref[...] + dxn

        @pl.when(i == 0)
        def _():
            dg_ref[...] = jnp.zeros_like(dg_ref)

        dg_ref[0:1, :] += dg

    return _call(
        body, name=name, grid=(s // tm,),
        in_specs=[_rowspec(tm, D), _fixspec(1, D), _rowspec(tm, INW), _rowspec(tm, D), _wspec(INW, l)],
        out_specs=[_rowspec(tm, D), _rowspec(tm, INW), _fixspec(8, D)],
        out_shape=[jax.ShapeDtypeStruct((s, D), F32), jax.ShapeDtypeStruct((s, INW), BF16),
                   jax.ShapeDtypeStruct((8, D), F32)],
        compiler_params=_params(("arbitrary",)),
    )(x, gain, dz, dx_in, w192)


def _mixout_fwd(x, y, w128, l, name):
    s = x.shape[0]
    tm = min(512, s)

    def body(x_ref, y_ref, w_ref, xo_ref):
        xo_ref[...] = x_ref[...] + _nn(y_ref[...], w_ref[...])

    return _call(
        body, name=name, grid=(s // tm,),
        in_specs=[_rowspec(tm, D), _rowspec(tm, D), _wspec(D, l)],
        out_specs=_rowspec(tm, D),
        out_shape=jax.ShapeDtypeStruct((s, D), F32),
        compiler_params=_params(("arbitrary",)),
    )(x, y, w128)


def _mixout_bwd(dx, w128, l, name):
    s = dx.shape[0]
    tm = min(512, s)

    def body(dx_ref, w_ref, dy_ref, dxb_ref):
        dxb = dx_ref[...].astype(BF16)
        dxb_ref[...] = dxb
        dy_ref[...] = _nt(dxb, w_ref[...])

    return _call(
        body, name=name, grid=(s // tm,),
        in_specs=[_rowspec(tm, D), _wspec(D, l)],
        out_specs=[_rowspec(tm, D), _rowspec(tm, D)],
        out_shape=[jax.ShapeDtypeStruct((s, D), F32), jax.ShapeDtypeStruct((s, D), BF16)],
        compiler_params=_params(("arbitrary",)),
    )(dx, w128)


def _loss_head(x, gain, tgt, name):
    s = x.shape[0]
    tm = min(512, s)

    def body(x_ref, g_ref, t_ref, dx_ref, loss_ref, dg_ref):
        i = pl.program_id(0)
        g = g_ref[...]
        xh, r = _rms(x_ref[...])
        err = xh * g - t_ref[...]
        tok = jnp.mean(err * err, axis=-1, keepdims=True)
        lp = 0.5 * jnp.sum(tok, axis=0, keepdims=True)
        dxn, dg = _rms_bwd(err * (1.0 / D), xh, r, g)
        dx_ref[...] = dxn

        @pl.when(i == 0)
        def _():
            dg_ref[...] = jnp.zeros_like(dg_ref)
            loss_ref[...] = jnp.zeros_like(loss_ref)

        dg_ref[0:1, :] += dg
        loss_ref[0:1, :] += lp + jnp.zeros((1, 128), F32)

    return _call(
        body, name=name, grid=(s // tm,),
        in_specs=[_rowspec(tm, D), _fixspec(1, D), _rowspec(tm, D)],
        out_specs=[_rowspec(tm, D), _fixspec(8, 128), _fixspec(8, D)],
        out_shape=[jax.ShapeDtypeStruct((s, D), F32), jax.ShapeDtypeStruct((8, 128), F32),
                   jax.ShapeDtypeStruct((8, D), F32)],
        compiler_params=_params(("arbitrary",)),
    )(x, gain, tgt)


def _attn_mask(n):
    row = lax.broadcasted_iota(jnp.int32, (BLK, 2 * BLK), 0)
    col = lax.broadcasted_iota(jnp.int32, (BLK, 2 * BLK), 1)
    lo = jnp.where(n > 0, 0, BLK)
    return (col > row) & (col <= row + BLK) & (col >= lo)


def _tril_mask():
    row = lax.broadcasted_iota(jnp.int32, (BLK, BLK), 0)
    col = lax.broadcasted_iota(jnp.int32, (BLK, BLK), 1)
    return col <= row


def _lane_group_select(vals):
    lane = lax.broadcasted_iota(jnp.int32, vals[0].shape, 1)
    return jnp.where(lane < 64, vals[0], jnp.where(lane < 128, vals[1], jnp.where(lane < 192, vals[2], vals[3])))


def _pool_count(n):
    row = lax.broadcasted_iota(jnp.int32, (BLK, 256), 0)
    pos1 = (n * BLK + row + 1).astype(F32)
    wl = _lane_group_select([jnp.full((BLK, 256), float(w), F32) for w in POOL_WINDOWS])
    return jnp.minimum(pos1, wl)


def _window_sums(e, forward):
    tot = e.shape[0]
    lv = e
    out = []
    for sh in (1, 2, 4, 8):
        lv = lv + pltpu.roll(lv, sh if forward else tot - sh, 0)
        out.append(lv)
    return _lane_group_select(out)


def _mix_common(n, zc_ref, zkvp_ref, zpp_ref, vn_ref, ws_ref, bt_ref, pw_ref):
    u, tu = _gelu(zc_ref[:, O_U:O_G])
    gv, tv = _gelu(zc_ref[:, O_G:O_P])
    xh, rv = _rms(gv)
    vnb = (xh * vn_ref[...]).astype(BF16)
    tril = _tril_mask()
    wm = [jnp.where(tril, ws_ref[h], 0.0).astype(BF16) for h in range(4)]
    f = jnp.concatenate(
        [_nn(wm[h], vnb[:, h * HD:(h + 1) * HD]) + bt_ref[:, h:h + 1] for h in range(4)], axis=1)
    p = zc_ref[:, O_P:INW]
    pm = jnp.where(n > 0, 1.0, 0.0)
    e = jnp.concatenate([zpp_ref[...] * pm, p], axis=0)
    cnt = _pool_count(n)
    diff = (_window_sums(e, True)[BLK:, :] / cnt - p).astype(BF16)
    pwb = [pw_ref[g].astype(BF16) for g in range(4)]
    pout = jnp.concatenate([_nn(diff[:, g * HD:(g + 1) * HD], pwb[g]) for g in range(4)], axis=1)
    return dict(u=u, tu=tu, tv=tv, xh=xh, rv=rv, vnb=vnb, wm=wm, f=f, cnt=cnt, diff=diff, pwb=pwb, pout=pout,
                tril=tril)


def _kv_window(zc_ref, zkvp_ref, g):
    kk = jnp.concatenate([zkvp_ref[:, g * HD:(g + 1) * HD],
                          zc_ref[:, O_K + g * HD:O_K + (g + 1) * HD]], axis=0).astype(BF16)
    vv = jnp.concatenate([zkvp_ref[:, BLK + g * HD:BLK + (g + 1) * HD],
                          zc_ref[:, O_V + g * HD:O_V + (g + 1) * HD]], axis=0).astype(BF16)
    return kk, vv


def _mix_fwd(z, sinks, vnorm, ws, bias_t, pw, pscale, name):
    s = z.shape[0]
    nb = s // BLK

    def body(sink_ref, zc_ref, zkvp_ref, zpp_ref, vn_ref, ws_ref, bt_ref, pw_ref, ps_ref, y_ref, lse_ref):
        n = pl.program_id(0)
        valid = _attn_mask(n)
        lse_ref[...] = jnp.zeros_like(lse_ref)
        for g in range(N_KV):
            kk, vv = _kv_window(zc_ref, zkvp_ref, g)
            for rr in range(REP):
                h = g * REP + rr
                qh = zc_ref[:, h * HD:(h + 1) * HD].astype(BF16)
                sc = jnp.where(valid, _nt(qh, kk) * ATTN_SCALE, NEG)
                sink = sink_ref[h]
                m = jnp.maximum(jnp.max(sc, axis=-1, keepdims=True), sink)
                ex = jnp.exp(sc - m)
                den = jnp.sum(ex, axis=-1, keepdims=True) + jnp.exp(sink - m)
                pr = (ex / den).astype(BF16)
                y_ref[:, h * HD:(h + 1) * HD] = _nn(pr, vv).astype(BF16)
                lse_ref[:, h:h + 1] = m + jnp.log(den)
        c = _mix_common(n, zc_ref, zkvp_ref, zpp_ref, vn_ref, ws_ref, bt_ref, pw_ref)
        y_ref[:, 512:768] = (c["u"] * c["f"]).astype(BF16)
        y_ref[:, 768:1024] = (c["pout"] * ps_ref[...]).astype(BF16)

    prev = lambda n: jnp.maximum(n - 1, 0)
    return _call(
        body, name=name, grid=(nb,),
        in_specs=[pl.BlockSpec(memory_space=pltpu.SMEM),
                  pl.BlockSpec((BLK, INW), lambda n: (n, 0)),
                  pl.BlockSpec((BLK, 256), lambda n: (prev(n), 2)),
                  pl.BlockSpec((BLK, 256), lambda n: (prev(n), 5)),
                  _fixspec(1, 256), pl.BlockSpec((4, BLK, BLK), lambda n: (0, 0, 0)), _fixspec(BLK, 128),
                  pl.BlockSpec((4, HD, HD), lambda n: (0, 0, 0)), _fixspec(1, 256)],
        out_specs=[pl.BlockSpec((BLK, D), lambda n: (n, 0)), pl.BlockSpec((BLK, 128), lambda n: (n, 0))],
        out_shape=[jax.ShapeDtypeStruct((s, D), BF16), jax.ShapeDtypeStruct((s, 128), F32)],
        compiler_params=_params(("arbitrary",)),
    )(sinks, z, z, z, vnorm, ws, bias_t, pw, pscale)


def _mix_bwd(z, dy, lse, sinks, vnorm, ws, bias_t, pw, pscale, name):
    s = z.shape[0]
    nb = s // BLK

    def body(sink_ref, zc_ref, zkvp_ref, zpp_ref, dy_ref, lse_ref, vn_ref, ws_ref, bt_ref, pw_ref, ps_ref,
             dz_ref, dsink_ref, dvn_ref, dws_ref, dbt_ref, dpw_ref, dps_ref, carry_ref, ddc_ref):
        n = pl.program_id(0)

        @pl.when(n == 0)
        def _():
            carry_ref[...] = jnp.zeros_like(carry_ref)
            ddc_ref[...] = jnp.zeros_like(ddc_ref)
            dsink_ref[...] = jnp.zeros_like(dsink_ref)
            dvn_ref[...] = jnp.zeros_like(dvn_ref)
            dws_ref[...] = jnp.zeros_like(dws_ref)
            dbt_ref[...] = jnp.zeros_like(dbt_ref)
            dpw_ref[...] = jnp.zeros_like(dpw_ref)
            dps_ref[...] = jnp.zeros_like(dps_ref)

        def finish_prev(dd_cur):
            dz_ref[...] = carry_ref[...]
            rs = _window_sums(jnp.concatenate([ddc_ref[...], dd_cur], axis=0), False)
            dz_ref[:, O_P:INW] += rs[:BLK, :]

        @pl.when(n < nb)
        def _():
            valid = _attn_mask(n)
            dq = []
            dk_prev, dk_cur, dv_prev, dv_cur = [], [], [], []
            for g in range(N_KV):
                kk, vv = _kv_window(zc_ref, zkvp_ref, g)
                dkk = jnp.zeros((2 * BLK, HD), F32)
                dvv = jnp.zeros((2 * BLK, HD), F32)
                for rr in range(REP):
                    h = g * REP + rr
                    qh = zc_ref[:, h * HD:(h + 1) * HD].astype(BF16)
                    sc = jnp.where(valid, _nt(qh, kk) * ATTN_SCALE, NEG)
                    lse_h = lse_ref[:, h:h + 1]
                    pr = jnp.exp(sc - lse_h)
                    do = dy_ref[:, h * HD:(h + 1) * HD].astype(BF16)
                    dp = _nt(do, vv)
                    delta = jnp.sum(pr * dp, axis=-1, keepdims=True)
                    ds = ((pr * (dp - delta)) * ATTN_SCALE).astype(BF16)
                    psink = jnp.exp(sink_ref[h] - lse_h)
                    dsink_ref[h:h + 1, :] += (jnp.zeros((1, 128), F32)
                                              - jnp.sum(psink * delta, axis=0, keepdims=True))
                    dq.append(_nn(ds, kk))
                    dkk = dkk + _tn(ds, qh)
                    dvv = dvv + _tn(pr.astype(BF16), do)
                dk_prev.append(dkk[:BLK]); dk_cur.append(dkk[BLK:])
                dv_prev.append(dvv[:BLK]); dv_cur.append(dvv[BLK:])
            c = _mix_common(n, zc_ref, zkvp_ref, zpp_ref, vn_ref, ws_ref, bt_ref, pw_ref)
            dyg = dy_ref[:, 512:768]
            du = dyg * c["f"]
            df = dyg * c["u"]
            dzu = du * _gelu_grad(zc_ref[:, O_U:O_G], c["tu"])
            dvn_parts = []
            for h in range(4):
                dfh = df[:, h * HD:(h + 1) * HD]
                dfb = dfh.astype(BF16)
                dbt_ref[:, h:h + 1] += jnp.sum(dfh, axis=1, keepdims=True)
                dws_ref[h] += jnp.where(c["tril"], _nt(dfb, c["vnb"][:, h * HD:(h + 1) * HD]), 0.0)
                dvn_parts.append(_tn(c["wm"][h], dfb))
            dvn = jnp.concatenate(dvn_parts, axis=1)
            dgv, dvg = _rms_bwd(dvn, c["xh"], c["rv"], vn_ref[...])
            dvn_ref[0:1, :] += dvg
            dzv = dgv * _gelu_grad(zc_ref[:, O_G:O_P], c["tv"])
            dyp = dy_ref[:, 768:1024]
            dps_ref[0:1, :] += jnp.sum(dyp * c["pout"], axis=0, keepdims=True)
            dout = (dyp * ps_ref[...]).astype(BF16)
            ddiff_parts = []
            for g in range(4):
                dog = dout[:, g * HD:(g + 1) * HD]
                dpw_ref[g] += _tn(c["diff"][:, g * HD:(g + 1) * HD], dog)
                ddiff_parts.append(_nt(dog, c["pwb"][g]))
            ddiff = jnp.concatenate(ddiff_parts, axis=1)
            dd = ddiff / c["cnt"]
            finish_prev(dd)
            dz_ref[:, O_K:O_V] += jnp.concatenate(dk_prev, axis=1)
            dz_ref[:, O_V:O_U] += jnp.concatenate(dv_prev, axis=1)
            for h in range(N_HEADS):
                carry_ref[:, h * HD:(h + 1) * HD] = dq[h]
            carry_ref[:, O_K:O_V] = jnp.concatenate(dk_cur, axis=1)
            carry_ref[:, O_V:O_U] = jnp.concatenate(dv_cur, axis=1)
            carry_ref[:, O_U:O_G] = dzu
            carry_ref[:, O_G:O_P] = dzv
            carry_ref[:, O_P:INW] = -ddiff
            ddc_ref[...] = dd

        @pl.when(n == nb)
        def _():
            finish_prev(jnp.zeros((BLK, 256), F32))

    cur = lambda n: jnp.minimum(n, nb - 1)
    prev = lambda n: jnp.maximum(jnp.minimum(n, nb - 1) - 1, 0)
    fix3 = lambda a, b, c: pl.BlockSpec((a, b, c), lambda n: (0, 0, 0))
    return _call(
        body, name=name, grid=(nb + 1,),
        in_specs=[pl.BlockSpec(memory_space=pltpu.SMEM),
                  pl.BlockSpec((BLK, INW), lambda n: (cur(n), 0)),
                  pl.BlockSpec((BLK, 256), lambda n: (prev(n), 2)),
                  pl.BlockSpec((BLK, 256), lambda n: (prev(n), 5)),
                  pl.BlockSpec((BLK, D), lambda n: (cur(n), 0)),
                  pl.BlockSpec((BLK, 128), lambda n: (cur(n), 0)),
                  _fixspec(1, 256), fix3(4, BLK, BLK), _fixspec(BLK, 128), fix3(4, HD, HD), _fixspec(1, 256)],
        out_specs=[pl.BlockSpec((BLK, INW), lambda n: (jnp.maximum(n - 1, 0), 0)),
                   _fixspec(8, 128), _fixspec(8, 256), fix3(4, BLK, BLK), _fixspec(BLK, 128),
                   fix3(4, HD, HD), _fixspec(8, 256)],
        out_shape=[jax.ShapeDtypeStruct((s, INW), F32), jax.ShapeDtypeStruct((8, 128), F32),
                   jax.ShapeDtypeStruct((8, 256), F32), jax.ShapeDtypeStruct((4, BLK, BLK), F32),
                   jax.ShapeDtypeStruct((BLK, 128), F32), jax.ShapeDtypeStruct((4, HD, HD), F32),
                   jax.ShapeDtypeStruct((8, 256), F32)],
        scratch_shapes=[pltpu.VMEM((BLK, INW), F32), pltpu.VMEM((BLK, 256), F32)],
        compiler_params=_params(("arbitrary",)),
    )(sinks, z, z, z, dy, lse, vnorm, ws, bias_t, pw, pscale)


def _position():
    x, y, c = lax.axis_index("x"), lax.axis_index("y"), lax.axis_index("c")
    return x, y, c


def _all_gather(srcs, name):
    ng = len(srcs)

    def body(*refs):
        src, dst = refs[:ng], refs[ng:2 * ng]
        send_sems, recv_sems, local_sems = refs[2 * ng:]
        x, y, c = _position()
        me, sibling = (x, y, c), (x, y, 1 - c)
        chips = [(1 - x, y), (x, 1 - y), (1 - x, 1 - y)]

        def slot(pos):
            return 4 * pos[0] + 2 * pos[1] + pos[2]

        def copy(gi, k, block, to, from_src=False):
            rows = dst[gi].at[:, slot(block)]
            return pltpu.make_async_remote_copy(
                src_ref=src[gi] if from_src else rows, dst_ref=rows,
                send_sem=send_sems.at[gi, k], recv_sem=recv_sems.at[gi, k],
                device_id=to, device_id_type=MESH)

        mine = [pltpu.make_async_copy(src[gi], dst[gi].at[:, slot(me)], local_sems.at[gi]) for gi in range(ng)]
        for cp in mine:
            cp.start()
        first = []
        for gi in range(ng):
            first.append(copy(gi, 0, me, sibling, True))
            first += [copy(gi, 1 + j, me, (*chip, c), True) for j, chip in enumerate(chips)]
        for cp in first:
            cp.start()
        passed = []
        for j, chip in enumerate(chips):
            for gi in range(ng):
                copy(gi, 1 + j, (*chip, c), me).wait_recv()
                fw = copy(gi, 4 + j, (*chip, c), sibling)
                fw.start()
                passed.append(fw)
        for gi in range(ng):
            copy(gi, 0, sibling, me).wait_recv()
        for j, chip in enumerate(chips):
            for gi in range(ng):
                copy(gi, 4 + j, (*chip, 1 - c), me).wait_recv()
        for cp in first + passed:
            cp.wait_send()
        for cp in mine:
            cp.wait()

    any_spec = pl.BlockSpec(memory_space=pl.ANY)
    outs = _call(
        body, name=name,
        in_specs=[any_spec] * ng, out_specs=[any_spec] * ng,
        out_shape=[jax.ShapeDtypeStruct((a.shape[0], N_DEV) + a.shape[1:], a.dtype) for a in srcs],
        scratch_shapes=[pltpu.SemaphoreType.DMA((ng, 7)), pltpu.SemaphoreType.DMA((ng, 7)),
                        pltpu.SemaphoreType.DMA((ng,))],
    )(*srcs)
    return outs


def _sibling_exchange(g5s, name):
    ng = len(g5s)

    def body(*refs):
        src, dst = refs[:ng], refs[ng:2 * ng]
        send_sems, recv_sems = refs[2 * ng:]
        x, y, c = _position()
        cps = [pltpu.make_async_remote_copy(
            src_ref=src[gi].at[:, :, 1 - c], dst_ref=dst[gi],
            send_sem=send_sems.at[gi], recv_sem=recv_sems.at[gi],
            device_id=(x, y, 1 - c), device_id_type=MESH) for gi in range(ng)]
        for cp in cps:
            cp.start()
        for cp in cps:
            cp.wait()

    any_spec = pl.BlockSpec(memory_space=pl.ANY)
    return _call(
        body, name=name,
        in_specs=[any_spec] * ng, out_specs=[any_spec] * ng,
        out_shape=[jax.ShapeDtypeStruct((a.shape[0], 4) + a.shape[3:], a.dtype) for a in g5s],
        scratch_shapes=[pltpu.SemaphoreType.DMA((ng,)), pltpu.SemaphoreType.DMA((ng,))],
    )(*g5s)


def _chip_exchange(sbs, name):
    ng = len(sbs)

    def body(*refs):
        src, dst = refs[:ng], refs[ng:2 * ng]
        send_sems, recv_sems = refs[2 * ng:]
        x, y, c = _position()
        jme = 2 * x + y
        chips = [(1 - x, y), (x, 1 - y), (1 - x, 1 - y)]
        cps = []
        for k, chip in enumerate(chips):
            for gi in range(ng):
                cps.append(pltpu.make_async_remote_copy(
                    src_ref=src[gi].at[:, 2 * chip[0] + chip[1]], dst_ref=dst[gi].at[:, jme],
                    send_sem=send_sems.at[gi, k], recv_sem=recv_sems.at[gi, k],
                    device_id=(*chip, c), device_id_type=MESH))
        for cp in cps:
            cp.start()
        for cp in cps:
            cp.wait()

    any_spec = pl.BlockSpec(memory_space=pl.ANY)
    return _call(
        body, name=name,
        in_specs=[any_spec] * ng, out_specs=[any_spec] * ng,
        out_shape=[jax.ShapeDtypeStruct(a.shape, a.dtype) for a in sbs],
        scratch_shapes=[pltpu.SemaphoreType.DMA((ng, 3)), pltpu.SemaphoreType.DMA((ng, 3))],
    )(*sbs)


def _core_sum(ids, g5, r1, name):
    n, _, _, rows, _ = g5.shape

    def body(ids_ref, g_ref, r_ref, sb_ref, own_ref):
        j = pl.program_id(2)
        t = g_ref[...] + r_ref[...]
        sb_ref[...] = t.astype(BF16)

        @pl.when(j == ids_ref[1])
        def _():
            own_ref[...] = t

    grid_spec = pltpu.PrefetchScalarGridSpec(
        num_scalar_prefetch=1, grid=(n, 1, 4),
        in_specs=[pl.BlockSpec((None, None, None, rows, D), lambda i, t, j, ids: (i, j, ids[0], t, 0)),
                  pl.BlockSpec((None, None, rows, D), lambda i, t, j, ids: (i, j, t, 0))],
        out_specs=[pl.BlockSpec((None, None, rows, D), lambda i, t, j, ids: (i, j, t, 0)),
                   pl.BlockSpec((None, rows, D), lambda i, t, j, ids: (i, t, 0))])
    return _call(
        body, name=name, grid_spec=grid_spec,
        out_shape=[jax.ShapeDtypeStruct((n, 4, rows, D), BF16), jax.ShapeDtypeStruct((n, rows, D), F32)],
        compiler_params=_params(("arbitrary", "arbitrary", "arbitrary")),
    )(ids, g5, r1)


def _chip_sum(others, own, r2, name):
    n, rows, _ = own.shape

    def body(oth_ref, own_ref, r0_ref, r1_ref, r2_ref, out_ref):
        out_ref[...] = ((own_ref[...] + r0_ref[...].astype(F32)) + r1_ref[...].astype(F32)) \
            + r2_ref[...].astype(F32)

    def rspec(k):
        return pl.BlockSpec((None, None, rows, D), lambda i, oth, k=k: (i, oth[k], 0, 0))

    grid_spec = pltpu.PrefetchScalarGridSpec(
        num_scalar_prefetch=1, grid=(n,),
        in_specs=[pl.BlockSpec((None, rows, D), lambda i, oth: (i, 0, 0)), rspec(0), rspec(1), rspec(2)],
        out_specs=pl.BlockSpec((None, rows, D), lambda i, oth: (i, 0, 0)))
    return _call(
        body, name=name, grid_spec=grid_spec,
        out_shape=jax.ShapeDtypeStruct((n, rows, D), F32),
        compiler_params=_params(("arbitrary",)),
    )(others, own, r2, r2, r2)


def _adam_math(w, g, m, v):
    m = ADAM_B1 * m + (1.0 - ADAM_B1) * g
    v = ADAM_B2 * v + (1.0 - ADAM_B2) * (g * g)
    m_hat = m / (1.0 - ADAM_B1 ** ADAM_STEP)
    v_hat = v / (1.0 - ADAM_B2 ** ADAM_STEP)
    delta = -ADAM_LR * (m_hat / (jnp.sqrt(v_hat) + ADAM_EPS) + ADAM_WD * w)
    return delta, m, v


def _adamw(w, g, m, v, name):
    shape = w.shape
    c = shape[-1]
    r = w.size // c
    rb = max(d for d in range(8, min(r, 512) + 1, 8) if r % d == 0)

    def body(w_ref, g_ref, m_ref, v_ref, d_ref, mo_ref, vo_ref):
        d_ref[...], mo_ref[...], vo_ref[...] = _adam_math(w_ref[...], g_ref[...], m_ref[...], v_ref[...])

    spec = _rowspec(rb, c)
    outs = _call(
        body, name=name, grid=(r // rb,),
        in_specs=[spec] * 4, out_specs=[spec] * 3,
        out_shape=[jax.ShapeDtypeStruct((r, c), F32)] * 3,
        compiler_params=_params(("arbitrary",)),
    )(*[t.reshape(r, c) for t in (w, g, m, v)])
    return [o.reshape(shape) for o in outs]


def _adamw_small(parts, w, m, v, name):
    def body(p_ref, w_ref, m_ref, v_ref, g_ref, d_ref, mo_ref, vo_ref):
        g = p_ref[0]
        for dev in range(1, N_DEV):
            g = g + p_ref[dev]
        g_ref[...] = g
        d_ref[...], mo_ref[...], vo_ref[...] = _adam_math(w_ref[...], g, m_ref[...], v_ref[...])

    return _call(
        body, name=name,
        out_shape=[jax.ShapeDtypeStruct(w.shape, F32)] * 4,
        compiler_params=_params(),
    )(parts, w, m, v)


SMALL = ["ffn1_norm", "mix_norm", "attn_sinks", "gmlp_v_norm", "gmlp_w_s", "gmlp_b", "pool_w", "pool_scale",
         "ffn2_norm", "final_norm"]


def _pack_small(arrs, extra=None):
    pieces = []
    for a in arrs:
        f = a.reshape(-1)
        pieces.append(jnp.pad(f, (0, (-f.shape[0]) % 128)))
    if extra is not None:
        pieces.append(jnp.pad(extra.reshape(-1), (0, 127)))
    else:
        pieces.append(jnp.zeros((128,), F32))
    flat = jnp.concatenate(pieces)
    flat = jnp.pad(flat, (0, (-flat.shape[0]) % 1024))
    return flat.reshape(-1, 128)


def _unpack_small(packed, like):
    flat = packed.reshape(-1)
    out, off = [], 0
    for a in like:
        out.append(flat[off:off + a.size].reshape(a.shape))
        off += a.size + (-a.size) % 128
    return out, flat[off]


def kernel(x, ffn1_norm, ffn1_w_gate, ffn1_w_up, ffn1_w_down, mix_norm, w_in, attn_sinks, gmlp_v_norm, gmlp_w_s, gmlp_b, pool_w, pool_scale, w_out, ffn2_norm, ffn2_w_gate, ffn2_w_up, ffn2_w_down, final_norm, loss_target, m_ffn1_norm, m_ffn1_w_gate, m_ffn1_w_up, m_ffn1_w_down, m_mix_norm, m_w_in, m_attn_sinks, m_gmlp_v_norm, m_gmlp_w_s, m_gmlp_b, m_pool_w, m_pool_scale, m_w_out, m_ffn2_norm, m_ffn2_w_gate, m_ffn2_w_up, m_ffn2_w_down, m_final_norm, v_ffn1_norm, v_ffn1_w_gate, v_ffn1_w_up, v_ffn1_w_down, v_mix_norm, v_w_in, v_attn_sinks, v_gmlp_v_norm, v_gmlp_w_s, v_gmlp_b, v_pool_w, v_pool_scale, v_w_out, v_ffn2_norm, v_ffn2_w_gate, v_ffn2_w_up, v_ffn2_w_down, v_final_norm):
    s = x.shape[1]
    xi, yi, ci = _position()

    t = lambda a: jnp.swapaxes(a, -1, -2)
    loc352 = jnp.stack([mat for l in range(DEPTH) for mat in (
        t(ffn1_w_gate[l]), t(ffn1_w_up[l]), ffn1_w_down[l], t(ffn2_w_gate[l]), t(ffn2_w_up[l]), ffn2_w_down[l])])
    loc192 = t(w_in)
    loc128 = w_out
    w352, w192, w128 = _all_gather([loc352.astype(BF16), loc192.astype(BF16), loc128.astype(BF16)], "gather_weights")
    w352 = w352.reshape(6 * DEPTH, FF, D)
    w192 = w192.reshape(DEPTH, INW, D)
    w128 = w128.reshape(DEPTH, D, D)

    row = lambda a: a.reshape(1, -1)
    xc = x.reshape(s, D)
    saved = []
    for l in range(DEPTH):
        x0 = xc
        x1, *act1 = _ffn_fwd(x0, row(ffn1_norm[l]), w352, 6 * l, f"ffn1_fwd_{l}")
        z, hmix = _mixin_fwd(x1, row(mix_norm[l]), w192, l, f"mixin_fwd_{l}")
        bias_t = jnp.pad(t(gmlp_b[l]), ((0, 0), (0, 124)))
        mixp = (attn_sinks[l], row(gmlp_v_norm[l]), gmlp_w_s[l], bias_t, pool_w[l], row(pool_scale[l]))
        y, lse = _mix_fwd(z, *mixp, f"mix_fwd_{l}")
        x2 = _mixout_fwd(x1, y, w128, l, f"mixout_fwd_{l}")
        x3, *act2 = _ffn_fwd(x2, row(ffn2_norm[l]), w352, 6 * l + 3, f"ffn2_fwd_{l}")
        saved.append((x0, act1, x1, z, hmix, mixp, y, lse, x2, act2))
        xc = x3
    dx, loss_part, d_final = _loss_head(xc, row(final_norm), loss_target.reshape(s, D), "loss_head")

    g352 = g192 = g128 = None
    small = {}
    for l in reversed(range(DEPTH)):
        x0, (p11, p21, hid1), x1, z, hmix, mixp, y, lse, x2, (p12, p22, hid2) = saved[l]
        dx, da, db, h, dyb, dg = _ffn_bwd(x2, row(ffn2_norm[l]), dx, p12, p22, w352, 6 * l + 3, f"ffn2_bwd_{l}")
        small[("ffn2_norm", l)] = dg[0]
        g352 = _wgrad(da, h, g352, 6 * DEPTH, 6 * l + 3, f"wgrad_gate2_{l}")
        g352 = _wgrad(db, h, g352, 6 * DEPTH, 6 * l + 4, f"wgrad_up2_{l}")
        g352 = _wgrad(hid2, dyb, g352, 6 * DEPTH, 6 * l + 5, f"wgrad_down2_{l}")
        dymix, dxb = _mixout_bwd(dx, w128, l, f"mixout_bwd_{l}")
        g128 = _wgrad(y, dxb, g128, DEPTH, l, f"wgrad_out_{l}")
        dz, dsink, dvn, dws, dbt, dpw, dps = _mix_bwd(z, dymix, lse, *mixp, f"mix_bwd_{l}")
        small[("attn_sinks", l)] = dsink[:, 0]
        small[("gmlp_v_norm", l)] = dvn[0]
        small[("gmlp_w_s", l)] = dws
        small[("gmlp_b", l)] = t(dbt[:, :4])
        small[("pool_w", l)] = dpw
        small[("pool_scale", l)] = dps[0]
        dx, dzb, dg = _mixin_bwd(x1, row(mix_norm[l]), dz, dx, w192, l, f"mixin_bwd_{l}")
        small[("mix_norm", l)] = dg[0]
        g192 = _wgrad(dzb, hmix, g192, DEPTH, l, f"wgrad_in_{l}")
        dx, da, db, h, dyb, dg = _ffn_bwd(x0, row(ffn1_norm[l]), dx, p11, p21, w352, 6 * l, f"ffn1_bwd_{l}")
        small[("ffn1_norm", l)] = dg[0]
        g352 = _wgrad(da, h, g352, 6 * DEPTH, 6 * l, f"wgrad_gate1_{l}")
        g352 = _wgrad(db, h, g352, 6 * DEPTH, 6 * l + 1, f"wgrad_up1_{l}")
        g352 = _wgrad(hid1, dyb, g352, 6 * DEPTH, 6 * l + 2, f"wgrad_down1_{l}")
    grad_x = dx.reshape(1, s, D)

    g5s = [g352.reshape(6 * DEPTH, 4, 2, FF // N_DEV, D), g192.reshape(DEPTH, 4, 2, INW // N_DEV, D),
           g128.reshape(DEPTH, 4, 2, D // N_DEV, D)]
    r1s = _sibling_exchange(g5s, "reduce_sibling")
    ids = jnp.stack([ci, 2 * xi + yi]).astype(jnp.int32)
    sums = [_core_sum(ids, g5, r1, f"core_sum_{i}") for i, (g5, r1) in enumerate(zip(g5s, r1s))]
    r2s = _chip_exchange([sb for sb, _ in sums], "reduce_chips")
    jme = 2 * xi + yi
    others = jnp.stack([k + (k >= jme).astype(jnp.int32) for k in range(3)]).astype(jnp.int32)
    gr352, gr192, gr128 = [_chip_sum(others, own, r2, f"chip_sum_{i}")
                           for i, ((_, own), r2) in enumerate(zip(sums, r2s))]

    grads = {}
    for k, nm in enumerate(["ffn1_w_gate", "ffn1_w_up", "ffn1_w_down", "ffn2_w_gate", "ffn2_w_up", "ffn2_w_down"]):
        blk = jnp.stack([gr352[6 * l + k] for l in range(DEPTH)])
        grads[nm] = blk if nm.endswith("down") else t(blk)
    grads["w_in"] = t(gr192)
    grads["w_out"] = gr128

    small_w = dict(ffn1_norm=ffn1_norm, mix_norm=mix_norm, attn_sinks=attn_sinks, gmlp_v_norm=gmlp_v_norm,
                   gmlp_w_s=gmlp_w_s, gmlp_b=gmlp_b, pool_w=pool_w, pool_scale=pool_scale, ffn2_norm=ffn2_norm,
                   final_norm=final_norm)
    small_m = dict(ffn1_norm=m_ffn1_norm, mix_norm=m_mix_norm, attn_sinks=m_attn_sinks, gmlp_v_norm=m_gmlp_v_norm,
                   gmlp_w_s=m_gmlp_w_s, gmlp_b=m_gmlp_b, pool_w=m_pool_w, pool_scale=m_pool_scale,
                   ffn2_norm=m_ffn2_norm, final_norm=m_final_norm)
    small_v = dict(ffn1_norm=v_ffn1_norm, mix_norm=v_mix_norm, attn_sinks=v_attn_sinks, gmlp_v_norm=v_gmlp_v_norm,
                   gmlp_w_s=v_gmlp_w_s, gmlp_b=v_gmlp_b, pool_w=v_pool_w, pool_scale=v_pool_scale,
                   ffn2_norm=v_ffn2_norm, final_norm=v_final_norm)
    part = [d_final[0] if nm == "final_norm" else jnp.stack([small[(nm, l)] for l in range(DEPTH)]) for nm in SMALL]
    packed = _pack_small(part, loss_part[0, 0])
    (gathered,) = _all_gather([packed[None]], "gather_small")
    sg, sd, sm, sv = _adamw_small(gathered[0], _pack_small([small_w[nm] for nm in SMALL]),
                                  _pack_small([small_m[nm] for nm in SMALL]),
                                  _pack_small([small_v[nm] for nm in SMALL]), "adamw_small")
    like = [small_w[nm] for nm in SMALL]
    sg_l, loss = _unpack_small(sg, like)
    sd_l, _ = _unpack_small(sd, like)
    sm_l, _ = _unpack_small(sm, like)
    sv_l, _ = _unpack_small(sv, like)
    deltas, new_m, new_v = {}, {}, {}
    for i, nm in enumerate(SMALL):
        grads[nm], deltas[nm], new_m[nm], new_v[nm] = sg_l[i], sd_l[i], sm_l[i], sv_l[i]

    big_w = dict(ffn1_w_gate=ffn1_w_gate, ffn1_w_up=ffn1_w_up, ffn1_w_down=ffn1_w_down, w_in=w_in, w_out=w_out,
                 ffn2_w_gate=ffn2_w_gate, ffn2_w_up=ffn2_w_up, ffn2_w_down=ffn2_w_down)
    big_m = dict(ffn1_w_gate=m_ffn1_w_gate, ffn1_w_up=m_ffn1_w_up, ffn1_w_down=m_ffn1_w_down, w_in=m_w_in,
                 w_out=m_w_out, ffn2_w_gate=m_ffn2_w_gate, ffn2_w_up=m_ffn2_w_up, ffn2_w_down=m_ffn2_w_down)
    big_v = dict(ffn1_w_gate=v_ffn1_w_gate, ffn1_w_up=v_ffn1_w_up, ffn1_w_down=v_ffn1_w_down, w_in=v_w_in,
                 w_out=v_w_out, ffn2_w_gate=v_ffn2_w_gate, ffn2_w_up=v_ffn2_w_up, ffn2_w_down=v_ffn2_w_down)
    for nm in big_w:
        deltas[nm], new_m[nm], new_v[nm] = _adamw(big_w[nm], grads[nm], big_m[nm], big_v[nm], f"adamw_{nm}")

    order = ["ffn1_norm", "ffn1_w_gate", "ffn1_w_up", "ffn1_w_down", "mix_norm", "w_in", "attn_sinks", "gmlp_v_norm",
             "gmlp_w_s", "gmlp_b", "pool_w", "pool_scale", "w_out", "ffn2_norm", "ffn2_w_gate", "ffn2_w_up",
             "ffn2_w_down", "final_norm"]
    return (loss, grad_x, *[grads[n] for n in order], *[deltas[n] for n in order],
            *[new_m[n] for n in order], *[new_v[n] for n in order])
```

```python
import functools
import math

import jax
import jax.numpy as jnp
from jax import lax
from jax.experimental import pallas as pl
from jax.experimental.pallas import tpu as pltpu

F32 = jnp.float32
BF16 = jnp.bfloat16
MESH = pl.DeviceIdType.MESH

D = 1024
FF = 2816
INW = 1536
N_DEV = 8
DEPTH = 2
BLK = 128
HD = 64
N_HEADS = 8
N_KV = 2
REP = 4
ATTN_SCALE = HD ** -0.5
POOL_WINDOWS = (2, 4, 8, 16)
EPS = 1e-6
NEG = -1e30
FC = 256
GELU_C0 = math.sqrt(2.0 / math.pi)
GELU_C1 = 0.044715

ADAM_LR = 0.001
ADAM_B1 = 0.9
ADAM_B2 = 0.999
ADAM_EPS = 1e-08
ADAM_WD = 0.01
ADAM_STEP = 10

VMEM_LIMIT = 56 * 1024 * 1024

O_K, O_V, O_U, O_G, O_P = 512, 640, 768, 1024, 1280


def _call(body, **kw):
    return pl.pallas_call(body, **kw)


def _params(sem=None, vmem=VMEM_LIMIT):
    return pltpu.CompilerParams(dimension_semantics=sem, vmem_limit_bytes=vmem)


def _host(comm, body, *, name, grid, in_specs, out_specs, out_shape, args, scratch_shapes=(), aliases=None):
    single = not isinstance(out_shape, (list, tuple))
    out_specs_l = [out_specs] if single else list(out_specs)
    out_shape_l = [out_shape] if single else list(out_shape)
    n_in, n_out, n_scr = len(in_specs), len(out_shape_l), len(scratch_shapes)
    steps = grid[0]
    any_spec = pl.BlockSpec(memory_space=pl.ANY)

    def wrapped(*refs):
        pos = 0

        def take(n):
            nonlocal pos
            part = refs[pos:pos + n]
            pos += n
            return part

        ins = take(n_in)
        cins = [take(len(t.inputs)) for t in comm]
        outs = take(n_out)
        couts = [take(len(t.out_shape)) for t in comm]
        scr = take(n_scr)
        cscr = [take(len(t.scratch)) for t in comm]
        i = pl.program_id(0)
        for k, t in enumerate(comm):
            pl.when(i == 0)(functools.partial(t.start, cins[k], couts[k], cscr[k]))
        body(*ins, *outs, *scr)
        for k, t in enumerate(comm):
            pl.when(i == (3 * steps) // 4)(functools.partial(t.mid, cins[k], couts[k], cscr[k]))
            pl.when(i == steps - 1)(functools.partial(t.finish, cins[k], couts[k], cscr[k]))

    c_args = [a for t in comm for a in t.inputs]
    c_shapes = [sh for t in comm for sh in t.out_shape]
    c_scr = [sc for t in comm for sc in t.scratch]
    res = _call(
        wrapped, name=name, grid=grid,
        in_specs=list(in_specs) + [any_spec] * len(c_args),
        out_specs=out_specs_l + [any_spec] * len(c_shapes),
        out_shape=out_shape_l + c_shapes,
        scratch_shapes=list(scratch_shapes) + c_scr,
        input_output_aliases=aliases or {},
        compiler_params=_params(("arbitrary",)),
    )(*args, *c_args)
    outs = res[0] if single else list(res[:n_out])
    if not comm:
        return outs
    c_outs, pos = [], n_out
    for t in comm:
        c_outs.append(list(res[pos:pos + len(t.out_shape)]))
        pos += len(t.out_shape)
    return outs, c_outs


def _nn(a, b):
    return lax.dot_general(a, b, (((1,), (0,)), ((), ())), preferred_element_type=F32)


def _nt(a, b):
    return lax.dot_general(a, b, (((1,), (1,)), ((), ())), preferred_element_type=F32)


def _tn(a, b):
    return lax.dot_general(a, b, (((0,), (0,)), ((), ())), preferred_element_type=F32)


def _gelu(x):
    t = jnp.tanh(GELU_C0 * (x + GELU_C1 * x * x * x))
    return 0.5 * x * (1.0 + t), t


def _gelu_grad(x, t):
    return 0.5 * (1.0 + t) + 0.5 * x * (1.0 - t * t) * (GELU_C0 * (1.0 + 3.0 * GELU_C1 * x * x))


def _rms(x):
    r = lax.rsqrt(jnp.mean(x * x, axis=-1, keepdims=True) + EPS)
    return x * r, r


def _rms_bwd(dy, xh, r, g):
    dg = jnp.sum(dy * xh, axis=0, keepdims=True)
    dxh = dy * g
    dx = r * (dxh - xh * jnp.mean(dxh * xh, axis=-1, keepdims=True))
    return dx, dg


def _wspec(rows, m):
    return pl.BlockSpec((None, rows, D), lambda i, m=m: (m, 0, 0), pipeline_mode=pl.Buffered(1))


def _rowspec(tm, cols):
    return pl.BlockSpec((tm, cols), lambda i: (i, 0))


def _fixspec(rows, cols):
    return pl.BlockSpec((rows, cols), lambda i: (0, 0))


def _ffn_fwd(x, gain, w352, mg, name, comm=()):
    s = x.shape[0]
    tm = min(512, s)

    def body(x_ref, g_ref, wg_ref, wu_ref, wd_ref, xo_ref, p1_ref, p2_ref, hid_ref):
        xt = x_ref[...]
        xh, _ = _rms(xt)
        h = (xh * g_ref[...]).astype(BF16)
        for c in range(FF // FC):
            sl = slice(c * FC, (c + 1) * FC)
            a = _nt(h, wg_ref[sl, :])
            b = _nt(h, wu_ref[sl, :])
            sig = 0.5 * jnp.tanh(0.5 * a) + 0.5
            sa = a * sig
            p1_ref[:, sl] = (b * (sig + sa * (1.0 - sig))).astype(BF16)
            p2_ref[:, sl] = sa.astype(BF16)
            hid_ref[:, sl] = (sa * b).astype(BF16)
        xo_ref[...] = xt + 0.5 * _nn(hid_ref[...], wd_ref[...])

    act = jax.ShapeDtypeStruct((s, FF), BF16)
    return _host(
        comm, body, name=name, grid=(s // tm,),
        in_specs=[_rowspec(tm, D), _fixspec(1, D), _wspec(FF, mg), _wspec(FF, mg + 1), _wspec(FF, mg + 2)],
        out_specs=[_rowspec(tm, D), _rowspec(tm, FF), _rowspec(tm, FF), _rowspec(tm, FF)],
        out_shape=[jax.ShapeDtypeStruct((s, D), F32), act, act, act],
        args=(x, gain, w352, w352, w352))


def _ffn_bwd(x, gain, dy, p1, p2, w352, mg, name, comm=()):
    s = x.shape[0]
    tm = min(256, s)

    def body(x_ref, g_ref, dy_ref, p1_ref, p2_ref, wg_ref, wu_ref, wd_ref,
             dx_ref, da_ref, db_ref, h_ref, dyb_ref, dg_ref):
        i = pl.program_id(0)
        xt = x_ref[...]
        g = g_ref[...]
        xh, r = _rms(xt)
        h_ref[...] = (xh * g).astype(BF16)
        dyt = dy_ref[...]
        dyb = (0.5 * dyt).astype(BF16)
        dyb_ref[...] = dyb
        for c in range(FF // FC):
            sl = slice(c * FC, (c + 1) * FC)
            dhid = _nt(dyb, wd_ref[sl, :])
            da_ref[:, sl] = (dhid * p1_ref[:, sl].astype(F32)).astype(BF16)
            db_ref[:, sl] = (dhid * p2_ref[:, sl].astype(F32)).astype(BF16)
        dh = _nn(da_ref[...], wg_ref[...]) + _nn(db_ref[...], wu_ref[...])
        dxn, dg = _rms_bwd(dh, xh, r, g)
        dx_ref[...] = dyt + dxn

        @pl.when(i == 0)
        def _():
            dg_ref[...] = jnp.zeros_like(dg_ref)

        dg_ref[0:1, :] += dg

    act = jax.ShapeDtypeStruct((s, FF), BF16)
    tok = jax.ShapeDtypeStruct((s, D), BF16)
    return _host(
        comm, body, name=name, grid=(s // tm,),
        in_specs=[_rowspec(tm, D), _fixspec(1, D), _rowspec(tm, D), _rowspec(tm, FF), _rowspec(tm, FF),
                  _wspec(FF, mg), _wspec(FF, mg + 1), _wspec(FF, mg + 2)],
        out_specs=[_rowspec(tm, D), _rowspec(tm, FF), _rowspec(tm, FF),
                   _rowspec(tm, D), _rowspec(tm, D), _fixspec(8, D)],
        out_shape=[jax.ShapeDtypeStruct((s, D), F32), act, act, tok, tok, jax.ShapeDtypeStruct((8, D), F32)],
        args=(x, gain, dy, p1, p2, w352, w352, w352))


def _wgrad(a, b, g, n_slabs, m, name, comm=()):
    s, mm = a.shape
    mb = 256

    def body(*refs):
        refs[-1][...] = _tn(refs[0][...], refs[1][...])

    in_specs = [pl.BlockSpec((s, mb), lambda i: (0, i)),
                pl.BlockSpec((s, D), lambda i: (0, 0), pipeline_mode=pl.Buffered(1))]
    args = [a, b]
    aliases = {}
    if g is not None:
        in_specs.append(pl.BlockSpec(memory_space=pl.ANY))
        args.append(g)
        aliases = {2: 0}
    return _host(
        comm, body, name=name, grid=(mm // mb,),
        in_specs=in_specs,
        out_specs=pl.BlockSpec((None, mb, D), lambda i, m=m: (m, i, 0)),
        out_shape=jax.ShapeDtypeStruct((n_slabs, mm, D), F32),
        aliases=aliases, args=args)


def _mixin_fwd(x, gain, w192, l, name):
    s = x.shape[0]
    tm = min(512, s)

    def body(x_ref, g_ref, w_ref, z_ref, h_ref):
        xh, _ = _rms(x_ref[...])
        h = (xh * g_ref[...]).astype(BF16)
        h_ref[...] = h
        z_ref[...] = _nt(h, w_ref[...])

    return _call(
        body, name=name, grid=(s // tm,),
        in_specs=[_rowspec(tm, D), _fixspec(1, D), _wspec(INW, l)],
        out_specs=[_rowspec(tm, INW), _rowspec(tm, D)],
        out_shape=[jax.ShapeDtypeStruct((s, INW), F32), jax.ShapeDtypeStruct((s, D), BF16)],
        compiler_params=_params(("arbitrary",)),
    )(x, gain, w192)


def _mixin_bwd(x, gain, dz, dx_in, w192, l, name):
    s = x.shape[0]
    tm = min(256, s)

    def body(x_ref, g_ref, dz_ref, dxi_ref, w_ref, dx_ref, dzb_ref, dg_ref):
        i = pl.program_id(0)
        g = g_ref[...]
        xh, r = _rms(x_ref[...])
        dzb = dz_ref[...].astype(BF16)
        dzb_ref[...] = dzb
        dh = _nn(dzb, w_ref[...])
        dxn, dg = _rms_bwd(dh, xh, r, g)
        dx_ref[...] = dxi_ref[...] + dxn

        @pl.when(i == 0)
        def _():
            dg_ref[...] = jnp.zeros_like(dg_ref)

        dg_ref[0:1, :] += dg

    return _call(
        body, name=name, grid=(s // tm,),
        in_specs=[_rowspec(tm, D), _fixspec(1, D), _rowspec(tm, INW), _rowspec(tm, D), _wspec(INW, l)],
        out_specs=[_rowspec(tm, D), _rowspec(tm, INW), _fixspec(8, D)],
        out_shape=[jax.ShapeDtypeStruct((s, D), F32), jax.ShapeDtypeStruct((s, INW), BF16),
                   jax.ShapeDtypeStruct((8, D), F32)],
        compiler_params=_params(("arbitrary",)),
    )(x, gain, dz, dx_in, w192)


def _mixout_fwd(x, y, w128, l, name):
    s = x.shape[0]
    tm = min(512, s)

    def body(x_ref, y_ref, w_ref, xo_ref):
        xo_ref[...] = x_ref[...] + _nn(y_ref[...], w_ref[...])

    return _call(
        body, name=name, grid=(s // tm,),
        in_specs=[_rowspec(tm, D), _rowspec(tm, D), _wspec(D, l)],
        out_specs=_rowspec(tm, D),
        out_shape=jax.ShapeDtypeStruct((s, D), F32),
        compiler_params=_params(("arbitrary",)),
    )(x, y, w128)


def _mixout_bwd(dx, w128, l, name, comm=()):
    s = dx.shape[0]
    tm = min(512, s)

    def body(dx_ref, w_ref, dy_ref, dxb_ref):
        dxb = dx_ref[...].astype(BF16)
        dxb_ref[...] = dxb
        dy_ref[...] = _nt(dxb, w_ref[...])

    return _host(
        comm, body, name=name, grid=(s // tm,),
        in_specs=[_rowspec(tm, D), _wspec(D, l)],
        out_specs=[_rowspec(tm, D), _rowspec(tm, D)],
        out_shape=[jax.ShapeDtypeStruct((s, D), F32), jax.ShapeDtypeStruct((s, D), BF16)],
        args=(dx, w128))


def _loss_head(x, gain, tgt, name):
    s = x.shape[0]
    tm = min(512, s)

    def body(x_ref, g_ref, t_ref, dx_ref, loss_ref, dg_ref):
        i = pl.program_id(0)
        g = g_ref[...]
        xh, r = _rms(x_ref[...])
        err = xh * g - t_ref[...]
        tok = jnp.mean(err * err, axis=-1, keepdims=True)
        lp = 0.5 * jnp.sum(tok, axis=0, keepdims=True)
        dxn, dg = _rms_bwd(err * (1.0 / D), xh, r, g)
        dx_ref[...] = dxn

        @pl.when(i == 0)
        def _():
            dg_ref[...] = jnp.zeros_like(dg_ref)
            loss_ref[...] = jnp.zeros_like(loss_ref)

        dg_ref[0:1, :] += dg
        loss_ref[0:1, :] += lp + jnp.zeros((1, 128), F32)

    return _call(
        body, name=name, grid=(s // tm,),
        in_specs=[_rowspec(tm, D), _fixspec(1, D), _rowspec(tm, D)],
        out_specs=[_rowspec(tm, D), _fixspec(8, 128), _fixspec(8, D)],
        out_shape=[jax.ShapeDtypeStruct((s, D), F32), jax.ShapeDtypeStruct((8, 128), F32),
                   jax.ShapeDtypeStruct((8, D), F32)],
        compiler_params=_params(("arbitrary",)),
    )(x, gain, tgt)


def _attn_mask(n):
    row = lax.broadcasted_iota(jnp.int32, (BLK, 2 * BLK), 0)
    col = lax.broadcasted_iota(jnp.int32, (BLK, 2 * BLK), 1)
    lo = jnp.where(n > 0, 0, BLK)
    return (col > row) & (col <= row + BLK) & (col >= lo)


def _tril_mask():
    row = lax.broadcasted_iota(jnp.int32, (BLK, BLK), 0)
    col = lax.broadcasted_iota(jnp.int32, (BLK, BLK), 1)
    return col <= row


def _lane_group_select(vals):
    lane = lax.broadcasted_iota(jnp.int32, vals[0].shape, 1)
    return jnp.where(lane < 64, vals[0], jnp.where(lane < 128, vals[1], jnp.where(lane < 192, vals[2], vals[3])))


def _pool_count(n):
    row = lax.broadcasted_iota(jnp.int32, (BLK, 256), 0)
    pos1 = (n * BLK + row + 1).astype(F32)
    wl = _lane_group_select([jnp.full((BLK, 256), float(w), F32) for w in POOL_WINDOWS])
    return jnp.minimum(pos1, wl)


def _window_sums(e, forward):
    tot = e.shape[0]
    lv = e
    out = []
    for sh in (1, 2, 4, 8):
        lv = lv + pltpu.roll(lv, sh if forward else tot - sh, 0)
        out.append(lv)
    return _lane_group_select(out)


def _mix_common(n, zc_ref, zkvp_ref, zpp_ref, vn_ref, ws_ref, bt_ref, pw_ref):
    u, tu = _gelu(zc_ref[:, O_U:O_G])
    gv, tv = _gelu(zc_ref[:, O_G:O_P])
    xh, rv = _rms(gv)
    vnb = (xh * vn_ref[...]).astype(BF16)
    tril = _tril_mask()
    wm = [jnp.where(tril, ws_ref[h], 0.0).astype(BF16) for h in range(4)]
    f = jnp.concatenate(
        [_nn(wm[h], vnb[:, h * HD:(h + 1) * HD]) + bt_ref[:, h:h + 1] for h in range(4)], axis=1)
    p = zc_ref[:, O_P:INW]
    pm = jnp.where(n > 0, 1.0, 0.0)
    e = jnp.concatenate([zpp_ref[...] * pm, p], axis=0)
    cnt = _pool_count(n)
    diff = (_window_sums(e, True)[BLK:, :] / cnt - p).astype(BF16)
    pwb = [pw_ref[g].astype(BF16) for g in range(4)]
    pout = jnp.concatenate([_nn(diff[:, g * HD:(g + 1) * HD], pwb[g]) for g in range(4)], axis=1)
    return dict(u=u, tu=tu, tv=tv, xh=xh, rv=rv, vnb=vnb, wm=wm, f=f, cnt=cnt, diff=diff, pwb=pwb, pout=pout,
                tril=tril)


def _kv_window(zc_ref, zkvp_ref, g):
    kk = jnp.concatenate([zkvp_ref[:, g * HD:(g + 1) * HD],
                          zc_ref[:, O_K + g * HD:O_K + (g + 1) * HD]], axis=0).astype(BF16)
    vv = jnp.concatenate([zkvp_ref[:, BLK + g * HD:BLK + (g + 1) * HD],
                          zc_ref[:, O_V + g * HD:O_V + (g + 1) * HD]], axis=0).astype(BF16)
    return kk, vv


def _mix_fwd(z, sinks, vnorm, ws, bias_t, pw, pscale, name):
    s = z.shape[0]
    nb = s // BLK

    def body(sink_ref, zc_ref, zkvp_ref, zpp_ref, vn_ref, ws_ref, bt_ref, pw_ref, ps_ref, y_ref, lse_ref):
        n = pl.program_id(0)
        valid = _attn_mask(n)
        lse_ref[...] = jnp.zeros_like(lse_ref)
        for g in range(N_KV):
            kk, vv = _kv_window(zc_ref, zkvp_ref, g)
            for rr in range(REP):
                h = g * REP + rr
                qh = zc_ref[:, h * HD:(h + 1) * HD].astype(BF16)
                sc = jnp.where(valid, _nt(qh, kk) * ATTN_SCALE, NEG)
                sink = sink_ref[h]
                m = jnp.maximum(jnp.max(sc, axis=-1, keepdims=True), sink)
                ex = jnp.exp(sc - m)
                den = jnp.sum(ex, axis=-1, keepdims=True) + jnp.exp(sink - m)
                pr = (ex / den).astype(BF16)
                y_ref[:, h * HD:(h + 1) * HD] = _nn(pr, vv).astype(BF16)
                lse_ref[:, h:h + 1] = m + jnp.log(den)
        c = _mix_common(n, zc_ref, zkvp_ref, zpp_ref, vn_ref, ws_ref, bt_ref, pw_ref)
        y_ref[:, 512:768] = (c["u"] * c["f"]).astype(BF16)
        y_ref[:, 768:1024] = (c["pout"] * ps_ref[...]).astype(BF16)

    prev = lambda n: jnp.maximum(n - 1, 0)
    return _call(
        body, name=name, grid=(nb,),
        in_specs=[pl.BlockSpec(memory_space=pltpu.SMEM),
                  pl.BlockSpec((BLK, INW), lambda n: (n, 0)),
                  pl.BlockSpec((BLK, 256), lambda n: (prev(n), 2)),
                  pl.BlockSpec((BLK, 256), lambda n: (prev(n), 5)),
                  _fixspec(1, 256), pl.BlockSpec((4, BLK, BLK), lambda n: (0, 0, 0)), _fixspec(BLK, 128),
                  pl.BlockSpec((4, HD, HD), lambda n: (0, 0, 0)), _fixspec(1, 256)],
        out_specs=[pl.BlockSpec((BLK, D), lambda n: (n, 0)), pl.BlockSpec((BLK, 128), lambda n: (n, 0))],
        out_shape=[jax.ShapeDtypeStruct((s, D), BF16), jax.ShapeDtypeStruct((s, 128), F32)],
        compiler_params=_params(("arbitrary",)),
    )(sinks, z, z, z, vnorm, ws, bias_t, pw, pscale)


def _mix_bwd(z, dy, lse, sinks, vnorm, ws, bias_t, pw, pscale, name, comm=()):
    s = z.shape[0]
    nb = s // BLK

    def body(sink_ref, zc_ref, zkvp_ref, zpp_ref, dy_ref, lse_ref, vn_ref, ws_ref, bt_ref, pw_ref, ps_ref,
             dz_ref, dsink_ref, dvn_ref, dws_ref, dbt_ref, dpw_ref, dps_ref, carry_ref, ddc_ref):
        n = pl.program_id(0)

        @pl.when(n == 0)
        def _():
            carry_ref[...] = jnp.zeros_like(carry_ref)
            ddc_ref[...] = jnp.zeros_like(ddc_ref)
            dsink_ref[...] = jnp.zeros_like(dsink_ref)
            dvn_ref[...] = jnp.zeros_like(dvn_ref)
            dws_ref[...] = jnp.zeros_like(dws_ref)
            dbt_ref[...] = jnp.zeros_like(dbt_ref)
            dpw_ref[...] = jnp.zeros_like(dpw_ref)
            dps_ref[...] = jnp.zeros_like(dps_ref)

        def finish_prev(dd_cur):
            dz_ref[...] = carry_ref[...]
            rs = _window_sums(jnp.concatenate([ddc_ref[...], dd_cur], axis=0), False)
            dz_ref[:, O_P:INW] += rs[:BLK, :]

        @pl.when(n < nb)
        def _():
            valid = _attn_mask(n)
            dq = []
            dk_prev, dk_cur, dv_prev, dv_cur = [], [], [], []
            for g in range(N_KV):
                kk, vv = _kv_window(zc_ref, zkvp_ref, g)
                dkk = jnp.zeros((2 * BLK, HD), F32)
                dvv = jnp.zeros((2 * BLK, HD), F32)
                for rr in range(REP):
                    h = g * REP + rr
                    qh = zc_ref[:, h * HD:(h + 1) * HD].astype(BF16)
                    sc = jnp.where(valid, _nt(qh, kk) * ATTN_SCALE, NEG)
                    lse_h = lse_ref[:, h:h + 1]
                    pr = jnp.exp(sc - lse_h)
                    do = dy_ref[:, h * HD:(h + 1) * HD].astype(BF16)
                    dp = _nt(do, vv)
                    delta = jnp.sum(pr * dp, axis=-1, keepdims=True)
                    ds = ((pr * (dp - delta)) * ATTN_SCALE).astype(BF16)
                    psink = jnp.exp(sink_ref[h] - lse_h)
                    dsink_ref[h:h + 1, :] += (jnp.zeros((1, 128), F32)
                                              - jnp.sum(psink * delta, axis=0, keepdims=True))
                    dq.append(_nn(ds, kk))
                    dkk = dkk + _tn(ds, qh)
                    dvv = dvv + _tn(pr.astype(BF16), do)
                dk_prev.append(dkk[:BLK]); dk_cur.append(dkk[BLK:])
                dv_prev.append(dvv[:BLK]); dv_cur.append(dvv[BLK:])
            c = _mix_common(n, zc_ref, zkvp_ref, zpp_ref, vn_ref, ws_ref, bt_ref, pw_ref)
            dyg = dy_ref[:, 512:768]
            du = dyg * c["f"]
            df = dyg * c["u"]
            dzu = du * _gelu_grad(zc_ref[:, O_U:O_G], c["tu"])
            dvn_parts = []
            for h in range(4):
                dfh = df[:, h * HD:(h + 1) * HD]
                dfb = dfh.astype(BF16)
                dbt_ref[:, h:h + 1] += jnp.sum(dfh, axis=1, keepdims=True)
                dws_ref[h] += jnp.where(c["tril"], _nt(dfb, c["vnb"][:, h * HD:(h + 1) * HD]), 0.0)
                dvn_parts.append(_tn(c["wm"][h], dfb))
            dvn = jnp.concatenate(dvn_parts, axis=1)
            dgv, dvg = _rms_bwd(dvn, c["xh"], c["rv"], vn_ref[...])
            dvn_ref[0:1, :] += dvg
            dzv = dgv * _gelu_grad(zc_ref[:, O_G:O_P], c["tv"])
            dyp = dy_ref[:, 768:1024]
            dps_ref[0:1, :] += jnp.sum(dyp * c["pout"], axis=0, keepdims=True)
            dout = (dyp * ps_ref[...]).astype(BF16)
            ddiff_parts = []
            for g in range(4):
                dog = dout[:, g * HD:(g + 1) * HD]
                dpw_ref[g] += _tn(c["diff"][:, g * HD:(g + 1) * HD], dog)
                ddiff_parts.append(_nt(dog, c["pwb"][g]))
            ddiff = jnp.concatenate(ddiff_parts, axis=1)
            dd = ddiff / c["cnt"]
            finish_prev(dd)
            dz_ref[:, O_K:O_V] += jnp.concatenate(dk_prev, axis=1)
            dz_ref[:, O_V:O_U] += jnp.concatenate(dv_prev, axis=1)
            for h in range(N_HEADS):
                carry_ref[:, h * HD:(h + 1) * HD] = dq[h]
            carry_ref[:, O_K:O_V] = jnp.concatenate(dk_cur, axis=1)
            carry_ref[:, O_V:O_U] = jnp.concatenate(dv_cur, axis=1)
            carry_ref[:, O_U:O_G] = dzu
            carry_ref[:, O_G:O_P] = dzv
            carry_ref[:, O_P:INW] = -ddiff
            ddc_ref[...] = dd

        @pl.when(n == nb)
        def _():
            finish_prev(jnp.zeros((BLK, 256), F32))

    cur = lambda n: jnp.minimum(n, nb - 1)
    prev = lambda n: jnp.maximum(jnp.minimum(n, nb - 1) - 1, 0)
    fix3 = lambda a, b, c: pl.BlockSpec((a, b, c), lambda n: (0, 0, 0))
    return _host(
        comm, body, name=name, grid=(nb + 1,),
        in_specs=[pl.BlockSpec(memory_space=pltpu.SMEM),
                  pl.BlockSpec((BLK, INW), lambda n: (cur(n), 0)),
                  pl.BlockSpec((BLK, 256), lambda n: (prev(n), 2)),
                  pl.BlockSpec((BLK, 256), lambda n: (prev(n), 5)),
                  pl.BlockSpec((BLK, D), lambda n: (cur(n), 0)),
                  pl.BlockSpec((BLK, 128), lambda n: (cur(n), 0)),
                  _fixspec(1, 256), fix3(4, BLK, BLK), _fixspec(BLK, 128), fix3(4, HD, HD), _fixspec(1, 256)],
        out_specs=[pl.BlockSpec((BLK, INW), lambda n: (jnp.maximum(n - 1, 0), 0)),
                   _fixspec(8, 128), _fixspec(8, 256), fix3(4, BLK, BLK), _fixspec(BLK, 128),
                   fix3(4, HD, HD), _fixspec(8, 256)],
        out_shape=[jax.ShapeDtypeStruct((s, INW), F32), jax.ShapeDtypeStruct((8, 128), F32),
                   jax.ShapeDtypeStruct((8, 256), F32), jax.ShapeDtypeStruct((4, BLK, BLK), F32),
                   jax.ShapeDtypeStruct((BLK, 128), F32), jax.ShapeDtypeStruct((4, HD, HD), F32),
                   jax.ShapeDtypeStruct((8, 256), F32)],
        scratch_shapes=[pltpu.VMEM((BLK, INW), F32), pltpu.VMEM((BLK, 256), F32)],
        args=(sinks, z, z, z, dy, lse, vnorm, ws, bias_t, pw, pscale))


def _position():
    x, y, c = lax.axis_index("x"), lax.axis_index("y"), lax.axis_index("c")
    return x, y, c


class _GatherTask:
    def __init__(self, srcs):
        self.inputs = list(srcs)
        ng = len(srcs)
        self.out_shape = [jax.ShapeDtypeStruct((a.shape[0], N_DEV) + a.shape[1:], a.dtype) for a in srcs]
        self.scratch = [pltpu.SemaphoreType.DMA((ng, 7)), pltpu.SemaphoreType.DMA((ng, 7)),
                        pltpu.SemaphoreType.DMA((ng,))]

    def _plan(self, src, dst, sems):
        send_sems, recv_sems, local_sems = sems
        ng = len(src)
        x, y, c = _position()
        me, sibling = (x, y, c), (x, y, 1 - c)
        chips = [(1 - x, y), (x, 1 - y), (1 - x, 1 - y)]

        def slot(pos):
            return 4 * pos[0] + 2 * pos[1] + pos[2]

        def copy(gi, k, block, to, from_src=False):
            rows = dst[gi].at[:, slot(block)]
            return pltpu.make_async_remote_copy(
                src_ref=src[gi] if from_src else rows, dst_ref=rows,
                send_sem=send_sems.at[gi, k], recv_sem=recv_sems.at[gi, k],
                device_id=to, device_id_type=MESH)

        make = functools.partial
        mine = [make(pltpu.make_async_copy, src[gi], dst[gi].at[:, slot(me)], local_sems.at[gi]) for gi in range(ng)]
        first = []
        for gi in range(ng):
            first.append(make(copy, gi, 0, me, sibling, True))
            first += [make(copy, gi, 1 + j, me, (*chip, c), True) for j, chip in enumerate(chips)]
        passed = [make(copy, gi, 4 + j, (*chip, c), sibling) for j, chip in enumerate(chips) for gi in range(ng)]
        arrive_ici = [make(copy, gi, 1 + j, (*chip, c), me) for j, chip in enumerate(chips) for gi in range(ng)]
        arrive_d2d = [make(copy, gi, 0, sibling, me) for gi in range(ng)]
        arrive_d2d += [make(copy, gi, 4 + j, (*chip, 1 - c), me) for j, chip in enumerate(chips) for gi in range(ng)]
        return mine, first, passed, arrive_ici, arrive_d2d

    def start(self, src, dst, sems):
        mine, first, _, _, _ = self._plan(src, dst, sems)
        for cp in mine + first:
            cp().start()

    def mid(self, src, dst, sems):
        _, _, passed, arrive_ici, _ = self._plan(src, dst, sems)
        for arrived, fw in zip(arrive_ici, passed):
            arrived().wait_recv()
            fw().start()

    def finish(self, src, dst, sems):
        mine, first, passed, _, arrive_d2d = self._plan(src, dst, sems)
        for cp in arrive_d2d:
            cp().wait_recv()
        for cp in first + passed:
            cp().wait_send()
        for cp in mine:
            cp().wait()


class _SiblingTask:
    def __init__(self, g5s):
        self.inputs = list(g5s)
        ng = len(g5s)
        self.out_shape = [jax.ShapeDtypeStruct((a.shape[0], 4) + a.shape[3:], a.dtype) for a in g5s]
        self.scratch = [pltpu.SemaphoreType.DMA((ng,)), pltpu.SemaphoreType.DMA((ng,))]

    def _plan(self, src, dst, sems):
        send_sems, recv_sems = sems
        x, y, c = _position()
        return [functools.partial(
            pltpu.make_async_remote_copy,
            src_ref=src[gi].at[:, :, 1 - c], dst_ref=dst[gi],
            send_sem=send_sems.at[gi], recv_sem=recv_sems.at[gi],
            device_id=(x, y, 1 - c), device_id_type=MESH) for gi in range(len(src))]

    def start(self, src, dst, sems):
        for cp in self._plan(src, dst, sems):
            cp().start()

    def mid(self, src, dst, sems):
        pass

    def finish(self, src, dst, sems):
        for cp in self._plan(src, dst, sems):
            cp().wait()


class _ChipTask(_SiblingTask):
    def __init__(self, sbs):
        self.inputs = list(sbs)
        ng = len(sbs)
        self.out_shape = [jax.ShapeDtypeStruct(a.shape, a.dtype) for a in sbs]
        self.scratch = [pltpu.SemaphoreType.DMA((ng, 3)), pltpu.SemaphoreType.DMA((ng, 3))]

    def _plan(self, src, dst, sems):
        send_sems, recv_sems = sems
        x, y, c = _position()
        jme = 2 * x + y
        chips = [(1 - x, y), (x, 1 - y), (1 - x, 1 - y)]
        return [functools.partial(
            pltpu.make_async_remote_copy,
            src_ref=src[gi].at[:, 2 * chip[0] + chip[1]], dst_ref=dst[gi].at[:, jme],
            send_sem=send_sems.at[gi, k], recv_sem=recv_sems.at[gi, k],
            device_id=(*chip, c), device_id_type=MESH) for k, chip in enumerate(chips) for gi in range(len(src))]


def _alone(task, name):
    n_in, n_out = len(task.inputs), len(task.out_shape)

    def body(*refs):
        parts = (refs[:n_in], refs[n_in:n_in + n_out], refs[n_in + n_out:])
        task.start(*parts)
        task.mid(*parts)
        task.finish(*parts)

    any_spec = pl.BlockSpec(memory_space=pl.ANY)
    return _call(body, name=name, in_specs=[any_spec] * n_in, out_specs=[any_spec] * n_out,
                 out_shape=task.out_shape, scratch_shapes=task.scratch)(*task.inputs)


def _core_sum(ids, g5, r1, name):
    n, _, _, rows, _ = g5.shape

    def body(ids_ref, g_ref, r_ref, sb_ref, own_ref):
        j = pl.program_id(2)
        t = g_ref[...] + r_ref[...]
        sb_ref[...] = t.astype(BF16)

        @pl.when(j == ids_ref[1])
        def _():
            own_ref[...] = t

    grid_spec = pltpu.PrefetchScalarGridSpec(
        num_scalar_prefetch=1, grid=(n, 1, 4),
        in_specs=[pl.BlockSpec((None, None, None, rows, D), lambda i, t, j, ids: (i, j, ids[0], t, 0)),
                  pl.BlockSpec((None, None, rows, D), lambda i, t, j, ids: (i, j, t, 0))],
        out_specs=[pl.BlockSpec((None, None, rows, D), lambda i, t, j, ids: (i, j, t, 0)),
                   pl.BlockSpec((None, rows, D), lambda i, t, j, ids: (i, t, 0))])
    return _call(
        body, name=name, grid_spec=grid_spec,
        out_shape=[jax.ShapeDtypeStruct((n, 4, rows, D), BF16), jax.ShapeDtypeStruct((n, rows, D), F32)],
        compiler_params=_params(("arbitrary", "arbitrary", "arbitrary")),
    )(ids, g5, r1)


def _chip_sum(others, own, r2, name):
    n, rows, _ = own.shape

    def body(oth_ref, own_ref, r0_ref, r1_ref, r2_ref, out_ref):
        out_ref[...] = ((own_ref[...] + r0_ref[...].astype(F32)) + r1_ref[...].astype(F32)) \
            + r2_ref[...].astype(F32)

    def rspec(k):
        return pl.BlockSpec((None, None, rows, D), lambda i, oth, k=k: (i, oth[k], 0, 0))

    grid_spec = pltpu.PrefetchScalarGridSpec(
        num_scalar_prefetch=1, grid=(n,),
        in_specs=[pl.BlockSpec((None, rows, D), lambda i, oth: (i, 0, 0)), rspec(0), rspec(1), rspec(2)],
        out_specs=pl.BlockSpec((None, rows, D), lambda i, oth: (i, 0, 0)))
    return _call(
        body, name=name, grid_spec=grid_spec,
        out_shape=jax.ShapeDtypeStruct((n, rows, D), F32),
        compiler_params=_params(("arbitrary",)),
    )(others, own, r2, r2, r2)


def _adam_math(w, g, m, v):
    m = ADAM_B1 * m + (1.0 - ADAM_B1) * g
    v = ADAM_B2 * v + (1.0 - ADAM_B2) * (g * g)
    m_hat = m / (1.0 - ADAM_B1 ** ADAM_STEP)
    v_hat = v / (1.0 - ADAM_B2 ** ADAM_STEP)
    delta = -ADAM_LR * (m_hat / (jnp.sqrt(v_hat) + ADAM_EPS) + ADAM_WD * w)
    return delta, m, v


def _adamw(w, g, m, v, name):
    shape = w.shape
    c = shape[-1]
    r = w.size // c
    rb = max(d for d in range(8, min(r, 512) + 1, 8) if r % d == 0)

    def body(w_ref, g_ref, m_ref, v_ref, d_ref, mo_ref, vo_ref):
        d_ref[...], mo_ref[...], vo_ref[...] = _adam_math(w_ref[...], g_ref[...], m_ref[...], v_ref[...])

    spec = _rowspec(rb, c)
    outs = _call(
        body, name=name, grid=(r // rb,),
        in_specs=[spec] * 4, out_specs=[spec] * 3,
        out_shape=[jax.ShapeDtypeStruct((r, c), F32)] * 3,
        compiler_params=_params(("arbitrary",)),
    )(*[t.reshape(r, c) for t in (w, g, m, v)])
    return [o.reshape(shape) for o in outs]


def _adamw_small(parts, w, m, v, name):
    def body(p_ref, w_ref, m_ref, v_ref, g_ref, d_ref, mo_ref, vo_ref):
        g = p_ref[0]
        for dev in range(1, N_DEV):
            g = g + p_ref[dev]
        g_ref[...] = g
        d_ref[...], mo_ref[...], vo_ref[...] = _adam_math(w_ref[...], g, m_ref[...], v_ref[...])

    return _call(
        body, name=name,
        out_shape=[jax.ShapeDtypeStruct(w.shape, F32)] * 4,
        compiler_params=_params(),
    )(parts, w, m, v)


SMALL = ["ffn1_norm", "mix_norm", "attn_sinks", "gmlp_v_norm", "gmlp_w_s", "gmlp_b", "pool_w", "pool_scale",
         "ffn2_norm", "final_norm"]


def _pack_small(arrs, extra=None):
    pieces = []
    for a in arrs:
        f = a.reshape(-1)
        pieces.append(jnp.pad(f, (0, (-f.shape[0]) % 128)))
    if extra is not None:
        pieces.append(jnp.pad(extra.reshape(-1), (0, 127)))
    else:
        pieces.append(jnp.zeros((128,), F32))
    flat = jnp.concatenate(pieces)
    flat = jnp.pad(flat, (0, (-flat.shape[0]) % 1024))
    return flat.reshape(-1, 128)


def _unpack_small(packed, like):
    flat = packed.reshape(-1)
    out, off = [], 0
    for a in like:
        out.append(flat[off:off + a.size].reshape(a.shape))
        off += a.size + (-a.size) % 128
    return out, flat[off]


def kernel(x, ffn1_norm, ffn1_w_gate, ffn1_w_up, ffn1_w_down, mix_norm, w_in, attn_sinks, gmlp_v_norm, gmlp_w_s, gmlp_b, pool_w, pool_scale, w_out, ffn2_norm, ffn2_w_gate, ffn2_w_up, ffn2_w_down, final_norm, loss_target, m_ffn1_norm, m_ffn1_w_gate, m_ffn1_w_up, m_ffn1_w_down, m_mix_norm, m_w_in, m_attn_sinks, m_gmlp_v_norm, m_gmlp_w_s, m_gmlp_b, m_pool_w, m_pool_scale, m_w_out, m_ffn2_norm, m_ffn2_w_gate, m_ffn2_w_up, m_ffn2_w_down, m_final_norm, v_ffn1_norm, v_ffn1_w_gate, v_ffn1_w_up, v_ffn1_w_down, v_mix_norm, v_w_in, v_attn_sinks, v_gmlp_v_norm, v_gmlp_w_s, v_gmlp_b, v_pool_w, v_pool_scale, v_w_out, v_ffn2_norm, v_ffn2_w_gate, v_ffn2_w_up, v_ffn2_w_down, v_final_norm):
    s = x.shape[1]
    xi, yi, ci = _position()
    ids = jnp.stack([ci, 2 * xi + yi]).astype(jnp.int32)
    jme = 2 * xi + yi
    others = jnp.stack([k + (k >= jme).astype(jnp.int32) for k in range(3)]).astype(jnp.int32)
    t = lambda a: jnp.swapaxes(a, -1, -2)
    row = lambda a: a.reshape(1, -1)
    full = lambda a: a.reshape(a.shape[0], -1, D)

    loc_f1 = [jnp.stack([t(ffn1_w_gate[l]), t(ffn1_w_up[l]), ffn1_w_down[l]]).astype(BF16) for l in range(DEPTH)]
    loc_f2 = [jnp.stack([t(ffn2_w_gate[l]), t(ffn2_w_up[l]), ffn2_w_down[l]]).astype(BF16) for l in range(DEPTH)]
    loc_in = [t(w_in[l])[None].astype(BF16) for l in range(DEPTH)]
    loc_out = [w_out[l][None].astype(BF16) for l in range(DEPTH)]

    (wf1,) = _alone(_GatherTask([loc_f1[0]]), "gather_first")
    wf1 = full(wf1)
    xc = x.reshape(s, D)
    saved = []
    for l in range(DEPTH):
        x0 = xc
        if l == 0:
            (x1, *act1), ((wf2, win, wout),) = _ffn_fwd(
                x0, row(ffn1_norm[l]), wf1, 0, f"ffn1_fwd_{l}", comm=[_GatherTask([loc_f2[0], loc_in[0], loc_out[0]])])
        else:
            (x1, *act1), ((wf2,),) = _ffn_fwd(
                x0, row(ffn1_norm[l]), wf1, 0, f"ffn1_fwd_{l}", comm=[_GatherTask([loc_f2[1]])])
        wf2, win, wout = full(wf2), full(win), full(wout)
        z, hmix = _mixin_fwd(x1, row(mix_norm[l]), win, 0, f"mixin_fwd_{l}")
        bias_t = jnp.pad(t(gmlp_b[l]), ((0, 0), (0, 124)))
        mixp = (attn_sinks[l], row(gmlp_v_norm[l]), gmlp_w_s[l], bias_t, pool_w[l], row(pool_scale[l]))
        y, lse = _mix_fwd(z, *mixp, f"mix_fwd_{l}")
        x2 = _mixout_fwd(x1, y, wout, 0, f"mixout_fwd_{l}")
        saved.append((x0, act1, wf1, x1, z, hmix, mixp, y, lse, win, wout, x2, wf2))
        if l == 0:
            (x3, *act2), ((wf1, win, wout),) = _ffn_fwd(
                x2, row(ffn2_norm[l]), wf2, 0, f"ffn2_fwd_{l}", comm=[_GatherTask([loc_f1[1], loc_in[1], loc_out[1]])])
            wf1 = full(wf1)
        else:
            x3, *act2 = _ffn_fwd(x2, row(ffn2_norm[l]), wf2, 0, f"ffn2_fwd_{l}")
        saved[-1] = saved[-1] + (act2,)
        xc = x3
    dx, loss_part, d_final = _loss_head(xc, row(final_norm), loss_target.reshape(s, D), "loss_head")

    def five(g):
        return g.reshape(g.shape[0], 4, 2, g.shape[1] // N_DEV, D)

    def core_sums(g5s, r1s, tag):
        res = [_core_sum(ids, g5, r1, f"core_sum_{tag}_{i}") for i, (g5, r1) in enumerate(zip(g5s, r1s))]
        return [sb for sb, _ in res], [own for _, own in res]

    def chip_sums(owns, r2s, tag):
        return [_chip_sum(others, own, r2, f"chip_sum_{tag}_{i}") for i, (own, r2) in enumerate(zip(owns, r2s))]

    small = {}
    red = {}
    x0, (p11, p21, hid1), wf1, x1, z, hmix, mixp, y, lse, win, wout, x2, wf2, (p12, p22, hid2) = saved[1]
    dx, da, db, h, dyb, dg = _ffn_bwd(x2, row(ffn2_norm[1]), dx, p12, p22, wf2, 0, "ffn2_bwd_1")
    small[("ffn2_norm", 1)] = dg[0]
    g = _wgrad(da, h, None, 3, 0, "wgrad_gate2_1")
    g = _wgrad(db, h, g, 3, 1, "wgrad_up2_1")
    g = _wgrad(hid2, dyb, g, 3, 2, "wgrad_down2_1")
    a5 = [five(g)]
    (dymix, dxb), (a_r1,) = _mixout_bwd(dx, wout, 0, "mixout_bwd_1", comm=[_SiblingTask(a5)])
    a_sb, a_own = core_sums(a5, a_r1, "a")
    g_out = _wgrad(y, dxb, None, 1, 0, "wgrad_out_1")
    (dz, dsink, dvn, dws, dbt, dpw, dps), (a_r2,) = _mix_bwd(z, dymix, lse, *mixp, "mix_bwd_1", comm=[_ChipTask(a_sb)])
    (red[("f2", 1)],) = chip_sums(a_own, a_r2, "a")
    small.update({("attn_sinks", 1): dsink[:, 0], ("gmlp_v_norm", 1): dvn[0], ("gmlp_w_s", 1): dws,
                  ("gmlp_b", 1): t(dbt[:, :4]), ("pool_w", 1): dpw, ("pool_scale", 1): dps[0]})
    dx, dzb, dg = _mixin_bwd(x1, row(mix_norm[1]), dz, dx, win, 0, "mixin_bwd_1")
    small[("mix_norm", 1)] = dg[0]
    g_in = _wgrad(dzb, hmix, None, 1, 0, "wgrad_in_1")
    b5 = [five(g_out), five(g_in)]
    (dx, da, db, h, dyb, dg), (b_r1,) = _ffn_bwd(x0, row(ffn1_norm[1]), dx, p11, p21, wf1, 0, "ffn1_bwd_1",
                                                 comm=[_SiblingTask(b5)])
    small[("ffn1_norm", 1)] = dg[0]
    b_sb, b_own = core_sums(b5, b_r1, "b")
    g, (b_r2,) = _wgrad(da, h, None, 3, 0, "wgrad_gate1_1", comm=[_ChipTask(b_sb)])
    red[("out", 1)], red[("in", 1)] = chip_sums(b_own, b_r2, "b")
    g = _wgrad(db, h, g, 3, 1, "wgrad_up1_1")
    g = _wgrad(hid1, dyb, g, 3, 2, "wgrad_down1_1")
    c5 = [five(g)]
    x0, (p11, p21, hid1), wf1, x1, z, hmix, mixp, y, lse, win, wout, x2, wf2, (p12, p22, hid2) = saved[0]
    (dx, da, db, h, dyb, dg), (c_r1,) = _ffn_bwd(x2, row(ffn2_norm[0]), dx, p12, p22, wf2, 0, "ffn2_bwd_0",
                                                 comm=[_SiblingTask(c5)])
    small[("ffn2_norm", 0)] = dg[0]
    c_sb, c_own = core_sums(c5, c_r1, "c")
    g = _wgrad(da, h, None, 3, 0, "wgrad_gate2_0")
    g = _wgrad(db, h, g, 3, 1, "wgrad_up2_0")
    g = _wgrad(hid2, dyb, g, 3, 2, "wgrad_down2_0")
    d5 = [five(g)]
    (dymix, dxb), (d_r1,) = _mixout_bwd(dx, wout, 0, "mixout_bwd_0", comm=[_SiblingTask(d5)])
    d_sb, d_own = core_sums(d5, d_r1, "d")
    g_out = _wgrad(y, dxb, None, 1, 0, "wgrad_out_0")
    (dz, dsink, dvn, dws, dbt, dpw, dps), (c_r2,) = _mix_bwd(z, dymix, lse, *mixp, "mix_bwd_0", comm=[_ChipTask(c_sb)])
    (red[("f1", 1)],) = chip_sums(c_own, c_r2, "c")
    small.update({("attn_sinks", 0): dsink[:, 0], ("gmlp_v_norm", 0): dvn[0], ("gmlp_w_s", 0): dws,
                  ("gmlp_b", 0): t(dbt[:, :4]), ("pool_w", 0): dpw, ("pool_scale", 0): dps[0]})
    dx, dzb, dg = _mixin_bwd(x1, row(mix_norm[0]), dz, dx, win, 0, "mixin_bwd_0")
    small[("mix_norm", 0)] = dg[0]
    g_in = _wgrad(dzb, hmix, None, 1, 0, "wgrad_in_0")
    e5 = [five(g_out), five(g_in)]
    (dx, da, db, h, dyb, dg), (d_r2, e_r1) = _ffn_bwd(x0, row(ffn1_norm[0]), dx, p11, p21, wf1, 0, "ffn1_bwd_0",
                                                      comm=[_ChipTask(d_sb), _SiblingTask(e5)])
    small[("ffn1_norm", 0)] = dg[0]
    (red[("f2", 0)],) = chip_sums(d_own, d_r2, "d")
    e_sb, e_own = core_sums(e5, e_r1, "e")
    grad_x = dx.reshape(1, s, D)

    part = [d_final[0] if nm == "final_norm" else jnp.stack([small[(nm, l)] for l in range(DEPTH)]) for nm in SMALL]
    packed = _pack_small(part, loss_part[0, 0])
    g_gate, (e_r2, (gathered,)) = _wgrad(da, h, None, 1, 0, "wgrad_gate1_0",
                                         comm=[_ChipTask(e_sb), _GatherTask([packed[None]])])
    red[("out", 0)], red[("in", 0)] = chip_sums(e_own, e_r2, "e")
    f5 = [five(g_gate)]
    g, (f_r1,) = _wgrad(db, h, None, 2, 0, "wgrad_up1_0", comm=[_SiblingTask(f5)])
    f_sb, f_own = core_sums(f5, f_r1, "f")
    g, (f_r2,) = _wgrad(hid1, dyb, g, 2, 1, "wgrad_down1_0", comm=[_ChipTask(f_sb)])
    (red_gate,) = chip_sums(f_own, f_r2, "f")
    h5 = [five(g)]
    h_r1 = _alone(_SiblingTask(h5), "reduce_sibling_last")
    h_sb, h_own = core_sums(h5, h_r1, "h")
    h_r2 = _alone(_ChipTask(h_sb), "reduce_chips_last")
    (red_updown,) = chip_sums(h_own, h_r2, "h")
    red[("f1", 0)] = jnp.concatenate([red_gate, red_updown], axis=0)

    grads = {}
    for k, nm in enumerate(["w_gate", "w_up", "w_down"]):
        for f in ("f1", "f2"):
            blk = jnp.stack([red[(f, l)][k] for l in range(DEPTH)])
            grads[f"ffn{f[1]}_{nm}"] = blk if nm == "w_down" else t(blk)
    grads["w_in"] = t(jnp.concatenate([red[("in", l)] for l in range(DEPTH)], axis=0))
    grads["w_out"] = jnp.concatenate([red[("out", l)] for l in range(DEPTH)], axis=0)

    small_w = dict(ffn1_norm=ffn1_norm, mix_norm=mix_norm, attn_sinks=attn_sinks, gmlp_v_norm=gmlp_v_norm,
                   gmlp_w_s=gmlp_w_s, gmlp_b=gmlp_b, pool_w=pool_w, pool_scale=pool_scale, ffn2_norm=ffn2_norm,
                   final_norm=final_norm)
    small_m = dict(ffn1_norm=m_ffn1_norm, mix_norm=m_mix_norm, attn_sinks=m_attn_sinks, gmlp_v_norm=m_gmlp_v_norm,
                   gmlp_w_s=m_gmlp_w_s, gmlp_b=m_gmlp_b, pool_w=m_pool_w, pool_scale=m_pool_scale,
                   ffn2_norm=m_ffn2_norm, final_norm=m_final_norm)
    small_v = dict(ffn1_norm=v_ffn1_norm, mix_norm=v_mix_norm, attn_sinks=v_attn_sinks, gmlp_v_norm=v_gmlp_v_norm,
                   gmlp_w_s=v_gmlp_w_s, gmlp_b=v_gmlp_b, pool_w=v_pool_w, pool_scale=v_pool_scale,
                   ffn2_norm=v_ffn2_norm, final_norm=v_final_norm)
    sg, sd, sm, sv = _adamw_small(gathered[0], _pack_small([small_w[nm] for nm in SMALL]),
                                  _pack_small([small_m[nm] for nm in SMALL]),
                                  _pack_small([small_v[nm] for nm in SMALL]), "adamw_small")
    like = [small_w[nm] for nm in SMALL]
    sg_l, loss = _unpack_small(sg, like)
    sd_l, _ = _unpack_small(sd, like)
    sm_l, _ = _unpack_small(sm, like)
    sv_l, _ = _unpack_small(sv, like)
    deltas, new_m, new_v = {}, {}, {}
    for i, nm in enumerate(SMALL):
        grads[nm], deltas[nm], new_m[nm], new_v[nm] = sg_l[i], sd_l[i], sm_l[i], sv_l[i]

    big_w = dict(ffn1_w_gate=ffn1_w_gate, ffn1_w_up=ffn1_w_up, ffn1_w_down=ffn1_w_down, w_in=w_in, w_out=w_out,
                 ffn2_w_gate=ffn2_w_gate, ffn2_w_up=ffn2_w_up, ffn2_w_down=ffn2_w_down)
    big_m = dict(ffn1_w_gate=m_ffn1_w_gate, ffn1_w_up=m_ffn1_w_up, ffn1_w_down=m_ffn1_w_down, w_in=m_w_in,
                 w_out=m_w_out, ffn2_w_gate=m_ffn2_w_gate, ffn2_w_up=m_ffn2_w_up, ffn2_w_down=m_ffn2_w_down)
    big_v = dict(ffn1_w_gate=v_ffn1_w_gate, ffn1_w_up=v_ffn1_w_up, ffn1_w_down=v_ffn1_w_down, w_in=v_w_in,
                 w_out=v_w_out, ffn2_w_gate=v_ffn2_w_gate, ffn2_w_up=v_ffn2_w_up, ffn2_w_down=v_ffn2_w_down)
    for nm in big_w:
        deltas[nm], new_m[nm], new_v[nm] = _adamw(big_w[nm], grads[nm], big_m[nm], big_v[nm], f"adamw_{nm}")

    order = ["ffn1_norm", "ffn1_w_gate", "ffn1_w_up", "ffn1_w_down", "mix_norm", "w_in", "attn_sinks", "gmlp_v_norm",
             "gmlp_w_s", "gmlp_b", "pool_w", "pool_scale", "w_out", "ffn2_norm", "ffn2_w_gate", "ffn2_w_up",
             "ffn2_w_down", "final_norm"]
    return (loss, grad_x, *[grads[n] for n in order], *[deltas[n] for n in order],
            *[new_m[n] for n in order], *[new_v[n] for n in order])
```

```python
import functools
import math

import jax
import jax.numpy as jnp
from jax import lax
from jax.experimental import pallas as pl
from jax.experimental.pallas import tpu as pltpu

F32 = jnp.float32
BF16 = jnp.bfloat16
MESH = pl.DeviceIdType.MESH

D = 1024
FF = 2816
INW = 1536
N_DEV = 8
DEPTH = 2
BLK = 128
HD = 64
N_HEADS = 8
N_KV = 2
REP = 4
ATTN_SCALE = HD ** -0.5
POOL_WINDOWS = (2, 4, 8, 16)
EPS = 1e-6
NEG = -1e30
FC = 256
GELU_C0 = math.sqrt(2.0 / math.pi)
GELU_C1 = 0.044715

ADAM_LR = 0.001
ADAM_B1 = 0.9
ADAM_B2 = 0.999
ADAM_EPS = 1e-08
ADAM_WD = 0.01
ADAM_STEP = 10

VMEM_LIMIT = 56 * 1024 * 1024

O_K, O_V, O_U, O_G, O_P = 512, 640, 768, 1024, 1280


def _call(body, **kw):
    return pl.pallas_call(body, **kw)


def _params(sem=None, vmem=VMEM_LIMIT):
    return pltpu.CompilerParams(dimension_semantics=sem, vmem_limit_bytes=vmem)


def _host(comm, body, *, name, grid, in_specs, out_specs, out_shape, args, scratch_shapes=(), aliases=None):
    single = not isinstance(out_shape, (list, tuple))
    out_specs_l = [out_specs] if single else list(out_specs)
    out_shape_l = [out_shape] if single else list(out_shape)
    n_in, n_out, n_scr = len(in_specs), len(out_shape_l), len(scratch_shapes)
    steps = grid[0]
    any_spec = pl.BlockSpec(memory_space=pl.ANY)

    def wrapped(*refs):
        pos = 0

        def take(n):
            nonlocal pos
            part = refs[pos:pos + n]
            pos += n
            return part

        ins = take(n_in)
        cins = [take(len(t.inputs)) for t in comm]
        outs = take(n_out)
        couts = [take(len(t.out_shape)) for t in comm]
        scr = take(n_scr)
        cscr = [take(len(t.scratch)) for t in comm]
        i = pl.program_id(0)
        for k, t in enumerate(comm):
            pl.when(i == 0)(functools.partial(t.start, cins[k], couts[k], cscr[k]))
        body(*ins, *outs, *scr)
        for k, t in enumerate(comm):
            pl.when(i == (3 * steps) // 4)(functools.partial(t.mid, cins[k], couts[k], cscr[k]))
            pl.when(i == steps - 1)(functools.partial(t.finish, cins[k], couts[k], cscr[k]))

    c_args = [a for t in comm for a in t.inputs]
    c_shapes = [sh for t in comm for sh in t.out_shape]
    c_scr = [sc for t in comm for sc in t.scratch]
    res = _call(
        wrapped, name=name, grid=grid,
        in_specs=list(in_specs) + [any_spec] * len(c_args),
        out_specs=out_specs_l + [any_spec] * len(c_shapes),
        out_shape=out_shape_l + c_shapes,
        scratch_shapes=list(scratch_shapes) + c_scr,
        input_output_aliases=aliases or {},
        compiler_params=_params(("arbitrary",)),
    )(*args, *c_args)
    outs = res[0] if single else list(res[:n_out])
    if not comm:
        return outs
    c_outs, pos = [], n_out
    for t in comm:
        c_outs.append(list(res[pos:pos + len(t.out_shape)]))
        pos += len(t.out_shape)
    return outs, c_outs


def _nn(a, b):
    return lax.dot_general(a, b, (((1,), (0,)), ((), ())), preferred_element_type=F32)


def _nt(a, b):
    return lax.dot_general(a, b, (((1,), (1,)), ((), ())), preferred_element_type=F32)


def _tn(a, b):
    return lax.dot_general(a, b, (((0,), (0,)), ((), ())), preferred_element_type=F32)


def _gelu(x):
    t = jnp.tanh(GELU_C0 * (x + GELU_C1 * x * x * x))
    return 0.5 * x * (1.0 + t), t


def _gelu_grad(x, t):
    return 0.5 * (1.0 + t) + 0.5 * x * (1.0 - t * t) * (GELU_C0 * (1.0 + 3.0 * GELU_C1 * x * x))


def _rms(x):
    r = lax.rsqrt(jnp.mean(x * x, axis=-1, keepdims=True) + EPS)
    return x * r, r


def _rms_bwd(dy, xh, r, g):
    dg = jnp.sum(dy * xh, axis=0, keepdims=True)
    dxh = dy * g
    dx = r * (dxh - xh * jnp.mean(dxh * xh, axis=-1, keepdims=True))
    return dx, dg


def _wspec(rows, m):
    return pl.BlockSpec((None, rows, D), lambda i, m=m: (m, 0, 0), pipeline_mode=pl.Buffered(1))


def _rowspec(tm, cols):
    return pl.BlockSpec((tm, cols), lambda i: (i, 0))


def _fixspec(rows, cols):
    return pl.BlockSpec((rows, cols), lambda i: (0, 0))


def _ffn_fwd(x, gain, w352, mg, name, comm=()):
    s = x.shape[0]
    tm = min(512, s)

    def body(x_ref, g_ref, wg_ref, wu_ref, wd_ref, xo_ref, p1_ref, p2_ref, hid_ref):
        xt = x_ref[...]
        xh, _ = _rms(xt)
        h = (xh * g_ref[...]).astype(BF16)
        for c in range(FF // FC):
            sl = slice(c * FC, (c + 1) * FC)
            a = _nt(h, wg_ref[sl, :])
            b = _nt(h, wu_ref[sl, :])
            sig = 0.5 * jnp.tanh(0.5 * a) + 0.5
            sa = a * sig
            p1_ref[:, sl] = (b * (sig + sa * (1.0 - sig))).astype(BF16)
            p2_ref[:, sl] = sa.astype(BF16)
            hid_ref[:, sl] = (sa * b).astype(BF16)
        xo_ref[...] = xt + 0.5 * _nn(hid_ref[...], wd_ref[...])

    act = jax.ShapeDtypeStruct((s, FF), BF16)
    return _host(
        comm, body, name=name, grid=(s // tm,),
        in_specs=[_rowspec(tm, D), _fixspec(1, D), _wspec(FF, mg), _wspec(FF, mg + 1), _wspec(FF, mg + 2)],
        out_specs=[_rowspec(tm, D), _rowspec(tm, FF), _rowspec(tm, FF), _rowspec(tm, FF)],
        out_shape=[jax.ShapeDtypeStruct((s, D), F32), act, act, act],
        args=(x, gain, w352, w352, w352))


def _ffn_bwd(x, gain, dy, p1, p2, w352, mg, name, comm=()):
    s = x.shape[0]
    tm = min(256, s)

    def body(x_ref, g_ref, dy_ref, p1_ref, p2_ref, wg_ref, wu_ref, wd_ref,
             dx_ref, da_ref, db_ref, h_ref, dyb_ref, dg_ref):
        i = pl.program_id(0)
        xt = x_ref[...]
        g = g_ref[...]
        xh, r = _rms(xt)
        h_ref[...] = (xh * g).astype(BF16)
        dyt = dy_ref[...]
        dyb = (0.5 * dyt).astype(BF16)
        dyb_ref[...] = dyb
        for c in range(FF // FC):
            sl = slice(c * FC, (c + 1) * FC)
            dhid = _nt(dyb, wd_ref[sl, :])
            da_ref[:, sl] = (dhid * p1_ref[:, sl].astype(F32)).astype(BF16)
            db_ref[:, sl] = (dhid * p2_ref[:, sl].astype(F32)).astype(BF16)
        dh = _nn(da_ref[...], wg_ref[...]) + _nn(db_ref[...], wu_ref[...])
        dxn, dg = _rms_bwd(dh, xh, r, g)
        dx_ref[...] = dyt + dxn

        @pl.when(i == 0)
        def _():
            dg_ref[...] = jnp.zeros_like(dg_ref)

        dg_ref[0:1, :] += dg

    act = jax.ShapeDtypeStruct((s, FF), BF16)
    tok = jax.ShapeDtypeStruct((s, D), BF16)
    return _host(
        comm, body, name=name, grid=(s // tm,),
        in_specs=[_rowspec(tm, D), _fixspec(1, D), _rowspec(tm, D), _rowspec(tm, FF), _rowspec(tm, FF),
                  _wspec(FF, mg), _wspec(FF, mg + 1), _wspec(FF, mg + 2)],
        out_specs=[_rowspec(tm, D), _rowspec(tm, FF), _rowspec(tm, FF),
                   _rowspec(tm, D), _rowspec(tm, D), _fixspec(8, D)],
        out_shape=[jax.ShapeDtypeStruct((s, D), F32), act, act, tok, tok, jax.ShapeDtypeStruct((8, D), F32)],
        args=(x, gain, dy, p1, p2, w352, w352, w352))


def _wgrad(a, b, g, n_slabs, m, name, comm=()):
    s, mm = a.shape
    mb = 256

    def body(*refs):
        refs[-1][...] = _tn(refs[0][...], refs[1][...])

    in_specs = [pl.BlockSpec((s, mb), lambda i: (0, i)),
                pl.BlockSpec((s, D), lambda i: (0, 0), pipeline_mode=pl.Buffered(1))]
    args = [a, b]
    aliases = {}
    if g is not None:
        in_specs.append(pl.BlockSpec(memory_space=pl.ANY))
        args.append(g)
        aliases = {2: 0}
    return _host(
        comm, body, name=name, grid=(mm // mb,),
        in_specs=in_specs,
        out_specs=pl.BlockSpec((None, mb, D), lambda i, m=m: (m, i, 0)),
        out_shape=jax.ShapeDtypeStruct((n_slabs, mm, D), F32),
        aliases=aliases, args=args)


def _mixin_fwd(x, gain, w192, l, name):
    s = x.shape[0]
    tm = min(512, s)

    def body(x_ref, g_ref, w_ref, z_ref, h_ref):
        xh, _ = _rms(x_ref[...])
        h = (xh * g_ref[...]).astype(BF16)
        h_ref[...] = h
        z_ref[...] = _nt(h, w_ref[...])

    return _call(
        body, name=name, grid=(s // tm,),
        in_specs=[_rowspec(tm, D), _fixspec(1, D), _wspec(INW, l)],
        out_specs=[_rowspec(tm, INW), _rowspec(tm, D)],
        out_shape=[jax.ShapeDtypeStruct((s, INW), F32), jax.ShapeDtypeStruct((s, D), BF16)],
        compiler_params=_params(("arbitrary",)),
    )(x, gain, w192)


def _mixin_bwd(x, gain, dz, dx_in, w192, l, name):
    s = x.shape[0]
    tm = min(256, s)

    def body(x_ref, g_ref, dz_ref, dxi_ref, w_ref, dx_ref, dg_ref):
        i = pl.program_id(0)
        g = g_ref[...]
        xh, r = _rms(x_ref[...])
        dh = _nn(dz_ref[...], w_ref[...])
        dxn, dg = _rms_bwd(dh, xh, r, g)
        dx_ref[...] = dxi_ref[...] + dxn

        @pl.when(i == 0)
        def _():
            dg_ref[...] = jnp.zeros_like(dg_ref)

        dg_ref[0:1, :] += dg

    return _call(
        body, name=name, grid=(s // tm,),
        in_specs=[_rowspec(tm, D), _fixspec(1, D), _rowspec(tm, INW), _rowspec(tm, D), _wspec(INW, l)],
        out_specs=[_rowspec(tm, D), _fixspec(8, D)],
        out_shape=[jax.ShapeDtypeStruct((s, D), F32), jax.ShapeDtypeStruct((8, D), F32)],
        compiler_params=_params(("arbitrary",)),
    )(x, gain, dz, dx_in, w192)


def _mixout_fwd(x, y, w128, l, name):
    s = x.shape[0]
    tm = min(512, s)

    def body(x_ref, y_ref, w_ref, xo_ref):
        xo_ref[...] = x_ref[...] + _nn(y_ref[...], w_ref[...])

    return _call(
        body, name=name, grid=(s // tm,),
        in_specs=[_rowspec(tm, D), _rowspec(tm, D), _wspec(D, l)],
        out_specs=_rowspec(tm, D),
        out_shape=jax.ShapeDtypeStruct((s, D), F32),
        compiler_params=_params(("arbitrary",)),
    )(x, y, w128)


def _mixout_bwd(dx, w128, l, name, comm=()):
    s = dx.shape[0]
    tm = min(512, s)

    def body(dx_ref, w_ref, dy_ref, dxb_ref):
        dxb = dx_ref[...].astype(BF16)
        dxb_ref[...] = dxb
        dy_ref[...] = _nt(dxb, w_ref[...])

    return _host(
        comm, body, name=name, grid=(s // tm,),
        in_specs=[_rowspec(tm, D), _wspec(D, l)],
        out_specs=[_rowspec(tm, D), _rowspec(tm, D)],
        out_shape=[jax.ShapeDtypeStruct((s, D), F32), jax.ShapeDtypeStruct((s, D), BF16)],
        args=(dx, w128))


def _loss_head(x, gain, tgt, name):
    s = x.shape[0]
    tm = min(512, s)

    def body(x_ref, g_ref, t_ref, dx_ref, loss_ref, dg_ref):
        i = pl.program_id(0)
        g = g_ref[...]
        xh, r = _rms(x_ref[...])
        err = xh * g - t_ref[...]
        tok = jnp.mean(err * err, axis=-1, keepdims=True)
        lp = 0.5 * jnp.sum(tok, axis=0, keepdims=True)
        dxn, dg = _rms_bwd(err * (1.0 / D), xh, r, g)
        dx_ref[...] = dxn

        @pl.when(i == 0)
        def _():
            dg_ref[...] = jnp.zeros_like(dg_ref)
            loss_ref[...] = jnp.zeros_like(loss_ref)

        dg_ref[0:1, :] += dg
        loss_ref[0:1, :] += lp + jnp.zeros((1, 128), F32)

    return _call(
        body, name=name, grid=(s // tm,),
        in_specs=[_rowspec(tm, D), _fixspec(1, D), _rowspec(tm, D)],
        out_specs=[_rowspec(tm, D), _fixspec(8, 128), _fixspec(8, D)],
        out_shape=[jax.ShapeDtypeStruct((s, D), F32), jax.ShapeDtypeStruct((8, 128), F32),
                   jax.ShapeDtypeStruct((8, D), F32)],
        compiler_params=_params(("arbitrary",)),
    )(x, gain, tgt)


MIX_NB = 2
TILE = MIX_NB * BLK
GROUP_ROWS = REP * BLK


class _Block:
    def __init__(self, n, j, zc_ref, zkvp_ref, zpp_ref):
        self.zc, self.zkvp, self.zpp = zc_ref, zkvp_ref, zpp_ref
        self.first = j == 0
        self.r = slice(j * BLK, (j + 1) * BLK)
        self.rp = slice((j - 1) * BLK, j * BLK)
        self.index = n * MIX_NB + j
        self.lo = jnp.where(n > 0, 0, BLK) if self.first else 0
        self.has_prev = jnp.where(n > 0, 1.0, 0.0) if self.first else 1.0

    def cols(self, c0, c1):
        return self.zc[self.r, c0:c1]

    def prev_kv(self, c0, c1):
        return self.zkvp[:, c0:c1] if self.first else self.zc[self.rp, O_K + c0:O_K + c1]

    def prev_p(self):
        return self.zpp[...] * self.has_prev if self.first else self.zc[self.rp, O_P:INW]


def _attn_mask(lo):
    row = lax.broadcasted_iota(jnp.int32, (GROUP_ROWS, 2 * BLK), 0) & (BLK - 1)
    col = lax.broadcasted_iota(jnp.int32, (GROUP_ROWS, 2 * BLK), 1)
    return (col > row) & (col <= row + BLK) & (col >= lo)


def _lane_head(shape):
    return lax.broadcasted_iota(jnp.int32, shape, 1) // HD


def _lane_group_select(vals):
    grp = _lane_head(vals[0].shape)
    return jnp.where(grp == 0, vals[0], jnp.where(grp == 1, vals[1], jnp.where(grp == 2, vals[2], vals[3])))


def _pool_count(index):
    row = lax.broadcasted_iota(jnp.int32, (BLK, 256), 0)
    pos1 = (index * BLK + row + 1).astype(F32)
    wl = _lane_group_select([jnp.full((BLK, 256), float(w), F32) for w in POOL_WINDOWS])
    return jnp.minimum(pos1, wl)


def _window_sums(e, forward):
    tot = e.shape[0]
    lv = e
    out = []
    for sh in (1, 2, 4, 8):
        lv = lv + pltpu.roll(lv, sh if forward else tot - sh, 0)
        out.append(lv)
    return _lane_group_select(out)


def _stack_heads(get, g):
    return jnp.concatenate([get((g * REP + rr) * HD, (g * REP + rr + 1) * HD) for rr in range(REP)], axis=0)


def _sink_column(sink_ref, g):
    return jnp.concatenate([jnp.full((BLK, 1), sink_ref[g * REP + rr], F32) for rr in range(REP)], axis=0)


def _kv_window(blk, g):
    kk = jnp.concatenate([blk.prev_kv(g * HD, (g + 1) * HD),
                          blk.cols(O_K + g * HD, O_K + (g + 1) * HD)], axis=0).astype(BF16)
    vv = jnp.concatenate([blk.prev_kv(BLK + g * HD, BLK + (g + 1) * HD),
                          blk.cols(O_V + g * HD, O_V + (g + 1) * HD)], axis=0).astype(BF16)
    return kk, vv


def _mix_common(blk, vn_ref, wcat_ref, bexp_ref, pwbd_ref):
    u, tu = _gelu(blk.cols(O_U, O_G))
    gv, tv = _gelu(blk.cols(O_G, O_P))
    xh, rv = _rms(gv)
    vnb = (xh * vn_ref[...]).astype(BF16)
    head = _lane_head((BLK, 256))
    vn_bd = jnp.concatenate([jnp.where(head == h, vnb, jnp.zeros_like(vnb)) for h in range(4)], axis=0)
    row = lax.broadcasted_iota(jnp.int32, (BLK, 4 * BLK), 0)
    col = lax.broadcasted_iota(jnp.int32, (BLK, 4 * BLK), 1) & (BLK - 1)
    tril = col <= row
    wcat = jnp.where(tril, wcat_ref[...], 0.0).astype(BF16)
    f = _nn(wcat, vn_bd) + bexp_ref[...]
    p = blk.cols(O_P, INW)
    e = jnp.concatenate([blk.prev_p(), p], axis=0)
    cnt = _pool_count(blk.index)
    diff = (_window_sums(e, True)[BLK:, :] / cnt - p).astype(BF16)
    pwbd = pwbd_ref[...].astype(BF16)
    pout = _nn(diff, pwbd)
    return dict(u=u, tu=tu, tv=tv, xh=xh, rv=rv, vn_bd=vn_bd, wcat=wcat, f=f, cnt=cnt, diff=diff, pwbd=pwbd,
                pout=pout, tril=tril, head=head)


def _mix_fwd(z, sinks, vnorm, wcat, bexp, pwbd, pscale, name):
    s = z.shape[0]
    nt = s // TILE

    def body(sink_ref, zc_ref, zkvp_ref, zpp_ref, vn_ref, wcat_ref, bexp_ref, pwbd_ref, ps_ref, y_ref, lse_ref):
        n = pl.program_id(0)
        lse_ref[...] = jnp.zeros_like(lse_ref)
        for j in range(MIX_NB):
            blk = _Block(n, j, zc_ref, zkvp_ref, zpp_ref)
            valid = _attn_mask(blk.lo)[:BLK]
            for g in range(N_KV):
                kk, vv = _kv_window(blk, g)
                for rr in range(REP):
                    h = g * REP + rr
                    qh = blk.cols(h * HD, (h + 1) * HD).astype(BF16)
                    sc = jnp.where(valid, _nt(qh, kk) * ATTN_SCALE, NEG)
                    sink = sink_ref[h]
                    m = jnp.maximum(jnp.max(sc, axis=-1, keepdims=True), sink)
                    ex = jnp.exp(sc - m)
                    den = jnp.sum(ex, axis=-1, keepdims=True) + jnp.exp(sink - m)
                    y_ref[blk.r, h * HD:(h + 1) * HD] = _nn((ex / den).astype(BF16), vv).astype(BF16)
                    lse_ref[blk.r, h:h + 1] = m + jnp.log(den)
            c = _mix_common(blk, vn_ref, wcat_ref, bexp_ref, pwbd_ref)
            y_ref[blk.r, 512:768] = (c["u"] * c["f"]).astype(BF16)
            y_ref[blk.r, 768:1024] = (c["pout"] * ps_ref[...]).astype(BF16)

    halo = lambda n: jnp.maximum(MIX_NB * n - 1, 0)
    return _call(
        body, name=name, grid=(nt,),
        in_specs=[pl.BlockSpec(memory_space=pltpu.SMEM),
                  pl.BlockSpec((TILE, INW), lambda n: (n, 0)),
                  pl.BlockSpec((BLK, 256), lambda n: (halo(n), 2)),
                  pl.BlockSpec((BLK, 256), lambda n: (halo(n), 5)),
                  _fixspec(1, 256), _fixspec(BLK, 4 * BLK), _fixspec(BLK, 256), _fixspec(256, 256), _fixspec(1, 256)],
        out_specs=[pl.BlockSpec((TILE, D), lambda n: (n, 0)), pl.BlockSpec((TILE, 128), lambda n: (n, 0))],
        out_shape=[jax.ShapeDtypeStruct((s, D), BF16), jax.ShapeDtypeStruct((s, 128), F32)],
        compiler_params=_params(("arbitrary",)),
    )(sinks, z, z, z, vnorm, wcat, bexp, pwbd, pscale)


def _mix_bwd(z, dy, lse, sinks, vnorm, wcat, bexp, pwbd, pscale, name, comm=()):
    s = z.shape[0]
    nt = s // TILE
    last = slice(TILE - BLK, TILE)

    def body(sink_ref, zc_ref, zkvp_ref, zpp_ref, dy_ref, lse_ref, vn_ref, wcat_ref, bexp_ref, pwbd_ref, ps_ref,
             dz_ref, dsink_ref, dvn_ref, dws_ref, dbt_ref, dpw_ref, dps_ref, carry_ref, ddc_ref):
        n = pl.program_id(0)

        @pl.when(n == 0)
        def _():
            carry_ref[...] = jnp.zeros_like(carry_ref)
            ddc_ref[...] = jnp.zeros_like(ddc_ref)
            dsink_ref[...] = jnp.zeros_like(dsink_ref)
            dvn_ref[...] = jnp.zeros_like(dvn_ref)
            dws_ref[...] = jnp.zeros_like(dws_ref)
            dbt_ref[...] = jnp.zeros_like(dbt_ref)
            dpw_ref[...] = jnp.zeros_like(dpw_ref)
            dps_ref[...] = jnp.zeros_like(dps_ref)

        def block_grads(j):
            blk = _Block(n, j, zc_ref, zkvp_ref, zpp_ref)
            valid = _attn_mask(blk.lo)
            out = dict(dq=[], dsink=[], dbt=[])
            dk_prev, dk_cur, dv_prev, dv_cur = [], [], [], []
            for g in range(N_KV):
                kk, vv = _kv_window(blk, g)
                q4 = _stack_heads(blk.cols, g).astype(BF16)
                do4 = _stack_heads(lambda c0, c1: dy_ref[blk.r, c0:c1], g).astype(BF16)
                lse4 = jnp.concatenate([lse_ref[blk.r, g * REP + rr:g * REP + rr + 1] for rr in range(REP)], axis=0)
                sc = jnp.where(valid, _nt(q4, kk) * ATTN_SCALE, NEG)
                pr = jnp.exp(sc - lse4)
                dp = _nt(do4, vv)
                delta = jnp.sum(pr * dp, axis=-1, keepdims=True)
                ds = ((pr * (dp - delta)) * ATTN_SCALE).astype(BF16)
                sunk = jnp.exp(_sink_column(sink_ref, g) - lse4) * delta
                dq4 = _nn(ds, kk)
                for rr in range(REP):
                    out["dsink"].append(-jnp.sum(sunk[rr * BLK:(rr + 1) * BLK], axis=0, keepdims=True))
                    out["dq"].append(dq4[rr * BLK:(rr + 1) * BLK])
                dkk = _tn(ds, q4)
                dvv = _tn(pr.astype(BF16), do4)
                dk_prev.append(dkk[:BLK]); dk_cur.append(dkk[BLK:])
                dv_prev.append(dvv[:BLK]); dv_cur.append(dvv[BLK:])
            out["dk_prev"], out["dk_cur"] = jnp.concatenate(dk_prev, axis=1), jnp.concatenate(dk_cur, axis=1)
            out["dv_prev"], out["dv_cur"] = jnp.concatenate(dv_prev, axis=1), jnp.concatenate(dv_cur, axis=1)
            c = _mix_common(blk, vn_ref, wcat_ref, bexp_ref, pwbd_ref)
            dyg = dy_ref[blk.r, 512:768]
            du = dyg * c["f"]
            df = dyg * c["u"]
            out["dzu"] = du * _gelu_grad(blk.cols(O_U, O_G), c["tu"])
            dfb = df.astype(BF16)
            for h in range(4):
                out["dbt"].append(jnp.sum(df[:, h * HD:(h + 1) * HD], axis=1, keepdims=True))
            out["dws"] = jnp.where(c["tril"], _nt(dfb, c["vn_bd"]), 0.0)
            dvn_bd = _tn(c["wcat"], dfb)
            dvn = functools.reduce(lambda a, b: a + b, [
                jnp.where(c["head"] == h, dvn_bd[h * BLK:(h + 1) * BLK], 0.0) for h in range(4)])
            dgv, out["dvn"] = _rms_bwd(dvn, c["xh"], c["rv"], vn_ref[...])
            out["dzv"] = dgv * _gelu_grad(blk.cols(O_G, O_P), c["tv"])
            dyp = dy_ref[blk.r, 768:1024]
            out["dps"] = jnp.sum(dyp * c["pout"], axis=0, keepdims=True)
            dout = (dyp * ps_ref[...]).astype(BF16)
            out["dpw"] = _tn(c["diff"], dout)
            out["ddiff"] = _nt(dout, c["pwbd"])
            out["dd"] = out["ddiff"] / c["cnt"]
            return out

        def write_previous_tile(dd_next, dk_next, dv_next):
            if MIX_NB > 1:
                dz_ref[0:TILE - BLK, :] = carry_ref[0:TILE - BLK, :].astype(BF16)
            rs = _window_sums(jnp.concatenate([ddc_ref[...], dd_next], axis=0), False)
            dz_ref[last, 0:O_K] = carry_ref[last, 0:O_K].astype(BF16)
            dz_ref[last, O_K:O_V] = (carry_ref[last, O_K:O_V] + dk_next).astype(BF16)
            dz_ref[last, O_V:O_U] = (carry_ref[last, O_V:O_U] + dv_next).astype(BF16)
            dz_ref[last, O_U:O_P] = carry_ref[last, O_U:O_P].astype(BF16)
            dz_ref[last, O_P:INW] = (carry_ref[last, O_P:INW] + rs[:BLK, :]).astype(BF16)

        @pl.when(n < nt)
        def _():
            parts = [block_grads(j) for j in range(MIX_NB)]
            total = lambda key, i=None: functools.reduce(
                lambda a, b: a + b, [p[key] if i is None else p[key][i] for p in parts])
            for h in range(N_HEADS):
                dsink_ref[h:h + 1, :] += total("dsink", h) + jnp.zeros((1, 128), F32)
            for h in range(4):
                dbt_ref[:, h:h + 1] += total("dbt", h)
            dws_ref[...] += total("dws")
            dpw_ref[...] += total("dpw")
            dvn_ref[0:1, :] += total("dvn")
            dps_ref[0:1, :] += total("dps")
            write_previous_tile(parts[0]["dd"], parts[0]["dk_prev"], parts[0]["dv_prev"])
            for j, p in enumerate(parts):
                r = slice(j * BLK, (j + 1) * BLK)
                nxt = parts[j + 1] if j + 1 < MIX_NB else None
                for h in range(N_HEADS):
                    carry_ref[r, h * HD:(h + 1) * HD] = p["dq"][h]
                carry_ref[r, O_U:O_G] = p["dzu"]
                carry_ref[r, O_G:O_P] = p["dzv"]
                if nxt is None:
                    carry_ref[r, O_K:O_V] = p["dk_cur"]
                    carry_ref[r, O_V:O_U] = p["dv_cur"]
                    carry_ref[r, O_P:INW] = -p["ddiff"]
                    ddc_ref[...] = p["dd"]
                else:
                    rs = _window_sums(jnp.concatenate([p["dd"], nxt["dd"]], axis=0), False)
                    carry_ref[r, O_K:O_V] = p["dk_cur"] + nxt["dk_prev"]
                    carry_ref[r, O_V:O_U] = p["dv_cur"] + nxt["dv_prev"]
                    carry_ref[r, O_P:INW] = rs[:BLK, :] - p["ddiff"]

        @pl.when(n == nt)
        def _():
            none = jnp.zeros((BLK, BLK), F32)
            write_previous_tile(jnp.zeros((BLK, 256), F32), none, none)

    cur = lambda n: jnp.minimum(n, nt - 1)
    halo = lambda n: jnp.maximum(MIX_NB * jnp.minimum(n, nt - 1) - 1, 0)
    return _host(
        comm, body, name=name, grid=(nt + 1,),
        in_specs=[pl.BlockSpec(memory_space=pltpu.SMEM),
                  pl.BlockSpec((TILE, INW), lambda n: (cur(n), 0)),
                  pl.BlockSpec((BLK, 256), lambda n: (halo(n), 2)),
                  pl.BlockSpec((BLK, 256), lambda n: (halo(n), 5)),
                  pl.BlockSpec((TILE, D), lambda n: (cur(n), 0)),
                  pl.BlockSpec((TILE, 128), lambda n: (cur(n), 0)),
                  _fixspec(1, 256), _fixspec(BLK, 4 * BLK), _fixspec(BLK, 256), _fixspec(256, 256), _fixspec(1, 256)],
        out_specs=[pl.BlockSpec((TILE, INW), lambda n: (jnp.maximum(n - 1, 0), 0)),
                   _fixspec(8, 128), _fixspec(8, 256), _fixspec(BLK, 4 * BLK), _fixspec(BLK, 128),
                   _fixspec(256, 256), _fixspec(8, 256)],
        out_shape=[jax.ShapeDtypeStruct((s, INW), BF16), jax.ShapeDtypeStruct((8, 128), F32),
                   jax.ShapeDtypeStruct((8, 256), F32), jax.ShapeDtypeStruct((BLK, 4 * BLK), F32),
                   jax.ShapeDtypeStruct((BLK, 128), F32), jax.ShapeDtypeStruct((256, 256), F32),
                   jax.ShapeDtypeStruct((8, 256), F32)],
        scratch_shapes=[pltpu.VMEM((TILE, INW), F32), pltpu.VMEM((BLK, 256), F32)],
        args=(sinks, z, z, z, dy, lse, vnorm, wcat, bexp, pwbd, pscale))


def _position():
    x, y, c = lax.axis_index("x"), lax.axis_index("y"), lax.axis_index("c")
    return x, y, c


class _GatherTask:
    def __init__(self, srcs):
        self.inputs = list(srcs)
        ng = len(srcs)
        self.out_shape = [jax.ShapeDtypeStruct((a.shape[0], N_DEV) + a.shape[1:], a.dtype) for a in srcs]
        self.scratch = [pltpu.SemaphoreType.DMA((ng, 7)), pltpu.SemaphoreType.DMA((ng, 7)),
                        pltpu.SemaphoreType.DMA((ng,))]

    def _plan(self, src, dst, sems):
        send_sems, recv_sems, local_sems = sems
        ng = len(src)
        x, y, c = _position()
        me, sibling = (x, y, c), (x, y, 1 - c)
        chips = [(1 - x, y), (x, 1 - y), (1 - x, 1 - y)]

        def slot(pos):
            return 4 * pos[0] + 2 * pos[1] + pos[2]

        def copy(gi, k, block, to, from_src=False):
            rows = dst[gi].at[:, slot(block)]
            return pltpu.make_async_remote_copy(
                src_ref=src[gi] if from_src else rows, dst_ref=rows,
                send_sem=send_sems.at[gi, k], recv_sem=recv_sems.at[gi, k],
                device_id=to, device_id_type=MESH)

        make = functools.partial
        mine = [make(pltpu.make_async_copy, src[gi], dst[gi].at[:, slot(me)], local_sems.at[gi]) for gi in range(ng)]
        first = []
        for gi in range(ng):
            first.append(make(copy, gi, 0, me, sibling, True))
            first += [make(copy, gi, 1 + j, me, (*chip, c), True) for j, chip in enumerate(chips)]
        passed = [make(copy, gi, 4 + j, (*chip, c), sibling) for j, chip in enumerate(chips) for gi in range(ng)]
        arrive_ici = [make(copy, gi, 1 + j, (*chip, c), me) for j, chip in enumerate(chips) for gi in range(ng)]
        arrive_d2d = [make(copy, gi, 0, sibling, me) for gi in range(ng)]
        arrive_d2d += [make(copy, gi, 4 + j, (*chip, 1 - c), me) for j, chip in enumerate(chips) for gi in range(ng)]
        return mine, first, passed, arrive_ici, arrive_d2d

    def start(self, src, dst, sems):
        mine, first, _, _, _ = self._plan(src, dst, sems)
        for cp in mine + first:
            cp().start()

    def mid(self, src, dst, sems):
        _, _, passed, arrive_ici, _ = self._plan(src, dst, sems)
        for arrived, fw in zip(arrive_ici, passed):
            arrived().wait_recv()
            fw().start()

    def finish(self, src, dst, sems):
        mine, first, passed, _, arrive_d2d = self._plan(src, dst, sems)
        for cp in arrive_d2d:
            cp().wait_recv()
        for cp in first + passed:
            cp().wait_send()
        for cp in mine:
            cp().wait()


class _SiblingTask:
    def __init__(self, g5s):
        self.inputs = list(g5s)
        ng = len(g5s)
        self.out_shape = [jax.ShapeDtypeStruct((a.shape[0], 4) + a.shape[3:], a.dtype) for a in g5s]
        self.scratch = [pltpu.SemaphoreType.DMA((ng,)), pltpu.SemaphoreType.DMA((ng,))]

    def _plan(self, src, dst, sems):
        send_sems, recv_sems = sems
        x, y, c = _position()
        return [functools.partial(
            pltpu.make_async_remote_copy,
            src_ref=src[gi].at[:, :, 1 - c], dst_ref=dst[gi],
            send_sem=send_sems.at[gi], recv_sem=recv_sems.at[gi],
            device_id=(x, y, 1 - c), device_id_type=MESH) for gi in range(len(src))]

    def start(self, src, dst, sems):
        for cp in self._plan(src, dst, sems):
            cp().start()

    def mid(self, src, dst, sems):
        pass

    def finish(self, src, dst, sems):
        for cp in self._plan(src, dst, sems):
            cp().wait()


class _ChipTask(_SiblingTask):
    def __init__(self, sbs):
        self.inputs = list(sbs)
        ng = len(sbs)
        self.out_shape = [jax.ShapeDtypeStruct(a.shape, a.dtype) for a in sbs]
        self.scratch = [pltpu.SemaphoreType.DMA((ng, 3)), pltpu.SemaphoreType.DMA((ng, 3))]

    def _plan(self, src, dst, sems):
        send_sems, recv_sems = sems
        x, y, c = _position()
        jme = 2 * x + y
        chips = [(1 - x, y), (x, 1 - y), (1 - x, 1 - y)]
        return [functools.partial(
            pltpu.make_async_remote_copy,
            src_ref=src[gi].at[:, 2 * chip[0] + chip[1]], dst_ref=dst[gi].at[:, jme],
            send_sem=send_sems.at[gi, k], recv_sem=recv_sems.at[gi, k],
            device_id=(*chip, c), device_id_type=MESH) for k, chip in enumerate(chips) for gi in range(len(src))]


def _alone(task, name):
    n_in, n_out = len(task.inputs), len(task.out_shape)

    def body(*refs):
        parts = (refs[:n_in], refs[n_in:n_in + n_out], refs[n_in + n_out:])
        task.start(*parts)
        task.mid(*parts)
        task.finish(*parts)

    any_spec = pl.BlockSpec(memory_space=pl.ANY)
    return _call(body, name=name, in_specs=[any_spec] * n_in, out_specs=[any_spec] * n_out,
                 out_shape=task.out_shape, scratch_shapes=task.scratch)(*task.inputs)


def _core_sum(ids, g5, r1, name):
    n, _, _, rows, _ = g5.shape

    def body(ids_ref, g_ref, r_ref, sb_ref, own_ref):
        j = pl.program_id(2)
        t = g_ref[...] + r_ref[...]
        sb_ref[...] = t.astype(BF16)

        @pl.when(j == ids_ref[1])
        def _():
            own_ref[...] = t

    grid_spec = pltpu.PrefetchScalarGridSpec(
        num_scalar_prefetch=1, grid=(n, 1, 4),
        in_specs=[pl.BlockSpec((None, None, None, rows, D), lambda i, t, j, ids: (i, j, ids[0], t, 0)),
                  pl.BlockSpec((None, None, rows, D), lambda i, t, j, ids: (i, j, t, 0))],
        out_specs=[pl.BlockSpec((None, None, rows, D), lambda i, t, j, ids: (i, j, t, 0)),
                   pl.BlockSpec((None, rows, D), lambda i, t, j, ids: (i, t, 0))])
    return _call(
        body, name=name, grid_spec=grid_spec,
        out_shape=[jax.ShapeDtypeStruct((n, 4, rows, D), BF16), jax.ShapeDtypeStruct((n, rows, D), F32)],
        compiler_params=_params(("arbitrary", "arbitrary", "arbitrary")),
    )(ids, g5, r1)


def _chip_sum(others, own, r2, name):
    n, rows, _ = own.shape

    def body(oth_ref, own_ref, r0_ref, r1_ref, r2_ref, out_ref):
        out_ref[...] = ((own_ref[...] + r0_ref[...].astype(F32)) + r1_ref[...].astype(F32)) \
            + r2_ref[...].astype(F32)

    def rspec(k):
        return pl.BlockSpec((None, None, rows, D), lambda i, oth, k=k: (i, oth[k], 0, 0))

    grid_spec = pltpu.PrefetchScalarGridSpec(
        num_scalar_prefetch=1, grid=(n,),
        in_specs=[pl.BlockSpec((None, rows, D), lambda i, oth: (i, 0, 0)), rspec(0), rspec(1), rspec(2)],
        out_specs=pl.BlockSpec((None, rows, D), lambda i, oth: (i, 0, 0)))
    return _call(
        body, name=name, grid_spec=grid_spec,
        out_shape=jax.ShapeDtypeStruct((n, rows, D), F32),
        compiler_params=_params(("arbitrary",)),
    )(others, own, r2, r2, r2)


def _adam_math(w, g, m, v):
    m = ADAM_B1 * m + (1.0 - ADAM_B1) * g
    v = ADAM_B2 * v + (1.0 - ADAM_B2) * (g * g)
    m_hat = m / (1.0 - ADAM_B1 ** ADAM_STEP)
    v_hat = v / (1.0 - ADAM_B2 ** ADAM_STEP)
    delta = -ADAM_LR * (m_hat / (jnp.sqrt(v_hat) + ADAM_EPS) + ADAM_WD * w)
    return delta, m, v


def _adamw(w, g, m, v, name):
    shape = w.shape
    c = shape[-1]
    r = w.size // c
    rb = max(d for d in range(8, min(r, 512) + 1, 8) if r % d == 0)

    def body(w_ref, g_ref, m_ref, v_ref, d_ref, mo_ref, vo_ref):
        d_ref[...], mo_ref[...], vo_ref[...] = _adam_math(w_ref[...], g_ref[...], m_ref[...], v_ref[...])

    spec = _rowspec(rb, c)
    outs = _call(
        body, name=name, grid=(r // rb,),
        in_specs=[spec] * 4, out_specs=[spec] * 3,
        out_shape=[jax.ShapeDtypeStruct((r, c), F32)] * 3,
        compiler_params=_params(("arbitrary",)),
    )(*[t.reshape(r, c) for t in (w, g, m, v)])
    return [o.reshape(shape) for o in outs]


def _adamw_small(parts, w, m, v, name):
    def body(p_ref, w_ref, m_ref, v_ref, g_ref, d_ref, mo_ref, vo_ref):
        g = p_ref[0]
        for dev in range(1, N_DEV):
            g = g + p_ref[dev]
        g_ref[...] = g
        d_ref[...], mo_ref[...], vo_ref[...] = _adam_math(w_ref[...], g, m_ref[...], v_ref[...])

    return _call(
        body, name=name,
        out_shape=[jax.ShapeDtypeStruct(w.shape, F32)] * 4,
        compiler_params=_params(),
    )(parts, w, m, v)


SMALL = ["ffn1_norm", "mix_norm", "attn_sinks", "gmlp_v_norm", "gmlp_w_s", "gmlp_b", "pool_w", "pool_scale",
         "ffn2_norm", "final_norm"]


def _piece_rows(size):
    return -(-size // 128)


def _pack_small(arrs, extra=None):
    pieces = []
    for a in arrs:
        f = a.reshape(-1)
        pieces.append(jnp.pad(f, (0, (-f.shape[0]) % 128)).reshape(-1, 128))
    last = jnp.zeros((1,), F32) if extra is None else extra.reshape(1)
    pieces.append(jnp.pad(last, (0, 127)).reshape(1, 128))
    rows = sum(p.shape[0] for p in pieces)
    pieces.append(jnp.zeros(((-rows) % 8, 128), F32))
    return jnp.concatenate(pieces, axis=0)


def _unpack_small(packed, like):
    out, off = [], 0
    for a in like:
        rows = _piece_rows(a.size)
        out.append(packed[off:off + rows].reshape(-1)[:a.size].reshape(a.shape))
        off += rows
    return out, packed[off, 0]


def kernel(x, ffn1_norm, ffn1_w_gate, ffn1_w_up, ffn1_w_down, mix_norm, w_in, attn_sinks, gmlp_v_norm, gmlp_w_s, gmlp_b, pool_w, pool_scale, w_out, ffn2_norm, ffn2_w_gate, ffn2_w_up, ffn2_w_down, final_norm, loss_target, m_ffn1_norm, m_ffn1_w_gate, m_ffn1_w_up, m_ffn1_w_down, m_mix_norm, m_w_in, m_attn_sinks, m_gmlp_v_norm, m_gmlp_w_s, m_gmlp_b, m_pool_w, m_pool_scale, m_w_out, m_ffn2_norm, m_ffn2_w_gate, m_ffn2_w_up, m_ffn2_w_down, m_final_norm, v_ffn1_norm, v_ffn1_w_gate, v_ffn1_w_up, v_ffn1_w_down, v_mix_norm, v_w_in, v_attn_sinks, v_gmlp_v_norm, v_gmlp_w_s, v_gmlp_b, v_pool_w, v_pool_scale, v_w_out, v_ffn2_norm, v_ffn2_w_gate, v_ffn2_w_up, v_ffn2_w_down, v_final_norm):
    s = x.shape[1]
    xi, yi, ci = _position()
    ids = jnp.stack([ci, 2 * xi + yi]).astype(jnp.int32)
    jme = 2 * xi + yi
    others = jnp.stack([k + (k >= jme).astype(jnp.int32) for k in range(3)]).astype(jnp.int32)
    t = lambda a: jnp.swapaxes(a, -1, -2)
    row = lambda a: a.reshape(1, -1)
    full = lambda a: a.reshape(a.shape[0], -1, D)

    loc_f1 = [jnp.stack([t(ffn1_w_gate[l]), t(ffn1_w_up[l]), ffn1_w_down[l]]).astype(BF16) for l in range(DEPTH)]
    loc_f2 = [jnp.stack([t(ffn2_w_gate[l]), t(ffn2_w_up[l]), ffn2_w_down[l]]).astype(BF16) for l in range(DEPTH)]
    loc_in = [t(w_in[l])[None].astype(BF16) for l in range(DEPTH)]
    loc_out = [w_out[l][None].astype(BF16) for l in range(DEPTH)]

    (wf1,) = _alone(_GatherTask([loc_f1[0]]), "gather_first")
    wf1 = full(wf1)
    xc = x.reshape(s, D)
    saved = []
    for l in range(DEPTH):
        x0 = xc
        if l == 0:
            (x1, *act1), ((wf2, win, wout),) = _ffn_fwd(
                x0, row(ffn1_norm[l]), wf1, 0, f"ffn1_fwd_{l}", comm=[_GatherTask([loc_f2[0], loc_in[0], loc_out[0]])])
        else:
            (x1, *act1), ((wf2,),) = _ffn_fwd(
                x0, row(ffn1_norm[l]), wf1, 0, f"ffn1_fwd_{l}", comm=[_GatherTask([loc_f2[1]])])
        wf2, win, wout = full(wf2), full(win), full(wout)
        z, hmix = _mixin_fwd(x1, row(mix_norm[l]), win, 0, f"mixin_fwd_{l}")
        wcat = jnp.concatenate([gmlp_w_s[l][h] for h in range(4)], axis=1)
        bexp = jnp.repeat(t(gmlp_b[l]), HD, axis=1)
        pwbd = jnp.zeros((256, 256), F32)
        for g in range(4):
            pwbd = pwbd.at[g * HD:(g + 1) * HD, g * HD:(g + 1) * HD].set(pool_w[l][g])
        mixp = (attn_sinks[l], row(gmlp_v_norm[l]), wcat, bexp, pwbd, row(pool_scale[l]))
        y, lse = _mix_fwd(z, *mixp, f"mix_fwd_{l}")
        x2 = _mixout_fwd(x1, y, wout, 0, f"mixout_fwd_{l}")
        saved.append((x0, act1, wf1, x1, z, hmix, mixp, y, lse, win, wout, x2, wf2))
        if l == 0:
            (x3, *act2), ((wf1, win, wout),) = _ffn_fwd(
                x2, row(ffn2_norm[l]), wf2, 0, f"ffn2_fwd_{l}", comm=[_GatherTask([loc_f1[1], loc_in[1], loc_out[1]])])
            wf1 = full(wf1)
        else:
            x3, *act2 = _ffn_fwd(x2, row(ffn2_norm[l]), wf2, 0, f"ffn2_fwd_{l}")
        saved[-1] = saved[-1] + (act2,)
        xc = x3
    dx, loss_part, d_final = _loss_head(xc, row(final_norm), loss_target.reshape(s, D), "loss_head")

    def five(g):
        return g.reshape(g.shape[0], 4, 2, g.shape[1] // N_DEV, D)

    def core_sums(g5s, r1s, tag):
        res = [_core_sum(ids, g5, r1, f"core_sum_{tag}_{i}") for i, (g5, r1) in enumerate(zip(g5s, r1s))]
        return [sb for sb, _ in res], [own for _, own in res]

    def chip_sums(owns, r2s, tag):
        return [_chip_sum(others, own, r2, f"chip_sum_{tag}_{i}") for i, (own, r2) in enumerate(zip(owns, r2s))]

    def mix_small(l, dsink, dvn, dws, dbt, dpw, dps):
        return {("attn_sinks", l): dsink[:, 0], ("gmlp_v_norm", l): dvn[0],
                ("gmlp_w_s", l): jnp.stack([dws[:, h * BLK:(h + 1) * BLK] for h in range(4)]),
                ("gmlp_b", l): t(dbt[:, :4]),
                ("pool_w", l): jnp.stack([dpw[g * HD:(g + 1) * HD, g * HD:(g + 1) * HD] for g in range(4)]),
                ("pool_scale", l): dps[0]}

    small = {}
    red = {}
    x0, (p11, p21, hid1), wf1, x1, z, hmix, mixp, y, lse, win, wout, x2, wf2, (p12, p22, hid2) = saved[1]
    dx, da, db, h, dyb, dg = _ffn_bwd(x2, row(ffn2_norm[1]), dx, p12, p22, wf2, 0, "ffn2_bwd_1")
    small[("ffn2_norm", 1)] = dg[0]
    g = _wgrad(da, h, None, 3, 0, "wgrad_gate2_1")
    g = _wgrad(db, h, g, 3, 1, "wgrad_up2_1")
    g = _wgrad(hid2, dyb, g, 3, 2, "wgrad_down2_1")
    a5 = [five(g)]
    (dymix, dxb), (a_r1,) = _mixout_bwd(dx, wout, 0, "mixout_bwd_1", comm=[_SiblingTask(a5)])
    a_sb, a_own = core_sums(a5, a_r1, "a")
    g_out = _wgrad(y, dxb, None, 1, 0, "wgrad_out_1")
    (dz, dsink, dvn, dws, dbt, dpw, dps), (a_r2,) = _mix_bwd(z, dymix, lse, *mixp, "mix_bwd_1", comm=[_ChipTask(a_sb)])
    (red[("f2", 1)],) = chip_sums(a_own, a_r2, "a")
    small.update(mix_small(1, dsink, dvn, dws, dbt, dpw, dps))
    dx, dg = _mixin_bwd(x1, row(mix_norm[1]), dz, dx, win, 0, "mixin_bwd_1")
    small[("mix_norm", 1)] = dg[0]
    g_in = _wgrad(dz, hmix, None, 1, 0, "wgrad_in_1")
    b5 = [five(g_out), five(g_in)]
    (dx, da, db, h, dyb, dg), (b_r1,) = _ffn_bwd(x0, row(ffn1_norm[1]), dx, p11, p21, wf1, 0, "ffn1_bwd_1",
                                                 comm=[_SiblingTask(b5)])
    small[("ffn1_norm", 1)] = dg[0]
    b_sb, b_own = core_sums(b5, b_r1, "b")
    g, (b_r2,) = _wgrad(da, h, None, 3, 0, "wgrad_gate1_1", comm=[_ChipTask(b_sb)])
    red[("out", 1)], red[("in", 1)] = chip_sums(b_own, b_r2, "b")
    g = _wgrad(db, h, g, 3, 1, "wgrad_up1_1")
    g = _wgrad(hid1, dyb, g, 3, 2, "wgrad_down1_1")
    c5 = [five(g)]
    x0, (p11, p21, hid1), wf1, x1, z, hmix, mixp, y, lse, win, wout, x2, wf2, (p12, p22, hid2) = saved[0]
    (dx, da, db, h, dyb, dg), (c_r1,) = _ffn_bwd(x2, row(ffn2_norm[0]), dx, p12, p22, wf2, 0, "ffn2_bwd_0",
                                                 comm=[_SiblingTask(c5)])
    small[("ffn2_norm", 0)] = dg[0]
    c_sb, c_own = core_sums(c5, c_r1, "c")
    g = _wgrad(da, h, None, 3, 0, "wgrad_gate2_0")
    g = _wgrad(db, h, g, 3, 1, "wgrad_up2_0")
    g = _wgrad(hid2, dyb, g, 3, 2, "wgrad_down2_0")
    d5 = [five(g)]
    (dymix, dxb), (d_r1,) = _mixout_bwd(dx, wout, 0, "mixout_bwd_0", comm=[_SiblingTask(d5)])
    d_sb, d_own = core_sums(d5, d_r1, "d")
    g_out = _wgrad(y, dxb, None, 1, 0, "wgrad_out_0")
    (dz, dsink, dvn, dws, dbt, dpw, dps), (c_r2, d_r2) = _mix_bwd(
        z, dymix, lse, *mixp, "mix_bwd_0", comm=[_ChipTask(c_sb), _ChipTask(d_sb)])
    (red[("f1", 1)],) = chip_sums(c_own, c_r2, "c")
    (red[("f2", 0)],) = chip_sums(d_own, d_r2, "d")
    small.update(mix_small(0, dsink, dvn, dws, dbt, dpw, dps))
    dx, dg = _mixin_bwd(x1, row(mix_norm[0]), dz, dx, win, 0, "mixin_bwd_0")
    small[("mix_norm", 0)] = dg[0]
    g_in = _wgrad(dz, hmix, None, 1, 0, "wgrad_in_0")
    e5 = [five(g_out), five(g_in)]
    dx, da, db, h, dyb, dg = _ffn_bwd(x0, row(ffn1_norm[0]), dx, p11, p21, wf1, 0, "ffn1_bwd_0")
    small[("ffn1_norm", 0)] = dg[0]
    grad_x = dx.reshape(1, s, D)

    part = [d_final[0] if nm == "final_norm" else jnp.stack([small[(nm, l)] for l in range(DEPTH)]) for nm in SMALL]
    packed = _pack_small(part, loss_part[0, 0])
    g_gate, (e_r1, (gathered,)) = _wgrad(da, h, None, 1, 0, "wgrad_gate1_0",
                                         comm=[_SiblingTask(e5), _GatherTask([packed[None]])])
    e_sb, e_own = core_sums(e5, e_r1, "e")
    f5 = [five(g_gate)]
    g, (e_r2, f_r1) = _wgrad(db, h, None, 2, 0, "wgrad_up1_0", comm=[_ChipTask(e_sb), _SiblingTask(f5)])
    red[("out", 0)], red[("in", 0)] = chip_sums(e_own, e_r2, "e")
    f_sb, f_own = core_sums(f5, f_r1, "f")
    g, (f_r2,) = _wgrad(hid1, dyb, g, 2, 1, "wgrad_down1_0", comm=[_ChipTask(f_sb)])
    (red_gate,) = chip_sums(f_own, f_r2, "f")
    h5 = [five(g)]
    h_r1 = _alone(_SiblingTask(h5), "reduce_sibling_last")
    h_sb, h_own = core_sums(h5, h_r1, "h")
    h_r2 = _alone(_ChipTask(h_sb), "reduce_chips_last")
    (red_updown,) = chip_sums(h_own, h_r2, "h")
    red[("f1", 0)] = jnp.concatenate([red_gate, red_updown], axis=0)

    grads = {}
    red_rows = {}
    for k, nm in enumerate(["w_gate", "w_up", "w_down"]):
        for f in ("f1", "f2"):
            red_rows[f"ffn{f[1]}_{nm}"] = jnp.stack([red[(f, l)][k] for l in range(DEPTH)])
    red_rows["w_in"] = jnp.concatenate([red[("in", l)] for l in range(DEPTH)], axis=0)
    red_rows["w_out"] = jnp.concatenate([red[("out", l)] for l in range(DEPTH)], axis=0)
    transposed = ("ffn1_w_gate", "ffn1_w_up", "ffn2_w_gate", "ffn2_w_up", "w_in")

    small_w = dict(ffn1_norm=ffn1_norm, mix_norm=mix_norm, attn_sinks=attn_sinks, gmlp_v_norm=gmlp_v_norm,
                   gmlp_w_s=gmlp_w_s, gmlp_b=gmlp_b, pool_w=pool_w, pool_scale=pool_scale, ffn2_norm=ffn2_norm,
                   final_norm=final_norm)
    small_m = dict(ffn1_norm=m_ffn1_norm, mix_norm=m_mix_norm, attn_sinks=m_attn_sinks, gmlp_v_norm=m_gmlp_v_norm,
                   gmlp_w_s=m_gmlp_w_s, gmlp_b=m_gmlp_b, pool_w=m_pool_w, pool_scale=m_pool_scale,
                   ffn2_norm=m_ffn2_norm, final_norm=m_final_norm)
    small_v = dict(ffn1_norm=v_ffn1_norm, mix_norm=v_mix_norm, attn_sinks=v_attn_sinks, gmlp_v_norm=v_gmlp_v_norm,
                   gmlp_w_s=v_gmlp_w_s, gmlp_b=v_gmlp_b, pool_w=v_pool_w, pool_scale=v_pool_scale,
                   ffn2_norm=v_ffn2_norm, final_norm=v_final_norm)
    sg, sd, sm, sv = _adamw_small(gathered[0], _pack_small([small_w[nm] for nm in SMALL]),
                                  _pack_small([small_m[nm] for nm in SMALL]),
                                  _pack_small([small_v[nm] for nm in SMALL]), "adamw_small")
    like = [small_w[nm] for nm in SMALL]
    sg_l, loss = _unpack_small(sg, like)
    sd_l, _ = _unpack_small(sd, like)
    sm_l, _ = _unpack_small(sm, like)
    sv_l, _ = _unpack_small(sv, like)
    deltas, new_m, new_v = {}, {}, {}
    for i, nm in enumerate(SMALL):
        grads[nm], deltas[nm], new_m[nm], new_v[nm] = sg_l[i], sd_l[i], sm_l[i], sv_l[i]

    big_w = dict(ffn1_w_gate=ffn1_w_gate, ffn1_w_up=ffn1_w_up, ffn1_w_down=ffn1_w_down, w_in=w_in, w_out=w_out,
                 ffn2_w_gate=ffn2_w_gate, ffn2_w_up=ffn2_w_up, ffn2_w_down=ffn2_w_down)
    big_m = dict(ffn1_w_gate=m_ffn1_w_gate, ffn1_w_up=m_ffn1_w_up, ffn1_w_down=m_ffn1_w_down, w_in=m_w_in,
                 w_out=m_w_out, ffn2_w_gate=m_ffn2_w_gate, ffn2_w_up=m_ffn2_w_up, ffn2_w_down=m_ffn2_w_down)
    big_v = dict(ffn1_w_gate=v_ffn1_w_gate, ffn1_w_up=v_ffn1_w_up, ffn1_w_down=v_ffn1_w_down, w_in=v_w_in,
                 w_out=v_w_out, ffn2_w_gate=v_ffn2_w_gate, ffn2_w_up=v_ffn2_w_up, ffn2_w_down=v_ffn2_w_down)
    for nm in big_w:
        view = t if nm in transposed else (lambda a: a)
        res = _adamw(view(big_w[nm]), red_rows[nm], view(big_m[nm]), view(big_v[nm]), f"adamw_{nm}")
        grads[nm] = view(red_rows[nm])
        deltas[nm], new_m[nm], new_v[nm] = [view(r) for r in res]

    order = ["ffn1_norm", "ffn1_w_gate", "ffn1_w_up", "ffn1_w_down", "mix_norm", "w_in", "attn_sinks", "gmlp_v_norm",
             "gmlp_w_s", "gmlp_b", "pool_w", "pool_scale", "w_out", "ffn2_norm", "ffn2_w_gate", "ffn2_w_up",
             "ffn2_w_down", "final_norm"]
    return (loss, grad_x, *[grads[n] for n in order], *[deltas[n] for n in order],
            *[new_m[n] for n in order], *[new_v[n] for n in order])
```

```python
import functools
import math

import jax
import jax.numpy as jnp
from jax import lax
from jax.experimental import pallas as pl
from jax.experimental.pallas import tpu as pltpu

F32 = jnp.float32
BF16 = jnp.bfloat16
MESH = pl.DeviceIdType.MESH

D = 1024
FF = 2816
INW = 1536
N_DEV = 8
DEPTH = 2
BLK = 128
HD = 64
N_HEADS = 8
N_KV = 2
REP = 4
ATTN_SCALE = HD ** -0.5
POOL_WINDOWS = (2, 4, 8, 16)
EPS = 1e-6
NEG = -1e30
FC = 256
GELU_C0 = math.sqrt(2.0 / math.pi)
GELU_C1 = 0.044715

ADAM_LR = 0.001
ADAM_B1 = 0.9
ADAM_B2 = 0.999
ADAM_EPS = 1e-08
ADAM_WD = 0.01
ADAM_STEP = 10

VMEM_LIMIT = 56 * 1024 * 1024

O_K, O_V, O_U, O_G, O_P = 512, 640, 768, 1024, 1280


def _call(body, **kw):
    return pl.pallas_call(body, **kw)


def _params(sem=None, vmem=VMEM_LIMIT):
    return pltpu.CompilerParams(dimension_semantics=sem, vmem_limit_bytes=vmem)


def _host(comm, body, *, name, grid, in_specs, out_specs, out_shape, args, scratch_shapes=(), aliases=None):
    single = not isinstance(out_shape, (list, tuple))
    out_specs_l = [out_specs] if single else list(out_specs)
    out_shape_l = [out_shape] if single else list(out_shape)
    n_in, n_out, n_scr = len(in_specs), len(out_shape_l), len(scratch_shapes)
    steps = grid[0]
    any_spec = pl.BlockSpec(memory_space=pl.ANY)

    def wrapped(*refs):
        pos = 0

        def take(n):
            nonlocal pos
            part = refs[pos:pos + n]
            pos += n
            return part

        ins = take(n_in)
        cins = [take(len(t.inputs)) for t in comm]
        outs = take(n_out)
        couts = [take(len(t.out_shape)) for t in comm]
        scr = take(n_scr)
        cscr = [take(len(t.scratch)) for t in comm]
        i = pl.program_id(0)
        for k, t in enumerate(comm):
            pl.when(i == 0)(functools.partial(t.start, cins[k], couts[k], cscr[k]))
        body(*ins, *outs, *scr)
        for k, t in enumerate(comm):
            pl.when(i == (3 * steps) // 4)(functools.partial(t.mid, cins[k], couts[k], cscr[k]))
            pl.when(i == steps - 1)(functools.partial(t.finish, cins[k], couts[k], cscr[k]))

    c_args = [a for t in comm for a in t.inputs]
    c_shapes = [sh for t in comm for sh in t.out_shape]
    c_scr = [sc for t in comm for sc in t.scratch]
    res = _call(
        wrapped, name=name, grid=grid,
        in_specs=list(in_specs) + [any_spec] * len(c_args),
        out_specs=out_specs_l + [any_spec] * len(c_shapes),
        out_shape=out_shape_l + c_shapes,
        scratch_shapes=list(scratch_shapes) + c_scr,
        input_output_aliases=aliases or {},
        compiler_params=_params(("arbitrary",)),
    )(*args, *c_args)
    outs = res[0] if single else list(res[:n_out])
    if not comm:
        return outs
    c_outs, pos = [], n_out
    for t in comm:
        c_outs.append(list(res[pos:pos + len(t.out_shape)]))
        pos += len(t.out_shape)
    return outs, c_outs


def _nn(a, b):
    return lax.dot_general(a, b, (((1,), (0,)), ((), ())), preferred_element_type=F32)


def _nt(a, b):
    return lax.dot_general(a, b, (((1,), (1,)), ((), ())), preferred_element_type=F32)


def _tn(a, b):
    return lax.dot_general(a, b, (((0,), (0,)), ((), ())), preferred_element_type=F32)


def _gelu(x):
    t = jnp.tanh(GELU_C0 * (x + GELU_C1 * x * x * x))
    return 0.5 * x * (1.0 + t), t


def _gelu_grad(x, t):
    return 0.5 * (1.0 + t) + 0.5 * x * (1.0 - t * t) * (GELU_C0 * (1.0 + 3.0 * GELU_C1 * x * x))


def _rms(x):
    r = lax.rsqrt(jnp.mean(x * x, axis=-1, keepdims=True) + EPS)
    return x * r, r


def _rms_bwd(dy, xh, r, g):
    dg = jnp.sum(dy * xh, axis=0, keepdims=True)
    dxh = dy * g
    dx = r * (dxh - xh * jnp.mean(dxh * xh, axis=-1, keepdims=True))
    return dx, dg


def _wspec(rows, m):
    return pl.BlockSpec((None, rows, D), lambda i, m=m: (m, 0, 0), pipeline_mode=pl.Buffered(1))


def _rowspec(tm, cols):
    return pl.BlockSpec((tm, cols), lambda i: (i, 0))


def _fixspec(rows, cols):
    return pl.BlockSpec((rows, cols), lambda i: (0, 0))


def _ffn_fwd(x, gain, w352, mg, name, comm=()):
    s = x.shape[0]
    tm = min(512, s)

    def body(x_ref, g_ref, wg_ref, wu_ref, wd_ref, xo_ref, p1_ref, p2_ref, hid_ref):
        xt = x_ref[...]
        xh, _ = _rms(xt)
        h = (xh * g_ref[...]).astype(BF16)
        for c in range(FF // FC):
            sl = slice(c * FC, (c + 1) * FC)
            a = _nt(h, wg_ref[sl, :])
            b = _nt(h, wu_ref[sl, :])
            sig = 0.5 * jnp.tanh(0.5 * a) + 0.5
            sa = a * sig
            p1_ref[:, sl] = (b * (sig + sa * (1.0 - sig))).astype(BF16)
            p2_ref[:, sl] = sa.astype(BF16)
            hid_ref[:, sl] = (sa * b).astype(BF16)
        xo_ref[...] = xt + 0.5 * _nn(hid_ref[...], wd_ref[...])

    act = jax.ShapeDtypeStruct((s, FF), BF16)
    return _host(
        comm, body, name=name, grid=(s // tm,),
        in_specs=[_rowspec(tm, D), _fixspec(1, D), _wspec(FF, mg), _wspec(FF, mg + 1), _wspec(FF, mg + 2)],
        out_specs=[_rowspec(tm, D), _rowspec(tm, FF), _rowspec(tm, FF), _rowspec(tm, FF)],
        out_shape=[jax.ShapeDtypeStruct((s, D), F32), act, act, act],
        args=(x, gain, w352, w352, w352))


def _ffn_bwd(x, gain, dy, p1, p2, w352, mg, name, comm=()):
    s = x.shape[0]
    tm = min(256, s)

    def body(x_ref, g_ref, dy_ref, p1_ref, p2_ref, wg_ref, wu_ref, wd_ref,
             dx_ref, da_ref, db_ref, h_ref, dyb_ref, dg_ref):
        i = pl.program_id(0)
        xt = x_ref[...]
        g = g_ref[...]
        xh, r = _rms(xt)
        h_ref[...] = (xh * g).astype(BF16)
        dyt = dy_ref[...]
        dyb = (0.5 * dyt).astype(BF16)
        dyb_ref[...] = dyb
        for c in range(FF // FC):
            sl = slice(c * FC, (c + 1) * FC)
            dhid = _nt(dyb, wd_ref[sl, :])
            da_ref[:, sl] = (dhid * p1_ref[:, sl].astype(F32)).astype(BF16)
            db_ref[:, sl] = (dhid * p2_ref[:, sl].astype(F32)).astype(BF16)
        dh = _nn(da_ref[...], wg_ref[...]) + _nn(db_ref[...], wu_ref[...])
        dxn, dg = _rms_bwd(dh, xh, r, g)
        dx_ref[...] = dyt + dxn

        @pl.when(i == 0)
        def _():
            dg_ref[...] = jnp.zeros_like(dg_ref)

        dg_ref[0:1, :] += dg

    act = jax.ShapeDtypeStruct((s, FF), BF16)
    tok = jax.ShapeDtypeStruct((s, D), BF16)
    return _host(
        comm, body, name=name, grid=(s // tm,),
        in_specs=[_rowspec(tm, D), _fixspec(1, D), _rowspec(tm, D), _rowspec(tm, FF), _rowspec(tm, FF),
                  _wspec(FF, mg), _wspec(FF, mg + 1), _wspec(FF, mg + 2)],
        out_specs=[_rowspec(tm, D), _rowspec(tm, FF), _rowspec(tm, FF),
                   _rowspec(tm, D), _rowspec(tm, D), _fixspec(8, D)],
        out_shape=[jax.ShapeDtypeStruct((s, D), F32), act, act, tok, tok, jax.ShapeDtypeStruct((8, D), F32)],
        args=(x, gain, dy, p1, p2, w352, w352, w352))


def _wgrad(a, b, g, n_slabs, m, name, comm=()):
    s, mm = a.shape
    mb = 256

    def body(*refs):
        refs[-1][...] = _tn(refs[0][...], refs[1][...])

    in_specs = [pl.BlockSpec((s, mb), lambda i: (0, i)),
                pl.BlockSpec((s, D), lambda i: (0, 0), pipeline_mode=pl.Buffered(1))]
    args = [a, b]
    aliases = {}
    if g is not None:
        in_specs.append(pl.BlockSpec(memory_space=pl.ANY))
        args.append(g)
        aliases = {2: 0}
    return _host(
        comm, body, name=name, grid=(mm // mb,),
        in_specs=in_specs,
        out_specs=pl.BlockSpec((None, mb, D), lambda i, m=m: (m, i, 0)),
        out_shape=jax.ShapeDtypeStruct((n_slabs, mm, D), F32),
        aliases=aliases, args=args)


def _mixin_fwd(x, gain, w192, l, name):
    s = x.shape[0]
    tm = min(512, s)

    def body(x_ref, g_ref, w_ref, z_ref, h_ref):
        xh, _ = _rms(x_ref[...])
        h = (xh * g_ref[...]).astype(BF16)
        h_ref[...] = h
        z_ref[...] = _nt(h, w_ref[...])

    return _call(
        body, name=name, grid=(s // tm,),
        in_specs=[_rowspec(tm, D), _fixspec(1, D), _wspec(INW, l)],
        out_specs=[_rowspec(tm, INW), _rowspec(tm, D)],
        out_shape=[jax.ShapeDtypeStruct((s, INW), F32), jax.ShapeDtypeStruct((s, D), BF16)],
        compiler_params=_params(("arbitrary",)),
    )(x, gain, w192)


def _mixin_bwd(x, gain, dz, dx_in, w192, l, name):
    s = x.shape[0]
    tm = min(256, s)

    def body(x_ref, g_ref, dz_ref, dxi_ref, w_ref, dx_ref, dg_ref):
        i = pl.program_id(0)
        g = g_ref[...]
        xh, r = _rms(x_ref[...])
        dh = _nn(dz_ref[...], w_ref[...])
        dxn, dg = _rms_bwd(dh, xh, r, g)
        dx_ref[...] = dxi_ref[...] + dxn

        @pl.when(i == 0)
        def _():
            dg_ref[...] = jnp.zeros_like(dg_ref)

        dg_ref[0:1, :] += dg

    return _call(
        body, name=name, grid=(s // tm,),
        in_specs=[_rowspec(tm, D), _fixspec(1, D), _rowspec(tm, INW), _rowspec(tm, D), _wspec(INW, l)],
        out_specs=[_rowspec(tm, D), _fixspec(8, D)],
        out_shape=[jax.ShapeDtypeStruct((s, D), F32), jax.ShapeDtypeStruct((8, D), F32)],
        compiler_params=_params(("arbitrary",)),
    )(x, gain, dz, dx_in, w192)


def _mixout_fwd(x, y, w128, l, name):
    s = x.shape[0]
    tm = min(512, s)

    def body(x_ref, y_ref, w_ref, xo_ref):
        xo_ref[...] = x_ref[...] + _nn(y_ref[...], w_ref[...])

    return _call(
        body, name=name, grid=(s // tm,),
        in_specs=[_rowspec(tm, D), _rowspec(tm, D), _wspec(D, l)],
        out_specs=_rowspec(tm, D),
        out_shape=jax.ShapeDtypeStruct((s, D), F32),
        compiler_params=_params(("arbitrary",)),
    )(x, y, w128)


def _mixout_bwd(dx, w128, l, name, comm=()):
    s = dx.shape[0]
    tm = min(512, s)

    def body(dx_ref, w_ref, dy_ref, dxb_ref):
        dxb = dx_ref[...].astype(BF16)
        dxb_ref[...] = dxb
        dy_ref[...] = _nt(dxb, w_ref[...])

    return _host(
        comm, body, name=name, grid=(s // tm,),
        in_specs=[_rowspec(tm, D), _wspec(D, l)],
        out_specs=[_rowspec(tm, D), _rowspec(tm, D)],
        out_shape=[jax.ShapeDtypeStruct((s, D), F32), jax.ShapeDtypeStruct((s, D), BF16)],
        args=(dx, w128))


def _loss_head(x, gain, tgt, name):
    s = x.shape[0]
    tm = min(512, s)

    def body(x_ref, g_ref, t_ref, dx_ref, loss_ref, dg_ref):
        i = pl.program_id(0)
        g = g_ref[...]
        xh, r = _rms(x_ref[...])
        err = xh * g - t_ref[...]
        tok = jnp.mean(err * err, axis=-1, keepdims=True)
        lp = 0.5 * jnp.sum(tok, axis=0, keepdims=True)
        dxn, dg = _rms_bwd(err * (1.0 / D), xh, r, g)
        dx_ref[...] = dxn

        @pl.when(i == 0)
        def _():
            dg_ref[...] = jnp.zeros_like(dg_ref)
            loss_ref[...] = jnp.zeros_like(loss_ref)

        dg_ref[0:1, :] += dg
        loss_ref[0:1, :] += lp + jnp.zeros((1, 128), F32)

    return _call(
        body, name=name, grid=(s // tm,),
        in_specs=[_rowspec(tm, D), _fixspec(1, D), _rowspec(tm, D)],
        out_specs=[_rowspec(tm, D), _fixspec(8, 128), _fixspec(8, D)],
        out_shape=[jax.ShapeDtypeStruct((s, D), F32), jax.ShapeDtypeStruct((8, 128), F32),
                   jax.ShapeDtypeStruct((8, D), F32)],
        compiler_params=_params(("arbitrary",)),
    )(x, gain, tgt)


MIX_NB = 2
TILE = MIX_NB * BLK
GROUP_ROWS = REP * BLK


class _Block:
    def __init__(self, n, j, zc_ref, zkvp_ref, zpp_ref):
        self.zc, self.zkvp, self.zpp = zc_ref, zkvp_ref, zpp_ref
        self.first = j == 0
        self.r = slice(j * BLK, (j + 1) * BLK)
        self.rp = slice((j - 1) * BLK, j * BLK)
        self.index = n * MIX_NB + j
        self.lo = jnp.where(n > 0, 0, BLK) if self.first else 0
        self.has_prev = jnp.where(n > 0, 1.0, 0.0) if self.first else 1.0

    def cols(self, c0, c1):
        return self.zc[self.r, c0:c1]

    def prev_kv(self, c0, c1):
        return self.zkvp[:, c0:c1] if self.first else self.zc[self.rp, O_K + c0:O_K + c1]

    def prev_p(self):
        return self.zpp[...] * self.has_prev if self.first else self.zc[self.rp, O_P:INW]


def _attn_mask(lo):
    row = lax.broadcasted_iota(jnp.int32, (GROUP_ROWS, 2 * BLK), 0) & (BLK - 1)
    col = lax.broadcasted_iota(jnp.int32, (GROUP_ROWS, 2 * BLK), 1)
    return (col > row) & (col <= row + BLK) & (col >= lo)


def _lane_head(shape):
    return lax.broadcasted_iota(jnp.int32, shape, 1) // HD


def _lane_group_select(vals):
    grp = _lane_head(vals[0].shape)
    return jnp.where(grp == 0, vals[0], jnp.where(grp == 1, vals[1], jnp.where(grp == 2, vals[2], vals[3])))


def _pool_count(index):
    row = lax.broadcasted_iota(jnp.int32, (BLK, 256), 0)
    pos1 = (index * BLK + row + 1).astype(F32)
    wl = _lane_group_select([jnp.full((BLK, 256), float(w), F32) for w in POOL_WINDOWS])
    return jnp.minimum(pos1, wl)


def _window_sums(e, forward):
    tot = e.shape[0]
    lv = e
    out = []
    for sh in (1, 2, 4, 8):
        lv = lv + pltpu.roll(lv, sh if forward else tot - sh, 0)
        out.append(lv)
    return _lane_group_select(out)


def _stack_heads(get, g):
    return jnp.concatenate([get((g * REP + rr) * HD, (g * REP + rr + 1) * HD) for rr in range(REP)], axis=0)


def _sink_column(sink_ref, g):
    return jnp.concatenate([jnp.full((BLK, 1), sink_ref[g * REP + rr], F32) for rr in range(REP)], axis=0)


def _kv_window(blk, g):
    kk = jnp.concatenate([blk.prev_kv(g * HD, (g + 1) * HD),
                          blk.cols(O_K + g * HD, O_K + (g + 1) * HD)], axis=0).astype(BF16)
    vv = jnp.concatenate([blk.prev_kv(BLK + g * HD, BLK + (g + 1) * HD),
                          blk.cols(O_V + g * HD, O_V + (g + 1) * HD)], axis=0).astype(BF16)
    return kk, vv


def _mix_common(blk, vn_ref, wcat_ref, bexp_ref, pwbd_ref):
    u, tu = _gelu(blk.cols(O_U, O_G))
    gv, tv = _gelu(blk.cols(O_G, O_P))
    xh, rv = _rms(gv)
    vnb = (xh * vn_ref[...]).astype(BF16)
    head = _lane_head((BLK, 256))
    vn_bd = jnp.concatenate([jnp.where(head == h, vnb, jnp.zeros_like(vnb)) for h in range(4)], axis=0)
    row = lax.broadcasted_iota(jnp.int32, (BLK, 4 * BLK), 0)
    col = lax.broadcasted_iota(jnp.int32, (BLK, 4 * BLK), 1) & (BLK - 1)
    tril = col <= row
    wcat = jnp.where(tril, wcat_ref[...], 0.0).astype(BF16)
    f = _nn(wcat, vn_bd) + bexp_ref[...]
    p = blk.cols(O_P, INW)
    e = jnp.concatenate([blk.prev_p(), p], axis=0)
    cnt = _pool_count(blk.index)
    diff = (_window_sums(e, True)[BLK:, :] / cnt - p).astype(BF16)
    pwbd = pwbd_ref[...].astype(BF16)
    pout = _nn(diff, pwbd)
    return dict(u=u, tu=tu, tv=tv, xh=xh, rv=rv, vn_bd=vn_bd, wcat=wcat, f=f, cnt=cnt, diff=diff, pwbd=pwbd,
                pout=pout, tril=tril, head=head)


def _mix_fwd(z, sinks, vnorm, wcat, bexp, pwbd, pscale, name):
    s = z.shape[0]
    nt = s // TILE

    def body(sink_ref, zc_ref, zkvp_ref, zpp_ref, vn_ref, wcat_ref, bexp_ref, pwbd_ref, ps_ref, y_ref, lse_ref):
        n = pl.program_id(0)
        lse_ref[...] = jnp.zeros_like(lse_ref)
        for j in range(MIX_NB):
            blk = _Block(n, j, zc_ref, zkvp_ref, zpp_ref)
            valid = _attn_mask(blk.lo)[:BLK]
            for g in range(N_KV):
                kk, vv = _kv_window(blk, g)
                for rr in range(REP):
                    h = g * REP + rr
                    qh = blk.cols(h * HD, (h + 1) * HD).astype(BF16)
                    sc = jnp.where(valid, _nt(qh, kk) * ATTN_SCALE, NEG)
                    sink = sink_ref[h]
                    m = jnp.maximum(jnp.max(sc, axis=-1, keepdims=True), sink)
                    ex = jnp.exp(sc - m)
                    den = jnp.sum(ex, axis=-1, keepdims=True) + jnp.exp(sink - m)
                    y_ref[blk.r, h * HD:(h + 1) * HD] = _nn((ex / den).astype(BF16), vv).astype(BF16)
                    lse_ref[blk.r, h:h + 1] = m + jnp.log(den)
            c = _mix_common(blk, vn_ref, wcat_ref, bexp_ref, pwbd_ref)
            y_ref[blk.r, 512:768] = (c["u"] * c["f"]).astype(BF16)
            y_ref[blk.r, 768:1024] = (c["pout"] * ps_ref[...]).astype(BF16)

    halo = lambda n: jnp.maximum(MIX_NB * n - 1, 0)
    return _call(
        body, name=name, grid=(nt,),
        in_specs=[pl.BlockSpec(memory_space=pltpu.SMEM),
                  pl.BlockSpec((TILE, INW), lambda n: (n, 0)),
                  pl.BlockSpec((BLK, 256), lambda n: (halo(n), 2)),
                  pl.BlockSpec((BLK, 256), lambda n: (halo(n), 5)),
                  _fixspec(1, 256), _fixspec(BLK, 4 * BLK), _fixspec(BLK, 256), _fixspec(256, 256), _fixspec(1, 256)],
        out_specs=[pl.BlockSpec((TILE, D), lambda n: (n, 0)), pl.BlockSpec((TILE, 128), lambda n: (n, 0))],
        out_shape=[jax.ShapeDtypeStruct((s, D), BF16), jax.ShapeDtypeStruct((s, 128), F32)],
        compiler_params=_params(("arbitrary",)),
    )(sinks, z, z, z, vnorm, wcat, bexp, pwbd, pscale)


def _mix_bwd(z, dy, lse, sinks, vnorm, wcat, bexp, pwbd, pscale, name, comm=()):
    s = z.shape[0]
    nt = s // TILE
    last = slice(TILE - BLK, TILE)

    def body(sink_ref, zc_ref, zkvp_ref, zpp_ref, dy_ref, lse_ref, vn_ref, wcat_ref, bexp_ref, pwbd_ref, ps_ref,
             dz_ref, dsink_ref, dvn_ref, dws_ref, dbt_ref, dpw_ref, dps_ref, carry_ref, ddc_ref):
        n = pl.program_id(0)

        @pl.when(n == 0)
        def _():
            carry_ref[...] = jnp.zeros_like(carry_ref)
            ddc_ref[...] = jnp.zeros_like(ddc_ref)
            dsink_ref[...] = jnp.zeros_like(dsink_ref)
            dvn_ref[...] = jnp.zeros_like(dvn_ref)
            dws_ref[...] = jnp.zeros_like(dws_ref)
            dbt_ref[...] = jnp.zeros_like(dbt_ref)
            dpw_ref[...] = jnp.zeros_like(dpw_ref)
            dps_ref[...] = jnp.zeros_like(dps_ref)

        def block_grads(j):
            blk = _Block(n, j, zc_ref, zkvp_ref, zpp_ref)
            valid = _attn_mask(blk.lo)
            out = dict(dq=[], dsink=[], dbt=[])
            dk_prev, dk_cur, dv_prev, dv_cur = [], [], [], []
            for g in range(N_KV):
                kk, vv = _kv_window(blk, g)
                q4 = _stack_heads(blk.cols, g).astype(BF16)
                do4 = _stack_heads(lambda c0, c1: dy_ref[blk.r, c0:c1], g).astype(BF16)
                lse4 = jnp.concatenate([lse_ref[blk.r, g * REP + rr:g * REP + rr + 1] for rr in range(REP)], axis=0)
                sc = jnp.where(valid, _nt(q4, kk) * ATTN_SCALE, NEG)
                pr = jnp.exp(sc - lse4)
                dp = _nt(do4, vv)
                delta = jnp.sum(pr * dp, axis=-1, keepdims=True)
                ds = ((pr * (dp - delta)) * ATTN_SCALE).astype(BF16)
                sunk = jnp.exp(_sink_column(sink_ref, g) - lse4) * delta
                dq4 = _nn(ds, kk)
                for rr in range(REP):
                    out["dsink"].append(-jnp.sum(sunk[rr * BLK:(rr + 1) * BLK], axis=0, keepdims=True))
                    out["dq"].append(dq4[rr * BLK:(rr + 1) * BLK])
                dkk = _tn(ds, q4)
                dvv = _tn(pr.astype(BF16), do4)
                dk_prev.append(dkk[:BLK]); dk_cur.append(dkk[BLK:])
                dv_prev.append(dvv[:BLK]); dv_cur.append(dvv[BLK:])
            out["dk_prev"], out["dk_cur"] = jnp.concatenate(dk_prev, axis=1), jnp.concatenate(dk_cur, axis=1)
            out["dv_prev"], out["dv_cur"] = jnp.concatenate(dv_prev, axis=1), jnp.concatenate(dv_cur, axis=1)
            c = _mix_common(blk, vn_ref, wcat_ref, bexp_ref, pwbd_ref)
            dyg = dy_ref[blk.r, 512:768]
            du = dyg * c["f"]
            df = dyg * c["u"]
            out["dzu"] = du * _gelu_grad(blk.cols(O_U, O_G), c["tu"])
            dfb = df.astype(BF16)
            for h in range(4):
                out["dbt"].append(jnp.sum(df[:, h * HD:(h + 1) * HD], axis=1, keepdims=True))
            out["dws"] = jnp.where(c["tril"], _nt(dfb, c["vn_bd"]), 0.0)
            dvn_bd = _tn(c["wcat"], dfb)
            dvn = functools.reduce(lambda a, b: a + b, [
                jnp.where(c["head"] == h, dvn_bd[h * BLK:(h + 1) * BLK], 0.0) for h in range(4)])
            dgv, out["dvn"] = _rms_bwd(dvn, c["xh"], c["rv"], vn_ref[...])
            out["dzv"] = dgv * _gelu_grad(blk.cols(O_G, O_P), c["tv"])
            dyp = dy_ref[blk.r, 768:1024]
            out["dps"] = jnp.sum(dyp * c["pout"], axis=0, keepdims=True)
            dout = (dyp * ps_ref[...]).astype(BF16)
            out["dpw"] = _tn(c["diff"], dout)
            out["ddiff"] = _nt(dout, c["pwbd"])
            out["dd"] = out["ddiff"] / c["cnt"]
            return out

        def write_previous_tile(dd_next, dk_next, dv_next):
            if MIX_NB > 1:
                dz_ref[0:TILE - BLK, :] = carry_ref[0:TILE - BLK, :].astype(BF16)
            rs = _window_sums(jnp.concatenate([ddc_ref[...], dd_next], axis=0), False)
            dz_ref[last, 0:O_K] = carry_ref[last, 0:O_K].astype(BF16)
            dz_ref[last, O_K:O_V] = (carry_ref[last, O_K:O_V] + dk_next).astype(BF16)
            dz_ref[last, O_V:O_U] = (carry_ref[last, O_V:O_U] + dv_next).astype(BF16)
            dz_ref[last, O_U:O_P] = carry_ref[last, O_U:O_P].astype(BF16)
            dz_ref[last, O_P:INW] = (carry_ref[last, O_P:INW] + rs[:BLK, :]).astype(BF16)

        @pl.when(n < nt)
        def _():
            parts = [block_grads(j) for j in range(MIX_NB)]
            total = lambda key, i=None: functools.reduce(
                lambda a, b: a + b, [p[key] if i is None else p[key][i] for p in parts])
            for h in range(N_HEADS):
                dsink_ref[h:h + 1, :] += total("dsink", h) + jnp.zeros((1, 128), F32)
            for h in range(4):
                dbt_ref[:, h:h + 1] += total("dbt", h)
            dws_ref[...] += total("dws")
            dpw_ref[...] += total("dpw")
            dvn_ref[0:1, :] += total("dvn")
            dps_ref[0:1, :] += total("dps")
            write_previous_tile(parts[0]["dd"], parts[0]["dk_prev"], parts[0]["dv_prev"])
            for j, p in enumerate(parts):
                r = slice(j * BLK, (j + 1) * BLK)
                nxt = parts[j + 1] if j + 1 < MIX_NB else None
                for h in range(N_HEADS):
                    carry_ref[r, h * HD:(h + 1) * HD] = p["dq"][h]
                carry_ref[r, O_U:O_G] = p["dzu"]
                carry_ref[r, O_G:O_P] = p["dzv"]
                if nxt is None:
                    carry_ref[r, O_K:O_V] = p["dk_cur"]
                    carry_ref[r, O_V:O_U] = p["dv_cur"]
                    carry_ref[r, O_P:INW] = -p["ddiff"]
                    ddc_ref[...] = p["dd"]
                else:
                    rs = _window_sums(jnp.concatenate([p["dd"], nxt["dd"]], axis=0), False)
                    carry_ref[r, O_K:O_V] = p["dk_cur"] + nxt["dk_prev"]
                    carry_ref[r, O_V:O_U] = p["dv_cur"] + nxt["dv_prev"]
                    carry_ref[r, O_P:INW] = rs[:BLK, :] - p["ddiff"]

        @pl.when(n == nt)
        def _():
            none = jnp.zeros((BLK, BLK), F32)
            write_previous_tile(jnp.zeros((BLK, 256), F32), none, none)

    cur = lambda n: jnp.minimum(n, nt - 1)
    halo = lambda n: jnp.maximum(MIX_NB * jnp.minimum(n, nt - 1) - 1, 0)
    return _host(
        comm, body, name=name, grid=(nt + 1,),
        in_specs=[pl.BlockSpec(memory_space=pltpu.SMEM),
                  pl.BlockSpec((TILE, INW), lambda n: (cur(n), 0)),
                  pl.BlockSpec((BLK, 256), lambda n: (halo(n), 2)),
                  pl.BlockSpec((BLK, 256), lambda n: (halo(n), 5)),
                  pl.BlockSpec((TILE, D), lambda n: (cur(n), 0)),
                  pl.BlockSpec((TILE, 128), lambda n: (cur(n), 0)),
                  _fixspec(1, 256), _fixspec(BLK, 4 * BLK), _fixspec(BLK, 256), _fixspec(256, 256), _fixspec(1, 256)],
        out_specs=[pl.BlockSpec((TILE, INW), lambda n: (jnp.maximum(n - 1, 0), 0)),
                   _fixspec(8, 128), _fixspec(8, 256), _fixspec(BLK, 4 * BLK), _fixspec(BLK, 128),
                   _fixspec(256, 256), _fixspec(8, 256)],
        out_shape=[jax.ShapeDtypeStruct((s, INW), BF16), jax.ShapeDtypeStruct((8, 128), F32),
                   jax.ShapeDtypeStruct((8, 256), F32), jax.ShapeDtypeStruct((BLK, 4 * BLK), F32),
                   jax.ShapeDtypeStruct((BLK, 128), F32), jax.ShapeDtypeStruct((256, 256), F32),
                   jax.ShapeDtypeStruct((8, 256), F32)],
        scratch_shapes=[pltpu.VMEM((TILE, INW), F32), pltpu.VMEM((BLK, 256), F32)],
        args=(sinks, z, z, z, dy, lse, vnorm, wcat, bexp, pwbd, pscale))


def _position():
    x, y, c = lax.axis_index("x"), lax.axis_index("y"), lax.axis_index("c")
    return x, y, c


class _GatherTask:
    def __init__(self, srcs):
        self.inputs = list(srcs)
        ng = len(srcs)
        self.out_shape = [jax.ShapeDtypeStruct((a.shape[0], N_DEV) + a.shape[1:], a.dtype) for a in srcs]
        self.scratch = [pltpu.SemaphoreType.DMA((ng, 7)), pltpu.SemaphoreType.DMA((ng, 7)),
                        pltpu.SemaphoreType.DMA((ng,))]

    def _plan(self, src, dst, sems):
        send_sems, recv_sems, local_sems = sems
        ng = len(src)
        x, y, c = _position()
        me, sibling = (x, y, c), (x, y, 1 - c)
        chips = [(1 - x, y), (x, 1 - y), (1 - x, 1 - y)]

        def slot(pos):
            return 4 * pos[0] + 2 * pos[1] + pos[2]

        def copy(gi, k, block, to, from_src=False):
            rows = dst[gi].at[:, slot(block)]
            return pltpu.make_async_remote_copy(
                src_ref=src[gi] if from_src else rows, dst_ref=rows,
                send_sem=send_sems.at[gi, k], recv_sem=recv_sems.at[gi, k],
                device_id=to, device_id_type=MESH)

        make = functools.partial
        mine = [make(pltpu.make_async_copy, src[gi], dst[gi].at[:, slot(me)], local_sems.at[gi]) for gi in range(ng)]
        first = []
        for gi in range(ng):
            first.append(make(copy, gi, 0, me, sibling, True))
            first += [make(copy, gi, 1 + j, me, (*chip, c), True) for j, chip in enumerate(chips)]
        passed = [make(copy, gi, 4 + j, (*chip, c), sibling) for j, chip in enumerate(chips) for gi in range(ng)]
        arrive_ici = [make(copy, gi, 1 + j, (*chip, c), me) for j, chip in enumerate(chips) for gi in range(ng)]
        arrive_d2d = [make(copy, gi, 0, sibling, me) for gi in range(ng)]
        arrive_d2d += [make(copy, gi, 4 + j, (*chip, 1 - c), me) for j, chip in enumerate(chips) for gi in range(ng)]
        return mine, first, passed, arrive_ici, arrive_d2d

    def start(self, src, dst, sems):
        mine, first, _, _, _ = self._plan(src, dst, sems)
        for cp in mine + first:
            cp().start()

    def mid(self, src, dst, sems):
        _, _, passed, arrive_ici, _ = self._plan(src, dst, sems)
        for arrived, fw in zip(arrive_ici, passed):
            arrived().wait_recv()
            fw().start()

    def finish(self, src, dst, sems):
        mine, first, passed, _, arrive_d2d = self._plan(src, dst, sems)
        for cp in arrive_d2d:
            cp().wait_recv()
        for cp in first + passed:
            cp().wait_send()
        for cp in mine:
            cp().wait()


class _SiblingTask:
    def __init__(self, g5s):
        self.inputs = list(g5s)
        ng = len(g5s)
        self.out_shape = [jax.ShapeDtypeStruct((a.shape[0], 4) + a.shape[3:], a.dtype) for a in g5s]
        self.scratch = [pltpu.SemaphoreType.DMA((ng,)), pltpu.SemaphoreType.DMA((ng,))]

    def _plan(self, src, dst, sems):
        send_sems, recv_sems = sems
        x, y, c = _position()
        return [functools.partial(
            pltpu.make_async_remote_copy,
            src_ref=src[gi].at[:, :, 1 - c], dst_ref=dst[gi],
            send_sem=send_sems.at[gi], recv_sem=recv_sems.at[gi],
            device_id=(x, y, 1 - c), device_id_type=MESH) for gi in range(len(src))]

    def start(self, src, dst, sems):
        for cp in self._plan(src, dst, sems):
            cp().start()

    def mid(self, src, dst, sems):
        pass

    def finish(self, src, dst, sems):
        for cp in self._plan(src, dst, sems):
            cp().wait()


class _ChipTask(_SiblingTask):
    def __init__(self, sbs):
        self.inputs = list(sbs)
        ng = len(sbs)
        self.out_shape = [jax.ShapeDtypeStruct(a.shape, a.dtype) for a in sbs]
        self.scratch = [pltpu.SemaphoreType.DMA((ng, 3)), pltpu.SemaphoreType.DMA((ng, 3))]

    def _plan(self, src, dst, sems):
        send_sems, recv_sems = sems
        x, y, c = _position()
        jme = 2 * x + y
        chips = [(1 - x, y), (x, 1 - y), (1 - x, 1 - y)]
        return [functools.partial(
            pltpu.make_async_remote_copy,
            src_ref=src[gi].at[:, 2 * chip[0] + chip[1]], dst_ref=dst[gi].at[:, jme],
            send_sem=send_sems.at[gi, k], recv_sem=recv_sems.at[gi, k],
            device_id=(*chip, c), device_id_type=MESH) for k, chip in enumerate(chips) for gi in range(len(src))]


def _alone(task, name):
    n_in, n_out = len(task.inputs), len(task.out_shape)

    def body(*refs):
        parts = (refs[:n_in], refs[n_in:n_in + n_out], refs[n_in + n_out:])
        task.start(*parts)
        task.mid(*parts)
        task.finish(*parts)

    any_spec = pl.BlockSpec(memory_space=pl.ANY)
    return _call(body, name=name, in_specs=[any_spec] * n_in, out_specs=[any_spec] * n_out,
                 out_shape=task.out_shape, scratch_shapes=task.scratch)(*task.inputs)


def _core_sum(ids, g5, r1, name):
    n, _, _, rows, _ = g5.shape

    def body(ids_ref, g_ref, r_ref, sb_ref, own_ref):
        j = pl.program_id(2)
        t = g_ref[...] + r_ref[...]
        sb_ref[...] = t.astype(BF16)

        @pl.when(j == ids_ref[1])
        def _():
            own_ref[...] = t

    grid_spec = pltpu.PrefetchScalarGridSpec(
        num_scalar_prefetch=1, grid=(n, 1, 4),
        in_specs=[pl.BlockSpec((None, None, None, rows, D), lambda i, t, j, ids: (i, j, ids[0], t, 0)),
                  pl.BlockSpec((None, None, rows, D), lambda i, t, j, ids: (i, j, t, 0))],
        out_specs=[pl.BlockSpec((None, None, rows, D), lambda i, t, j, ids: (i, j, t, 0)),
                   pl.BlockSpec((None, rows, D), lambda i, t, j, ids: (i, t, 0))])
    return _call(
        body, name=name, grid_spec=grid_spec,
        out_shape=[jax.ShapeDtypeStruct((n, 4, rows, D), BF16), jax.ShapeDtypeStruct((n, rows, D), F32)],
        compiler_params=_params(("arbitrary", "arbitrary", "arbitrary")),
    )(ids, g5, r1)


def _chip_sum(others, own, r2, name):
    n, rows, _ = own.shape

    def body(oth_ref, own_ref, r0_ref, r1_ref, r2_ref, out_ref):
        out_ref[...] = ((own_ref[...] + r0_ref[...].astype(F32)) + r1_ref[...].astype(F32)) \
            + r2_ref[...].astype(F32)

    def rspec(k):
        return pl.BlockSpec((None, None, rows, D), lambda i, oth, k=k: (i, oth[k], 0, 0))

    grid_spec = pltpu.PrefetchScalarGridSpec(
        num_scalar_prefetch=1, grid=(n,),
        in_specs=[pl.BlockSpec((None, rows, D), lambda i, oth: (i, 0, 0)), rspec(0), rspec(1), rspec(2)],
        out_specs=pl.BlockSpec((None, rows, D), lambda i, oth: (i, 0, 0)))
    return _call(
        body, name=name, grid_spec=grid_spec,
        out_shape=jax.ShapeDtypeStruct((n, rows, D), F32),
        compiler_params=_params(("arbitrary",)),
    )(others, own, r2, r2, r2)


def _adam_math(w, g, m, v):
    m = ADAM_B1 * m + (1.0 - ADAM_B1) * g
    v = ADAM_B2 * v + (1.0 - ADAM_B2) * (g * g)
    m_hat = m / (1.0 - ADAM_B1 ** ADAM_STEP)
    v_hat = v / (1.0 - ADAM_B2 ** ADAM_STEP)
    delta = -ADAM_LR * (m_hat / (jnp.sqrt(v_hat) + ADAM_EPS) + ADAM_WD * w)
    return delta, m, v


def _adamw(w, g, m, v, name):
    shape = w.shape
    c = shape[-1]
    r = w.size // c
    rb = max(d for d in range(8, min(r, 512) + 1, 8) if r % d == 0)

    def body(w_ref, g_ref, m_ref, v_ref, d_ref, mo_ref, vo_ref):
        d_ref[...], mo_ref[...], vo_ref[...] = _adam_math(w_ref[...], g_ref[...], m_ref[...], v_ref[...])

    spec = _rowspec(rb, c)
    outs = _call(
        body, name=name, grid=(r // rb,),
        in_specs=[spec] * 4, out_specs=[spec] * 3,
        out_shape=[jax.ShapeDtypeStruct((r, c), F32)] * 3,
        compiler_params=_params(("arbitrary",)),
    )(*[t.reshape(r, c) for t in (w, g, m, v)])
    return [o.reshape(shape) for o in outs]


def _adamw_small(parts, w, m, v, name):
    def body(p_ref, w_ref, m_ref, v_ref, g_ref, d_ref, mo_ref, vo_ref):
        g = p_ref[0]
        for dev in range(1, N_DEV):
            g = g + p_ref[dev]
        g_ref[...] = g
        d_ref[...], mo_ref[...], vo_ref[...] = _adam_math(w_ref[...], g, m_ref[...], v_ref[...])

    return _call(
        body, name=name,
        out_shape=[jax.ShapeDtypeStruct(w.shape, F32)] * 4,
        compiler_params=_params(),
    )(parts, w, m, v)


SMALL = ["ffn1_norm", "mix_norm", "attn_sinks", "gmlp_v_norm", "gmlp_w_s", "gmlp_b", "pool_w", "pool_scale",
         "ffn2_norm", "final_norm"]


def _piece_rows(size):
    return -(-size // 1024) * 8


def _pack_small(arrs, extra=None):
    pieces = []
    for a in list(arrs) + [jnp.zeros((1,), F32) if extra is None else extra]:
        f = a.reshape(-1)
        pieces.append(jnp.pad(f, (0, _piece_rows(f.shape[0]) * 128 - f.shape[0])).reshape(-1, 128))
    return jnp.concatenate(pieces, axis=0)


def _unpack_small(packed, like):
    out, off = [], 0
    for a in like:
        rows = _piece_rows(a.size)
        out.append(packed[off:off + rows].reshape(-1)[:a.size].reshape(a.shape))
        off += rows
    return out, packed[off, 0]


def kernel(x, ffn1_norm, ffn1_w_gate, ffn1_w_up, ffn1_w_down, mix_norm, w_in, attn_sinks, gmlp_v_norm, gmlp_w_s, gmlp_b, pool_w, pool_scale, w_out, ffn2_norm, ffn2_w_gate, ffn2_w_up, ffn2_w_down, final_norm, loss_target, m_ffn1_norm, m_ffn1_w_gate, m_ffn1_w_up, m_ffn1_w_down, m_mix_norm, m_w_in, m_attn_sinks, m_gmlp_v_norm, m_gmlp_w_s, m_gmlp_b, m_pool_w, m_pool_scale, m_w_out, m_ffn2_norm, m_ffn2_w_gate, m_ffn2_w_up, m_ffn2_w_down, m_final_norm, v_ffn1_norm, v_ffn1_w_gate, v_ffn1_w_up, v_ffn1_w_down, v_mix_norm, v_w_in, v_attn_sinks, v_gmlp_v_norm, v_gmlp_w_s, v_gmlp_b, v_pool_w, v_pool_scale, v_w_out, v_ffn2_norm, v_ffn2_w_gate, v_ffn2_w_up, v_ffn2_w_down, v_final_norm):
    s = x.shape[1]
    xi, yi, ci = _position()
    ids = jnp.stack([ci, 2 * xi + yi]).astype(jnp.int32)
    jme = 2 * xi + yi
    others = jnp.stack([k + (k >= jme).astype(jnp.int32) for k in range(3)]).astype(jnp.int32)
    t = lambda a: jnp.swapaxes(a, -1, -2)
    row = lambda a: a.reshape(1, -1)
    full = lambda a: a.reshape(a.shape[0], -1, D)

    loc_f1 = [jnp.stack([t(ffn1_w_gate[l]), t(ffn1_w_up[l]), ffn1_w_down[l]]).astype(BF16) for l in range(DEPTH)]
    loc_f2 = [jnp.stack([t(ffn2_w_gate[l]), t(ffn2_w_up[l]), ffn2_w_down[l]]).astype(BF16) for l in range(DEPTH)]
    loc_in = [t(w_in[l])[None].astype(BF16) for l in range(DEPTH)]
    loc_out = [w_out[l][None].astype(BF16) for l in range(DEPTH)]

    (wf1,) = _alone(_GatherTask([loc_f1[0]]), "gather_first")
    wf1 = full(wf1)
    xc = x.reshape(s, D)
    saved = []
    for l in range(DEPTH):
        x0 = xc
        if l == 0:
            (x1, *act1), ((wf2, win, wout),) = _ffn_fwd(
                x0, row(ffn1_norm[l]), wf1, 0, f"ffn1_fwd_{l}", comm=[_GatherTask([loc_f2[0], loc_in[0], loc_out[0]])])
        else:
            (x1, *act1), ((wf2,),) = _ffn_fwd(
                x0, row(ffn1_norm[l]), wf1, 0, f"ffn1_fwd_{l}", comm=[_GatherTask([loc_f2[1]])])
        wf2, win, wout = full(wf2), full(win), full(wout)
        z, hmix = _mixin_fwd(x1, row(mix_norm[l]), win, 0, f"mixin_fwd_{l}")
        wcat = jnp.concatenate([gmlp_w_s[l][h] for h in range(4)], axis=1)
        bexp = jnp.repeat(t(gmlp_b[l]), HD, axis=1)
        pwbd = jnp.zeros((256, 256), F32)
        for g in range(4):
            pwbd = pwbd.at[g * HD:(g + 1) * HD, g * HD:(g + 1) * HD].set(pool_w[l][g])
        mixp = (attn_sinks[l], row(gmlp_v_norm[l]), wcat, bexp, pwbd, row(pool_scale[l]))
        y, lse = _mix_fwd(z, *mixp, f"mix_fwd_{l}")
        x2 = _mixout_fwd(x1, y, wout, 0, f"mixout_fwd_{l}")
        saved.append((x0, act1, wf1, x1, z, hmix, mixp, y, lse, win, wout, x2, wf2))
        if l == 0:
            (x3, *act2), ((wf1, win, wout),) = _ffn_fwd(
                x2, row(ffn2_norm[l]), wf2, 0, f"ffn2_fwd_{l}", comm=[_GatherTask([loc_f1[1], loc_in[1], loc_out[1]])])
            wf1 = full(wf1)
        else:
            x3, *act2 = _ffn_fwd(x2, row(ffn2_norm[l]), wf2, 0, f"ffn2_fwd_{l}")
        saved[-1] = saved[-1] + (act2,)
        xc = x3
    dx, loss_part, d_final = _loss_head(xc, row(final_norm), loss_target.reshape(s, D), "loss_head")

    def five(g):
        return g.reshape(g.shape[0], 4, 2, g.shape[1] // N_DEV, D)

    def core_sums(g5s, r1s, tag):
        res = [_core_sum(ids, g5, r1, f"core_sum_{tag}_{i}") for i, (g5, r1) in enumerate(zip(g5s, r1s))]
        return [sb for sb, _ in res], [own for _, own in res]

    def chip_sums(owns, r2s, tag):
        return [_chip_sum(others, own, r2, f"chip_sum_{tag}_{i}") for i, (own, r2) in enumerate(zip(owns, r2s))]

    def mix_small(l, dsink, dvn, dws, dbt, dpw, dps):
        return {("attn_sinks", l): dsink[:, 0], ("gmlp_v_norm", l): dvn[0],
                ("gmlp_w_s", l): jnp.stack([dws[:, h * BLK:(h + 1) * BLK] for h in range(4)]),
                ("gmlp_b", l): t(dbt[:, :4]),
                ("pool_w", l): jnp.stack([dpw[g * HD:(g + 1) * HD, g * HD:(g + 1) * HD] for g in range(4)]),
                ("pool_scale", l): dps[0]}

    small = {}
    red = {}
    x0, (p11, p21, hid1), wf1, x1, z, hmix, mixp, y, lse, win, wout, x2, wf2, (p12, p22, hid2) = saved[1]
    dx, da, db, h, dyb, dg = _ffn_bwd(x2, row(ffn2_norm[1]), dx, p12, p22, wf2, 0, "ffn2_bwd_1")
    small[("ffn2_norm", 1)] = dg[0]
    g = _wgrad(da, h, None, 3, 0, "wgrad_gate2_1")
    g = _wgrad(db, h, g, 3, 1, "wgrad_up2_1")
    g = _wgrad(hid2, dyb, g, 3, 2, "wgrad_down2_1")
    a5 = [five(g)]
    (dymix, dxb), (a_r1,) = _mixout_bwd(dx, wout, 0, "mixout_bwd_1", comm=[_SiblingTask(a5)])
    a_sb, a_own = core_sums(a5, a_r1, "a")
    g_out = _wgrad(y, dxb, None, 1, 0, "wgrad_out_1")
    (dz, dsink, dvn, dws, dbt, dpw, dps), (a_r2,) = _mix_bwd(z, dymix, lse, *mixp, "mix_bwd_1", comm=[_ChipTask(a_sb)])
    (red[("f2", 1)],) = chip_sums(a_own, a_r2, "a")
    small.update(mix_small(1, dsink, dvn, dws, dbt, dpw, dps))
    dx, dg = _mixin_bwd(x1, row(mix_norm[1]), dz, dx, win, 0, "mixin_bwd_1")
    small[("mix_norm", 1)] = dg[0]
    g_in = _wgrad(dz, hmix, None, 1, 0, "wgrad_in_1")
    b5 = [five(g_out), five(g_in)]
    (dx, da, db, h, dyb, dg), (b_r1,) = _ffn_bwd(x0, row(ffn1_norm[1]), dx, p11, p21, wf1, 0, "ffn1_bwd_1",
                                                 comm=[_SiblingTask(b5)])
    small[("ffn1_norm", 1)] = dg[0]
    b_sb, b_own = core_sums(b5, b_r1, "b")
    g_gate, (b_r2,) = _wgrad(da, h, None, 1, 0, "wgrad_gate1_1", comm=[_ChipTask(b_sb)])
    red[("out", 1)], red[("in", 1)] = chip_sums(b_own, b_r2, "b")
    g_up = _wgrad(db, h, None, 1, 0, "wgrad_up1_1")
    g_down = _wgrad(hid1, dyb, None, 1, 0, "wgrad_down1_1")
    c5 = [five(g_gate), five(g_up), five(g_down)]
    x0, (p11, p21, hid1), wf1, x1, z, hmix, mixp, y, lse, win, wout, x2, wf2, (p12, p22, hid2) = saved[0]
    (dx, da, db, h, dyb, dg), (c_r1,) = _ffn_bwd(x2, row(ffn2_norm[0]), dx, p12, p22, wf2, 0, "ffn2_bwd_0",
                                                 comm=[_SiblingTask(c5)])
    small[("ffn2_norm", 0)] = dg[0]
    c_sb, c_own = core_sums(c5, c_r1, "c")
    g, (c_r2a,) = _wgrad(da, h, None, 3, 0, "wgrad_gate2_0", comm=[_ChipTask(c_sb[0:1])])
    g, (c_r2b,) = _wgrad(db, h, g, 3, 1, "wgrad_up2_0", comm=[_ChipTask(c_sb[1:2])])
    g, (c_r2c,) = _wgrad(hid2, dyb, g, 3, 2, "wgrad_down2_0", comm=[_ChipTask(c_sb[2:3])])
    red[("f1", 1)] = jnp.concatenate(chip_sums(c_own, c_r2a + c_r2b + c_r2c, "c"), axis=0)
    d5 = [five(g)]
    (dymix, dxb), (d_r1,) = _mixout_bwd(dx, wout, 0, "mixout_bwd_0", comm=[_SiblingTask(d5)])
    d_sb, d_own = core_sums(d5, d_r1, "d")
    (dz, dsink, dvn, dws, dbt, dpw, dps), (d_r2,) = _mix_bwd(z, dymix, lse, *mixp, "mix_bwd_0", comm=[_ChipTask(d_sb)])
    (red[("f2", 0)],) = chip_sums(d_own, d_r2, "d")
    small.update(mix_small(0, dsink, dvn, dws, dbt, dpw, dps))
    dx, dg = _mixin_bwd(x1, row(mix_norm[0]), dz, dx, win, 0, "mixin_bwd_0")
    small[("mix_norm", 0)] = dg[0]
    dx, da, db, h, dyb, dg = _ffn_bwd(x0, row(ffn1_norm[0]), dx, p11, p21, wf1, 0, "ffn1_bwd_0")
    small[("ffn1_norm", 0)] = dg[0]
    grad_x = dx.reshape(1, s, D)

    part = [d_final[0] if nm == "final_norm" else jnp.stack([small[(nm, l)] for l in range(DEPTH)]) for nm in SMALL]
    packed = _pack_small(part, loss_part[0, 0])
    g_gate, ((gathered,),) = _wgrad(da, h, None, 1, 0, "wgrad_gate1_0", comm=[_GatherTask([packed[None]])])
    f5 = [five(g_gate)]
    g_up, (f_r1,) = _wgrad(db, h, None, 1, 0, "wgrad_up1_0", comm=[_SiblingTask(f5)])
    f_sb, f_own = core_sums(f5, f_r1, "f")
    u5 = [five(g_up)]
    g_down, (f_r2, u_r1) = _wgrad(hid1, dyb, None, 1, 0, "wgrad_down1_0", comm=[_ChipTask(f_sb), _SiblingTask(u5)])
    u_sb, u_own = core_sums(u5, u_r1, "u")
    w5 = [five(g_down)]
    g_out, (u_r2, w_r1) = _wgrad(y, dxb, None, 1, 0, "wgrad_out_0", comm=[_ChipTask(u_sb), _SiblingTask(w5)])
    w_sb, w_own = core_sums(w5, w_r1, "w")
    o5 = [five(g_out)]
    g_in, (w_r2, o_r1) = _wgrad(dz, hmix, None, 1, 0, "wgrad_in_0", comm=[_ChipTask(w_sb), _SiblingTask(o5)])
    red[("f1", 0)] = jnp.concatenate(chip_sums(f_own + u_own + w_own, f_r2 + u_r2 + w_r2, "f"), axis=0)
    o_sb, o_own = core_sums(o5, o_r1, "o")
    i5 = [five(g_in)]
    i_r1 = _alone(_SiblingTask(i5), "reduce_sibling_last")
    i_sb, i_own = core_sums(i5, i_r1, "i")
    e_r2 = _alone(_ChipTask(o_sb + i_sb), "reduce_chips_last")
    red[("out", 0)], red[("in", 0)] = chip_sums(o_own + i_own, e_r2, "e")

    grads = {}
    red_rows = {}
    for k, nm in enumerate(["w_gate", "w_up", "w_down"]):
        for f in ("f1", "f2"):
            red_rows[f"ffn{f[1]}_{nm}"] = jnp.stack([red[(f, l)][k] for l in range(DEPTH)])
    red_rows["w_in"] = jnp.concatenate([red[("in", l)] for l in range(DEPTH)], axis=0)
    red_rows["w_out"] = jnp.concatenate([red[("out", l)] for l in range(DEPTH)], axis=0)
    transposed = ("ffn1_w_gate", "ffn1_w_up", "ffn2_w_gate", "ffn2_w_up", "w_in")

    small_w = dict(ffn1_norm=ffn1_norm, mix_norm=mix_norm, attn_sinks=attn_sinks, gmlp_v_norm=gmlp_v_norm,
                   gmlp_w_s=gmlp_w_s, gmlp_b=gmlp_b, pool_w=pool_w, pool_scale=pool_scale, ffn2_norm=ffn2_norm,
                   final_norm=final_norm)
    small_m = dict(ffn1_norm=m_ffn1_norm, mix_norm=m_mix_norm, attn_sinks=m_attn_sinks, gmlp_v_norm=m_gmlp_v_norm,
                   gmlp_w_s=m_gmlp_w_s, gmlp_b=m_gmlp_b, pool_w=m_pool_w, pool_scale=m_pool_scale,
                   ffn2_norm=m_ffn2_norm, final_norm=m_final_norm)
    small_v = dict(ffn1_norm=v_ffn1_norm, mix_norm=v_mix_norm, attn_sinks=v_attn_sinks, gmlp_v_norm=v_gmlp_v_norm,
                   gmlp_w_s=v_gmlp_w_s, gmlp_b=v_gmlp_b, pool_w=v_pool_w, pool_scale=v_pool_scale,
                   ffn2_norm=v_ffn2_norm, final_norm=v_final_norm)
    sg, sd, sm, sv = _adamw_small(gathered[0], _pack_small([small_w[nm] for nm in SMALL]),
                                  _pack_small([small_m[nm] for nm in SMALL]),
                                  _pack_small([small_v[nm] for nm in SMALL]), "adamw_small")
    like = [small_w[nm] for nm in SMALL]
    sg_l, loss = _unpack_small(sg, like)
    sd_l, _ = _unpack_small(sd, like)
    sm_l, _ = _unpack_small(sm, like)
    sv_l, _ = _unpack_small(sv, like)
    deltas, new_m, new_v = {}, {}, {}
    for i, nm in enumerate(SMALL):
        grads[nm], deltas[nm], new_m[nm], new_v[nm] = sg_l[i], sd_l[i], sm_l[i], sv_l[i]

    big_w = dict(ffn1_w_gate=ffn1_w_gate, ffn1_w_up=ffn1_w_up, ffn1_w_down=ffn1_w_down, w_in=w_in, w_out=w_out,
                 ffn2_w_gate=ffn2_w_gate, ffn2_w_up=ffn2_w_up, ffn2_w_down=ffn2_w_down)
    big_m = dict(ffn1_w_gate=m_ffn1_w_gate, ffn1_w_up=m_ffn1_w_up, ffn1_w_down=m_ffn1_w_down, w_in=m_w_in,
                 w_out=m_w_out, ffn2_w_gate=m_ffn2_w_gate, ffn2_w_up=m_ffn2_w_up, ffn2_w_down=m_ffn2_w_down)
    big_v = dict(ffn1_w_gate=v_ffn1_w_gate, ffn1_w_up=v_ffn1_w_up, ffn1_w_down=v_ffn1_w_down, w_in=v_w_in,
                 w_out=v_w_out, ffn2_w_gate=v_ffn2_w_gate, ffn2_w_up=v_ffn2_w_up, ffn2_w_down=v_ffn2_w_down)
    for nm in big_w:
        view = t if nm in transposed else (lambda a: a)
        res = _adamw(view(big_w[nm]), red_rows[nm], view(big_m[nm]), view(big_v[nm]), f"adamw_{nm}")
        grads[nm] = view(red_rows[nm])
        deltas[nm], new_m[nm], new_v[nm] = [view(r) for r in res]

    order = ["ffn1_norm", "ffn1_w_gate", "ffn1_w_up", "ffn1_w_down", "mix_norm", "w_in", "attn_sinks", "gmlp_v_norm",
             "gmlp_w_s", "gmlp_b", "pool_w", "pool_scale", "w_out", "ffn2_norm", "ffn2_w_gate", "ffn2_w_up",
             "ffn2_w_down", "final_norm"]
    return (loss, grad_x, *[grads[n] for n in order], *[deltas[n] for n in order],
            *[new_m[n] for n in order], *[new_v[n] for n in order])
```

```python
import functools
import math

import jax
import jax.numpy as jnp
from jax import lax
from jax.experimental import pallas as pl
from jax.experimental.pallas import tpu as pltpu

F32 = jnp.float32
BF16 = jnp.bfloat16
MESH = pl.DeviceIdType.MESH

D = 1024
FF = 2816
INW = 1536
N_DEV = 8
DEPTH = 2
BLK = 128
HD = 64
N_HEADS = 8
N_KV = 2
REP = 4
ATTN_SCALE = HD ** -0.5
POOL_WINDOWS = (2, 4, 8, 16)
EPS = 1e-6
NEG = -1e30
FC = 256
GELU_C0 = math.sqrt(2.0 / math.pi)
GELU_C1 = 0.044715

ADAM_LR = 0.001
ADAM_B1 = 0.9
ADAM_B2 = 0.999
ADAM_EPS = 1e-08
ADAM_WD = 0.01
ADAM_STEP = 10

VMEM_LIMIT = 56 * 1024 * 1024

O_K, O_V, O_U, O_G, O_P = 512, 640, 768, 1024, 1280


def _call(body, **kw):
    return pl.pallas_call(body, **kw)


def _params(sem=None, vmem=VMEM_LIMIT):
    return pltpu.CompilerParams(dimension_semantics=sem, vmem_limit_bytes=vmem)


def _host(comm, body, *, name, grid, in_specs, out_specs, out_shape, args, scratch_shapes=(), aliases=None):
    single = not isinstance(out_shape, (list, tuple))
    out_specs_l = [out_specs] if single else list(out_specs)
    out_shape_l = [out_shape] if single else list(out_shape)
    n_in, n_out, n_scr = len(in_specs), len(out_shape_l), len(scratch_shapes)
    steps = grid[0]
    any_spec = pl.BlockSpec(memory_space=pl.ANY)

    def wrapped(*refs):
        pos = 0

        def take(n):
            nonlocal pos
            part = refs[pos:pos + n]
            pos += n
            return part

        ins = take(n_in)
        cins = [take(len(t.inputs)) for t in comm]
        outs = take(n_out)
        couts = [take(len(t.out_shape)) for t in comm]
        scr = take(n_scr)
        cscr = [take(len(t.scratch)) for t in comm]
        i = pl.program_id(0)
        for k, t in enumerate(comm):
            pl.when(i == 0)(functools.partial(t.start, cins[k], couts[k], cscr[k]))
        body(*ins, *outs, *scr)
        for k, t in enumerate(comm):
            pl.when(i == (3 * steps) // 4)(functools.partial(t.mid, cins[k], couts[k], cscr[k]))
            pl.when(i == steps - 1)(functools.partial(t.finish, cins[k], couts[k], cscr[k]))

    c_args = [a for t in comm for a in t.inputs]
    c_shapes = [sh for t in comm for sh in t.out_shape]
    c_scr = [sc for t in comm for sc in t.scratch]
    res = _call(
        wrapped, name=name, grid=grid,
        in_specs=list(in_specs) + [any_spec] * len(c_args),
        out_specs=out_specs_l + [any_spec] * len(c_shapes),
        out_shape=out_shape_l + c_shapes,
        scratch_shapes=list(scratch_shapes) + c_scr,
        input_output_aliases=aliases or {},
        compiler_params=_params(("arbitrary",)),
    )(*args, *c_args)
    outs = res[0] if single else list(res[:n_out])
    if not comm:
        return outs
    c_outs, pos = [], n_out
    for t in comm:
        c_outs.append(list(res[pos:pos + len(t.out_shape)]))
        pos += len(t.out_shape)
    return outs, c_outs


def _nn(a, b):
    return lax.dot_general(a, b, (((1,), (0,)), ((), ())), preferred_element_type=F32)


def _nt(a, b):
    return lax.dot_general(a, b, (((1,), (1,)), ((), ())), preferred_element_type=F32)


def _tn(a, b):
    return lax.dot_general(a, b, (((0,), (0,)), ((), ())), preferred_element_type=F32)


def _gelu(x):
    t = jnp.tanh(GELU_C0 * (x + GELU_C1 * x * x * x))
    return 0.5 * x * (1.0 + t), t


def _gelu_grad(x, t):
    return 0.5 * (1.0 + t) + 0.5 * x * (1.0 - t * t) * (GELU_C0 * (1.0 + 3.0 * GELU_C1 * x * x))


def _rms(x):
    r = lax.rsqrt(jnp.mean(x * x, axis=-1, keepdims=True) + EPS)
    return x * r, r


def _rms_bwd(dy, xh, r, g):
    dg = jnp.sum(dy * xh, axis=0, keepdims=True)
    dxh = dy * g
    dx = r * (dxh - xh * jnp.mean(dxh * xh, axis=-1, keepdims=True))
    return dx, dg


def _wspec(rows, m):
    return pl.BlockSpec((None, rows, D), lambda i, m=m: (m, 0, 0), pipeline_mode=pl.Buffered(1))


def _rowspec(tm, cols):
    return pl.BlockSpec((tm, cols), lambda i: (i, 0))


def _fixspec(rows, cols):
    return pl.BlockSpec((rows, cols), lambda i: (0, 0))


def _ffn_fwd(x, gain, w352, mg, name, comm=(), mixer=None):
    s = x.shape[0]
    tm = min(512, s)

    def body(*refs):
        if mixer is None:
            x_ref, g_ref, wg_ref, wu_ref, wd_ref, xo_ref, p1_ref, p2_ref, hid_ref = refs
            xt = x_ref[...]
        else:
            x_ref, g_ref, wg_ref, wu_ref, wd_ref, y_ref, wo_ref, xo_ref, p1_ref, p2_ref, hid_ref, xin_ref = refs
            xt = x_ref[...] + _nn(y_ref[...], wo_ref[...])
            xin_ref[...] = xt
        xh, _ = _rms(xt)
        h = (xh * g_ref[...]).astype(BF16)
        for c in range(FF // FC):
            sl = slice(c * FC, (c + 1) * FC)
            a = _nt(h, wg_ref[sl, :])
            b = _nt(h, wu_ref[sl, :])
            sig = 0.5 * jnp.tanh(0.5 * a) + 0.5
            sa = a * sig
            p1_ref[:, sl] = (b * (sig + sa * (1.0 - sig))).astype(BF16)
            p2_ref[:, sl] = sa.astype(BF16)
            hid_ref[:, sl] = (sa * b).astype(BF16)
        xo_ref[...] = xt + 0.5 * _nn(hid_ref[...], wd_ref[...])

    act = jax.ShapeDtypeStruct((s, FF), BF16)
    tok = jax.ShapeDtypeStruct((s, D), F32)
    extra = mixer is not None
    return _host(
        comm, body, name=name, grid=(s // tm,),
        in_specs=[_rowspec(tm, D), _fixspec(1, D), _wspec(FF, mg), _wspec(FF, mg + 1), _wspec(FF, mg + 2)]
        + ([_rowspec(tm, D), _wspec(D, 0)] if extra else []),
        out_specs=[_rowspec(tm, D), _rowspec(tm, FF), _rowspec(tm, FF), _rowspec(tm, FF)]
        + ([_rowspec(tm, D)] if extra else []),
        out_shape=[tok, act, act, act] + ([tok] if extra else []),
        args=(x, gain, w352, w352, w352) + (tuple(mixer) if extra else ()))


def _ffn_bwd(x, gain, dy, p1, p2, w352, mg, name, comm=()):
    s = x.shape[0]
    tm = min(256, s)

    def body(x_ref, g_ref, dy_ref, p1_ref, p2_ref, wg_ref, wu_ref, wd_ref,
             dx_ref, da_ref, db_ref, h_ref, dyb_ref, dg_ref):
        i = pl.program_id(0)
        xt = x_ref[...]
        g = g_ref[...]
        xh, r = _rms(xt)
        h_ref[...] = (xh * g).astype(BF16)
        dyt = dy_ref[...]
        dyb = (0.5 * dyt).astype(BF16)
        dyb_ref[...] = dyb
        for c in range(FF // FC):
            sl = slice(c * FC, (c + 1) * FC)
            dhid = _nt(dyb, wd_ref[sl, :])
            da_ref[:, sl] = (dhid * p1_ref[:, sl].astype(F32)).astype(BF16)
            db_ref[:, sl] = (dhid * p2_ref[:, sl].astype(F32)).astype(BF16)
        dh = _nn(da_ref[...], wg_ref[...]) + _nn(db_ref[...], wu_ref[...])
        dxn, dg = _rms_bwd(dh, xh, r, g)
        dx_ref[...] = dyt + dxn

        @pl.when(i == 0)
        def _():
            dg_ref[...] = jnp.zeros_like(dg_ref)

        dg_ref[0:1, :] += dg

    act = jax.ShapeDtypeStruct((s, FF), BF16)
    tok = jax.ShapeDtypeStruct((s, D), BF16)
    return _host(
        comm, body, name=name, grid=(s // tm,),
        in_specs=[_rowspec(tm, D), _fixspec(1, D), _rowspec(tm, D), _rowspec(tm, FF), _rowspec(tm, FF),
                  _wspec(FF, mg), _wspec(FF, mg + 1), _wspec(FF, mg + 2)],
        out_specs=[_rowspec(tm, D), _rowspec(tm, FF), _rowspec(tm, FF),
                   _rowspec(tm, D), _rowspec(tm, D), _fixspec(8, D)],
        out_shape=[jax.ShapeDtypeStruct((s, D), F32), act, act, tok, tok, jax.ShapeDtypeStruct((8, D), F32)],
        args=(x, gain, dy, p1, p2, w352, w352, w352))


def _wgrad(a, b, g, n_slabs, m, name, comm=()):
    s, mm = a.shape
    mb = 256

    def body(*refs):
        refs[-1][...] = _tn(refs[0][...], refs[1][...])

    in_specs = [pl.BlockSpec((s, mb), lambda i: (0, i)),
                pl.BlockSpec((s, D), lambda i: (0, 0), pipeline_mode=pl.Buffered(1))]
    args = [a, b]
    aliases = {}
    if g is not None:
        in_specs.append(pl.BlockSpec(memory_space=pl.ANY))
        args.append(g)
        aliases = {2: 0}
    return _host(
        comm, body, name=name, grid=(mm // mb,),
        in_specs=in_specs,
        out_specs=pl.BlockSpec((None, mb, D), lambda i, m=m: (m, i, 0)),
        out_shape=jax.ShapeDtypeStruct((n_slabs, mm, D), F32),
        aliases=aliases, args=args)


def _mixin_fwd(x, gain, w192, l, name):
    s = x.shape[0]
    tm = min(512, s)

    def body(x_ref, g_ref, w_ref, z_ref, h_ref):
        xh, _ = _rms(x_ref[...])
        h = (xh * g_ref[...]).astype(BF16)
        h_ref[...] = h
        z_ref[...] = _nt(h, w_ref[...])

    return _call(
        body, name=name, grid=(s // tm,),
        in_specs=[_rowspec(tm, D), _fixspec(1, D), _wspec(INW, l)],
        out_specs=[_rowspec(tm, INW), _rowspec(tm, D)],
        out_shape=[jax.ShapeDtypeStruct((s, INW), F32), jax.ShapeDtypeStruct((s, D), BF16)],
        compiler_params=_params(("arbitrary",)),
    )(x, gain, w192)


def _mixin_bwd(x, gain, dz, dx_in, w192, l, name):
    s = x.shape[0]
    tm = min(512, s)

    def body(x_ref, g_ref, dz_ref, dxi_ref, w_ref, dx_ref, dg_ref):
        i = pl.program_id(0)
        g = g_ref[...]
        xh, r = _rms(x_ref[...])
        dh = _nn(dz_ref[...], w_ref[...])
        dxn, dg = _rms_bwd(dh, xh, r, g)
        dx_ref[...] = dxi_ref[...] + dxn

        @pl.when(i == 0)
        def _():
            dg_ref[...] = jnp.zeros_like(dg_ref)

        dg_ref[0:1, :] += dg

    return _call(
        body, name=name, grid=(s // tm,),
        in_specs=[_rowspec(tm, D), _fixspec(1, D), _rowspec(tm, INW), _rowspec(tm, D), _wspec(INW, l)],
        out_specs=[_rowspec(tm, D), _fixspec(8, D)],
        out_shape=[jax.ShapeDtypeStruct((s, D), F32), jax.ShapeDtypeStruct((8, D), F32)],
        compiler_params=_params(("arbitrary",)),
    )(x, gain, dz, dx_in, w192)


def _mixout_bwd(dx, w128, l, name, comm=()):
    s = dx.shape[0]
    tm = min(512, s)

    def body(dx_ref, w_ref, dy_ref, dxb_ref):
        dxb = dx_ref[...].astype(BF16)
        dxb_ref[...] = dxb
        dy_ref[...] = _nt(dxb, w_ref[...])

    return _host(
        comm, body, name=name, grid=(s // tm,),
        in_specs=[_rowspec(tm, D), _wspec(D, l)],
        out_specs=[_rowspec(tm, D), _rowspec(tm, D)],
        out_shape=[jax.ShapeDtypeStruct((s, D), F32), jax.ShapeDtypeStruct((s, D), BF16)],
        args=(dx, w128))


def _loss_head(x, gain, tgt, name):
    s = x.shape[0]
    tm = min(512, s)

    def body(x_ref, g_ref, t_ref, dx_ref, loss_ref, dg_ref):
        i = pl.program_id(0)
        g = g_ref[...]
        xh, r = _rms(x_ref[...])
        err = xh * g - t_ref[...]
        tok = jnp.mean(err * err, axis=-1, keepdims=True)
        lp = 0.5 * jnp.sum(tok, axis=0, keepdims=True)
        dxn, dg = _rms_bwd(err * (1.0 / D), xh, r, g)
        dx_ref[...] = dxn

        @pl.when(i == 0)
        def _():
            dg_ref[...] = jnp.zeros_like(dg_ref)
            loss_ref[...] = jnp.zeros_like(loss_ref)

        dg_ref[0:1, :] += dg
        loss_ref[0:1, :] += lp + jnp.zeros((1, 128), F32)

    return _call(
        body, name=name, grid=(s // tm,),
        in_specs=[_rowspec(tm, D), _fixspec(1, D), _rowspec(tm, D)],
        out_specs=[_rowspec(tm, D), _fixspec(8, 128), _fixspec(8, D)],
        out_shape=[jax.ShapeDtypeStruct((s, D), F32), jax.ShapeDtypeStruct((8, 128), F32),
                   jax.ShapeDtypeStruct((8, D), F32)],
        compiler_params=_params(("arbitrary",)),
    )(x, gain, tgt)


MIX_NB = 4
TILE = MIX_NB * BLK
GROUP_ROWS = REP * BLK


class _Block:
    def __init__(self, n, j, zc_ref, zkvp_ref, zpp_ref):
        self.zc, self.zkvp, self.zpp = zc_ref, zkvp_ref, zpp_ref
        self.first = j == 0
        self.r = slice(j * BLK, (j + 1) * BLK)
        self.rp = slice((j - 1) * BLK, j * BLK)
        self.index = n * MIX_NB + j
        self.lo = jnp.where(n > 0, 0, BLK) if self.first else 0
        self.has_prev = jnp.where(n > 0, 1.0, 0.0) if self.first else 1.0

    def cols(self, c0, c1):
        return self.zc[self.r, c0:c1]

    def prev_kv(self, c0, c1):
        return self.zkvp[:, c0:c1] if self.first else self.zc[self.rp, O_K + c0:O_K + c1]

    def prev_p(self):
        return self.zpp[...] * self.has_prev if self.first else self.zc[self.rp, O_P:INW]


def _attn_mask(lo):
    row = lax.broadcasted_iota(jnp.int32, (GROUP_ROWS, 2 * BLK), 0) & (BLK - 1)
    col = lax.broadcasted_iota(jnp.int32, (GROUP_ROWS, 2 * BLK), 1)
    return (col > row) & (col <= row + BLK) & (col >= lo)


def _lane_head(shape):
    return lax.broadcasted_iota(jnp.int32, shape, 1) // HD


def _lane_group_select(vals):
    grp = _lane_head(vals[0].shape)
    return jnp.where(grp == 0, vals[0], jnp.where(grp == 1, vals[1], jnp.where(grp == 2, vals[2], vals[3])))


def _pool_count(index):
    row = lax.broadcasted_iota(jnp.int32, (BLK, 256), 0)
    pos1 = (index * BLK + row + 1).astype(F32)
    wl = _lane_group_select([jnp.full((BLK, 256), float(w), F32) for w in POOL_WINDOWS])
    return jnp.minimum(pos1, wl)


def _window_sums(e, forward):
    tot = e.shape[0]
    lv = e
    out = []
    for sh in (1, 2, 4, 8):
        lv = lv + pltpu.roll(lv, sh if forward else tot - sh, 0)
        out.append(lv)
    return _lane_group_select(out)


def _stack_heads(get, g):
    return jnp.concatenate([get((g * REP + rr) * HD, (g * REP + rr + 1) * HD) for rr in range(REP)], axis=0)


def _sink_column(sink_ref, g):
    return jnp.concatenate([jnp.full((BLK, 1), sink_ref[g * REP + rr], F32) for rr in range(REP)], axis=0)


def _kv_window(blk, g):
    kk = jnp.concatenate([blk.prev_kv(g * HD, (g + 1) * HD),
                          blk.cols(O_K + g * HD, O_K + (g + 1) * HD)], axis=0).astype(BF16)
    vv = jnp.concatenate([blk.prev_kv(BLK + g * HD, BLK + (g + 1) * HD),
                          blk.cols(O_V + g * HD, O_V + (g + 1) * HD)], axis=0).astype(BF16)
    return kk, vv


def _mix_common(blk, vn_ref, wcat_ref, bexp_ref, pwbd_ref):
    u, tu = _gelu(blk.cols(O_U, O_G))
    gv, tv = _gelu(blk.cols(O_G, O_P))
    xh, rv = _rms(gv)
    vnb = (xh * vn_ref[...]).astype(BF16)
    head = _lane_head((BLK, 256))
    vn_bd = jnp.concatenate([jnp.where(head == h, vnb, jnp.zeros_like(vnb)) for h in range(4)], axis=0)
    row = lax.broadcasted_iota(jnp.int32, (BLK, 4 * BLK), 0)
    col = lax.broadcasted_iota(jnp.int32, (BLK, 4 * BLK), 1) & (BLK - 1)
    tril = col <= row
    wcat = jnp.where(tril, wcat_ref[...], 0.0).astype(BF16)
    f = _nn(wcat, vn_bd) + bexp_ref[...]
    p = blk.cols(O_P, INW)
    e = jnp.concatenate([blk.prev_p(), p], axis=0)
    cnt = _pool_count(blk.index)
    diff = (_window_sums(e, True)[BLK:, :] / cnt - p).astype(BF16)
    pwbd = pwbd_ref[...].astype(BF16)
    pout = _nn(diff, pwbd)
    return dict(u=u, tu=tu, tv=tv, xh=xh, rv=rv, vn_bd=vn_bd, wcat=wcat, f=f, cnt=cnt, diff=diff, pwbd=pwbd,
                pout=pout, tril=tril, head=head)


def _mix_fwd(z, sinks, vnorm, wcat, bexp, pwbd, pscale, name):
    s = z.shape[0]
    nt = s // TILE

    def body(sink_ref, zc_ref, zkvp_ref, zpp_ref, vn_ref, wcat_ref, bexp_ref, pwbd_ref, ps_ref, y_ref, lse_ref):
        n = pl.program_id(0)
        lse_ref[...] = jnp.zeros_like(lse_ref)
        for j in range(MIX_NB):
            blk = _Block(n, j, zc_ref, zkvp_ref, zpp_ref)
            valid = _attn_mask(blk.lo)[:BLK]
            for g in range(N_KV):
                kk, vv = _kv_window(blk, g)
                for rr in range(REP):
                    h = g * REP + rr
                    qh = blk.cols(h * HD, (h + 1) * HD).astype(BF16)
                    sc = jnp.where(valid, _nt(qh, kk) * ATTN_SCALE, NEG)
                    sink = sink_ref[h]
                    m = jnp.maximum(jnp.max(sc, axis=-1, keepdims=True), sink)
                    ex = jnp.exp(sc - m)
                    den = jnp.sum(ex, axis=-1, keepdims=True) + jnp.exp(sink - m)
                    y_ref[blk.r, h * HD:(h + 1) * HD] = _nn((ex / den).astype(BF16), vv).astype(BF16)
                    lse_ref[blk.r, h:h + 1] = m + jnp.log(den)
            c = _mix_common(blk, vn_ref, wcat_ref, bexp_ref, pwbd_ref)
            y_ref[blk.r, 512:768] = (c["u"] * c["f"]).astype(BF16)
            y_ref[blk.r, 768:1024] = (c["pout"] * ps_ref[...]).astype(BF16)

    halo = lambda n: jnp.maximum(MIX_NB * n - 1, 0)
    return _call(
        body, name=name, grid=(nt,),
        in_specs=[pl.BlockSpec(memory_space=pltpu.SMEM),
                  pl.BlockSpec((TILE, INW), lambda n: (n, 0)),
                  pl.BlockSpec((BLK, 256), lambda n: (halo(n), 2)),
                  pl.BlockSpec((BLK, 256), lambda n: (halo(n), 5)),
                  _fixspec(1, 256), _fixspec(BLK, 4 * BLK), _fixspec(BLK, 256), _fixspec(256, 256), _fixspec(1, 256)],
        out_specs=[pl.BlockSpec((TILE, D), lambda n: (n, 0)), pl.BlockSpec((TILE, 128), lambda n: (n, 0))],
        out_shape=[jax.ShapeDtypeStruct((s, D), BF16), jax.ShapeDtypeStruct((s, 128), F32)],
        compiler_params=_params(("arbitrary",)),
    )(sinks, z, z, z, vnorm, wcat, bexp, pwbd, pscale)


def _mix_bwd(z, dy, lse, sinks, vnorm, wcat, bexp, pwbd, pscale, name, comm=()):
    s = z.shape[0]
    nt = s // TILE
    last = slice(TILE - BLK, TILE)

    def body(sink_ref, zc_ref, zkvp_ref, zpp_ref, dy_ref, lse_ref, vn_ref, wcat_ref, bexp_ref, pwbd_ref, ps_ref,
             dz_ref, dsink_ref, dvn_ref, dws_ref, dbt_ref, dpw_ref, dps_ref, carry_ref, ddc_ref):
        n = pl.program_id(0)

        @pl.when(n == 0)
        def _():
            carry_ref[...] = jnp.zeros_like(carry_ref)
            ddc_ref[...] = jnp.zeros_like(ddc_ref)
            dsink_ref[...] = jnp.zeros_like(dsink_ref)
            dvn_ref[...] = jnp.zeros_like(dvn_ref)
            dws_ref[...] = jnp.zeros_like(dws_ref)
            dbt_ref[...] = jnp.zeros_like(dbt_ref)
            dpw_ref[...] = jnp.zeros_like(dpw_ref)
            dps_ref[...] = jnp.zeros_like(dps_ref)

        def block_grads(j):
            blk = _Block(n, j, zc_ref, zkvp_ref, zpp_ref)
            valid = _attn_mask(blk.lo)
            out = dict(dq=[], dsink=[], dbt=[])
            dk_prev, dk_cur, dv_prev, dv_cur = [], [], [], []
            for g in range(N_KV):
                kk, vv = _kv_window(blk, g)
                q4 = _stack_heads(blk.cols, g).astype(BF16)
                do4 = _stack_heads(lambda c0, c1: dy_ref[blk.r, c0:c1], g).astype(BF16)
                lse4 = jnp.concatenate([lse_ref[blk.r, g * REP + rr:g * REP + rr + 1] for rr in range(REP)], axis=0)
                sc = jnp.where(valid, _nt(q4, kk) * ATTN_SCALE, NEG)
                pr = jnp.exp(sc - lse4)
                dp = _nt(do4, vv)
                delta = jnp.sum(pr * dp, axis=-1, keepdims=True)
                ds = ((pr * (dp - delta)) * ATTN_SCALE).astype(BF16)
                sunk = jnp.exp(_sink_column(sink_ref, g) - lse4) * delta
                dq4 = _nn(ds, kk)
                for rr in range(REP):
                    out["dsink"].append(-jnp.sum(sunk[rr * BLK:(rr + 1) * BLK], axis=0, keepdims=True))
                    out["dq"].append(dq4[rr * BLK:(rr + 1) * BLK])
                dkk = _tn(ds, q4)
                dvv = _tn(pr.astype(BF16), do4)
                dk_prev.append(dkk[:BLK]); dk_cur.append(dkk[BLK:])
                dv_prev.append(dvv[:BLK]); dv_cur.append(dvv[BLK:])
            out["dk_prev"], out["dk_cur"] = jnp.concatenate(dk_prev, axis=1), jnp.concatenate(dk_cur, axis=1)
            out["dv_prev"], out["dv_cur"] = jnp.concatenate(dv_prev, axis=1), jnp.concatenate(dv_cur, axis=1)
            c = _mix_common(blk, vn_ref, wcat_ref, bexp_ref, pwbd_ref)
            dyg = dy_ref[blk.r, 512:768]
            du = dyg * c["f"]
            df = dyg * c["u"]
            out["dzu"] = du * _gelu_grad(blk.cols(O_U, O_G), c["tu"])
            dfb = df.astype(BF16)
            for h in range(4):
                out["dbt"].append(jnp.sum(df[:, h * HD:(h + 1) * HD], axis=1, keepdims=True))
            out["dws"] = jnp.where(c["tril"], _nt(dfb, c["vn_bd"]), 0.0)
            dvn_bd = _tn(c["wcat"], dfb)
            dvn = functools.reduce(lambda a, b: a + b, [
                jnp.where(c["head"] == h, dvn_bd[h * BLK:(h + 1) * BLK], 0.0) for h in range(4)])
            dgv, out["dvn"] = _rms_bwd(dvn, c["xh"], c["rv"], vn_ref[...])
            out["dzv"] = dgv * _gelu_grad(blk.cols(O_G, O_P), c["tv"])
            dyp = dy_ref[blk.r, 768:1024]
            out["dps"] = jnp.sum(dyp * c["pout"], axis=0, keepdims=True)
            dout = (dyp * ps_ref[...]).astype(BF16)
            out["dpw"] = _tn(c["diff"], dout)
            out["ddiff"] = _nt(dout, c["pwbd"])
            out["dd"] = out["ddiff"] / c["cnt"]
            return out

        def write_previous_tile(dd_next, dk_next, dv_next):
            if MIX_NB > 1:
                dz_ref[0:TILE - BLK, :] = carry_ref[0:TILE - BLK, :].astype(BF16)
            rs = _window_sums(jnp.concatenate([ddc_ref[...], dd_next], axis=0), False)
            dz_ref[last, 0:O_K] = carry_ref[last, 0:O_K].astype(BF16)
            dz_ref[last, O_K:O_V] = (carry_ref[last, O_K:O_V] + dk_next).astype(BF16)
            dz_ref[last, O_V:O_U] = (carry_ref[last, O_V:O_U] + dv_next).astype(BF16)
            dz_ref[last, O_U:O_P] = carry_ref[last, O_U:O_P].astype(BF16)
            dz_ref[last, O_P:INW] = (carry_ref[last, O_P:INW] + rs[:BLK, :]).astype(BF16)

        @pl.when(n < nt)
        def _():
            parts = [block_grads(j) for j in range(MIX_NB)]
            total = lambda key, i=None: functools.reduce(
                lambda a, b: a + b, [p[key] if i is None else p[key][i] for p in parts])
            for h in range(N_HEADS):
                dsink_ref[h:h + 1, :] += total("dsink", h) + jnp.zeros((1, 128), F32)
            for h in range(4):
                dbt_ref[:, h:h + 1] += total("dbt", h)
            dws_ref[...] += total("dws")
            dpw_ref[...] += total("dpw")
            dvn_ref[0:1, :] += total("dvn")
            dps_ref[0:1, :] += total("dps")
            write_previous_tile(parts[0]["dd"], parts[0]["dk_prev"], parts[0]["dv_prev"])
            for j, p in enumerate(parts):
                r = slice(j * BLK, (j + 1) * BLK)
                nxt = parts[j + 1] if j + 1 < MIX_NB else None
                for h in range(N_HEADS):
                    carry_ref[r, h * HD:(h + 1) * HD] = p["dq"][h]
                carry_ref[r, O_U:O_G] = p["dzu"]
                carry_ref[r, O_G:O_P] = p["dzv"]
                if nxt is None:
                    carry_ref[r, O_K:O_V] = p["dk_cur"]
                    carry_ref[r, O_V:O_U] = p["dv_cur"]
                    carry_ref[r, O_P:INW] = -p["ddiff"]
                    ddc_ref[...] = p["dd"]
                else:
                    rs = _window_sums(jnp.concatenate([p["dd"], nxt["dd"]], axis=0), False)
                    carry_ref[r, O_K:O_V] = p["dk_cur"] + nxt["dk_prev"]
                    carry_ref[r, O_V:O_U] = p["dv_cur"] + nxt["dv_prev"]
                    carry_ref[r, O_P:INW] = rs[:BLK, :] - p["ddiff"]

        @pl.when(n == nt)
        def _():
            none = jnp.zeros((BLK, BLK), F32)
            write_previous_tile(jnp.zeros((BLK, 256), F32), none, none)

    cur = lambda n: jnp.minimum(n, nt - 1)
    done = lambda n: jnp.maximum(n - 1, 0)
    halo = lambda n: jnp.maximum(MIX_NB * jnp.minimum(n, nt - 1) - 1, 0)
    return _host(
        comm, body, name=name, grid=(nt + 1,),
        in_specs=[pl.BlockSpec(memory_space=pltpu.SMEM),
                  pl.BlockSpec((TILE, INW), lambda n: (cur(n), 0)),
                  pl.BlockSpec((BLK, 256), lambda n: (halo(n), 2)),
                  pl.BlockSpec((BLK, 256), lambda n: (halo(n), 5)),
                  pl.BlockSpec((TILE, D), lambda n: (cur(n), 0)),
                  pl.BlockSpec((TILE, 128), lambda n: (cur(n), 0)),
                  _fixspec(1, 256), _fixspec(BLK, 4 * BLK), _fixspec(BLK, 256), _fixspec(256, 256), _fixspec(1, 256)],
        out_specs=[pl.BlockSpec((TILE, INW), lambda n: (done(n), 0)),
                   _fixspec(8, 128), _fixspec(8, 256), _fixspec(BLK, 4 * BLK), _fixspec(BLK, 128),
                   _fixspec(256, 256), _fixspec(8, 256)],
        out_shape=[jax.ShapeDtypeStruct((s, INW), BF16), jax.ShapeDtypeStruct((8, 128), F32),
                   jax.ShapeDtypeStruct((8, 256), F32), jax.ShapeDtypeStruct((BLK, 4 * BLK), F32),
                   jax.ShapeDtypeStruct((BLK, 128), F32), jax.ShapeDtypeStruct((256, 256), F32),
                   jax.ShapeDtypeStruct((8, 256), F32)],
        scratch_shapes=[pltpu.VMEM((TILE, INW), F32), pltpu.VMEM((BLK, 256), F32)],
        args=(sinks, z, z, z, dy, lse, vnorm, wcat, bexp, pwbd, pscale))


def _position():
    x, y, c = lax.axis_index("x"), lax.axis_index("y"), lax.axis_index("c")
    return x, y, c


class _GatherTask:
    def __init__(self, srcs):
        self.inputs = list(srcs)
        ng = len(srcs)
        self.out_shape = [jax.ShapeDtypeStruct((a.shape[0], N_DEV) + a.shape[1:], a.dtype) for a in srcs]
        self.scratch = [pltpu.SemaphoreType.DMA((ng, 7)), pltpu.SemaphoreType.DMA((ng, 7)),
                        pltpu.SemaphoreType.DMA((ng,))]

    def _plan(self, src, dst, sems):
        send_sems, recv_sems, local_sems = sems
        ng = len(src)
        x, y, c = _position()
        me, sibling = (x, y, c), (x, y, 1 - c)
        chips = [(1 - x, y), (x, 1 - y), (1 - x, 1 - y)]

        def slot(pos):
            return 4 * pos[0] + 2 * pos[1] + pos[2]

        def copy(gi, k, block, to, from_src=False):
            rows = dst[gi].at[:, slot(block)]
            return pltpu.make_async_remote_copy(
                src_ref=src[gi] if from_src else rows, dst_ref=rows,
                send_sem=send_sems.at[gi, k], recv_sem=recv_sems.at[gi, k],
                device_id=to, device_id_type=MESH)

        make = functools.partial
        mine = [make(pltpu.make_async_copy, src[gi], dst[gi].at[:, slot(me)], local_sems.at[gi]) for gi in range(ng)]
        first = []
        for gi in range(ng):
            first.append(make(copy, gi, 0, me, sibling, True))
            first += [make(copy, gi, 1 + j, me, (*chip, c), True) for j, chip in enumerate(chips)]
        passed = [make(copy, gi, 4 + j, (*chip, c), sibling) for j, chip in enumerate(chips) for gi in range(ng)]
        arrive_ici = [make(copy, gi, 1 + j, (*chip, c), me) for j, chip in enumerate(chips) for gi in range(ng)]
        arrive_d2d = [make(copy, gi, 0, sibling, me) for gi in range(ng)]
        arrive_d2d += [make(copy, gi, 4 + j, (*chip, 1 - c), me) for j, chip in enumerate(chips) for gi in range(ng)]
        return mine, first, passed, arrive_ici, arrive_d2d

    def start(self, src, dst, sems):
        mine, first, _, _, _ = self._plan(src, dst, sems)
        for cp in mine + first:
            cp().start()

    def mid(self, src, dst, sems):
        _, _, passed, arrive_ici, _ = self._plan(src, dst, sems)
        for arrived, fw in zip(arrive_ici, passed):
            arrived().wait_recv()
            fw().start()

    def finish(self, src, dst, sems):
        mine, first, passed, _, arrive_d2d = self._plan(src, dst, sems)
        for cp in arrive_d2d:
            cp().wait_recv()
        for cp in first + passed:
            cp().wait_send()
        for cp in mine:
            cp().wait()


class _SiblingTask:
    def __init__(self, g5s):
        self.inputs = list(g5s)
        ng = len(g5s)
        self.out_shape = [jax.ShapeDtypeStruct((a.shape[0], 4) + a.shape[3:], a.dtype) for a in g5s]
        self.scratch = [pltpu.SemaphoreType.DMA((ng,)), pltpu.SemaphoreType.DMA((ng,))]

    def _plan(self, src, dst, sems):
        send_sems, recv_sems = sems
        x, y, c = _position()
        return [functools.partial(
            pltpu.make_async_remote_copy,
            src_ref=src[gi].at[:, :, 1 - c], dst_ref=dst[gi],
            send_sem=send_sems.at[gi], recv_sem=recv_sems.at[gi],
            device_id=(x, y, 1 - c), device_id_type=MESH) for gi in range(len(src))]

    def start(self, src, dst, sems):
        for cp in self._plan(src, dst, sems):
            cp().start()

    def mid(self, src, dst, sems):
        pass

    def finish(self, src, dst, sems):
        for cp in self._plan(src, dst, sems):
            cp().wait()


class _ChipTask(_SiblingTask):
    def __init__(self, sbs):
        self.inputs = list(sbs)
        ng = len(sbs)
        self.out_shape = [jax.ShapeDtypeStruct(a.shape, a.dtype) for a in sbs]
        self.scratch = [pltpu.SemaphoreType.DMA((ng, 3)), pltpu.SemaphoreType.DMA((ng, 3))]

    def _plan(self, src, dst, sems):
        send_sems, recv_sems = sems
        x, y, c = _position()
        jme = 2 * x + y
        chips = [(1 - x, y), (x, 1 - y), (1 - x, 1 - y)]
        return [functools.partial(
            pltpu.make_async_remote_copy,
            src_ref=src[gi].at[:, 2 * chip[0] + chip[1]], dst_ref=dst[gi].at[:, jme],
            send_sem=send_sems.at[gi, k], recv_sem=recv_sems.at[gi, k],
            device_id=(*chip, c), device_id_type=MESH) for k, chip in enumerate(chips) for gi in range(len(src))]


def _alone(task, name):
    n_in, n_out = len(task.inputs), len(task.out_shape)

    def body(*refs):
        parts = (refs[:n_in], refs[n_in:n_in + n_out], refs[n_in + n_out:])
        task.start(*parts)
        task.mid(*parts)
        task.finish(*parts)

    any_spec = pl.BlockSpec(memory_space=pl.ANY)
    return _call(body, name=name, in_specs=[any_spec] * n_in, out_specs=[any_spec] * n_out,
                 out_shape=task.out_shape, scratch_shapes=task.scratch)(*task.inputs)


def _core_sum(ids, g5, r1, name):
    n, _, _, rows, _ = g5.shape

    def body(ids_ref, g_ref, r_ref, sb_ref, own_ref):
        j = pl.program_id(2)
        t = g_ref[...] + r_ref[...]
        sb_ref[...] = t.astype(BF16)

        @pl.when(j == ids_ref[1])
        def _():
            own_ref[...] = t

    grid_spec = pltpu.PrefetchScalarGridSpec(
        num_scalar_prefetch=1, grid=(n, 1, 4),
        in_specs=[pl.BlockSpec((None, None, None, rows, D), lambda i, t, j, ids: (i, j, ids[0], t, 0)),
                  pl.BlockSpec((None, None, rows, D), lambda i, t, j, ids: (i, j, t, 0))],
        out_specs=[pl.BlockSpec((None, None, rows, D), lambda i, t, j, ids: (i, j, t, 0)),
                   pl.BlockSpec((None, rows, D), lambda i, t, j, ids: (i, t, 0))])
    return _call(
        body, name=name, grid_spec=grid_spec,
        out_shape=[jax.ShapeDtypeStruct((n, 4, rows, D), BF16), jax.ShapeDtypeStruct((n, rows, D), F32)],
        compiler_params=_params(("arbitrary", "arbitrary", "arbitrary")),
    )(ids, g5, r1)


def _chip_sum(others, own, r2, name):
    n, rows, _ = own.shape

    def body(oth_ref, own_ref, r0_ref, r1_ref, r2_ref, out_ref):
        out_ref[...] = ((own_ref[...] + r0_ref[...].astype(F32)) + r1_ref[...].astype(F32)) \
            + r2_ref[...].astype(F32)

    def rspec(k):
        return pl.BlockSpec((None, None, rows, D), lambda i, oth, k=k: (i, oth[k], 0, 0))

    grid_spec = pltpu.PrefetchScalarGridSpec(
        num_scalar_prefetch=1, grid=(n,),
        in_specs=[pl.BlockSpec((None, rows, D), lambda i, oth: (i, 0, 0)), rspec(0), rspec(1), rspec(2)],
        out_specs=pl.BlockSpec((None, rows, D), lambda i, oth: (i, 0, 0)))
    return _call(
        body, name=name, grid_spec=grid_spec,
        out_shape=jax.ShapeDtypeStruct((n, rows, D), F32),
        compiler_params=_params(("arbitrary",)),
    )(others, own, r2, r2, r2)


def _adam_math(w, g, m, v):
    m = ADAM_B1 * m + (1.0 - ADAM_B1) * g
    v = ADAM_B2 * v + (1.0 - ADAM_B2) * (g * g)
    m_hat = m / (1.0 - ADAM_B1 ** ADAM_STEP)
    v_hat = v / (1.0 - ADAM_B2 ** ADAM_STEP)
    delta = -ADAM_LR * (m_hat / (jnp.sqrt(v_hat) + ADAM_EPS) + ADAM_WD * w)
    return delta, m, v


def _adamw(w, g, m, v, name):
    shape = w.shape
    c = shape[-1]
    r = w.size // c
    rb = max(d for d in range(8, min(r, 512) + 1, 8) if r % d == 0)

    def body(w_ref, g_ref, m_ref, v_ref, d_ref, mo_ref, vo_ref):
        d_ref[...], mo_ref[...], vo_ref[...] = _adam_math(w_ref[...], g_ref[...], m_ref[...], v_ref[...])

    spec = _rowspec(rb, c)
    outs = _call(
        body, name=name, grid=(r // rb,),
        in_specs=[spec] * 4, out_specs=[spec] * 3,
        out_shape=[jax.ShapeDtypeStruct((r, c), F32)] * 3,
        compiler_params=_params(("arbitrary",)),
    )(*[t.reshape(r, c) for t in (w, g, m, v)])
    return [o.reshape(shape) for o in outs]


def _adamw_small(parts, w, m, v, name):
    def body(p_ref, w_ref, m_ref, v_ref, g_ref, d_ref, mo_ref, vo_ref):
        g = p_ref[0]
        for dev in range(1, N_DEV):
            g = g + p_ref[dev]
        g_ref[...] = g
        d_ref[...], mo_ref[...], vo_ref[...] = _adam_math(w_ref[...], g, m_ref[...], v_ref[...])

    return _call(
        body, name=name,
        out_shape=[jax.ShapeDtypeStruct(w.shape, F32)] * 4,
        compiler_params=_params(),
    )(parts, w, m, v)


SMALL = ["ffn1_norm", "mix_norm", "attn_sinks", "gmlp_v_norm", "gmlp_w_s", "gmlp_b", "pool_w", "pool_scale",
         "ffn2_norm", "final_norm"]


def _piece_rows(size):
    return -(-size // 1024) * 8


def _pack_small(arrs, extra=None):
    pieces = []
    for a in list(arrs) + [jnp.zeros((1,), F32) if extra is None else extra]:
        f = a.reshape(-1)
        pieces.append(jnp.pad(f, (0, _piece_rows(f.shape[0]) * 128 - f.shape[0])).reshape(-1, 128))
    return jnp.concatenate(pieces, axis=0)


def _unpack_small(packed, like):
    out, off = [], 0
    for a in like:
        rows = _piece_rows(a.size)
        out.append(packed[off:off + rows].reshape(-1)[:a.size].reshape(a.shape))
        off += rows
    return out, packed[off, 0]


def kernel(x, ffn1_norm, ffn1_w_gate, ffn1_w_up, ffn1_w_down, mix_norm, w_in, attn_sinks, gmlp_v_norm, gmlp_w_s, gmlp_b, pool_w, pool_scale, w_out, ffn2_norm, ffn2_w_gate, ffn2_w_up, ffn2_w_down, final_norm, loss_target, m_ffn1_norm, m_ffn1_w_gate, m_ffn1_w_up, m_ffn1_w_down, m_mix_norm, m_w_in, m_attn_sinks, m_gmlp_v_norm, m_gmlp_w_s, m_gmlp_b, m_pool_w, m_pool_scale, m_w_out, m_ffn2_norm, m_ffn2_w_gate, m_ffn2_w_up, m_ffn2_w_down, m_final_norm, v_ffn1_norm, v_ffn1_w_gate, v_ffn1_w_up, v_ffn1_w_down, v_mix_norm, v_w_in, v_attn_sinks, v_gmlp_v_norm, v_gmlp_w_s, v_gmlp_b, v_pool_w, v_pool_scale, v_w_out, v_ffn2_norm, v_ffn2_w_gate, v_ffn2_w_up, v_ffn2_w_down, v_final_norm):
    s = x.shape[1]
    xi, yi, ci = _position()
    ids = jnp.stack([ci, 2 * xi + yi]).astype(jnp.int32)
    jme = 2 * xi + yi
    others = jnp.stack([k + (k >= jme).astype(jnp.int32) for k in range(3)]).astype(jnp.int32)
    t = lambda a: jnp.swapaxes(a, -1, -2)
    row = lambda a: a.reshape(1, -1)
    full = lambda a: a.reshape(a.shape[0], -1, D)

    loc_f1 = [jnp.stack([t(ffn1_w_gate[l]), t(ffn1_w_up[l]), ffn1_w_down[l]]).astype(BF16) for l in range(DEPTH)]
    loc_f2 = [jnp.stack([t(ffn2_w_gate[l]), t(ffn2_w_up[l]), ffn2_w_down[l]]).astype(BF16) for l in range(DEPTH)]
    loc_in = [t(w_in[l])[None].astype(BF16) for l in range(DEPTH)]
    loc_out = [w_out[l][None].astype(BF16) for l in range(DEPTH)]

    (wf1,) = _alone(_GatherTask([loc_f1[0]]), "gather_first")
    wf1 = full(wf1)
    xc = x.reshape(s, D)
    saved = []
    for l in range(DEPTH):
        x0 = xc
        if l == 0:
            (x1, *act1), ((wf2, win, wout),) = _ffn_fwd(
                x0, row(ffn1_norm[l]), wf1, 0, f"ffn1_fwd_{l}", comm=[_GatherTask([loc_f2[0], loc_in[0], loc_out[0]])])
        else:
            (x1, *act1), ((wf2,),) = _ffn_fwd(
                x0, row(ffn1_norm[l]), wf1, 0, f"ffn1_fwd_{l}", comm=[_GatherTask([loc_f2[1]])])
        wf2, win, wout = full(wf2), full(win), full(wout)
        z, hmix = _mixin_fwd(x1, row(mix_norm[l]), win, 0, f"mixin_fwd_{l}")
        wcat = jnp.concatenate([gmlp_w_s[l][h] for h in range(4)], axis=1)
        bexp = jnp.repeat(t(gmlp_b[l]), HD, axis=1)
        pwbd = jnp.zeros((256, 256), F32)
        for g in range(4):
            pwbd = pwbd.at[g * HD:(g + 1) * HD, g * HD:(g + 1) * HD].set(pool_w[l][g])
        mixp = (attn_sinks[l], row(gmlp_v_norm[l]), wcat, bexp, pwbd, row(pool_scale[l]))
        y, lse = _mix_fwd(z, *mixp, f"mix_fwd_{l}")
        keep = (x0, act1, wf1, x1, z, hmix, mixp, y, lse, win, wout)
        if l == 0:
            (x3, *act2, x2), ((wf1, win, wout),) = _ffn_fwd(
                x1, row(ffn2_norm[l]), wf2, 0, f"ffn2_fwd_{l}", mixer=(y, wout),
                comm=[_GatherTask([loc_f1[1], loc_in[1], loc_out[1]])])
            wf1 = full(wf1)
        else:
            x3, *act2, x2 = _ffn_fwd(x1, row(ffn2_norm[l]), wf2, 0, f"ffn2_fwd_{l}", mixer=(y, wout))
        saved.append(keep + (x2, wf2, act2))
        xc = x3
    dx, loss_part, d_final = _loss_head(xc, row(final_norm), loss_target.reshape(s, D), "loss_head")

    def five(g):
        return g.reshape(g.shape[0], 4, 2, g.shape[1] // N_DEV, D)

    def core_sums(g5s, r1s, tag):
        res = [_core_sum(ids, g5, r1, f"core_sum_{tag}_{i}") for i, (g5, r1) in enumerate(zip(g5s, r1s))]
        return [sb for sb, _ in res], [own for _, own in res]

    def chip_sums(owns, r2s, tag):
        return [_chip_sum(others, own, r2, f"chip_sum_{tag}_{i}") for i, (own, r2) in enumerate(zip(owns, r2s))]

    def mix_small(l, dsink, dvn, dws, dbt, dpw, dps):
        return {("attn_sinks", l): dsink[:, 0], ("gmlp_v_norm", l): dvn[0],
                ("gmlp_w_s", l): jnp.stack([dws[:, h * BLK:(h + 1) * BLK] for h in range(4)]),
                ("gmlp_b", l): t(dbt[:, :4]),
                ("pool_w", l): jnp.stack([dpw[g * HD:(g + 1) * HD, g * HD:(g + 1) * HD] for g in range(4)]),
                ("pool_scale", l): dps[0]}

    small = {}
    red = {}
    x0, (p11, p21, hid1), wf1, x1, z, hmix, mixp, y, lse, win, wout, x2, wf2, (p12, p22, hid2) = saved[1]
    dx, da, db, h, dyb, dg = _ffn_bwd(x2, row(ffn2_norm[1]), dx, p12, p22, wf2, 0, "ffn2_bwd_1")
    small[("ffn2_norm", 1)] = dg[0]
    g = _wgrad(da, h, None, 3, 0, "wgrad_gate2_1")
    g = _wgrad(db, h, g, 3, 1, "wgrad_up2_1")
    g = _wgrad(hid2, dyb, g, 3, 2, "wgrad_down2_1")
    a5 = [five(g)]
    (dymix, dxb), (a_r1,) = _mixout_bwd(dx, wout, 0, "mixout_bwd_1", comm=[_SiblingTask(a5)])
    a_sb, a_own = core_sums(a5, a_r1, "a")
    g_out = _wgrad(y, dxb, None, 1, 0, "wgrad_out_1")
    (dz, dsink, dvn, dws, dbt, dpw, dps), (a_r2,) = _mix_bwd(z, dymix, lse, *mixp, "mix_bwd_1", comm=[_ChipTask(a_sb)])
    (red[("f2", 1)],) = chip_sums(a_own, a_r2, "a")
    small.update(mix_small(1, dsink, dvn, dws, dbt, dpw, dps))
    dx, dg = _mixin_bwd(x1, row(mix_norm[1]), dz, dx, win, 0, "mixin_bwd_1")
    small[("mix_norm", 1)] = dg[0]
    g_in = _wgrad(dz, hmix, None, 1, 0, "wgrad_in_1")
    b5 = [five(g_out), five(g_in)]
    (dx, da, db, h, dyb, dg), (b_r1,) = _ffn_bwd(x0, row(ffn1_norm[1]), dx, p11, p21, wf1, 0, "ffn1_bwd_1",
                                                 comm=[_SiblingTask(b5)])
    small[("ffn1_norm", 1)] = dg[0]
    b_sb, b_own = core_sums(b5, b_r1, "b")
    g_gate, (b_r2,) = _wgrad(da, h, None, 1, 0, "wgrad_gate1_1", comm=[_ChipTask(b_sb)])
    red[("out", 1)], red[("in", 1)] = chip_sums(b_own, b_r2, "b")
    g_up = _wgrad(db, h, None, 1, 0, "wgrad_up1_1")
    g_down = _wgrad(hid1, dyb, None, 1, 0, "wgrad_down1_1")
    c5 = [five(g_gate), five(g_up), five(g_down)]
    x0, (p11, p21, hid1), wf1, x1, z, hmix, mixp, y, lse, win, wout, x2, wf2, (p12, p22, hid2) = saved[0]
    (dx, da, db, h, dyb, dg), (c_r1,) = _ffn_bwd(x2, row(ffn2_norm[0]), dx, p12, p22, wf2, 0, "ffn2_bwd_0",
                                                 comm=[_SiblingTask(c5)])
    small[("ffn2_norm", 0)] = dg[0]
    c_sb, c_own = core_sums(c5, c_r1, "c")
    g, (c_r2a,) = _wgrad(da, h, None, 3, 0, "wgrad_gate2_0", comm=[_ChipTask(c_sb[0:1])])
    g, (c_r2b,) = _wgrad(db, h, g, 3, 1, "wgrad_up2_0", comm=[_ChipTask(c_sb[1:2])])
    g, (c_r2c,) = _wgrad(hid2, dyb, g, 3, 2, "wgrad_down2_0", comm=[_ChipTask(c_sb[2:3])])
    red[("f1", 1)] = jnp.concatenate(chip_sums(c_own, c_r2a + c_r2b + c_r2c, "c"), axis=0)
    d5 = [five(g)]
    (dymix, dxb), (d_r1,) = _mixout_bwd(dx, wout, 0, "mixout_bwd_0", comm=[_SiblingTask(d5)])
    d_sb, d_own = core_sums(d5, d_r1, "d")
    (dz, dsink, dvn, dws, dbt, dpw, dps), (d_r2,) = _mix_bwd(z, dymix, lse, *mixp, "mix_bwd_0", comm=[_ChipTask(d_sb)])
    (red[("f2", 0)],) = chip_sums(d_own, d_r2, "d")
    small.update(mix_small(0, dsink, dvn, dws, dbt, dpw, dps))
    dx, dg = _mixin_bwd(x1, row(mix_norm[0]), dz, dx, win, 0, "mixin_bwd_0")
    small[("mix_norm", 0)] = dg[0]
    dx, da, db, h, dyb, dg = _ffn_bwd(x0, row(ffn1_norm[0]), dx, p11, p21, wf1, 0, "ffn1_bwd_0")
    small[("ffn1_norm", 0)] = dg[0]
    grad_x = dx.reshape(1, s, D)

    part = [d_final[0] if nm == "final_norm" else jnp.stack([small[(nm, l)] for l in range(DEPTH)]) for nm in SMALL]
    packed = _pack_small(part, loss_part[0, 0])
    g_gate, ((gathered,),) = _wgrad(da, h, None, 1, 0, "wgrad_gate1_0", comm=[_GatherTask([packed[None]])])
    f5 = [five(g_gate)]
    g_up, (f_r1,) = _wgrad(db, h, None, 1, 0, "wgrad_up1_0", comm=[_SiblingTask(f5)])
    f_sb, f_own = core_sums(f5, f_r1, "f")
    u5 = [five(g_up)]
    g_down, (f_r2, u_r1) = _wgrad(hid1, dyb, None, 1, 0, "wgrad_down1_0", comm=[_ChipTask(f_sb), _SiblingTask(u5)])
    u_sb, u_own = core_sums(u5, u_r1, "u")
    w5 = [five(g_down)]
    g_out, (u_r2, w_r1) = _wgrad(y, dxb, None, 1, 0, "wgrad_out_0", comm=[_ChipTask(u_sb), _SiblingTask(w5)])
    w_sb, w_own = core_sums(w5, w_r1, "w")
    o5 = [five(g_out)]
    g_in, (w_r2, o_r1) = _wgrad(dz, hmix, None, 1, 0, "wgrad_in_0", comm=[_ChipTask(w_sb), _SiblingTask(o5)])
    red[("f1", 0)] = jnp.concatenate(chip_sums(f_own + u_own + w_own, f_r2 + u_r2 + w_r2, "f"), axis=0)
    o_sb, o_own = core_sums(o5, o_r1, "o")
    i5 = [five(g_in)]
    i_r1 = _alone(_SiblingTask(i5), "reduce_sibling_last")
    i_sb, i_own = core_sums(i5, i_r1, "i")
    e_r2 = _alone(_ChipTask(o_sb + i_sb), "reduce_chips_last")
    red[("out", 0)], red[("in", 0)] = chip_sums(o_own + i_own, e_r2, "e")

    grads = {}
    red_rows = {}
    for k, nm in enumerate(["w_gate", "w_up", "w_down"]):
        for f in ("f1", "f2"):
            red_rows[f"ffn{f[1]}_{nm}"] = jnp.stack([red[(f, l)][k] for l in range(DEPTH)])
    red_rows["w_in"] = jnp.concatenate([red[("in", l)] for l in range(DEPTH)], axis=0)
    red_rows["w_out"] = jnp.concatenate([red[("out", l)] for l in range(DEPTH)], axis=0)
    transposed = ("ffn1_w_gate", "ffn1_w_up", "ffn2_w_gate", "ffn2_w_up", "w_in")

    small_w = dict(ffn1_norm=ffn1_norm, mix_norm=mix_norm, attn_sinks=attn_sinks, gmlp_v_norm=gmlp_v_norm,
                   gmlp_w_s=gmlp_w_s, gmlp_b=gmlp_b, pool_w=pool_w, pool_scale=pool_scale, ffn2_norm=ffn2_norm,
                   final_norm=final_norm)
    small_m = dict(ffn1_norm=m_ffn1_norm, mix_norm=m_mix_norm, attn_sinks=m_attn_sinks, gmlp_v_norm=m_gmlp_v_norm,
                   gmlp_w_s=m_gmlp_w_s, gmlp_b=m_gmlp_b, pool_w=m_pool_w, pool_scale=m_pool_scale,
                   ffn2_norm=m_ffn2_norm, final_norm=m_final_norm)
    small_v = dict(ffn1_norm=v_ffn1_norm, mix_norm=v_mix_norm, attn_sinks=v_attn_sinks, gmlp_v_norm=v_gmlp_v_norm,
                   gmlp_w_s=v_gmlp_w_s, gmlp_b=v_gmlp_b, pool_w=v_pool_w, pool_scale=v_pool_scale,
                   ffn2_norm=v_ffn2_norm, final_norm=v_final_norm)
    sg, sd, sm, sv = _adamw_small(gathered[0], _pack_small([small_w[nm] for nm in SMALL]),
                                  _pack_small([small_m[nm] for nm in SMALL]),
                                  _pack_small([small_v[nm] for nm in SMALL]), "adamw_small")
    like = [small_w[nm] for nm in SMALL]
    sg_l, loss = _unpack_small(sg, like)
    sd_l, _ = _unpack_small(sd, like)
    sm_l, _ = _unpack_small(sm, like)
    sv_l, _ = _unpack_small(sv, like)
    deltas, new_m, new_v = {}, {}, {}
    for i, nm in enumerate(SMALL):
        grads[nm], deltas[nm], new_m[nm], new_v[nm] = sg_l[i], sd_l[i], sm_l[i], sv_l[i]

    big_w = dict(ffn1_w_gate=ffn1_w_gate, ffn1_w_up=ffn1_w_up, ffn1_w_down=ffn1_w_down, w_in=w_in, w_out=w_out,
                 ffn2_w_gate=ffn2_w_gate, ffn2_w_up=ffn2_w_up, ffn2_w_down=ffn2_w_down)
    big_m = dict(ffn1_w_gate=m_ffn1_w_gate, ffn1_w_up=m_ffn1_w_up, ffn1_w_down=m_ffn1_w_down, w_in=m_w_in,
                 w_out=m_w_out, ffn2_w_gate=m_ffn2_w_gate, ffn2_w_up=m_ffn2_w_up, ffn2_w_down=m_ffn2_w_down)
    big_v = dict(ffn1_w_gate=v_ffn1_w_gate, ffn1_w_up=v_ffn1_w_up, ffn1_w_down=v_ffn1_w_down, w_in=v_w_in,
                 w_out=v_w_out, ffn2_w_gate=v_ffn2_w_gate, ffn2_w_up=v_ffn2_w_up, ffn2_w_down=v_ffn2_w_down)
    for nm in big_w:
        view = t if nm in transposed else (lambda a: a)
        res = _adamw(view(big_w[nm]), red_rows[nm], view(big_m[nm]), view(big_v[nm]), f"adamw_{nm}")
        grads[nm] = view(red_rows[nm])
        deltas[nm], new_m[nm], new_v[nm] = [view(r) for r in res]

    order = ["ffn1_norm", "ffn1_w_gate", "ffn1_w_up", "ffn1_w_down", "mix_norm", "w_in", "attn_sinks", "gmlp_v_norm",
             "gmlp_w_s", "gmlp_b", "pool_w", "pool_scale", "w_out", "ffn2_norm", "ffn2_w_gate", "ffn2_w_up",
             "ffn2_w_down", "final_norm"]
    return (loss, grad_x, *[grads[n] for n in order], *[deltas[n] for n in order],
            *[new_m[n] for n in order], *[new_v[n] for n in order])
```

```python
import functools
import math

import jax
import jax.numpy as jnp
from jax import lax
from jax.experimental import pallas as pl
from jax.experimental.pallas import tpu as pltpu

F32 = jnp.float32
BF16 = jnp.bfloat16
MESH = pl.DeviceIdType.MESH

D = 1024
FF = 2816
INW = 1536
N_DEV = 8
DEPTH = 2
BLK = 128
HD = 64
N_HEADS = 8
N_KV = 2
REP = 4
ATTN_SCALE = HD ** -0.5
POOL_WINDOWS = (2, 4, 8, 16)
EPS = 1e-6
NEG = -1e30
FC = 256
GELU_C0 = math.sqrt(2.0 / math.pi)
GELU_C1 = 0.044715

ADAM_LR = 0.001
ADAM_B1 = 0.9
ADAM_B2 = 0.999
ADAM_EPS = 1e-08
ADAM_WD = 0.01
ADAM_STEP = 10

VMEM_LIMIT = 56 * 1024 * 1024

O_K, O_V, O_U, O_G, O_P = 512, 640, 768, 1024, 1280


def _call(body, **kw):
    return pl.pallas_call(body, **kw)


def _params(sem=None, vmem=VMEM_LIMIT):
    return pltpu.CompilerParams(dimension_semantics=sem, vmem_limit_bytes=vmem)


def _host(comm, body, *, name, grid, in_specs, out_specs, out_shape, args, scratch_shapes=(), aliases=None):
    single = not isinstance(out_shape, (list, tuple))
    out_specs_l = [out_specs] if single else list(out_specs)
    out_shape_l = [out_shape] if single else list(out_shape)
    n_in, n_out, n_scr = len(in_specs), len(out_shape_l), len(scratch_shapes)
    steps = grid[0]
    any_spec = pl.BlockSpec(memory_space=pl.ANY)

    def wrapped(*refs):
        pos = 0

        def take(n):
            nonlocal pos
            part = refs[pos:pos + n]
            pos += n
            return part

        ins = take(n_in)
        cins = [take(len(t.inputs)) for t in comm]
        outs = take(n_out)
        couts = [take(len(t.out_shape)) for t in comm]
        scr = take(n_scr)
        cscr = [take(len(t.scratch)) for t in comm]
        i = pl.program_id(0)
        for k, t in enumerate(comm):
            pl.when(i == 0)(functools.partial(t.start, cins[k], couts[k], cscr[k]))
        body(*ins, *outs, *scr)
        for k, t in enumerate(comm):
            pl.when(i == (3 * steps) // 4)(functools.partial(t.mid, cins[k], couts[k], cscr[k]))
            pl.when(i == steps - 1)(functools.partial(t.finish, cins[k], couts[k], cscr[k]))

    c_args = [a for t in comm for a in t.inputs]
    c_shapes = [sh for t in comm for sh in t.out_shape]
    c_scr = [sc for t in comm for sc in t.scratch]
    res = _call(
        wrapped, name=name, grid=grid,
        in_specs=list(in_specs) + [any_spec] * len(c_args),
        out_specs=out_specs_l + [any_spec] * len(c_shapes),
        out_shape=out_shape_l + c_shapes,
        scratch_shapes=list(scratch_shapes) + c_scr,
        input_output_aliases=aliases or {},
        compiler_params=_params(("arbitrary",)),
    )(*args, *c_args)
    outs = res[0] if single else list(res[:n_out])
    if not comm:
        return outs
    c_outs, pos = [], n_out
    for t in comm:
        c_outs.append(list(res[pos:pos + len(t.out_shape)]))
        pos += len(t.out_shape)
    return outs, c_outs


def _nn(a, b):
    return lax.dot_general(a, b, (((1,), (0,)), ((), ())), preferred_element_type=F32)


def _nt(a, b):
    return lax.dot_general(a, b, (((1,), (1,)), ((), ())), preferred_element_type=F32)


def _tn(a, b):
    return lax.dot_general(a, b, (((0,), (0,)), ((), ())), preferred_element_type=F32)


def _gelu(x):
    x2 = x * x
    t = jnp.tanh(x * (GELU_C0 + (GELU_C0 * GELU_C1) * x2))
    hx = 0.5 * x
    return hx + hx * t, (hx, x2, t)


def _gelu_grad(parts):
    hx, x2, t = parts
    return (0.5 + 0.5 * t) + (hx * (1.0 - t * t)) * (GELU_C0 + (3.0 * GELU_C0 * GELU_C1) * x2)


def _rms(x):
    r = lax.rsqrt(jnp.mean(x * x, axis=-1, keepdims=True) + EPS)
    return x * r, r


def _rms_bwd(dy, xh, r, g):
    dg = jnp.sum(dy * xh, axis=0, keepdims=True)
    dxh = dy * g
    dx = r * (dxh - xh * jnp.mean(dxh * xh, axis=-1, keepdims=True))
    return dx, dg


def _wspec(rows, m):
    return pl.BlockSpec((None, rows, D), lambda i, m=m: (m, 0, 0), pipeline_mode=pl.Buffered(1))


def _rowspec(tm, cols):
    return pl.BlockSpec((tm, cols), lambda i: (i, 0))


def _fixspec(rows, cols):
    return pl.BlockSpec((rows, cols), lambda i: (0, 0))


def _ffn_fwd(x, gain, w352, mg, name, comm=(), mixer=None):
    s = x.shape[0]
    tm = min(512, s)

    def body(*refs):
        if mixer is None:
            x_ref, g_ref, wg_ref, wu_ref, wd_ref, xo_ref, p1_ref, p2_ref, hid_ref = refs
            xt = x_ref[...]
        else:
            x_ref, g_ref, wg_ref, wu_ref, wd_ref, y_ref, wo_ref, xo_ref, p1_ref, p2_ref, hid_ref, xin_ref = refs
            xt = x_ref[...] + _nn(y_ref[...], wo_ref[...])
            xin_ref[...] = xt
        xh, _ = _rms(xt)
        h = (xh * g_ref[...]).astype(BF16)
        for c in range(FF // FC):
            sl = slice(c * FC, (c + 1) * FC)
            a = _nt(h, wg_ref[sl, :])
            b = _nt(h, wu_ref[sl, :])
            sig = 0.5 * jnp.tanh(0.5 * a) + 0.5
            sa = a * sig
            p1_ref[:, sl] = (b * (sig + sa * (1.0 - sig))).astype(BF16)
            p2_ref[:, sl] = sa.astype(BF16)
            hid_ref[:, sl] = (sa * b).astype(BF16)
        xo_ref[...] = xt + 0.5 * _nn(hid_ref[...], wd_ref[...])

    act = jax.ShapeDtypeStruct((s, FF), BF16)
    tok = jax.ShapeDtypeStruct((s, D), F32)
    extra = mixer is not None
    return _host(
        comm, body, name=name, grid=(s // tm,),
        in_specs=[_rowspec(tm, D), _fixspec(1, D), _wspec(FF, mg), _wspec(FF, mg + 1), _wspec(FF, mg + 2)]
        + ([_rowspec(tm, D), _wspec(D, 0)] if extra else []),
        out_specs=[_rowspec(tm, D), _rowspec(tm, FF), _rowspec(tm, FF), _rowspec(tm, FF)]
        + ([_rowspec(tm, D)] if extra else []),
        out_shape=[tok, act, act, act] + ([tok] if extra else []),
        args=(x, gain, w352, w352, w352) + (tuple(mixer) if extra else ()))


def _ffn_bwd(x, gain, dy, p1, p2, w352, mg, name, comm=(), w_out=None):
    s = x.shape[0]
    tm = min(256, s)

    def body(*refs):
        if w_out is None:
            (x_ref, g_ref, dy_ref, p1_ref, p2_ref, wg_ref, wu_ref, wd_ref,
             dx_ref, dab_ref, h_ref, dyb_ref, dg_ref) = refs
        else:
            (x_ref, g_ref, dy_ref, p1_ref, p2_ref, wg_ref, wu_ref, wd_ref, wo_ref,
             dx_ref, dab_ref, h_ref, dyb_ref, dg_ref, dym_ref, dxb_ref) = refs
        i = pl.program_id(0)
        xt = x_ref[...]
        g = g_ref[...]
        xh, r = _rms(xt)
        h_ref[...] = (xh * g).astype(BF16)
        dyt = dy_ref[...]
        dyb = (0.5 * dyt).astype(BF16)
        dyb_ref[...] = dyb
        for c in range(FF // FC):
            sl = slice(c * FC, (c + 1) * FC)
            dhid = _nt(dyb, wd_ref[sl, :])
            dab_ref[:, sl] = (dhid * p1_ref[:, sl].astype(F32)).astype(BF16)
            dab_ref[:, FF + c * FC:FF + (c + 1) * FC] = (dhid * p2_ref[:, sl].astype(F32)).astype(BF16)
        dh = _nn(dab_ref[:, :FF], wg_ref[...]) + _nn(dab_ref[:, FF:], wu_ref[...])
        dxn, dg = _rms_bwd(dh, xh, r, g)
        dx = dyt + dxn
        dx_ref[...] = dx
        if w_out is not None:
            dxb = dx.astype(BF16)
            dxb_ref[...] = dxb
            dym_ref[...] = _nt(dxb, wo_ref[...])

        @pl.when(i == 0)
        def _():
            dg_ref[...] = jnp.zeros_like(dg_ref)

        dg_ref[0:1, :] += dg

    tok = jax.ShapeDtypeStruct((s, D), BF16)
    tok32 = jax.ShapeDtypeStruct((s, D), F32)
    extra = w_out is not None
    return _host(
        comm, body, name=name, grid=(s // tm,),
        in_specs=[_rowspec(tm, D), _fixspec(1, D), _rowspec(tm, D), _rowspec(tm, FF), _rowspec(tm, FF),
                  _wspec(FF, mg), _wspec(FF, mg + 1), _wspec(FF, mg + 2)] + ([_wspec(D, 0)] if extra else []),
        out_specs=[_rowspec(tm, D), _rowspec(tm, 2 * FF), _rowspec(tm, D), _rowspec(tm, D), _fixspec(8, D)]
        + ([_rowspec(tm, D), _rowspec(tm, D)] if extra else []),
        out_shape=[tok32, jax.ShapeDtypeStruct((s, 2 * FF), BF16), tok, tok, jax.ShapeDtypeStruct((8, D), F32)]
        + ([tok32, tok] if extra else []),
        args=(x, gain, dy, p1, p2, w352, w352, w352) + ((w_out,) if extra else ()))


def _wgrad(a, b, g, n_slabs, m, name, comm=(), col0=0, cols=None, slab_rows=None):
    s = a.shape[0]
    cols = a.shape[1] if cols is None else cols
    slab_rows = cols if slab_rows is None else slab_rows
    mb = 256
    per_slab = slab_rows // mb

    def body(*refs):
        refs[-1][...] = _tn(refs[0][...], refs[1][...])

    in_specs = [pl.BlockSpec((s, mb), lambda i: (0, col0 // mb + i)),
                pl.BlockSpec((s, D), lambda i: (0, 0), pipeline_mode=pl.Buffered(1))]
    args = [a, b]
    aliases = {}
    if g is not None:
        in_specs.append(pl.BlockSpec(memory_space=pl.ANY))
        args.append(g)
        aliases = {2: 0}
    return _host(
        comm, body, name=name, grid=(cols // mb,),
        in_specs=in_specs,
        out_specs=pl.BlockSpec((None, mb, D), lambda i: (m + i // per_slab, i % per_slab, 0)),
        out_shape=jax.ShapeDtypeStruct((n_slabs, slab_rows, D), F32),
        aliases=aliases, args=args)


def _mixin_fwd(x, gain, w192, l, name):
    s = x.shape[0]
    tm = min(512, s)

    def body(x_ref, g_ref, w_ref, z_ref, h_ref):
        xh, _ = _rms(x_ref[...])
        h = (xh * g_ref[...]).astype(BF16)
        h_ref[...] = h
        z_ref[...] = _nt(h, w_ref[...])

    return _call(
        body, name=name, grid=(s // tm,),
        in_specs=[_rowspec(tm, D), _fixspec(1, D), _wspec(INW, l)],
        out_specs=[_rowspec(tm, INW), _rowspec(tm, D)],
        out_shape=[jax.ShapeDtypeStruct((s, INW), F32), jax.ShapeDtypeStruct((s, D), BF16)],
        compiler_params=_params(("arbitrary",)),
    )(x, gain, w192)


def _mixin_bwd(x, gain, dz, dx_in, w192, l, name, comm=()):
    s = x.shape[0]
    tm = min(512, s)

    def body(x_ref, g_ref, dz_ref, dxi_ref, w_ref, dx_ref, dg_ref):
        i = pl.program_id(0)
        g = g_ref[...]
        xh, r = _rms(x_ref[...])
        dh = _nn(dz_ref[...], w_ref[...])
        dxn, dg = _rms_bwd(dh, xh, r, g)
        dx_ref[...] = dxi_ref[...] + dxn

        @pl.when(i == 0)
        def _():
            dg_ref[...] = jnp.zeros_like(dg_ref)

        dg_ref[0:1, :] += dg

    return _host(
        comm, body, name=name, grid=(s // tm,),
        in_specs=[_rowspec(tm, D), _fixspec(1, D), _rowspec(tm, INW), _rowspec(tm, D), _wspec(INW, l)],
        out_specs=[_rowspec(tm, D), _fixspec(8, D)],
        out_shape=[jax.ShapeDtypeStruct((s, D), F32), jax.ShapeDtypeStruct((8, D), F32)],
        args=(x, gain, dz, dx_in, w192))


def _loss_head(x, gain, tgt, name):
    s = x.shape[0]
    tm = min(512, s)

    def body(x_ref, g_ref, t_ref, dx_ref, loss_ref, dg_ref):
        i = pl.program_id(0)
        g = g_ref[...]
        xh, r = _rms(x_ref[...])
        err = xh * g - t_ref[...]
        tok = jnp.mean(err * err, axis=-1, keepdims=True)
        lp = 0.5 * jnp.sum(tok, axis=0, keepdims=True)
        dxn, dg = _rms_bwd(err * (1.0 / D), xh, r, g)
        dx_ref[...] = dxn

        @pl.when(i == 0)
        def _():
            dg_ref[...] = jnp.zeros_like(dg_ref)
            loss_ref[...] = jnp.zeros_like(loss_ref)

        dg_ref[0:1, :] += dg
        loss_ref[0:1, :] += lp + jnp.zeros((1, 128), F32)

    return _call(
        body, name=name, grid=(s // tm,),
        in_specs=[_rowspec(tm, D), _fixspec(1, D), _rowspec(tm, D)],
        out_specs=[_rowspec(tm, D), _fixspec(8, 128), _fixspec(8, D)],
        out_shape=[jax.ShapeDtypeStruct((s, D), F32), jax.ShapeDtypeStruct((8, 128), F32),
                   jax.ShapeDtypeStruct((8, D), F32)],
        compiler_params=_params(("arbitrary",)),
    )(x, gain, tgt)


MIX_NB = 4
TILE = MIX_NB * BLK
GROUP_ROWS = REP * BLK


class _Block:
    def __init__(self, n, j, zc_ref, zkvp_ref, zpp_ref):
        self.zc, self.zkvp, self.zpp = zc_ref, zkvp_ref, zpp_ref
        self.first = j == 0
        self.r = slice(j * BLK, (j + 1) * BLK)
        self.rp = slice((j - 1) * BLK, j * BLK)
        self.index = n * MIX_NB + j
        self.lo = jnp.where(n > 0, 0, BLK) if self.first else 0
        self.has_prev = jnp.where(n > 0, 1.0, 0.0) if self.first else 1.0

    def cols(self, c0, c1):
        return self.zc[self.r, c0:c1]

    def prev_kv(self, c0, c1):
        return self.zkvp[:, c0:c1] if self.first else self.zc[self.rp, O_K + c0:O_K + c1]

    def prev_p(self):
        return self.zpp[...] * self.has_prev if self.first else self.zc[self.rp, O_P:INW]


def _band_mask(rows):
    row = lax.broadcasted_iota(jnp.int32, (rows, 2 * BLK), 0) & (BLK - 1)
    col = lax.broadcasted_iota(jnp.int32, (rows, 2 * BLK), 1)
    return (col > row) & (col <= row + BLK)


def _block_mask(band, blk):
    if not blk.first:
        return band
    return band & (lax.broadcasted_iota(jnp.int32, band.shape, 1) >= blk.lo)


def _lane_head(shape):
    return lax.broadcasted_iota(jnp.int32, shape, 1) // HD


def _lane_group_select(vals):
    grp = _lane_head(vals[0].shape)
    return jnp.where(grp == 0, vals[0], jnp.where(grp == 1, vals[1], jnp.where(grp == 2, vals[2], vals[3])))


def _pool_count(index):
    row = lax.broadcasted_iota(jnp.int32, (BLK, 256), 0)
    pos1 = (index * BLK + row + 1).astype(F32)
    wl = _lane_group_select([jnp.full((BLK, 256), float(w), F32) for w in POOL_WINDOWS])
    return jnp.minimum(pos1, wl)


def _window_sums(e, forward):
    tot = e.shape[0]
    lv = e
    out = []
    for sh in (1, 2, 4, 8):
        lv = lv + pltpu.roll(lv, sh if forward else tot - sh, 0)
        out.append(lv)
    return _lane_group_select(out)


def _stack_heads(get, g):
    return jnp.concatenate([get((g * REP + rr) * HD, (g * REP + rr + 1) * HD) for rr in range(REP)], axis=0)


def _sink_column(sink_ref, g):
    return jnp.concatenate([jnp.full((BLK, 1), sink_ref[g * REP + rr], F32) for rr in range(REP)], axis=0)


def _kv_window(blk, g):
    kk = jnp.concatenate([blk.prev_kv(g * HD, (g + 1) * HD),
                          blk.cols(O_K + g * HD, O_K + (g + 1) * HD)], axis=0).astype(BF16)
    vv = jnp.concatenate([blk.prev_kv(BLK + g * HD, BLK + (g + 1) * HD),
                          blk.cols(O_V + g * HD, O_V + (g + 1) * HD)], axis=0).astype(BF16)
    return kk, vv


def _mix_common(blk, vn_ref, wcat_ref, bexp_ref, pwbd_ref):
    u, tu = _gelu(blk.cols(O_U, O_G))
    gv, tv = _gelu(blk.cols(O_G, O_P))
    xh, rv = _rms(gv)
    vnb = (xh * vn_ref[...]).astype(BF16)
    head = _lane_head((BLK, 256))
    vn_bd = jnp.concatenate([jnp.where(head == h, vnb, jnp.zeros_like(vnb)) for h in range(4)], axis=0)
    row = lax.broadcasted_iota(jnp.int32, (BLK, 4 * BLK), 0)
    col = lax.broadcasted_iota(jnp.int32, (BLK, 4 * BLK), 1) & (BLK - 1)
    tril = col <= row
    wcat = jnp.where(tril, wcat_ref[...], 0.0).astype(BF16)
    f = _nn(wcat, vn_bd) + bexp_ref[...]
    p = blk.cols(O_P, INW)
    e = jnp.concatenate([blk.prev_p(), p], axis=0)
    cnt = _pool_count(blk.index)
    diff = (_window_sums(e, True)[BLK:, :] / cnt - p).astype(BF16)
    pwbd = pwbd_ref[...].astype(BF16)
    pout = _nn(diff, pwbd)
    return dict(u=u, tu=tu, tv=tv, xh=xh, rv=rv, vn_bd=vn_bd, wcat=wcat, f=f, cnt=cnt, diff=diff, pwbd=pwbd,
                pout=pout, tril=tril, head=head)


def _mix_fwd(z, sinks, vnorm, wcat, bexp, pwbd, pscale, name):
    s = z.shape[0]
    nt = s // TILE

    def body(sink_ref, zc_ref, zkvp_ref, zpp_ref, vn_ref, wcat_ref, bexp_ref, pwbd_ref, ps_ref, y_ref, lse_ref):
        n = pl.program_id(0)
        lse_ref[...] = jnp.zeros_like(lse_ref)
        band = _band_mask(BLK)
        for j in range(MIX_NB):
            blk = _Block(n, j, zc_ref, zkvp_ref, zpp_ref)
            valid = _block_mask(band, blk)
            for g in range(N_KV):
                kk, vv = _kv_window(blk, g)
                for rr in range(REP):
                    h = g * REP + rr
                    qh = (blk.cols(h * HD, (h + 1) * HD) * ATTN_SCALE).astype(BF16)
                    sc = jnp.where(valid, _nt(qh, kk), NEG)
                    sink = sink_ref[h]
                    m = jnp.maximum(jnp.max(sc, axis=-1, keepdims=True), sink)
                    ex = jnp.exp(sc - m)
                    den = jnp.sum(ex, axis=-1, keepdims=True) + jnp.exp(sink - m)
                    y_ref[blk.r, h * HD:(h + 1) * HD] = _nn((ex / den).astype(BF16), vv).astype(BF16)
                    lse_ref[blk.r, h:h + 1] = m + jnp.log(den)
            c = _mix_common(blk, vn_ref, wcat_ref, bexp_ref, pwbd_ref)
            y_ref[blk.r, 512:768] = (c["u"] * c["f"]).astype(BF16)
            y_ref[blk.r, 768:1024] = (c["pout"] * ps_ref[...]).astype(BF16)

    halo = lambda n: jnp.maximum(MIX_NB * n - 1, 0)
    return _call(
        body, name=name, grid=(nt,),
        in_specs=[pl.BlockSpec(memory_space=pltpu.SMEM),
                  pl.BlockSpec((TILE, INW), lambda n: (n, 0)),
                  pl.BlockSpec((BLK, 256), lambda n: (halo(n), 2)),
                  pl.BlockSpec((BLK, 256), lambda n: (halo(n), 5)),
                  _fixspec(1, 256), _fixspec(BLK, 4 * BLK), _fixspec(BLK, 256), _fixspec(256, 256), _fixspec(1, 256)],
        out_specs=[pl.BlockSpec((TILE, D), lambda n: (n, 0)), pl.BlockSpec((TILE, 128), lambda n: (n, 0))],
        out_shape=[jax.ShapeDtypeStruct((s, D), BF16), jax.ShapeDtypeStruct((s, 128), F32)],
        compiler_params=_params(("arbitrary",)),
    )(sinks, z, z, z, vnorm, wcat, bexp, pwbd, pscale)


def _mix_bwd(z, dy, lse, sinks, vnorm, wcat, bexp, pwbd, pscale, name, comm=()):
    s = z.shape[0]
    nt = s // TILE
    last = slice(TILE - BLK, TILE)

    def body(sink_ref, zc_ref, zkvp_ref, zpp_ref, dy_ref, lse_ref, vn_ref, wcat_ref, bexp_ref, pwbd_ref, ps_ref,
             dz_ref, dsink_ref, dvn_ref, dws_ref, dbt_ref, dpw_ref, dps_ref, carry_ref, ddc_ref):
        n = pl.program_id(0)

        @pl.when(n == 0)
        def _():
            carry_ref[...] = jnp.zeros_like(carry_ref)
            ddc_ref[...] = jnp.zeros_like(ddc_ref)
            dsink_ref[...] = jnp.zeros_like(dsink_ref)
            dvn_ref[...] = jnp.zeros_like(dvn_ref)
            dws_ref[...] = jnp.zeros_like(dws_ref)
            dbt_ref[...] = jnp.zeros_like(dbt_ref)
            dpw_ref[...] = jnp.zeros_like(dpw_ref)
            dps_ref[...] = jnp.zeros_like(dps_ref)

        def block_grads(j):
            blk = _Block(n, j, zc_ref, zkvp_ref, zpp_ref)
            valid = _block_mask(_band_mask(GROUP_ROWS), blk)
            out = dict(dq=[], dsink=[], dbt=[])
            dk_prev, dk_cur, dv_prev, dv_cur = [], [], [], []
            for g in range(N_KV):
                kk, vv = _kv_window(blk, g)
                q4 = _stack_heads(blk.cols, g).astype(BF16)
                do4 = _stack_heads(lambda c0, c1: dy_ref[blk.r, c0:c1], g).astype(BF16)
                lse4 = jnp.concatenate([lse_ref[blk.r, g * REP + rr:g * REP + rr + 1] for rr in range(REP)], axis=0)
                sc = jnp.where(valid, _nt(q4, kk) * ATTN_SCALE, NEG)
                pr = jnp.exp(sc - lse4)
                dp = _nt(do4, vv)
                delta = jnp.sum(pr * dp, axis=-1, keepdims=True)
                ds = ((pr * (dp - delta)) * ATTN_SCALE).astype(BF16)
                sunk = jnp.exp(_sink_column(sink_ref, g) - lse4) * delta
                dq4 = _nn(ds, kk)
                for rr in range(REP):
                    out["dsink"].append(-jnp.sum(sunk[rr * BLK:(rr + 1) * BLK], axis=0, keepdims=True))
                    out["dq"].append(dq4[rr * BLK:(rr + 1) * BLK])
                dkk = _tn(ds, q4)
                dvv = _tn(pr.astype(BF16), do4)
                dk_prev.append(dkk[:BLK]); dk_cur.append(dkk[BLK:])
                dv_prev.append(dvv[:BLK]); dv_cur.append(dvv[BLK:])
            out["dk_prev"], out["dk_cur"] = jnp.concatenate(dk_prev, axis=1), jnp.concatenate(dk_cur, axis=1)
            out["dv_prev"], out["dv_cur"] = jnp.concatenate(dv_prev, axis=1), jnp.concatenate(dv_cur, axis=1)
            c = _mix_common(blk, vn_ref, wcat_ref, bexp_ref, pwbd_ref)
            dyg = dy_ref[blk.r, 512:768]
            du = dyg * c["f"]
            df = dyg * c["u"]
            out["dzu"] = du * _gelu_grad(c["tu"])
            dfb = df.astype(BF16)
            for h in range(4):
                out["dbt"].append(jnp.sum(df[:, h * HD:(h + 1) * HD], axis=1, keepdims=True))
            out["dws"] = jnp.where(c["tril"], _nt(dfb, c["vn_bd"]), 0.0)
            dvn_bd = _tn(c["wcat"], dfb)
            dvn = functools.reduce(lambda a, b: a + b, [
                jnp.where(c["head"] == h, dvn_bd[h * BLK:(h + 1) * BLK], 0.0) for h in range(4)])
            dgv, out["dvn"] = _rms_bwd(dvn, c["xh"], c["rv"], vn_ref[...])
            out["dzv"] = dgv * _gelu_grad(c["tv"])
            dyp = dy_ref[blk.r, 768:1024]
            out["dps"] = jnp.sum(dyp * c["pout"], axis=0, keepdims=True)
            dout = (dyp * ps_ref[...]).astype(BF16)
            out["dpw"] = _tn(c["diff"], dout)
            out["ddiff"] = _nt(dout, c["pwbd"])
            out["dd"] = out["ddiff"] / c["cnt"]
            return out

        def write_previous_tile(dd_next, dk_next, dv_next):
            if MIX_NB > 1:
                dz_ref[0:TILE - BLK, :] = carry_ref[0:TILE - BLK, :].astype(BF16)
            rs = _window_sums(jnp.concatenate([ddc_ref[...], dd_next], axis=0), False)
            dz_ref[last, 0:O_K] = carry_ref[last, 0:O_K].astype(BF16)
            dz_ref[last, O_K:O_V] = (carry_ref[last, O_K:O_V] + dk_next).astype(BF16)
            dz_ref[last, O_V:O_U] = (carry_ref[last, O_V:O_U] + dv_next).astype(BF16)
            dz_ref[last, O_U:O_P] = carry_ref[last, O_U:O_P].astype(BF16)
            dz_ref[last, O_P:INW] = (carry_ref[last, O_P:INW] + rs[:BLK, :]).astype(BF16)

        @pl.when(n < nt)
        def _():
            parts = [block_grads(j) for j in range(MIX_NB)]
            total = lambda key, i=None: functools.reduce(
                lambda a, b: a + b, [p[key] if i is None else p[key][i] for p in parts])
            for h in range(N_HEADS):
                dsink_ref[h:h + 1, :] += total("dsink", h) + jnp.zeros((1, 128), F32)
            for h in range(4):
                dbt_ref[:, h:h + 1] += total("dbt", h)
            dws_ref[...] += total("dws")
            dpw_ref[...] += total("dpw")
            dvn_ref[0:1, :] += total("dvn")
            dps_ref[0:1, :] += total("dps")
            write_previous_tile(parts[0]["dd"], parts[0]["dk_prev"], parts[0]["dv_prev"])
            for j, p in enumerate(parts):
                r = slice(j * BLK, (j + 1) * BLK)
                nxt = parts[j + 1] if j + 1 < MIX_NB else None
                for h in range(N_HEADS):
                    carry_ref[r, h * HD:(h + 1) * HD] = p["dq"][h]
                carry_ref[r, O_U:O_G] = p["dzu"]
                carry_ref[r, O_G:O_P] = p["dzv"]
                if nxt is None:
                    carry_ref[r, O_K:O_V] = p["dk_cur"]
                    carry_ref[r, O_V:O_U] = p["dv_cur"]
                    carry_ref[r, O_P:INW] = -p["ddiff"]
                    ddc_ref[...] = p["dd"]
                else:
                    rs = _window_sums(jnp.concatenate([p["dd"], nxt["dd"]], axis=0), False)
                    carry_ref[r, O_K:O_V] = p["dk_cur"] + nxt["dk_prev"]
                    carry_ref[r, O_V:O_U] = p["dv_cur"] + nxt["dv_prev"]
                    carry_ref[r, O_P:INW] = rs[:BLK, :] - p["ddiff"]

        @pl.when(n == nt)
        def _():
            none = jnp.zeros((BLK, BLK), F32)
            write_previous_tile(jnp.zeros((BLK, 256), F32), none, none)

    cur = lambda n: jnp.minimum(n, nt - 1)
    done = lambda n: jnp.maximum(n - 1, 0)
    halo = lambda n: jnp.maximum(MIX_NB * jnp.minimum(n, nt - 1) - 1, 0)
    return _host(
        comm, body, name=name, grid=(nt + 1,),
        in_specs=[pl.BlockSpec(memory_space=pltpu.SMEM),
                  pl.BlockSpec((TILE, INW), lambda n: (cur(n), 0)),
                  pl.BlockSpec((BLK, 256), lambda n: (halo(n), 2)),
                  pl.BlockSpec((BLK, 256), lambda n: (halo(n), 5)),
                  pl.BlockSpec((TILE, D), lambda n: (cur(n), 0)),
                  pl.BlockSpec((TILE, 128), lambda n: (cur(n), 0)),
                  _fixspec(1, 256), _fixspec(BLK, 4 * BLK), _fixspec(BLK, 256), _fixspec(256, 256), _fixspec(1, 256)],
        out_specs=[pl.BlockSpec((TILE, INW), lambda n: (done(n), 0)),
                   _fixspec(8, 128), _fixspec(8, 256), _fixspec(BLK, 4 * BLK), _fixspec(BLK, 128),
                   _fixspec(256, 256), _fixspec(8, 256)],
        out_shape=[jax.ShapeDtypeStruct((s, INW), BF16), jax.ShapeDtypeStruct((8, 128), F32),
                   jax.ShapeDtypeStruct((8, 256), F32), jax.ShapeDtypeStruct((BLK, 4 * BLK), F32),
                   jax.ShapeDtypeStruct((BLK, 128), F32), jax.ShapeDtypeStruct((256, 256), F32),
                   jax.ShapeDtypeStruct((8, 256), F32)],
        scratch_shapes=[pltpu.VMEM((TILE, INW), F32), pltpu.VMEM((BLK, 256), F32)],
        args=(sinks, z, z, z, dy, lse, vnorm, wcat, bexp, pwbd, pscale))


def _position():
    x, y, c = lax.axis_index("x"), lax.axis_index("y"), lax.axis_index("c")
    return x, y, c


class _GatherTask:
    def __init__(self, srcs):
        self.inputs = list(srcs)
        ng = len(srcs)
        self.out_shape = [jax.ShapeDtypeStruct((a.shape[0], N_DEV) + a.shape[1:], a.dtype) for a in srcs]
        self.scratch = [pltpu.SemaphoreType.DMA((ng, 7)), pltpu.SemaphoreType.DMA((ng, 7)),
                        pltpu.SemaphoreType.DMA((ng,))]

    def _plan(self, src, dst, sems):
        send_sems, recv_sems, local_sems = sems
        ng = len(src)
        x, y, c = _position()
        me, sibling = (x, y, c), (x, y, 1 - c)
        chips = [(1 - x, y), (x, 1 - y), (1 - x, 1 - y)]

        def slot(pos):
            return 4 * pos[0] + 2 * pos[1] + pos[2]

        def copy(gi, k, block, to, from_src=False):
            rows = dst[gi].at[:, slot(block)]
            return pltpu.make_async_remote_copy(
                src_ref=src[gi] if from_src else rows, dst_ref=rows,
                send_sem=send_sems.at[gi, k], recv_sem=recv_sems.at[gi, k],
                device_id=to, device_id_type=MESH)

        make = functools.partial
        mine = [make(pltpu.make_async_copy, src[gi], dst[gi].at[:, slot(me)], local_sems.at[gi]) for gi in range(ng)]
        first = []
        for gi in range(ng):
            first.append(make(copy, gi, 0, me, sibling, True))
            first += [make(copy, gi, 1 + j, me, (*chip, c), True) for j, chip in enumerate(chips)]
        passed = [make(copy, gi, 4 + j, (*chip, c), sibling) for j, chip in enumerate(chips) for gi in range(ng)]
        arrive_ici = [make(copy, gi, 1 + j, (*chip, c), me) for j, chip in enumerate(chips) for gi in range(ng)]
        arrive_d2d = [make(copy, gi, 0, sibling, me) for gi in range(ng)]
        arrive_d2d += [make(copy, gi, 4 + j, (*chip, 1 - c), me) for j, chip in enumerate(chips) for gi in range(ng)]
        return mine, first, passed, arrive_ici, arrive_d2d

    def start(self, src, dst, sems):
        mine, first, _, _, _ = self._plan(src, dst, sems)
        for cp in mine + first:
            cp().start()

    def mid(self, src, dst, sems):
        _, _, passed, arrive_ici, _ = self._plan(src, dst, sems)
        for arrived, fw in zip(arrive_ici, passed):
            arrived().wait_recv()
            fw().start()

    def finish(self, src, dst, sems):
        mine, first, passed, _, arrive_d2d = self._plan(src, dst, sems)
        for cp in arrive_d2d:
            cp().wait_recv()
        for cp in first + passed:
            cp().wait_send()
        for cp in mine:
            cp().wait()


class _SiblingTask:
    def __init__(self, g5s):
        self.inputs = list(g5s)
        ng = len(g5s)
        self.out_shape = [jax.ShapeDtypeStruct((a.shape[0], 4) + a.shape[3:], a.dtype) for a in g5s]
        self.scratch = [pltpu.SemaphoreType.DMA((ng,)), pltpu.SemaphoreType.DMA((ng,))]

    def _plan(self, src, dst, sems):
        send_sems, recv_sems = sems
        x, y, c = _position()
        return [functools.partial(
            pltpu.make_async_remote_copy,
            src_ref=src[gi].at[:, :, 1 - c], dst_ref=dst[gi],
            send_sem=send_sems.at[gi], recv_sem=recv_sems.at[gi],
            device_id=(x, y, 1 - c), device_id_type=MESH) for gi in range(len(src))]

    def start(self, src, dst, sems):
        for cp in self._plan(src, dst, sems):
            cp().start()

    def mid(self, src, dst, sems):
        pass

    def finish(self, src, dst, sems):
        for cp in self._plan(src, dst, sems):
            cp().wait()


class _ChipTask(_SiblingTask):
    def __init__(self, sbs):
        self.inputs = list(sbs)
        ng = len(sbs)
        self.out_shape = [jax.ShapeDtypeStruct(a.shape, a.dtype) for a in sbs]
        self.scratch = [pltpu.SemaphoreType.DMA((ng, 3)), pltpu.SemaphoreType.DMA((ng, 3))]

    def _plan(self, src, dst, sems):
        send_sems, recv_sems = sems
        x, y, c = _position()
        jme = 2 * x + y
        chips = [(1 - x, y), (x, 1 - y), (1 - x, 1 - y)]
        return [functools.partial(
            pltpu.make_async_remote_copy,
            src_ref=src[gi].at[:, 2 * chip[0] + chip[1]], dst_ref=dst[gi].at[:, jme],
            send_sem=send_sems.at[gi, k], recv_sem=recv_sems.at[gi, k],
            device_id=(*chip, c), device_id_type=MESH) for k, chip in enumerate(chips) for gi in range(len(src))]


def _alone(task, name):
    n_in, n_out = len(task.inputs), len(task.out_shape)

    def body(*refs):
        parts = (refs[:n_in], refs[n_in:n_in + n_out], refs[n_in + n_out:])
        task.start(*parts)
        task.mid(*parts)
        task.finish(*parts)

    any_spec = pl.BlockSpec(memory_space=pl.ANY)
    return _call(body, name=name, in_specs=[any_spec] * n_in, out_specs=[any_spec] * n_out,
                 out_shape=task.out_shape, scratch_shapes=task.scratch)(*task.inputs)


def _core_sum(ids, g5, r1, name):
    n, _, _, rows, _ = g5.shape

    def body(ids_ref, g_ref, r_ref, sb_ref, own_ref):
        j = pl.program_id(2)
        t = g_ref[...] + r_ref[...]
        sb_ref[...] = t.astype(BF16)

        @pl.when(j == ids_ref[1])
        def _():
            own_ref[...] = t

    grid_spec = pltpu.PrefetchScalarGridSpec(
        num_scalar_prefetch=1, grid=(n, 1, 4),
        in_specs=[pl.BlockSpec((None, None, None, rows, D), lambda i, t, j, ids: (i, j, ids[0], t, 0)),
                  pl.BlockSpec((None, None, rows, D), lambda i, t, j, ids: (i, j, t, 0))],
        out_specs=[pl.BlockSpec((None, None, rows, D), lambda i, t, j, ids: (i, j, t, 0)),
                   pl.BlockSpec((None, rows, D), lambda i, t, j, ids: (i, t, 0))])
    return _call(
        body, name=name, grid_spec=grid_spec,
        out_shape=[jax.ShapeDtypeStruct((n, 4, rows, D), BF16), jax.ShapeDtypeStruct((n, rows, D), F32)],
        compiler_params=_params(("arbitrary", "arbitrary", "arbitrary")),
    )(ids, g5, r1)


def _chip_sum(others, own, r2, name):
    n, rows, _ = own.shape

    def body(oth_ref, own_ref, r0_ref, r1_ref, r2_ref, out_ref):
        out_ref[...] = ((own_ref[...] + r0_ref[...].astype(F32)) + r1_ref[...].astype(F32)) \
            + r2_ref[...].astype(F32)

    def rspec(k):
        return pl.BlockSpec((None, None, rows, D), lambda i, oth, k=k: (i, oth[k], 0, 0))

    grid_spec = pltpu.PrefetchScalarGridSpec(
        num_scalar_prefetch=1, grid=(n,),
        in_specs=[pl.BlockSpec((None, rows, D), lambda i, oth: (i, 0, 0)), rspec(0), rspec(1), rspec(2)],
        out_specs=pl.BlockSpec((None, rows, D), lambda i, oth: (i, 0, 0)))
    return _call(
        body, name=name, grid_spec=grid_spec,
        out_shape=jax.ShapeDtypeStruct((n, rows, D), F32),
        compiler_params=_params(("arbitrary",)),
    )(others, own, r2, r2, r2)


def _adam_math(w, g, m, v):
    m = ADAM_B1 * m + (1.0 - ADAM_B1) * g
    v = ADAM_B2 * v + (1.0 - ADAM_B2) * (g * g)
    m_hat = m / (1.0 - ADAM_B1 ** ADAM_STEP)
    v_hat = v / (1.0 - ADAM_B2 ** ADAM_STEP)
    delta = -ADAM_LR * (m_hat / (jnp.sqrt(v_hat) + ADAM_EPS) + ADAM_WD * w)
    return delta, m, v


def _adamw(w, g, m, v, name):
    shape = w.shape
    c = shape[-1]
    r = w.size // c
    rb = max(d for d in range(8, min(r, 512) + 1, 8) if r % d == 0)

    def body(w_ref, g_ref, m_ref, v_ref, d_ref, mo_ref, vo_ref):
        d_ref[...], mo_ref[...], vo_ref[...] = _adam_math(w_ref[...], g_ref[...], m_ref[...], v_ref[...])

    spec = _rowspec(rb, c)
    outs = _call(
        body, name=name, grid=(r // rb,),
        in_specs=[spec] * 4, out_specs=[spec] * 3,
        out_shape=[jax.ShapeDtypeStruct((r, c), F32)] * 3,
        compiler_params=_params(("arbitrary",)),
    )(*[t.reshape(r, c) for t in (w, g, m, v)])
    return [o.reshape(shape) for o in outs]


def _adamw_small(parts, w, m, v, name):
    def body(p_ref, w_ref, m_ref, v_ref, g_ref, d_ref, mo_ref, vo_ref):
        g = p_ref[0]
        for dev in range(1, N_DEV):
            g = g + p_ref[dev]
        g_ref[...] = g
        d_ref[...], mo_ref[...], vo_ref[...] = _adam_math(w_ref[...], g, m_ref[...], v_ref[...])

    return _call(
        body, name=name,
        out_shape=[jax.ShapeDtypeStruct(w.shape, F32)] * 4,
        compiler_params=_params(),
    )(parts, w, m, v)


SMALL = ["ffn1_norm", "mix_norm", "attn_sinks", "gmlp_v_norm", "gmlp_w_s", "gmlp_b", "pool_w", "pool_scale",
         "ffn2_norm", "final_norm"]


def _piece_rows(size):
    return -(-size // 1024) * 8


def _pack_small(arrs, extra=None):
    pieces = []
    for a in list(arrs) + [jnp.zeros((1,), F32) if extra is None else extra]:
        f = a.reshape(-1)
        pieces.append(jnp.pad(f, (0, _piece_rows(f.shape[0]) * 128 - f.shape[0])).reshape(-1, 128))
    return jnp.concatenate(pieces, axis=0)


def _unpack_small(packed, like):
    out, off = [], 0
    for a in like:
        rows = _piece_rows(a.size)
        out.append(packed[off:off + rows].reshape(-1)[:a.size].reshape(a.shape))
        off += rows
    return out, packed[off, 0]


def kernel(x, ffn1_norm, ffn1_w_gate, ffn1_w_up, ffn1_w_down, mix_norm, w_in, attn_sinks, gmlp_v_norm, gmlp_w_s, gmlp_b, pool_w, pool_scale, w_out, ffn2_norm, ffn2_w_gate, ffn2_w_up, ffn2_w_down, final_norm, loss_target, m_ffn1_norm, m_ffn1_w_gate, m_ffn1_w_up, m_ffn1_w_down, m_mix_norm, m_w_in, m_attn_sinks, m_gmlp_v_norm, m_gmlp_w_s, m_gmlp_b, m_pool_w, m_pool_scale, m_w_out, m_ffn2_norm, m_ffn2_w_gate, m_ffn2_w_up, m_ffn2_w_down, m_final_norm, v_ffn1_norm, v_ffn1_w_gate, v_ffn1_w_up, v_ffn1_w_down, v_mix_norm, v_w_in, v_attn_sinks, v_gmlp_v_norm, v_gmlp_w_s, v_gmlp_b, v_pool_w, v_pool_scale, v_w_out, v_ffn2_norm, v_ffn2_w_gate, v_ffn2_w_up, v_ffn2_w_down, v_final_norm):
    s = x.shape[1]
    xi, yi, ci = _position()
    ids = jnp.stack([ci, 2 * xi + yi]).astype(jnp.int32)
    jme = 2 * xi + yi
    others = jnp.stack([k + (k >= jme).astype(jnp.int32) for k in range(3)]).astype(jnp.int32)
    t = lambda a: jnp.swapaxes(a, -1, -2)
    row = lambda a: a.reshape(1, -1)
    full = lambda a: a.reshape(a.shape[0], -1, D)

    loc_f1 = [jnp.stack([t(ffn1_w_gate[l]), t(ffn1_w_up[l]), ffn1_w_down[l]]).astype(BF16) for l in range(DEPTH)]
    loc_f2 = [jnp.stack([t(ffn2_w_gate[l]), t(ffn2_w_up[l]), ffn2_w_down[l]]).astype(BF16) for l in range(DEPTH)]
    loc_in = [t(w_in[l])[None].astype(BF16) for l in range(DEPTH)]
    loc_out = [w_out[l][None].astype(BF16) for l in range(DEPTH)]

    (wf1,) = _alone(_GatherTask([loc_f1[0]]), "gather_first")
    wf1 = full(wf1)
    xc = x.reshape(s, D)
    saved = []
    for l in range(DEPTH):
        x0 = xc
        if l == 0:
            (x1, *act1), ((wf2, win, wout),) = _ffn_fwd(
                x0, row(ffn1_norm[l]), wf1, 0, f"ffn1_fwd_{l}", comm=[_GatherTask([loc_f2[0], loc_in[0], loc_out[0]])])
        else:
            (x1, *act1), ((wf2,),) = _ffn_fwd(
                x0, row(ffn1_norm[l]), wf1, 0, f"ffn1_fwd_{l}", comm=[_GatherTask([loc_f2[1]])])
        wf2, win, wout = full(wf2), full(win), full(wout)
        z, hmix = _mixin_fwd(x1, row(mix_norm[l]), win, 0, f"mixin_fwd_{l}")
        wcat = jnp.concatenate([gmlp_w_s[l][h] for h in range(4)], axis=1)
        bexp = jnp.repeat(t(gmlp_b[l]), HD, axis=1)
        pwbd = jnp.zeros((256, 256), F32)
        for g in range(4):
            pwbd = pwbd.at[g * HD:(g + 1) * HD, g * HD:(g + 1) * HD].set(pool_w[l][g])
        mixp = (attn_sinks[l], row(gmlp_v_norm[l]), wcat, bexp, pwbd, row(pool_scale[l]))
        y, lse = _mix_fwd(z, *mixp, f"mix_fwd_{l}")
        keep = (x0, act1, wf1, x1, z, hmix, mixp, y, lse, win, wout)
        if l == 0:
            (x3, *act2, x2), ((wf1, win, wout),) = _ffn_fwd(
                x1, row(ffn2_norm[l]), wf2, 0, f"ffn2_fwd_{l}", mixer=(y, wout),
                comm=[_GatherTask([loc_f1[1], loc_in[1], loc_out[1]])])
            wf1 = full(wf1)
        else:
            x3, *act2, x2 = _ffn_fwd(x1, row(ffn2_norm[l]), wf2, 0, f"ffn2_fwd_{l}", mixer=(y, wout))
        saved.append(keep + (x2, wf2, act2))
        xc = x3
    dx, loss_part, d_final = _loss_head(xc, row(final_norm), loss_target.reshape(s, D), "loss_head")

    def five(g):
        return g.reshape(g.shape[0], 4, 2, g.shape[1] // N_DEV, D)

    def core_sums(g5s, r1s, tag):
        res = [_core_sum(ids, g5, r1, f"core_sum_{tag}_{i}") for i, (g5, r1) in enumerate(zip(g5s, r1s))]
        return [sb for sb, _ in res], [own for _, own in res]

    def chip_sums(owns, r2s, tag):
        return [_chip_sum(others, own, r2, f"chip_sum_{tag}_{i}") for i, (own, r2) in enumerate(zip(owns, r2s))]

    def mix_small(l, dsink, dvn, dws, dbt, dpw, dps):
        return {("attn_sinks", l): dsink[:, 0], ("gmlp_v_norm", l): dvn[0],
                ("gmlp_w_s", l): jnp.stack([dws[:, h * BLK:(h + 1) * BLK] for h in range(4)]),
                ("gmlp_b", l): t(dbt[:, :4]),
                ("pool_w", l): jnp.stack([dpw[g * HD:(g + 1) * HD, g * HD:(g + 1) * HD] for g in range(4)]),
                ("pool_scale", l): dps[0]}

    small = {}
    red = {}
    gate_up = dict(cols=2 * FF, slab_rows=FF)
    x0, (p11, p21, hid1), wf1, x1, z, hmix, mixp, y, lse, win, wout, x2, wf2, (p12, p22, hid2) = saved[1]
    dx, dab, h, dyb, dg, dymix, dxb = _ffn_bwd(x2, row(ffn2_norm[1]), dx, p12, p22, wf2, 0, "ffn2_bwd_1", w_out=wout)
    small[("ffn2_norm", 1)] = dg[0]
    g = _wgrad(dab, h, None, 3, 0, "wgrad_gate_up2_1", **gate_up)
    g = _wgrad(hid2, dyb, g, 3, 2, "wgrad_down2_1")
    a5 = [five(g)]
    g_out, (a_r1,) = _wgrad(y, dxb, None, 1, 0, "wgrad_out_1", comm=[_SiblingTask(a5)])
    a_sb, a_own = core_sums(a5, a_r1, "a")
    (dz, dsink, dvn, dws, dbt, dpw, dps), (a_r2,) = _mix_bwd(z, dymix, lse, *mixp, "mix_bwd_1", comm=[_ChipTask(a_sb)])
    (red[("f2", 1)],) = chip_sums(a_own, a_r2, "a")
    small.update(mix_small(1, dsink, dvn, dws, dbt, dpw, dps))
    dx, dg = _mixin_bwd(x1, row(mix_norm[1]), dz, dx, win, 0, "mixin_bwd_1")
    small[("mix_norm", 1)] = dg[0]
    g_in = _wgrad(dz, hmix, None, 1, 0, "wgrad_in_1")
    b5 = [five(g_out), five(g_in)]
    (dx, dab, h, dyb, dg), (b_r1,) = _ffn_bwd(x0, row(ffn1_norm[1]), dx, p11, p21, wf1, 0, "ffn1_bwd_1",
                                              comm=[_SiblingTask(b5)])
    small[("ffn1_norm", 1)] = dg[0]
    b_sb, b_own = core_sums(b5, b_r1, "b")
    g_gu, (b_r2,) = _wgrad(dab, h, None, 2, 0, "wgrad_gate_up1_1", comm=[_ChipTask(b_sb)], **gate_up)
    red[("out", 1)], red[("in", 1)] = chip_sums(b_own, b_r2, "b")
    g_down = _wgrad(hid1, dyb, None, 1, 0, "wgrad_down1_1")
    c5 = [five(g_gu), five(g_down)]
    x0, (p11, p21, hid1), wf1, x1, z, hmix, mixp, y, lse, win, wout, x2, wf2, (p12, p22, hid2) = saved[0]
    (dx, dab, h, dyb, dg, dymix, dxb), (c_r1,) = _ffn_bwd(x2, row(ffn2_norm[0]), dx, p12, p22, wf2, 0, "ffn2_bwd_0",
                                                          comm=[_SiblingTask(c5)], w_out=wout)
    small[("ffn2_norm", 0)] = dg[0]
    c_sb, c_own = core_sums(c5, c_r1, "c")
    g_gu, (c_r2a,) = _wgrad(dab, h, None, 2, 0, "wgrad_gate_up2_0", comm=[_ChipTask(c_sb[0:1])], **gate_up)
    d5a = [five(g_gu)]
    g_down, (c_r2b, d_r1a) = _wgrad(hid2, dyb, None, 1, 0, "wgrad_down2_0",
                                    comm=[_ChipTask(c_sb[1:2]), _SiblingTask(d5a)])
    red[("f1", 1)] = jnp.concatenate(chip_sums(c_own, c_r2a + c_r2b, "c"), axis=0)
    da_sb, da_own = core_sums(d5a, d_r1a, "da")
    d5b = [five(g_down)]
    (dz, dsink, dvn, dws, dbt, dpw, dps), (d_r2a, d_r1b) = _mix_bwd(
        z, dymix, lse, *mixp, "mix_bwd_0", comm=[_ChipTask(da_sb), _SiblingTask(d5b)])
    small.update(mix_small(0, dsink, dvn, dws, dbt, dpw, dps))
    db_sb, db_own = core_sums(d5b, d_r1b, "db")
    (dx, dg), (d_r2b,) = _mixin_bwd(x1, row(mix_norm[0]), dz, dx, win, 0, "mixin_bwd_0", comm=[_ChipTask(db_sb)])
    small[("mix_norm", 0)] = dg[0]
    red[("f2", 0)] = jnp.concatenate(chip_sums(da_own + db_own, d_r2a + d_r2b, "d"), axis=0)
    dx, dab, h, dyb, dg = _ffn_bwd(x0, row(ffn1_norm[0]), dx, p11, p21, wf1, 0, "ffn1_bwd_0")
    small[("ffn1_norm", 0)] = dg[0]
    grad_x = dx.reshape(1, s, D)

    part = [d_final[0] if nm == "final_norm" else jnp.stack([small[(nm, l)] for l in range(DEPTH)]) for nm in SMALL]
    packed = _pack_small(part, loss_part[0, 0])
    g_gate, ((gathered,),) = _wgrad(dab, h, None, 1, 0, "wgrad_gate1_0", comm=[_GatherTask([packed[None]])], cols=FF)
    f5 = [five(g_gate)]
    g_up, (f_r1,) = _wgrad(dab, h, None, 1, 0, "wgrad_up1_0", comm=[_SiblingTask(f5)], col0=FF, cols=FF)
    f_sb, f_own = core_sums(f5, f_r1, "f")
    u5 = [five(g_up)]
    g_down, (f_r2, u_r1) = _wgrad(hid1, dyb, None, 1, 0, "wgrad_down1_0", comm=[_ChipTask(f_sb), _SiblingTask(u5)])
    u_sb, u_own = core_sums(u5, u_r1, "u")
    w5 = [five(g_down)]
    g_out, (u_r2, w_r1) = _wgrad(y, dxb, None, 1, 0, "wgrad_out_0", comm=[_ChipTask(u_sb), _SiblingTask(w5)])
    w_sb, w_own = core_sums(w5, w_r1, "w")
    o5 = [five(g_out)]
    g_in, (w_r2, o_r1) = _wgrad(dz, hmix, None, 1, 0, "wgrad_in_0", comm=[_ChipTask(w_sb), _SiblingTask(o5)])
    red[("f1", 0)] = jnp.concatenate(chip_sums(f_own + u_own + w_own, f_r2 + u_r2 + w_r2, "f"), axis=0)
    o_sb, o_own = core_sums(o5, o_r1, "o")
    i5 = [five(g_in)]
    i_r1 = _alone(_SiblingTask(i5), "reduce_sibling_last")
    i_sb, i_own = core_sums(i5, i_r1, "i")
    e_r2 = _alone(_ChipTask(o_sb + i_sb), "reduce_chips_last")
    red[("out", 0)], red[("in", 0)] = chip_sums(o_own + i_own, e_r2, "e")

    grads = {}
    red_rows = {}
    for k, nm in enumerate(["w_gate", "w_up", "w_down"]):
        for f in ("f1", "f2"):
            red_rows[f"ffn{f[1]}_{nm}"] = jnp.stack([red[(f, l)][k] for l in range(DEPTH)])
    red_rows["w_in"] = jnp.concatenate([red[("in", l)] for l in range(DEPTH)], axis=0)
    red_rows["w_out"] = jnp.concatenate([red[("out", l)] for l in range(DEPTH)], axis=0)
    transposed = ("ffn1_w_gate", "ffn1_w_up", "ffn2_w_gate", "ffn2_w_up", "w_in")

    small_w = dict(ffn1_norm=ffn1_norm, mix_norm=mix_norm, attn_sinks=attn_sinks, gmlp_v_norm=gmlp_v_norm,
                   gmlp_w_s=gmlp_w_s, gmlp_b=gmlp_b, pool_w=pool_w, pool_scale=pool_scale, ffn2_norm=ffn2_norm,
                   final_norm=final_norm)
    small_m = dict(ffn1_norm=m_ffn1_norm, mix_norm=m_mix_norm, attn_sinks=m_attn_sinks, gmlp_v_norm=m_gmlp_v_norm,
                   gmlp_w_s=m_gmlp_w_s, gmlp_b=m_gmlp_b, pool_w=m_pool_w, pool_scale=m_pool_scale,
                   ffn2_norm=m_ffn2_norm, final_norm=m_final_norm)
    small_v = dict(ffn1_norm=v_ffn1_norm, mix_norm=v_mix_norm, attn_sinks=v_attn_sinks, gmlp_v_norm=v_gmlp_v_norm,
                   gmlp_w_s=v_gmlp_w_s, gmlp_b=v_gmlp_b, pool_w=v_pool_w, pool_scale=v_pool_scale,
                   ffn2_norm=v_ffn2_norm, final_norm=v_final_norm)
    sg, sd, sm, sv = _adamw_small(gathered[0], _pack_small([small_w[nm] for nm in SMALL]),
                                  _pack_small([small_m[nm] for nm in SMALL]),
                                  _pack_small([small_v[nm] for nm in SMALL]), "adamw_small")
    like = [small_w[nm] for nm in SMALL]
    sg_l, loss = _unpack_small(sg, like)
    sd_l, _ = _unpack_small(sd, like)
    sm_l, _ = _unpack_small(sm, like)
    sv_l, _ = _unpack_small(sv, like)
    deltas, new_m, new_v = {}, {}, {}
    for i, nm in enumerate(SMALL):
        grads[nm], deltas[nm], new_m[nm], new_v[nm] = sg_l[i], sd_l[i], sm_l[i], sv_l[i]

    big_w = dict(ffn1_w_gate=ffn1_w_gate, ffn1_w_up=ffn1_w_up, ffn1_w_down=ffn1_w_down, w_in=w_in, w_out=w_out,
                 ffn2_w_gate=ffn2_w_gate, ffn2_w_up=ffn2_w_up, ffn2_w_down=ffn2_w_down)
    big_m = dict(ffn1_w_gate=m_ffn1_w_gate, ffn1_w_up=m_ffn1_w_up, ffn1_w_down=m_ffn1_w_down, w_in=m_w_in,
                 w_out=m_w_out, ffn2_w_gate=m_ffn2_w_gate, ffn2_w_up=m_ffn2_w_up, ffn2_w_down=m_ffn2_w_down)
    big_v = dict(ffn1_w_gate=v_ffn1_w_gate, ffn1_w_up=v_ffn1_w_up, ffn1_w_down=v_ffn1_w_down, w_in=v_w_in,
                 w_out=v_w_out, ffn2_w_gate=v_ffn2_w_gate, ffn2_w_up=v_ffn2_w_up, ffn2_w_down=v_ffn2_w_down)
    for nm in big_w:
        view = t if nm in transposed else (lambda a: a)
        res = _adamw(view(big_w[nm]), red_rows[nm], view(big_m[nm]), view(big_v[nm]), f"adamw_{nm}")
        grads[nm] = view(red_rows[nm])
        deltas[nm], new_m[nm], new_v[nm] = [view(r) for r in res]

    order = ["ffn1_norm", "ffn1_w_gate", "ffn1_w_up", "ffn1_w_down", "mix_norm", "w_in", "attn_sinks", "gmlp_v_norm",
             "gmlp_w_s", "gmlp_b", "pool_w", "pool_scale", "w_out", "ffn2_norm", "ffn2_w_gate", "ffn2_w_up",
             "ffn2_w_down", "final_norm"]
    return (loss, grad_x, *[grads[n] for n in order], *[deltas[n] for n in order],
            *[new_m[n] for n in order], *[new_v[n] for n in order])
```

```python
import functools
import math

import jax
import jax.numpy as jnp
from jax import lax
from jax.experimental import pallas as pl
from jax.experimental.pallas import tpu as pltpu

F32 = jnp.float32
BF16 = jnp.bfloat16
MESH = pl.DeviceIdType.MESH

D = 1024
FF = 2816
INW = 1536
N_DEV = 8
DEPTH = 2
BLK = 128
HD = 64
N_HEADS = 8
N_KV = 2
REP = 4
ATTN_SCALE = HD ** -0.5
POOL_WINDOWS = (2, 4, 8, 16)
EPS = 1e-6
NEG = -1e30
FC = 256
GELU_C0 = math.sqrt(2.0 / math.pi)
GELU_C1 = 0.044715

ADAM_LR = 0.001
ADAM_B1 = 0.9
ADAM_B2 = 0.999
ADAM_EPS = 1e-08
ADAM_WD = 0.01
ADAM_STEP = 10

VMEM_LIMIT = 60 * 1024 * 1024

O_K, O_V, O_U, O_G, O_P = 512, 640, 768, 1024, 1280


def _call(body, **kw):
    return pl.pallas_call(body, **kw)


def _params(sem=None, vmem=VMEM_LIMIT):
    return pltpu.CompilerParams(dimension_semantics=sem, vmem_limit_bytes=vmem)


def _host(comm, body, *, name, grid, in_specs, out_specs, out_shape, args, scratch_shapes=(), aliases=None):
    single = not isinstance(out_shape, (list, tuple))
    out_specs_l = [out_specs] if single else list(out_specs)
    out_shape_l = [out_shape] if single else list(out_shape)
    n_in, n_out, n_scr = len(in_specs), len(out_shape_l), len(scratch_shapes)
    steps = grid[0]
    any_spec = pl.BlockSpec(memory_space=pl.ANY)

    def wrapped(*refs):
        pos = 0

        def take(n):
            nonlocal pos
            part = refs[pos:pos + n]
            pos += n
            return part

        ins = take(n_in)
        cins = [take(len(t.inputs)) for t in comm]
        outs = take(n_out)
        couts = [take(len(t.out_shape)) for t in comm]
        scr = take(n_scr)
        cscr = [take(len(t.scratch)) for t in comm]
        i = pl.program_id(0)
        for k, t in enumerate(comm):
            pl.when(i == 0)(functools.partial(t.start, cins[k], couts[k], cscr[k]))
        body(*ins, *outs, *scr)
        for k, t in enumerate(comm):
            pl.when(i == (3 * steps) // 4)(functools.partial(t.mid, cins[k], couts[k], cscr[k]))
            pl.when(i == steps - 1)(functools.partial(t.finish, cins[k], couts[k], cscr[k]))

    c_args = [a for t in comm for a in t.inputs]
    c_shapes = [sh for t in comm for sh in t.out_shape]
    c_scr = [sc for t in comm for sc in t.scratch]
    res = _call(
        wrapped, name=name, grid=grid,
        in_specs=list(in_specs) + [any_spec] * len(c_args),
        out_specs=out_specs_l + [any_spec] * len(c_shapes),
        out_shape=out_shape_l + c_shapes,
        scratch_shapes=list(scratch_shapes) + c_scr,
        input_output_aliases=aliases or {},
        compiler_params=_params(("arbitrary",)),
    )(*args, *c_args)
    outs = res[0] if single else list(res[:n_out])
    if not comm:
        return outs
    c_outs, pos = [], n_out
    for t in comm:
        c_outs.append(list(res[pos:pos + len(t.out_shape)]))
        pos += len(t.out_shape)
    return outs, c_outs


def _nn(a, b):
    return lax.dot_general(a, b, (((1,), (0,)), ((), ())), preferred_element_type=F32)


def _nt(a, b):
    return lax.dot_general(a, b, (((1,), (1,)), ((), ())), preferred_element_type=F32)


def _tn(a, b):
    return lax.dot_general(a, b, (((0,), (0,)), ((), ())), preferred_element_type=F32)


def _gelu(x):
    x2 = x * x
    t = jnp.tanh(x * (GELU_C0 + (GELU_C0 * GELU_C1) * x2))
    hx = 0.5 * x
    return hx + hx * t, (hx, x2, t)


def _gelu_grad(parts):
    hx, x2, t = parts
    return (0.5 + 0.5 * t) + (hx * (1.0 - t * t)) * (GELU_C0 + (3.0 * GELU_C0 * GELU_C1) * x2)


def _rms(x):
    r = lax.rsqrt(jnp.mean(x * x, axis=-1, keepdims=True) + EPS)
    return x * r, r


def _rms_bwd(dy, xh, r, g):
    dg = jnp.sum(dy * xh, axis=0, keepdims=True)
    dxh = dy * g
    dx = r * (dxh - xh * jnp.mean(dxh * xh, axis=-1, keepdims=True))
    return dx, dg


def _wspec(rows, m):
    return pl.BlockSpec((None, rows, D), lambda i, m=m: (m, 0, 0), pipeline_mode=pl.Buffered(1))


def _rowspec(tm, cols):
    return pl.BlockSpec((tm, cols), lambda i: (i, 0))


def _fixspec(rows, cols):
    return pl.BlockSpec((rows, cols), lambda i: (0, 0))


def _ffn_fwd(x, gain, w352, mg, name, comm=(), mixer=None, loss=None):
    s = x.shape[0]
    tm = min(512, s)
    n_in = 5 + (2 if mixer is not None else 0) + (2 if loss is not None else 0)

    def body(*refs):
        x_ref, g_ref, wg_ref, wu_ref, wd_ref = refs[:5]
        more_in, outs = list(refs[5:n_in]), list(refs[n_in:])
        xo_ref, p1_ref, p2_ref, hid_ref = outs[:4]
        more_out = outs[4:]
        xt = x_ref[...]
        if mixer is not None:
            y_ref, wo_ref = more_in[:2]
            xt = xt + _nn(y_ref[...], wo_ref[...])
            more_out.pop(0)[...] = xt
        xh, _ = _rms(xt)
        h = (xh * g_ref[...]).astype(BF16)
        for c in range(FF // FC):
            sl = slice(c * FC, (c + 1) * FC)
            a = _nt(h, wg_ref[sl, :])
            b = _nt(h, wu_ref[sl, :])
            sig = 0.5 * jnp.tanh(0.5 * a) + 0.5
            sa = a * sig
            p1_ref[:, sl] = (b * (sig + sa * (1.0 - sig))).astype(BF16)
            p2_ref[:, sl] = sa.astype(BF16)
            hid_ref[:, sl] = (sa * b).astype(BF16)
        xo = xt + 0.5 * _nn(hid_ref[...], wd_ref[...])
        if loss is None:
            xo_ref[...] = xo
        else:
            gf_ref, t_ref = more_in[-2:]
            loss_ref, dgf_ref = more_out
            gf = gf_ref[...]
            xh, r = _rms(xo)
            err = xh * gf - t_ref[...]
            lp = 0.5 * jnp.sum(jnp.mean(err * err, axis=-1, keepdims=True), axis=0, keepdims=True)
            xo_ref[...], dgf = _rms_bwd(err * (1.0 / D), xh, r, gf)

            @pl.when(pl.program_id(0) == 0)
            def _():
                dgf_ref[...] = jnp.zeros_like(dgf_ref)
                loss_ref[...] = jnp.zeros_like(loss_ref)

            dgf_ref[0:1, :] += dgf
            loss_ref[0:1, :] += lp + jnp.zeros((1, 128), F32)

    act = jax.ShapeDtypeStruct((s, FF), BF16)
    tok = jax.ShapeDtypeStruct((s, D), F32)
    in_specs = [_rowspec(tm, D), _fixspec(1, D), _wspec(FF, mg), _wspec(FF, mg + 1), _wspec(FF, mg + 2)]
    out_specs = [_rowspec(tm, D), _rowspec(tm, FF), _rowspec(tm, FF), _rowspec(tm, FF)]
    out_shape = [tok, act, act, act]
    args = (x, gain, w352, w352, w352)
    if mixer is not None:
        in_specs += [_rowspec(tm, D), _wspec(D, 0)]
        out_specs += [_rowspec(tm, D)]
        out_shape += [tok]
        args += tuple(mixer)
    if loss is not None:
        in_specs += [_fixspec(1, D), _rowspec(tm, D)]
        out_specs += [_fixspec(8, 128), _fixspec(8, D)]
        out_shape += [jax.ShapeDtypeStruct((8, 128), F32), jax.ShapeDtypeStruct((8, D), F32)]
        args += tuple(loss)
    return _host(comm, body, name=name, grid=(s // tm,), in_specs=in_specs, out_specs=out_specs,
                 out_shape=out_shape, args=args)


def _ffn_bwd(x, gain, dy, p1, p2, w352, mg, name, comm=(), w_out=None, mixin=None):
    s = x.shape[0]
    tm = min(256, s)
    n_in = 8 + (1 if w_out is not None else 0) + (4 if mixin is not None else 0)

    def body(*refs):
        x_ref, g_ref, dy_ref, p1_ref, p2_ref, wg_ref, wu_ref, wd_ref = refs[:8]
        more_in, outs = list(refs[8:n_in]), list(refs[n_in:])
        dx_ref, dab_ref, h_ref, dyb_ref, dg_ref = outs[:5]
        more_out = outs[5:]
        i = pl.program_id(0)
        xt = x_ref[...]
        g = g_ref[...]
        xh, r = _rms(xt)
        h_ref[...] = (xh * g).astype(BF16)
        dyt = dy_ref[...]
        if mixin is not None:
            gm_ref, dz_ref, win_ref, x1_ref = more_in[-4:]
            dgm_ref = more_out[-1]
            xh1, r1 = _rms(x1_ref[...])
            dxm, dgm = _rms_bwd(_nn(dz_ref[...], win_ref[...]), xh1, r1, gm_ref[...])
            dyt = dyt + dxm

            @pl.when(i == 0)
            def _():
                dgm_ref[...] = jnp.zeros_like(dgm_ref)

            dgm_ref[0:1, :] += dgm
        dyb = (0.5 * dyt).astype(BF16)
        dyb_ref[...] = dyb
        for c in range(FF // FC):
            sl = slice(c * FC, (c + 1) * FC)
            dhid = _nt(dyb, wd_ref[sl, :])
            dab_ref[:, sl] = (dhid * p1_ref[:, sl].astype(F32)).astype(BF16)
            dab_ref[:, FF + c * FC:FF + (c + 1) * FC] = (dhid * p2_ref[:, sl].astype(F32)).astype(BF16)
        dh = _nn(dab_ref[:, :FF], wg_ref[...]) + _nn(dab_ref[:, FF:], wu_ref[...])
        dxn, dg = _rms_bwd(dh, xh, r, g)
        dx = dyt + dxn
        dx_ref[...] = dx
        if w_out is not None:
            dym_ref, dxb_ref = more_out[:2]
            dxb = dx.astype(BF16)
            dxb_ref[...] = dxb
            dym_ref[...] = _nt(dxb, more_in[0][...])

        @pl.when(i == 0)
        def _():
            dg_ref[...] = jnp.zeros_like(dg_ref)

        dg_ref[0:1, :] += dg

    tok = jax.ShapeDtypeStruct((s, D), BF16)
    tok32 = jax.ShapeDtypeStruct((s, D), F32)
    gain_grad = jax.ShapeDtypeStruct((8, D), F32)
    in_specs = [_rowspec(tm, D), _fixspec(1, D), _rowspec(tm, D), _rowspec(tm, FF), _rowspec(tm, FF),
                _wspec(FF, mg), _wspec(FF, mg + 1), _wspec(FF, mg + 2)]
    out_specs = [_rowspec(tm, D), _rowspec(tm, 2 * FF), _rowspec(tm, D), _rowspec(tm, D), _fixspec(8, D)]
    out_shape = [tok32, jax.ShapeDtypeStruct((s, 2 * FF), BF16), tok, tok, gain_grad]
    args = (x, gain, dy, p1, p2, w352, w352, w352)
    if w_out is not None:
        in_specs += [_wspec(D, 0)]
        out_specs += [_rowspec(tm, D), _rowspec(tm, D)]
        out_shape += [tok32, tok]
        args += (w_out,)
    if mixin is not None:
        in_specs += [_fixspec(1, D), _rowspec(tm, INW), _wspec(INW, 0), _rowspec(tm, D)]
        out_specs += [_fixspec(8, D)]
        out_shape += [gain_grad]
        args += tuple(mixin)
    return _host(comm, body, name=name, grid=(s // tm,), in_specs=in_specs, out_specs=out_specs,
                 out_shape=out_shape, args=args)


def _wgrad(a, b, g, n_slabs, m, name, comm=(), col0=0, cols=None, slab_rows=None):
    s = a.shape[0]
    cols = a.shape[1] if cols is None else cols
    slab_rows = cols if slab_rows is None else slab_rows
    mb = 256
    per_slab = slab_rows // mb

    def body(*refs):
        refs[-1][...] = _tn(refs[0][...], refs[1][...])

    in_specs = [pl.BlockSpec((s, mb), lambda i: (0, col0 // mb + i)),
                pl.BlockSpec((s, D), lambda i: (0, 0), pipeline_mode=pl.Buffered(1))]
    args = [a, b]
    aliases = {}
    if g is not None:
        in_specs.append(pl.BlockSpec(memory_space=pl.ANY))
        args.append(g)
        aliases = {2: 0}
    return _host(
        comm, body, name=name, grid=(cols // mb,),
        in_specs=in_specs,
        out_specs=pl.BlockSpec((None, mb, D), lambda i: (m + i // per_slab, i % per_slab, 0)),
        out_shape=jax.ShapeDtypeStruct((n_slabs, slab_rows, D), F32),
        aliases=aliases, args=args)


def _mixin_fwd(x, gain, w192, l, name):
    s = x.shape[0]
    tm = min(512, s)

    def body(x_ref, g_ref, w_ref, z_ref, h_ref):
        xh, _ = _rms(x_ref[...])
        h = (xh * g_ref[...]).astype(BF16)
        h_ref[...] = h
        z_ref[...] = _nt(h, w_ref[...])

    return _call(
        body, name=name, grid=(s // tm,),
        in_specs=[_rowspec(tm, D), _fixspec(1, D), _wspec(INW, l)],
        out_specs=[_rowspec(tm, INW), _rowspec(tm, D)],
        out_shape=[jax.ShapeDtypeStruct((s, INW), F32), jax.ShapeDtypeStruct((s, D), BF16)],
        compiler_params=_params(("arbitrary",)),
    )(x, gain, w192)


MIX_NB = 4
TILE = MIX_NB * BLK
GROUP_ROWS = REP * BLK


class _Block:
    def __init__(self, n, j, zc_ref, zkvp_ref, zpp_ref):
        self.zc, self.zkvp, self.zpp = zc_ref, zkvp_ref, zpp_ref
        self.first = j == 0
        self.r = slice(j * BLK, (j + 1) * BLK)
        self.rp = slice((j - 1) * BLK, j * BLK)
        self.index = n * MIX_NB + j
        self.lo = jnp.where(n > 0, 0, BLK) if self.first else 0
        self.has_prev = jnp.where(n > 0, 1.0, 0.0) if self.first else 1.0

    def cols(self, c0, c1):
        return self.zc[self.r, c0:c1]

    def prev_kv(self, c0, c1):
        return self.zkvp[:, c0:c1] if self.first else self.zc[self.rp, O_K + c0:O_K + c1]

    def prev_p(self):
        return self.zpp[...] * self.has_prev if self.first else self.zc[self.rp, O_P:INW]


def _band_mask(rows):
    row = lax.broadcasted_iota(jnp.int32, (rows, 2 * BLK), 0) & (BLK - 1)
    col = lax.broadcasted_iota(jnp.int32, (rows, 2 * BLK), 1)
    return (col > row) & (col <= row + BLK)


def _block_mask(band, blk):
    if not blk.first:
        return band
    return band & (lax.broadcasted_iota(jnp.int32, band.shape, 1) >= blk.lo)


def _lane_head(shape):
    return lax.broadcasted_iota(jnp.int32, shape, 1) // HD


def _lane_group_select(vals):
    grp = _lane_head(vals[0].shape)
    return jnp.where(grp == 0, vals[0], jnp.where(grp == 1, vals[1], jnp.where(grp == 2, vals[2], vals[3])))


def _pool_count(index):
    row = lax.broadcasted_iota(jnp.int32, (BLK, 256), 0)
    pos1 = (index * BLK + row + 1).astype(F32)
    wl = _lane_group_select([jnp.full((BLK, 256), float(w), F32) for w in POOL_WINDOWS])
    return jnp.minimum(pos1, wl)


def _window_sums(e, forward):
    tot = e.shape[0]
    lv = e
    out = []
    for sh in (1, 2, 4, 8):
        lv = lv + pltpu.roll(lv, sh if forward else tot - sh, 0)
        out.append(lv)
    return _lane_group_select(out)


def _stack_heads(get, g):
    return jnp.concatenate([get((g * REP + rr) * HD, (g * REP + rr + 1) * HD) for rr in range(REP)], axis=0)


def _sink_column(sink_ref, g):
    return jnp.concatenate([jnp.full((BLK, 1), sink_ref[g * REP + rr], F32) for rr in range(REP)], axis=0)


def _kv_window(blk, g):
    kk = jnp.concatenate([blk.prev_kv(g * HD, (g + 1) * HD),
                          blk.cols(O_K + g * HD, O_K + (g + 1) * HD)], axis=0).astype(BF16)
    vv = jnp.concatenate([blk.prev_kv(BLK + g * HD, BLK + (g + 1) * HD),
                          blk.cols(O_V + g * HD, O_V + (g + 1) * HD)], axis=0).astype(BF16)
    return kk, vv


def _mix_common(blk, vn_ref, wcat_ref, bexp_ref, pwbd_ref):
    u, tu = _gelu(blk.cols(O_U, O_G))
    gv, tv = _gelu(blk.cols(O_G, O_P))
    xh, rv = _rms(gv)
    vnb = (xh * vn_ref[...]).astype(BF16)
    head = _lane_head((BLK, 256))
    vn_bd = jnp.concatenate([jnp.where(head == h, vnb, jnp.zeros_like(vnb)) for h in range(4)], axis=0)
    row = lax.broadcasted_iota(jnp.int32, (BLK, 4 * BLK), 0)
    col = lax.broadcasted_iota(jnp.int32, (BLK, 4 * BLK), 1) & (BLK - 1)
    tril = col <= row
    wcat = jnp.where(tril, wcat_ref[...], 0.0).astype(BF16)
    f = _nn(wcat, vn_bd) + bexp_ref[...]
    p = blk.cols(O_P, INW)
    e = jnp.concatenate([blk.prev_p(), p], axis=0)
    cnt = _pool_count(blk.index)
    diff = (_window_sums(e, True)[BLK:, :] / cnt - p).astype(BF16)
    pwbd = pwbd_ref[...].astype(BF16)
    pout = _nn(diff, pwbd)
    return dict(u=u, tu=tu, tv=tv, xh=xh, rv=rv, vn_bd=vn_bd, wcat=wcat, f=f, cnt=cnt, diff=diff, pwbd=pwbd,
                pout=pout, tril=tril, head=head)


def _mix_fwd(z, sinks, vnorm, wcat, bexp, pwbd, pscale, name):
    s = z.shape[0]
    nt = s // TILE

    def body(sink_ref, zc_ref, zkvp_ref, zpp_ref, vn_ref, wcat_ref, bexp_ref, pwbd_ref, ps_ref, y_ref, lse_ref):
        n = pl.program_id(0)
        lse_ref[...] = jnp.zeros_like(lse_ref)
        band = _band_mask(BLK)
        for j in range(MIX_NB):
            blk = _Block(n, j, zc_ref, zkvp_ref, zpp_ref)
            valid = _block_mask(band, blk)
            for g in range(N_KV):
                kk, vv = _kv_window(blk, g)
                for rr in range(REP):
                    h = g * REP + rr
                    qh = (blk.cols(h * HD, (h + 1) * HD) * ATTN_SCALE).astype(BF16)
                    sc = jnp.where(valid, _nt(qh, kk), NEG)
                    sink = sink_ref[h]
                    m = jnp.maximum(jnp.max(sc, axis=-1, keepdims=True), sink)
                    ex = jnp.exp(sc - m)
                    den = jnp.sum(ex, axis=-1, keepdims=True) + jnp.exp(sink - m)
                    y_ref[blk.r, h * HD:(h + 1) * HD] = _nn((ex / den).astype(BF16), vv).astype(BF16)
                    lse_ref[blk.r, h:h + 1] = m + jnp.log(den)
            c = _mix_common(blk, vn_ref, wcat_ref, bexp_ref, pwbd_ref)
            y_ref[blk.r, 512:768] = (c["u"] * c["f"]).astype(BF16)
            y_ref[blk.r, 768:1024] = (c["pout"] * ps_ref[...]).astype(BF16)

    halo = lambda n: jnp.maximum(MIX_NB * n - 1, 0)
    return _call(
        body, name=name, grid=(nt,),
        in_specs=[pl.BlockSpec(memory_space=pltpu.SMEM),
                  pl.BlockSpec((TILE, INW), lambda n: (n, 0)),
                  pl.BlockSpec((BLK, 256), lambda n: (halo(n), 2)),
                  pl.BlockSpec((BLK, 256), lambda n: (halo(n), 5)),
                  _fixspec(1, 256), _fixspec(BLK, 4 * BLK), _fixspec(BLK, 256), _fixspec(256, 256), _fixspec(1, 256)],
        out_specs=[pl.BlockSpec((TILE, D), lambda n: (n, 0)), pl.BlockSpec((TILE, 128), lambda n: (n, 0))],
        out_shape=[jax.ShapeDtypeStruct((s, D), BF16), jax.ShapeDtypeStruct((s, 128), F32)],
        compiler_params=_params(("arbitrary",)),
    )(sinks, z, z, z, vnorm, wcat, bexp, pwbd, pscale)


def _mix_bwd(z, dy, lse, sinks, vnorm, wcat, bexp, pwbd, pscale, name, comm=()):
    s = z.shape[0]
    nt = s // TILE
    last = slice(TILE - BLK, TILE)

    def body(sink_ref, zc_ref, zkvp_ref, zpp_ref, dy_ref, lse_ref, vn_ref, wcat_ref, bexp_ref, pwbd_ref, ps_ref,
             dz_ref, dsink_ref, dvn_ref, dws_ref, dbt_ref, dpw_ref, dps_ref, carry_ref, ddc_ref):
        n = pl.program_id(0)

        @pl.when(n == 0)
        def _():
            carry_ref[...] = jnp.zeros_like(carry_ref)
            ddc_ref[...] = jnp.zeros_like(ddc_ref)
            dsink_ref[...] = jnp.zeros_like(dsink_ref)
            dvn_ref[...] = jnp.zeros_like(dvn_ref)
            dws_ref[...] = jnp.zeros_like(dws_ref)
            dbt_ref[...] = jnp.zeros_like(dbt_ref)
            dpw_ref[...] = jnp.zeros_like(dpw_ref)
            dps_ref[...] = jnp.zeros_like(dps_ref)

        def block_grads(j):
            blk = _Block(n, j, zc_ref, zkvp_ref, zpp_ref)
            valid = _block_mask(_band_mask(GROUP_ROWS), blk)
            out = dict(dq=[], dsink=[], dbt=[])
            dk_prev, dk_cur, dv_prev, dv_cur = [], [], [], []
            for g in range(N_KV):
                kk, vv = _kv_window(blk, g)
                q4 = _stack_heads(blk.cols, g).astype(BF16)
                do4 = _stack_heads(lambda c0, c1: dy_ref[blk.r, c0:c1], g).astype(BF16)
                lse4 = jnp.concatenate([lse_ref[blk.r, g * REP + rr:g * REP + rr + 1] for rr in range(REP)], axis=0)
                sc = jnp.where(valid, _nt(q4, kk) * ATTN_SCALE, NEG)
                pr = jnp.exp(sc - lse4)
                dp = _nt(do4, vv)
                delta = jnp.sum(pr * dp, axis=-1, keepdims=True)
                ds = ((pr * (dp - delta)) * ATTN_SCALE).astype(BF16)
                sunk = jnp.exp(_sink_column(sink_ref, g) - lse4) * delta
                dq4 = _nn(ds, kk)
                for rr in range(REP):
                    out["dsink"].append(-jnp.sum(sunk[rr * BLK:(rr + 1) * BLK], axis=0, keepdims=True))
                    out["dq"].append(dq4[rr * BLK:(rr + 1) * BLK])
                dkk = _tn(ds, q4)
                dvv = _tn(pr.astype(BF16), do4)
                dk_prev.append(dkk[:BLK]); dk_cur.append(dkk[BLK:])
                dv_prev.append(dvv[:BLK]); dv_cur.append(dvv[BLK:])
            out["dk_prev"], out["dk_cur"] = jnp.concatenate(dk_prev, axis=1), jnp.concatenate(dk_cur, axis=1)
            out["dv_prev"], out["dv_cur"] = jnp.concatenate(dv_prev, axis=1), jnp.concatenate(dv_cur, axis=1)
            c = _mix_common(blk, vn_ref, wcat_ref, bexp_ref, pwbd_ref)
            dyg = dy_ref[blk.r, 512:768]
            du = dyg * c["f"]
            df = dyg * c["u"]
            out["dzu"] = du * _gelu_grad(c["tu"])
            dfb = df.astype(BF16)
            for h in range(4):
                out["dbt"].append(jnp.sum(df[:, h * HD:(h + 1) * HD], axis=1, keepdims=True))
            out["dws"] = jnp.where(c["tril"], _nt(dfb, c["vn_bd"]), 0.0)
            dvn_bd = _tn(c["wcat"], dfb)
            dvn = functools.reduce(lambda a, b: a + b, [
                jnp.where(c["head"] == h, dvn_bd[h * BLK:(h + 1) * BLK], 0.0) for h in range(4)])
            dgv, out["dvn"] = _rms_bwd(dvn, c["xh"], c["rv"], vn_ref[...])
            out["dzv"] = dgv * _gelu_grad(c["tv"])
            dyp = dy_ref[blk.r, 768:1024]
            out["dps"] = jnp.sum(dyp * c["pout"], axis=0, keepdims=True)
            dout = (dyp * ps_ref[...]).astype(BF16)
            out["dpw"] = _tn(c["diff"], dout)
            out["ddiff"] = _nt(dout, c["pwbd"])
            out["dd"] = out["ddiff"] / c["cnt"]
            return out

        def write_previous_tile(dd_next, dk_next, dv_next):
            if MIX_NB > 1:
                dz_ref[0:TILE - BLK, :] = carry_ref[0:TILE - BLK, :].astype(BF16)
            rs = _window_sums(jnp.concatenate([ddc_ref[...], dd_next], axis=0), False)
            dz_ref[last, 0:O_K] = carry_ref[last, 0:O_K].astype(BF16)
            dz_ref[last, O_K:O_V] = (carry_ref[last, O_K:O_V] + dk_next).astype(BF16)
            dz_ref[last, O_V:O_U] = (carry_ref[last, O_V:O_U] + dv_next).astype(BF16)
            dz_ref[last, O_U:O_P] = carry_ref[last, O_U:O_P].astype(BF16)
            dz_ref[last, O_P:INW] = (carry_ref[last, O_P:INW] + rs[:BLK, :]).astype(BF16)

        @pl.when(n < nt)
        def _():
            parts = [block_grads(j) for j in range(MIX_NB)]
            total = lambda key, i=None: functools.reduce(
                lambda a, b: a + b, [p[key] if i is None else p[key][i] for p in parts])
            for h in range(N_HEADS):
                dsink_ref[h:h + 1, :] += total("dsink", h) + jnp.zeros((1, 128), F32)
            for h in range(4):
                dbt_ref[:, h:h + 1] += total("dbt", h)
            dws_ref[...] += total("dws")
            dpw_ref[...] += total("dpw")
            dvn_ref[0:1, :] += total("dvn")
            dps_ref[0:1, :] += total("dps")
            write_previous_tile(parts[0]["dd"], parts[0]["dk_prev"], parts[0]["dv_prev"])
            for j, p in enumerate(parts):
                r = slice(j * BLK, (j + 1) * BLK)
                nxt = parts[j + 1] if j + 1 < MIX_NB else None
                for h in range(N_HEADS):
                    carry_ref[r, h * HD:(h + 1) * HD] = p["dq"][h]
                carry_ref[r, O_U:O_G] = p["dzu"]
                carry_ref[r, O_G:O_P] = p["dzv"]
                if nxt is None:
                    carry_ref[r, O_K:O_V] = p["dk_cur"]
                    carry_ref[r, O_V:O_U] = p["dv_cur"]
                    carry_ref[r, O_P:INW] = -p["ddiff"]
                    ddc_ref[...] = p["dd"]
                else:
                    rs = _window_sums(jnp.concatenate([p["dd"], nxt["dd"]], axis=0), False)
                    carry_ref[r, O_K:O_V] = p["dk_cur"] + nxt["dk_prev"]
                    carry_ref[r, O_V:O_U] = p["dv_cur"] + nxt["dv_prev"]
                    carry_ref[r, O_P:INW] = rs[:BLK, :] - p["ddiff"]

        @pl.when(n == nt)
        def _():
            none = jnp.zeros((BLK, BLK), F32)
            write_previous_tile(jnp.zeros((BLK, 256), F32), none, none)

    cur = lambda n: jnp.minimum(n, nt - 1)
    done = lambda n: jnp.maximum(n - 1, 0)
    halo = lambda n: jnp.maximum(MIX_NB * jnp.minimum(n, nt - 1) - 1, 0)
    return _host(
        comm, body, name=name, grid=(nt + 1,),
        in_specs=[pl.BlockSpec(memory_space=pltpu.SMEM),
                  pl.BlockSpec((TILE, INW), lambda n: (cur(n), 0)),
                  pl.BlockSpec((BLK, 256), lambda n: (halo(n), 2)),
                  pl.BlockSpec((BLK, 256), lambda n: (halo(n), 5)),
                  pl.BlockSpec((TILE, D), lambda n: (cur(n), 0)),
                  pl.BlockSpec((TILE, 128), lambda n: (cur(n), 0)),
                  _fixspec(1, 256), _fixspec(BLK, 4 * BLK), _fixspec(BLK, 256), _fixspec(256, 256), _fixspec(1, 256)],
        out_specs=[pl.BlockSpec((TILE, INW), lambda n: (done(n), 0)),
                   _fixspec(8, 128), _fixspec(8, 256), _fixspec(BLK, 4 * BLK), _fixspec(BLK, 128),
                   _fixspec(256, 256), _fixspec(8, 256)],
        out_shape=[jax.ShapeDtypeStruct((s, INW), BF16), jax.ShapeDtypeStruct((8, 128), F32),
                   jax.ShapeDtypeStruct((8, 256), F32), jax.ShapeDtypeStruct((BLK, 4 * BLK), F32),
                   jax.ShapeDtypeStruct((BLK, 128), F32), jax.ShapeDtypeStruct((256, 256), F32),
                   jax.ShapeDtypeStruct((8, 256), F32)],
        scratch_shapes=[pltpu.VMEM((TILE, INW), F32), pltpu.VMEM((BLK, 256), F32)],
        args=(sinks, z, z, z, dy, lse, vnorm, wcat, bexp, pwbd, pscale))


def _position():
    x, y, c = lax.axis_index("x"), lax.axis_index("y"), lax.axis_index("c")
    return x, y, c


class _GatherTask:
    def __init__(self, srcs):
        self.inputs = list(srcs)
        ng = len(srcs)
        self.out_shape = [jax.ShapeDtypeStruct((a.shape[0], N_DEV) + a.shape[1:], a.dtype) for a in srcs]
        self.scratch = [pltpu.SemaphoreType.DMA((ng, 7)), pltpu.SemaphoreType.DMA((ng, 7)),
                        pltpu.SemaphoreType.DMA((ng,))]

    def _plan(self, src, dst, sems):
        send_sems, recv_sems, local_sems = sems
        ng = len(src)
        x, y, c = _position()
        me, sibling = (x, y, c), (x, y, 1 - c)
        chips = [(1 - x, y), (x, 1 - y), (1 - x, 1 - y)]

        def slot(pos):
            return 4 * pos[0] + 2 * pos[1] + pos[2]

        def copy(gi, k, block, to, from_src=False):
            rows = dst[gi].at[:, slot(block)]
            return pltpu.make_async_remote_copy(
                src_ref=src[gi] if from_src else rows, dst_ref=rows,
                send_sem=send_sems.at[gi, k], recv_sem=recv_sems.at[gi, k],
                device_id=to, device_id_type=MESH)

        make = functools.partial
        mine = [make(pltpu.make_async_copy, src[gi], dst[gi].at[:, slot(me)], local_sems.at[gi]) for gi in range(ng)]
        first = []
        for gi in range(ng):
            first.append(make(copy, gi, 0, me, sibling, True))
            first += [make(copy, gi, 1 + j, me, (*chip, c), True) for j, chip in enumerate(chips)]
        passed = [make(copy, gi, 4 + j, (*chip, c), sibling) for j, chip in enumerate(chips) for gi in range(ng)]
        arrive_ici = [make(copy, gi, 1 + j, (*chip, c), me) for j, chip in enumerate(chips) for gi in range(ng)]
        arrive_d2d = [make(copy, gi, 0, sibling, me) for gi in range(ng)]
        arrive_d2d += [make(copy, gi, 4 + j, (*chip, 1 - c), me) for j, chip in enumerate(chips) for gi in range(ng)]
        return mine, first, passed, arrive_ici, arrive_d2d

    def start(self, src, dst, sems):
        mine, first, _, _, _ = self._plan(src, dst, sems)
        for cp in mine + first:
            cp().start()

    def mid(self, src, dst, sems):
        _, _, passed, arrive_ici, _ = self._plan(src, dst, sems)
        for arrived, fw in zip(arrive_ici, passed):
            arrived().wait_recv()
            fw().start()

    def finish(self, src, dst, sems):
        mine, first, passed, _, arrive_d2d = self._plan(src, dst, sems)
        for cp in arrive_d2d:
            cp().wait_recv()
        for cp in first + passed:
            cp().wait_send()
        for cp in mine:
            cp().wait()


class _SiblingTask:
    def __init__(self, g5s):
        self.inputs = list(g5s)
        ng = len(g5s)
        self.out_shape = [jax.ShapeDtypeStruct((a.shape[0], 4) + a.shape[3:], a.dtype) for a in g5s]
        self.scratch = [pltpu.SemaphoreType.DMA((ng,)), pltpu.SemaphoreType.DMA((ng,))]

    def _plan(self, src, dst, sems):
        send_sems, recv_sems = sems
        x, y, c = _position()
        return [functools.partial(
            pltpu.make_async_remote_copy,
            src_ref=src[gi].at[:, :, 1 - c], dst_ref=dst[gi],
            send_sem=send_sems.at[gi], recv_sem=recv_sems.at[gi],
            device_id=(x, y, 1 - c), device_id_type=MESH) for gi in range(len(src))]

    def start(self, src, dst, sems):
        for cp in self._plan(src, dst, sems):
            cp().start()

    def mid(self, src, dst, sems):
        pass

    def finish(self, src, dst, sems):
        for cp in self._plan(src, dst, sems):
            cp().wait()


class _ChipTask(_SiblingTask):
    def __init__(self, sbs):
        self.inputs = list(sbs)
        ng = len(sbs)
        self.out_shape = [jax.ShapeDtypeStruct(a.shape, a.dtype) for a in sbs]
        self.scratch = [pltpu.SemaphoreType.DMA((ng, 3)), pltpu.SemaphoreType.DMA((ng, 3))]

    def _plan(self, src, dst, sems):
        send_sems, recv_sems = sems
        x, y, c = _position()
        jme = 2 * x + y
        chips = [(1 - x, y), (x, 1 - y), (1 - x, 1 - y)]
        return [functools.partial(
            pltpu.make_async_remote_copy,
            src_ref=src[gi].at[:, 2 * chip[0] + chip[1]], dst_ref=dst[gi].at[:, jme],
            send_sem=send_sems.at[gi, k], recv_sem=recv_sems.at[gi, k],
            device_id=(*chip, c), device_id_type=MESH) for k, chip in enumerate(chips) for gi in range(len(src))]


def _alone(task, name):
    n_in, n_out = len(task.inputs), len(task.out_shape)

    def body(*refs):
        parts = (refs[:n_in], refs[n_in:n_in + n_out], refs[n_in + n_out:])
        task.start(*parts)
        task.mid(*parts)
        task.finish(*parts)

    any_spec = pl.BlockSpec(memory_space=pl.ANY)
    return _call(body, name=name, in_specs=[any_spec] * n_in, out_specs=[any_spec] * n_out,
                 out_shape=task.out_shape, scratch_shapes=task.scratch)(*task.inputs)


def _core_sum(ids, g5, r1, name):
    n, _, _, rows, _ = g5.shape

    def body(ids_ref, g_ref, r_ref, sb_ref, own_ref):
        j = pl.program_id(2)
        t = g_ref[...] + r_ref[...]
        sb_ref[...] = t.astype(BF16)

        @pl.when(j == ids_ref[1])
        def _():
            own_ref[...] = t

    grid_spec = pltpu.PrefetchScalarGridSpec(
        num_scalar_prefetch=1, grid=(n, 1, 4),
        in_specs=[pl.BlockSpec((None, None, None, rows, D), lambda i, t, j, ids: (i, j, ids[0], t, 0)),
                  pl.BlockSpec((None, None, rows, D), lambda i, t, j, ids: (i, j, t, 0))],
        out_specs=[pl.BlockSpec((None, None, rows, D), lambda i, t, j, ids: (i, j, t, 0)),
                   pl.BlockSpec((None, rows, D), lambda i, t, j, ids: (i, t, 0))])
    return _call(
        body, name=name, grid_spec=grid_spec,
        out_shape=[jax.ShapeDtypeStruct((n, 4, rows, D), BF16), jax.ShapeDtypeStruct((n, rows, D), F32)],
        compiler_params=_params(("arbitrary", "arbitrary", "arbitrary")),
    )(ids, g5, r1)


def _chip_sum(others, own, r2, name):
    n, rows, _ = own.shape

    def body(oth_ref, own_ref, r0_ref, r1_ref, r2_ref, out_ref):
        out_ref[...] = ((own_ref[...] + r0_ref[...].astype(F32)) + r1_ref[...].astype(F32)) \
            + r2_ref[...].astype(F32)

    def rspec(k):
        return pl.BlockSpec((None, None, rows, D), lambda i, oth, k=k: (i, oth[k], 0, 0))

    grid_spec = pltpu.PrefetchScalarGridSpec(
        num_scalar_prefetch=1, grid=(n,),
        in_specs=[pl.BlockSpec((None, rows, D), lambda i, oth: (i, 0, 0)), rspec(0), rspec(1), rspec(2)],
        out_specs=pl.BlockSpec((None, rows, D), lambda i, oth: (i, 0, 0)))
    return _call(
        body, name=name, grid_spec=grid_spec,
        out_shape=jax.ShapeDtypeStruct((n, rows, D), F32),
        compiler_params=_params(("arbitrary",)),
    )(others, own, r2, r2, r2)


def _adam_math(w, g, m, v):
    m = ADAM_B1 * m + (1.0 - ADAM_B1) * g
    v = ADAM_B2 * v + (1.0 - ADAM_B2) * (g * g)
    m_hat = m / (1.0 - ADAM_B1 ** ADAM_STEP)
    v_hat = v / (1.0 - ADAM_B2 ** ADAM_STEP)
    delta = -ADAM_LR * (m_hat / (jnp.sqrt(v_hat) + ADAM_EPS) + ADAM_WD * w)
    return delta, m, v


def _adamw(w, g, m, v, name):
    shape = w.shape
    c = shape[-1]
    r = w.size // c
    rb = max(d for d in range(8, min(r, 512) + 1, 8) if r % d == 0)

    def body(w_ref, g_ref, m_ref, v_ref, d_ref, mo_ref, vo_ref):
        d_ref[...], mo_ref[...], vo_ref[...] = _adam_math(w_ref[...], g_ref[...], m_ref[...], v_ref[...])

    spec = _rowspec(rb, c)
    outs = _call(
        body, name=name, grid=(r // rb,),
        in_specs=[spec] * 4, out_specs=[spec] * 3,
        out_shape=[jax.ShapeDtypeStruct((r, c), F32)] * 3,
        compiler_params=_params(("arbitrary",)),
    )(*[t.reshape(r, c) for t in (w, g, m, v)])
    return [o.reshape(shape) for o in outs]


def _adamw_small(parts, w, m, v, name):
    def body(p_ref, w_ref, m_ref, v_ref, g_ref, d_ref, mo_ref, vo_ref):
        g = p_ref[0]
        for dev in range(1, N_DEV):
            g = g + p_ref[dev]
        g_ref[...] = g
        d_ref[...], mo_ref[...], vo_ref[...] = _adam_math(w_ref[...], g, m_ref[...], v_ref[...])

    return _call(
        body, name=name,
        out_shape=[jax.ShapeDtypeStruct(w.shape, F32)] * 4,
        compiler_params=_params(),
    )(parts, w, m, v)


SMALL = ["ffn1_norm", "mix_norm", "attn_sinks", "gmlp_v_norm", "gmlp_w_s", "gmlp_b", "pool_w", "pool_scale",
         "ffn2_norm", "final_norm"]


def _piece_rows(size):
    return -(-size // 1024) * 8


def _pack_small(arrs, extra=None):
    pieces = []
    for a in list(arrs) + [jnp.zeros((1,), F32) if extra is None else extra]:
        f = a.reshape(-1)
        pieces.append(jnp.pad(f, (0, _piece_rows(f.shape[0]) * 128 - f.shape[0])).reshape(-1, 128))
    return jnp.concatenate(pieces, axis=0)


def _unpack_small(packed, like):
    out, off = [], 0
    for a in like:
        rows = _piece_rows(a.size)
        out.append(packed[off:off + rows].reshape(-1)[:a.size].reshape(a.shape))
        off += rows
    return out, packed[off, 0]


def kernel(x, ffn1_norm, ffn1_w_gate, ffn1_w_up, ffn1_w_down, mix_norm, w_in, attn_sinks, gmlp_v_norm, gmlp_w_s, gmlp_b, pool_w, pool_scale, w_out, ffn2_norm, ffn2_w_gate, ffn2_w_up, ffn2_w_down, final_norm, loss_target, m_ffn1_norm, m_ffn1_w_gate, m_ffn1_w_up, m_ffn1_w_down, m_mix_norm, m_w_in, m_attn_sinks, m_gmlp_v_norm, m_gmlp_w_s, m_gmlp_b, m_pool_w, m_pool_scale, m_w_out, m_ffn2_norm, m_ffn2_w_gate, m_ffn2_w_up, m_ffn2_w_down, m_final_norm, v_ffn1_norm, v_ffn1_w_gate, v_ffn1_w_up, v_ffn1_w_down, v_mix_norm, v_w_in, v_attn_sinks, v_gmlp_v_norm, v_gmlp_w_s, v_gmlp_b, v_pool_w, v_pool_scale, v_w_out, v_ffn2_norm, v_ffn2_w_gate, v_ffn2_w_up, v_ffn2_w_down, v_final_norm):
    s = x.shape[1]
    xi, yi, ci = _position()
    ids = jnp.stack([ci, 2 * xi + yi]).astype(jnp.int32)
    jme = 2 * xi + yi
    others = jnp.stack([k + (k >= jme).astype(jnp.int32) for k in range(3)]).astype(jnp.int32)
    t = lambda a: jnp.swapaxes(a, -1, -2)
    row = lambda a: a.reshape(1, -1)
    full = lambda a: a.reshape(a.shape[0], -1, D)

    loc_f1 = [jnp.stack([t(ffn1_w_gate[l]), t(ffn1_w_up[l]), ffn1_w_down[l]]).astype(BF16) for l in range(DEPTH)]
    loc_f2 = [jnp.stack([t(ffn2_w_gate[l]), t(ffn2_w_up[l]), ffn2_w_down[l]]).astype(BF16) for l in range(DEPTH)]
    loc_in = [t(w_in[l])[None].astype(BF16) for l in range(DEPTH)]
    loc_out = [w_out[l][None].astype(BF16) for l in range(DEPTH)]

    (wf1,) = _alone(_GatherTask([loc_f1[0]]), "gather_first")
    wf1 = full(wf1)
    xc = x.reshape(s, D)
    saved = []
    for l in range(DEPTH):
        x0 = xc
        if l == 0:
            (x1, *act1), ((wf2, win, wout),) = _ffn_fwd(
                x0, row(ffn1_norm[l]), wf1, 0, f"ffn1_fwd_{l}", comm=[_GatherTask([loc_f2[0], loc_in[0], loc_out[0]])])
        else:
            (x1, *act1), ((wf2,),) = _ffn_fwd(
                x0, row(ffn1_norm[l]), wf1, 0, f"ffn1_fwd_{l}", comm=[_GatherTask([loc_f2[1]])])
        wf2, win, wout = full(wf2), full(win), full(wout)
        z, hmix = _mixin_fwd(x1, row(mix_norm[l]), win, 0, f"mixin_fwd_{l}")
        wcat = jnp.concatenate([gmlp_w_s[l][h] for h in range(4)], axis=1)
        bexp = jnp.repeat(t(gmlp_b[l]), HD, axis=1)
        pwbd = jnp.zeros((256, 256), F32)
        for g in range(4):
            pwbd = pwbd.at[g * HD:(g + 1) * HD, g * HD:(g + 1) * HD].set(pool_w[l][g])
        mixp = (attn_sinks[l], row(gmlp_v_norm[l]), wcat, bexp, pwbd, row(pool_scale[l]))
        y, lse = _mix_fwd(z, *mixp, f"mix_fwd_{l}")
        keep = (x0, act1, wf1, x1, z, hmix, mixp, y, lse, win, wout)
        if l == 0:
            (xc, *act2, x2), ((wf1, win, wout),) = _ffn_fwd(
                x1, row(ffn2_norm[l]), wf2, 0, f"ffn2_fwd_{l}", mixer=(y, wout),
                comm=[_GatherTask([loc_f1[1], loc_in[1], loc_out[1]])])
            wf1 = full(wf1)
        else:
            dx, *act2, x2, loss_part, d_final = _ffn_fwd(
                x1, row(ffn2_norm[l]), wf2, 0, f"ffn2_fwd_{l}", mixer=(y, wout),
                loss=(row(final_norm), loss_target.reshape(s, D)))
        saved.append(keep + (x2, wf2, act2))

    def five(g):
        return g.reshape(g.shape[0], 4, 2, g.shape[1] // N_DEV, D)

    def core_sums(g5s, r1s, tag):
        res = [_core_sum(ids, g5, r1, f"core_sum_{tag}_{i}") for i, (g5, r1) in enumerate(zip(g5s, r1s))]
        return [sb for sb, _ in res], [own for _, own in res]

    def chip_sums(owns, r2s, tag):
        return [_chip_sum(others, own, r2, f"chip_sum_{tag}_{i}") for i, (own, r2) in enumerate(zip(owns, r2s))]

    def mix_small(l, dsink, dvn, dws, dbt, dpw, dps):
        return {("attn_sinks", l): dsink[:, 0], ("gmlp_v_norm", l): dvn[0],
                ("gmlp_w_s", l): jnp.stack([dws[:, h * BLK:(h + 1) * BLK] for h in range(4)]),
                ("gmlp_b", l): t(dbt[:, :4]),
                ("pool_w", l): jnp.stack([dpw[g * HD:(g + 1) * HD, g * HD:(g + 1) * HD] for g in range(4)]),
                ("pool_scale", l): dps[0]}

    small = {}
    red = {}
    gate_up = dict(cols=2 * FF, slab_rows=FF)
    x0, (p11, p21, hid1), wf1, x1, z, hmix, mixp, y, lse, win, wout, x2, wf2, (p12, p22, hid2) = saved[1]
    dx, dab, h, dyb, dg, dymix, dxb = _ffn_bwd(x2, row(ffn2_norm[1]), dx, p12, p22, wf2, 0, "ffn2_bwd_1", w_out=wout)
    small[("ffn2_norm", 1)] = dg[0]
    g = _wgrad(dab, h, None, 3, 0, "wgrad_gate_up2_1", **gate_up)
    g = _wgrad(hid2, dyb, g, 3, 2, "wgrad_down2_1")
    a5 = [five(g)]
    g_out, (a_r1,) = _wgrad(y, dxb, None, 1, 0, "wgrad_out_1", comm=[_SiblingTask(a5)])
    a_sb, a_own = core_sums(a5, a_r1, "a")
    (dz, dsink, dvn, dws, dbt, dpw, dps), (a_r2,) = _mix_bwd(z, dymix, lse, *mixp, "mix_bwd_1", comm=[_ChipTask(a_sb)])
    (red[("f2", 1)],) = chip_sums(a_own, a_r2, "a")
    small.update(mix_small(1, dsink, dvn, dws, dbt, dpw, dps))
    g_in = _wgrad(dz, hmix, None, 1, 0, "wgrad_in_1")
    b5 = [five(g_out), five(g_in)]
    (dx, dab, h, dyb, dg, dgm), (b_r1,) = _ffn_bwd(x0, row(ffn1_norm[1]), dx, p11, p21, wf1, 0, "ffn1_bwd_1",
                                                   comm=[_SiblingTask(b5)], mixin=(row(mix_norm[1]), dz, win, x1))
    small[("ffn1_norm", 1)], small[("mix_norm", 1)] = dg[0], dgm[0]
    b_sb, b_own = core_sums(b5, b_r1, "b")
    g_gu, (b_r2,) = _wgrad(dab, h, None, 2, 0, "wgrad_gate_up1_1", comm=[_ChipTask(b_sb)], **gate_up)
    red[("out", 1)], red[("in", 1)] = chip_sums(b_own, b_r2, "b")
    g_down = _wgrad(hid1, dyb, None, 1, 0, "wgrad_down1_1")
    c5 = [five(g_gu), five(g_down)]
    x0, (p11, p21, hid1), wf1, x1, z, hmix, mixp, y, lse, win, wout, x2, wf2, (p12, p22, hid2) = saved[0]
    (dx, dab, h, dyb, dg, dymix, dxb), (c_r1,) = _ffn_bwd(x2, row(ffn2_norm[0]), dx, p12, p22, wf2, 0, "ffn2_bwd_0",
                                                          comm=[_SiblingTask(c5)], w_out=wout)
    small[("ffn2_norm", 0)] = dg[0]
    c_sb, c_own = core_sums(c5, c_r1, "c")
    g_gu, (c_r2a,) = _wgrad(dab, h, None, 2, 0, "wgrad_gate_up2_0", comm=[_ChipTask(c_sb[0:1])], **gate_up)
    d5a = [five(g_gu)]
    g_down, (c_r2b, d_r1a) = _wgrad(hid2, dyb, None, 1, 0, "wgrad_down2_0",
                                    comm=[_ChipTask(c_sb[1:2]), _SiblingTask(d5a)])
    red[("f1", 1)] = jnp.concatenate(chip_sums(c_own, c_r2a + c_r2b, "c"), axis=0)
    da_sb, da_own = core_sums(d5a, d_r1a, "da")
    d5b = [five(g_down)]
    (dz, dsink, dvn, dws, dbt, dpw, dps), (d_r2a, d_r1b) = _mix_bwd(
        z, dymix, lse, *mixp, "mix_bwd_0", comm=[_ChipTask(da_sb), _SiblingTask(d5b)])
    small.update(mix_small(0, dsink, dvn, dws, dbt, dpw, dps))
    db_sb, db_own = core_sums(d5b, d_r1b, "db")
    dx, dab, h, dyb, dg, dgm = _ffn_bwd(x0, row(ffn1_norm[0]), dx, p11, p21, wf1, 0, "ffn1_bwd_0",
                                        mixin=(row(mix_norm[0]), dz, win, x1))
    small[("ffn1_norm", 0)], small[("mix_norm", 0)] = dg[0], dgm[0]
    grad_x = dx.reshape(1, s, D)

    part = [d_final[0] if nm == "final_norm" else jnp.stack([small[(nm, l)] for l in range(DEPTH)]) for nm in SMALL]
    packed = _pack_small(part, loss_part[0, 0])
    g_gate, (d_r2b, (gathered,)) = _wgrad(dab, h, None, 1, 0, "wgrad_gate1_0", cols=FF,
                                          comm=[_ChipTask(db_sb), _GatherTask([packed[None]])])
    red[("f2", 0)] = jnp.concatenate(chip_sums(da_own + db_own, d_r2a + d_r2b, "d"), axis=0)
    f5 = [five(g_gate)]
    g_up, (f_r1,) = _wgrad(dab, h, None, 1, 0, "wgrad_up1_0", comm=[_SiblingTask(f5)], col0=FF, cols=FF)
    f_sb, f_own = core_sums(f5, f_r1, "f")
    u5 = [five(g_up)]
    g_down, (f_r2, u_r1) = _wgrad(hid1, dyb, None, 1, 0, "wgrad_down1_0", comm=[_ChipTask(f_sb), _SiblingTask(u5)])
    u_sb, u_own = core_sums(u5, u_r1, "u")
    w5 = [five(g_down)]
    g_out, (u_r2, w_r1) = _wgrad(y, dxb, None, 1, 0, "wgrad_out_0", comm=[_ChipTask(u_sb), _SiblingTask(w5)])
    w_sb, w_own = core_sums(w5, w_r1, "w")
    o5 = [five(g_out)]
    g_in, (w_r2, o_r1) = _wgrad(dz, hmix, None, 1, 0, "wgrad_in_0", comm=[_ChipTask(w_sb), _SiblingTask(o5)])
    red[("f1", 0)] = jnp.concatenate(chip_sums(f_own + u_own + w_own, f_r2 + u_r2 + w_r2, "f"), axis=0)
    o_sb, o_own = core_sums(o5, o_r1, "o")
    i5 = [five(g_in)]
    i_r1 = _alone(_SiblingTask(i5), "reduce_sibling_last")
    i_sb, i_own = core_sums(i5, i_r1, "i")
    e_r2 = _alone(_ChipTask(o_sb + i_sb), "reduce_chips_last")
    red[("out", 0)], red[("in", 0)] = chip_sums(o_own + i_own, e_r2, "e")

    grads = {}
    red_rows = {}
    for k, nm in enumerate(["w_gate", "w_up", "w_down"]):
        for f in ("f1", "f2"):
            red_rows[f"ffn{f[1]}_{nm}"] = jnp.stack([red[(f, l)][k] for l in range(DEPTH)])
    red_rows["w_in"] = jnp.concatenate([red[("in", l)] for l in range(DEPTH)], axis=0)
    red_rows["w_out"] = jnp.concatenate([red[("out", l)] for l in range(DEPTH)], axis=0)
    transposed = ("ffn1_w_gate", "ffn1_w_up", "ffn2_w_gate", "ffn2_w_up", "w_in")

    small_w = dict(ffn1_norm=ffn1_norm, mix_norm=mix_norm, attn_sinks=attn_sinks, gmlp_v_norm=gmlp_v_norm,
                   gmlp_w_s=gmlp_w_s, gmlp_b=gmlp_b, pool_w=pool_w, pool_scale=pool_scale, ffn2_norm=ffn2_norm,
                   final_norm=final_norm)
    small_m = dict(ffn1_norm=m_ffn1_norm, mix_norm=m_mix_norm, attn_sinks=m_attn_sinks, gmlp_v_norm=m_gmlp_v_norm,
                   gmlp_w_s=m_gmlp_w_s, gmlp_b=m_gmlp_b, pool_w=m_pool_w, pool_scale=m_pool_scale,
                   ffn2_norm=m_ffn2_norm, final_norm=m_final_norm)
    small_v = dict(ffn1_norm=v_ffn1_norm, mix_norm=v_mix_norm, attn_sinks=v_attn_sinks, gmlp_v_norm=v_gmlp_v_norm,
                   gmlp_w_s=v_gmlp_w_s, gmlp_b=v_gmlp_b, pool_w=v_pool_w, pool_scale=v_pool_scale,
                   ffn2_norm=v_ffn2_norm, final_norm=v_final_norm)
    sg, sd, sm, sv = _adamw_small(gathered[0], _pack_small([small_w[nm] for nm in SMALL]),
                                  _pack_small([small_m[nm] for nm in SMALL]),
                                  _pack_small([small_v[nm] for nm in SMALL]), "adamw_small")
    like = [small_w[nm] for nm in SMALL]
    sg_l, loss = _unpack_small(sg, like)
    sd_l, _ = _unpack_small(sd, like)
    sm_l, _ = _unpack_small(sm, like)
    sv_l, _ = _unpack_small(sv, like)
    deltas, new_m, new_v = {}, {}, {}
    for i, nm in enumerate(SMALL):
        grads[nm], deltas[nm], new_m[nm], new_v[nm] = sg_l[i], sd_l[i], sm_l[i], sv_l[i]

    big_w = dict(ffn1_w_gate=ffn1_w_gate, ffn1_w_up=ffn1_w_up, ffn1_w_down=ffn1_w_down, w_in=w_in, w_out=w_out,
                 ffn2_w_gate=ffn2_w_gate, ffn2_w_up=ffn2_w_up, ffn2_w_down=ffn2_w_down)
    big_m = dict(ffn1_w_gate=m_ffn1_w_gate, ffn1_w_up=m_ffn1_w_up, ffn1_w_down=m_ffn1_w_down, w_in=m_w_in,
                 w_out=m_w_out, ffn2_w_gate=m_ffn2_w_gate, ffn2_w_up=m_ffn2_w_up, ffn2_w_down=m_ffn2_w_down)
    big_v = dict(ffn1_w_gate=v_ffn1_w_gate, ffn1_w_up=v_ffn1_w_up, ffn1_w_down=v_ffn1_w_down, w_in=v_w_in,
                 w_out=v_w_out, ffn2_w_gate=v_ffn2_w_gate, ffn2_w_up=v_ffn2_w_up, ffn2_w_down=v_ffn2_w_down)
    for nm in big_w:
        view = t if nm in transposed else (lambda a: a)
        res = _adamw(view(big_w[nm]), red_rows[nm], view(big_m[nm]), view(big_v[nm]), f"adamw_{nm}")
        grads[nm] = view(red_rows[nm])
        deltas[nm], new_m[nm], new_v[nm] = [view(r) for r in res]

    order = ["ffn1_norm", "ffn1_w_gate", "ffn1_w_up", "ffn1_w_down", "mix_norm", "w_in", "attn_sinks", "gmlp_v_norm",
             "gmlp_w_s", "gmlp_b", "pool_w", "pool_scale", "w_out", "ffn2_norm", "ffn2_w_gate", "ffn2_w_up",
             "ffn2_w_down", "final_norm"]
    return (loss, grad_x, *[grads[n] for n in order], *[deltas[n] for n in order],
            *[new_m[n] for n in order], *[new_v[n] for n in order])
```

```python
import functools
import math

import jax
import jax.numpy as jnp
from jax import lax
from jax.experimental import pallas as pl
from jax.experimental.pallas import tpu as pltpu

F32 = jnp.float32
BF16 = jnp.bfloat16
MESH = pl.DeviceIdType.MESH

D = 1024
FF = 2816
INW = 1536
N_DEV = 8
DEPTH = 2
BLK = 128
HD = 64
N_HEADS = 8
N_KV = 2
REP = 4
ATTN_SCALE = HD ** -0.5
POOL_WINDOWS = (2, 4, 8, 16)
EPS = 1e-6
NEG = -1e30
FC = 256
GELU_C0 = math.sqrt(2.0 / math.pi)
GELU_C1 = 0.044715

ADAM_LR = 0.001
ADAM_B1 = 0.9
ADAM_B2 = 0.999
ADAM_EPS = 1e-08
ADAM_WD = 0.01
ADAM_STEP = 10

VMEM_LIMIT = 60 * 1024 * 1024

O_K, O_V, O_U, O_G, O_P = 512, 640, 768, 1024, 1280


def _call(body, **kw):
    return pl.pallas_call(body, **kw)


def _params(sem=None, vmem=VMEM_LIMIT):
    return pltpu.CompilerParams(dimension_semantics=sem, vmem_limit_bytes=vmem)


def _host(comm, body, *, name, grid, in_specs, out_specs, out_shape, args, scratch_shapes=(), aliases=None):
    single = not isinstance(out_shape, (list, tuple))
    out_specs_l = [out_specs] if single else list(out_specs)
    out_shape_l = [out_shape] if single else list(out_shape)
    n_in, n_out, n_scr = len(in_specs), len(out_shape_l), len(scratch_shapes)
    steps = grid[0]
    any_spec = pl.BlockSpec(memory_space=pl.ANY)

    def wrapped(*refs):
        pos = 0

        def take(n):
            nonlocal pos
            part = refs[pos:pos + n]
            pos += n
            return part

        ins = take(n_in)
        cins = [take(len(t.inputs)) for t in comm]
        outs = take(n_out)
        couts = [take(len(t.out_shape)) for t in comm]
        scr = take(n_scr)
        cscr = [take(len(t.scratch)) for t in comm]
        i = pl.program_id(0)
        for k, t in enumerate(comm):
            pl.when(i == 0)(functools.partial(t.start, cins[k], couts[k], cscr[k]))
        body(*ins, *outs, *scr)
        for k, t in enumerate(comm):
            pl.when(i == (3 * steps) // 4)(functools.partial(t.mid, cins[k], couts[k], cscr[k]))
            pl.when(i == steps - 1)(functools.partial(t.finish, cins[k], couts[k], cscr[k]))

    c_args = [a for t in comm for a in t.inputs]
    c_shapes = [sh for t in comm for sh in t.out_shape]
    c_scr = [sc for t in comm for sc in t.scratch]
    res = _call(
        wrapped, name=name, grid=grid,
        in_specs=list(in_specs) + [any_spec] * len(c_args),
        out_specs=out_specs_l + [any_spec] * len(c_shapes),
        out_shape=out_shape_l + c_shapes,
        scratch_shapes=list(scratch_shapes) + c_scr,
        input_output_aliases=aliases or {},
        compiler_params=_params(("arbitrary",)),
    )(*args, *c_args)
    outs = res[0] if single else list(res[:n_out])
    if not comm:
        return outs
    c_outs, pos = [], n_out
    for t in comm:
        c_outs.append(list(res[pos:pos + len(t.out_shape)]))
        pos += len(t.out_shape)
    return outs, c_outs


def _nn(a, b):
    return lax.dot_general(a, b, (((1,), (0,)), ((), ())), preferred_element_type=F32)


def _nt(a, b):
    return lax.dot_general(a, b, (((1,), (1,)), ((), ())), preferred_element_type=F32)


def _tn(a, b):
    return lax.dot_general(a, b, (((0,), (0,)), ((), ())), preferred_element_type=F32)


def _gelu(x):
    x2 = x * x
    t = jnp.tanh(x * (GELU_C0 + (GELU_C0 * GELU_C1) * x2))
    hx = 0.5 * x
    return hx + hx * t, (hx, x2, t)


def _gelu_grad(parts):
    hx, x2, t = parts
    return (0.5 + 0.5 * t) + (hx * (1.0 - t * t)) * (GELU_C0 + (3.0 * GELU_C0 * GELU_C1) * x2)


def _rms(x):
    r = lax.rsqrt(jnp.mean(x * x, axis=-1, keepdims=True) + EPS)
    return x * r, r


def _rms_bwd(dy, xh, r, g):
    dg = jnp.sum(dy * xh, axis=0, keepdims=True)
    dxh = dy * g
    dx = r * (dxh - xh * jnp.mean(dxh * xh, axis=-1, keepdims=True))
    return dx, dg


def _wspec(rows, m):
    return pl.BlockSpec((None, rows, D), lambda i, m=m: (m, 0, 0), pipeline_mode=pl.Buffered(1))


def _rowspec(tm, cols):
    return pl.BlockSpec((tm, cols), lambda i: (i, 0))


def _fixspec(rows, cols):
    return pl.BlockSpec((rows, cols), lambda i: (0, 0))


def _ffn_fwd(x, gain, w352, mg, name, comm=(), mixer=None, loss=None):
    s = x.shape[0]
    tm = min(512, s)
    n_in = 5 + (2 if mixer is not None else 0) + (2 if loss is not None else 0)

    def body(*refs):
        x_ref, g_ref, wg_ref, wu_ref, wd_ref = refs[:5]
        more_in, outs = list(refs[5:n_in]), list(refs[n_in:])
        xo_ref, p1_ref, p2_ref, hid_ref = outs[:4]
        more_out = outs[4:]
        xt = x_ref[...]
        if mixer is not None:
            y_ref, wo_ref = more_in[:2]
            xt = xt + _nn(y_ref[...], wo_ref[...])
            more_out.pop(0)[...] = xt
        xh, _ = _rms(xt)
        h = (xh * g_ref[...]).astype(BF16)
        for c in range(FF // FC):
            sl = slice(c * FC, (c + 1) * FC)
            a = _nt(h, wg_ref[sl, :])
            b = _nt(h, wu_ref[sl, :])
            sig = 0.5 * jnp.tanh(0.5 * a) + 0.5
            sa = a * sig
            p1_ref[:, sl] = (b * (sig + sa * (1.0 - sig))).astype(BF16)
            p2_ref[:, sl] = sa.astype(BF16)
            hid_ref[:, sl] = (sa * b).astype(BF16)
        xo = xt + 0.5 * _nn(hid_ref[...], wd_ref[...])
        if loss is None:
            xo_ref[...] = xo
        else:
            gf_ref, t_ref = more_in[-2:]
            loss_ref, dgf_ref = more_out
            gf = gf_ref[...]
            xh, r = _rms(xo)
            err = xh * gf - t_ref[...]
            lp = 0.5 * jnp.sum(jnp.mean(err * err, axis=-1, keepdims=True), axis=0, keepdims=True)
            xo_ref[...], dgf = _rms_bwd(err * (1.0 / D), xh, r, gf)

            @pl.when(pl.program_id(0) == 0)
            def _():
                dgf_ref[...] = jnp.zeros_like(dgf_ref)
                loss_ref[...] = jnp.zeros_like(loss_ref)

            dgf_ref[0:1, :] += dgf
            loss_ref[0:1, :] += lp + jnp.zeros((1, 128), F32)

    act = jax.ShapeDtypeStruct((s, FF), BF16)
    tok = jax.ShapeDtypeStruct((s, D), F32)
    in_specs = [_rowspec(tm, D), _fixspec(1, D), _wspec(FF, mg), _wspec(FF, mg + 1), _wspec(FF, mg + 2)]
    out_specs = [_rowspec(tm, D), _rowspec(tm, FF), _rowspec(tm, FF), _rowspec(tm, FF)]
    out_shape = [tok, act, act, act]
    args = (x, gain, w352, w352, w352)
    if mixer is not None:
        in_specs += [_rowspec(tm, D), _wspec(D, 0)]
        out_specs += [_rowspec(tm, D)]
        out_shape += [tok]
        args += tuple(mixer)
    if loss is not None:
        in_specs += [_fixspec(1, D), _rowspec(tm, D)]
        out_specs += [_fixspec(8, 128), _fixspec(8, D)]
        out_shape += [jax.ShapeDtypeStruct((8, 128), F32), jax.ShapeDtypeStruct((8, D), F32)]
        args += tuple(loss)
    return _host(comm, body, name=name, grid=(s // tm,), in_specs=in_specs, out_specs=out_specs,
                 out_shape=out_shape, args=args)


def _ffn_bwd(x, gain, dy, p1, p2, w352, mg, name, comm=(), w_out=None, mixin=None):
    s = x.shape[0]
    tm = min(256, s)
    n_in = 8 + (1 if w_out is not None else 0) + (4 if mixin is not None else 0)

    def body(*refs):
        x_ref, g_ref, dy_ref, p1_ref, p2_ref, wg_ref, wu_ref, wd_ref = refs[:8]
        more_in, outs = list(refs[8:n_in]), list(refs[n_in:])
        dx_ref, dab_ref, h_ref, dyb_ref, dg_ref = outs[:5]
        more_out = outs[5:]
        i = pl.program_id(0)
        xt = x_ref[...]
        g = g_ref[...]
        xh, r = _rms(xt)
        h_ref[...] = (xh * g).astype(BF16)
        dyt = dy_ref[...]
        if mixin is not None:
            gm_ref, dz_ref, win_ref, x1_ref = more_in[-4:]
            dgm_ref = more_out[-1]
            xh1, r1 = _rms(x1_ref[...])
            dxm, dgm = _rms_bwd(_nn(dz_ref[...], win_ref[...]), xh1, r1, gm_ref[...])
            dyt = dyt + dxm

            @pl.when(i == 0)
            def _():
                dgm_ref[...] = jnp.zeros_like(dgm_ref)

            dgm_ref[0:1, :] += dgm
        dyb = (0.5 * dyt).astype(BF16)
        dyb_ref[...] = dyb
        for c in range(FF // FC):
            sl = slice(c * FC, (c + 1) * FC)
            dhid = _nt(dyb, wd_ref[sl, :])
            dab_ref[:, sl] = (dhid * p1_ref[:, sl].astype(F32)).astype(BF16)
            dab_ref[:, FF + c * FC:FF + (c + 1) * FC] = (dhid * p2_ref[:, sl].astype(F32)).astype(BF16)
        dh = _nn(dab_ref[:, :FF], wg_ref[...]) + _nn(dab_ref[:, FF:], wu_ref[...])
        dxn, dg = _rms_bwd(dh, xh, r, g)
        dx = dyt + dxn
        dx_ref[...] = dx
        if w_out is not None:
            dym_ref, dxb_ref = more_out[:2]
            dxb = dx.astype(BF16)
            dxb_ref[...] = dxb
            dym_ref[...] = _nt(dxb, more_in[0][...])

        @pl.when(i == 0)
        def _():
            dg_ref[...] = jnp.zeros_like(dg_ref)

        dg_ref[0:1, :] += dg

    tok = jax.ShapeDtypeStruct((s, D), BF16)
    tok32 = jax.ShapeDtypeStruct((s, D), F32)
    gain_grad = jax.ShapeDtypeStruct((8, D), F32)
    in_specs = [_rowspec(tm, D), _fixspec(1, D), _rowspec(tm, D), _rowspec(tm, FF), _rowspec(tm, FF),
                _wspec(FF, mg), _wspec(FF, mg + 1), _wspec(FF, mg + 2)]
    out_specs = [_rowspec(tm, D), _rowspec(tm, 2 * FF), _rowspec(tm, D), _rowspec(tm, D), _fixspec(8, D)]
    out_shape = [tok32, jax.ShapeDtypeStruct((s, 2 * FF), BF16), tok, tok, gain_grad]
    args = (x, gain, dy, p1, p2, w352, w352, w352)
    if w_out is not None:
        in_specs += [_wspec(D, 0)]
        out_specs += [_rowspec(tm, D), _rowspec(tm, D)]
        out_shape += [tok32, tok]
        args += (w_out,)
    if mixin is not None:
        in_specs += [_fixspec(1, D), _rowspec(tm, INW), _wspec(INW, 0), _rowspec(tm, D)]
        out_specs += [_fixspec(8, D)]
        out_shape += [gain_grad]
        args += tuple(mixin)
    return _host(comm, body, name=name, grid=(s // tm,), in_specs=in_specs, out_specs=out_specs,
                 out_shape=out_shape, args=args)


def _wgrad(a, b, g, n_slabs, m, name, comm=(), col0=0, cols=None, slab_rows=None):
    s = a.shape[0]
    cols = a.shape[1] if cols is None else cols
    slab_rows = cols if slab_rows is None else slab_rows
    mb = 256
    per_slab = slab_rows // mb

    def body(*refs):
        refs[-1][...] = _tn(refs[0][...], refs[1][...])

    in_specs = [pl.BlockSpec((s, mb), lambda i: (0, col0 // mb + i)),
                pl.BlockSpec((s, D), lambda i: (0, 0), pipeline_mode=pl.Buffered(1))]
    args = [a, b]
    aliases = {}
    if g is not None:
        in_specs.append(pl.BlockSpec(memory_space=pl.ANY))
        args.append(g)
        aliases = {2: 0}
    return _host(
        comm, body, name=name, grid=(cols // mb,),
        in_specs=in_specs,
        out_specs=pl.BlockSpec((None, mb, D), lambda i: (m + i // per_slab, i % per_slab, 0)),
        out_shape=jax.ShapeDtypeStruct((n_slabs, slab_rows, D), F32),
        aliases=aliases, args=args)


def _mixin_fwd(x, gain, w192, l, name):
    s = x.shape[0]
    tm = min(512, s)

    def body(x_ref, g_ref, w_ref, z_ref, h_ref):
        xh, _ = _rms(x_ref[...])
        h = (xh * g_ref[...]).astype(BF16)
        h_ref[...] = h
        z_ref[...] = _nt(h, w_ref[...])

    return _call(
        body, name=name, grid=(s // tm,),
        in_specs=[_rowspec(tm, D), _fixspec(1, D), _wspec(INW, l)],
        out_specs=[_rowspec(tm, INW), _rowspec(tm, D)],
        out_shape=[jax.ShapeDtypeStruct((s, INW), F32), jax.ShapeDtypeStruct((s, D), BF16)],
        compiler_params=_params(("arbitrary",)),
    )(x, gain, w192)


MIX_NB = 4
TILE = MIX_NB * BLK
GROUP_ROWS = REP * BLK


class _Block:
    def __init__(self, n, j, zc_ref, zkvp_ref, zpp_ref):
        self.zc, self.zkvp, self.zpp = zc_ref, zkvp_ref, zpp_ref
        self.first = j == 0
        self.r = slice(j * BLK, (j + 1) * BLK)
        self.rp = slice((j - 1) * BLK, j * BLK)
        self.index = n * MIX_NB + j
        self.lo = jnp.where(n > 0, 0, BLK) if self.first else 0
        self.has_prev = jnp.where(n > 0, 1.0, 0.0) if self.first else 1.0

    def cols(self, c0, c1):
        return self.zc[self.r, c0:c1]

    def prev_kv(self, c0, c1):
        return self.zkvp[:, c0:c1] if self.first else self.zc[self.rp, O_K + c0:O_K + c1]

    def prev_p(self):
        return self.zpp[...] * self.has_prev if self.first else self.zc[self.rp, O_P:INW]


def _band_mask(rows):
    row = lax.broadcasted_iota(jnp.int32, (rows, 2 * BLK), 0) & (BLK - 1)
    col = lax.broadcasted_iota(jnp.int32, (rows, 2 * BLK), 1)
    return (col > row) & (col <= row + BLK)


def _block_mask(band, blk):
    if not blk.first:
        return band
    return band & (lax.broadcasted_iota(jnp.int32, band.shape, 1) >= blk.lo)


def _lane_head(shape):
    return lax.broadcasted_iota(jnp.int32, shape, 1) // HD


def _lane_group_select(vals):
    grp = _lane_head(vals[0].shape)
    return jnp.where(grp == 0, vals[0], jnp.where(grp == 1, vals[1], jnp.where(grp == 2, vals[2], vals[3])))


def _pool_count(index):
    row = lax.broadcasted_iota(jnp.int32, (BLK, 256), 0)
    pos1 = (index * BLK + row + 1).astype(F32)
    wl = _lane_group_select([jnp.full((BLK, 256), float(w), F32) for w in POOL_WINDOWS])
    return jnp.minimum(pos1, wl)


def _window_sums(e, forward):
    tot = e.shape[0]
    lv = e
    out = []
    for sh in (1, 2, 4, 8):
        lv = lv + pltpu.roll(lv, sh if forward else tot - sh, 0)
        out.append(lv)
    return _lane_group_select(out)


def _stack_heads(get, g):
    return jnp.concatenate([get((g * REP + rr) * HD, (g * REP + rr + 1) * HD) for rr in range(REP)], axis=0)


def _sink_column(sink_ref, g):
    return jnp.concatenate([jnp.full((BLK, 1), sink_ref[g * REP + rr], F32) for rr in range(REP)], axis=0)


def _kv_window(blk, g):
    kk = jnp.concatenate([blk.prev_kv(g * HD, (g + 1) * HD),
                          blk.cols(O_K + g * HD, O_K + (g + 1) * HD)], axis=0).astype(BF16)
    vv = jnp.concatenate([blk.prev_kv(BLK + g * HD, BLK + (g + 1) * HD),
                          blk.cols(O_V + g * HD, O_V + (g + 1) * HD)], axis=0).astype(BF16)
    return kk, vv


def _mix_common(blk, vn_ref, wcat_ref, bexp_ref, pwbd_ref):
    u, tu = _gelu(blk.cols(O_U, O_G))
    gv, tv = _gelu(blk.cols(O_G, O_P))
    xh, rv = _rms(gv)
    vnb = (xh * vn_ref[...]).astype(BF16)
    head = _lane_head((BLK, 256))
    vn_bd = jnp.concatenate([jnp.where(head == h, vnb, jnp.zeros_like(vnb)) for h in range(4)], axis=0)
    row = lax.broadcasted_iota(jnp.int32, (BLK, 4 * BLK), 0)
    col = lax.broadcasted_iota(jnp.int32, (BLK, 4 * BLK), 1) & (BLK - 1)
    tril = col <= row
    wcat = jnp.where(tril, wcat_ref[...], 0.0).astype(BF16)
    f = _nn(wcat, vn_bd) + bexp_ref[...]
    p = blk.cols(O_P, INW)
    e = jnp.concatenate([blk.prev_p(), p], axis=0)
    cnt = _pool_count(blk.index)
    diff = (_window_sums(e, True)[BLK:, :] / cnt - p).astype(BF16)
    pwbd = pwbd_ref[...].astype(BF16)
    pout = _nn(diff, pwbd)
    return dict(u=u, tu=tu, tv=tv, xh=xh, rv=rv, vn_bd=vn_bd, wcat=wcat, f=f, cnt=cnt, diff=diff, pwbd=pwbd,
                pout=pout, tril=tril, head=head)


def _mix_fwd(z, sinks, vnorm, wcat, bexp, pwbd, pscale, name):
    s = z.shape[0]
    nt = s // TILE

    def body(sink_ref, zc_ref, zkvp_ref, zpp_ref, vn_ref, wcat_ref, bexp_ref, pwbd_ref, ps_ref, y_ref, lse_ref):
        n = pl.program_id(0)
        lse_ref[...] = jnp.zeros_like(lse_ref)
        band = _band_mask(BLK)
        for j in range(MIX_NB):
            blk = _Block(n, j, zc_ref, zkvp_ref, zpp_ref)
            valid = _block_mask(band, blk)
            for g in range(N_KV):
                kk, vv = _kv_window(blk, g)
                for rr in range(REP):
                    h = g * REP + rr
                    qh = (blk.cols(h * HD, (h + 1) * HD) * ATTN_SCALE).astype(BF16)
                    sc = jnp.where(valid, _nt(qh, kk), NEG)
                    sink = sink_ref[h]
                    m = jnp.maximum(jnp.max(sc, axis=-1, keepdims=True), sink)
                    ex = jnp.exp(sc - m)
                    den = jnp.sum(ex, axis=-1, keepdims=True) + jnp.exp(sink - m)
                    y_ref[blk.r, h * HD:(h + 1) * HD] = _nn((ex / den).astype(BF16), vv).astype(BF16)
                    lse_ref[blk.r, h:h + 1] = m + jnp.log(den)
            c = _mix_common(blk, vn_ref, wcat_ref, bexp_ref, pwbd_ref)
            y_ref[blk.r, 512:768] = (c["u"] * c["f"]).astype(BF16)
            y_ref[blk.r, 768:1024] = (c["pout"] * ps_ref[...]).astype(BF16)

    halo = lambda n: jnp.maximum(MIX_NB * n - 1, 0)
    return _call(
        body, name=name, grid=(nt,),
        in_specs=[pl.BlockSpec(memory_space=pltpu.SMEM),
                  pl.BlockSpec((TILE, INW), lambda n: (n, 0)),
                  pl.BlockSpec((BLK, 256), lambda n: (halo(n), 2)),
                  pl.BlockSpec((BLK, 256), lambda n: (halo(n), 5)),
                  _fixspec(1, 256), _fixspec(BLK, 4 * BLK), _fixspec(BLK, 256), _fixspec(256, 256), _fixspec(1, 256)],
        out_specs=[pl.BlockSpec((TILE, D), lambda n: (n, 0)), pl.BlockSpec((TILE, 128), lambda n: (n, 0))],
        out_shape=[jax.ShapeDtypeStruct((s, D), BF16), jax.ShapeDtypeStruct((s, 128), F32)],
        compiler_params=_params(("arbitrary",)),
    )(sinks, z, z, z, vnorm, wcat, bexp, pwbd, pscale)


def _mix_bwd(z, dy, lse, sinks, vnorm, wcat, bexp, pwbd, pscale, name, comm=()):
    s = z.shape[0]
    nt = s // TILE
    last = slice(TILE - BLK, TILE)

    def body(sink_ref, zc_ref, zkvp_ref, zpp_ref, dy_ref, lse_ref, vn_ref, wcat_ref, bexp_ref, pwbd_ref, ps_ref,
             dz_ref, dsink_ref, dvn_ref, dws_ref, dbt_ref, dpw_ref, dps_ref, carry_ref, ddc_ref):
        n = pl.program_id(0)

        @pl.when(n == 0)
        def _():
            carry_ref[...] = jnp.zeros_like(carry_ref)
            ddc_ref[...] = jnp.zeros_like(ddc_ref)
            dsink_ref[...] = jnp.zeros_like(dsink_ref)
            dvn_ref[...] = jnp.zeros_like(dvn_ref)
            dws_ref[...] = jnp.zeros_like(dws_ref)
            dbt_ref[...] = jnp.zeros_like(dbt_ref)
            dpw_ref[...] = jnp.zeros_like(dpw_ref)
            dps_ref[...] = jnp.zeros_like(dps_ref)

        def block_grads(j):
            blk = _Block(n, j, zc_ref, zkvp_ref, zpp_ref)
            valid = _block_mask(_band_mask(GROUP_ROWS), blk)
            out = dict(dq=[], dsink=[], dbt=[])
            dk_prev, dk_cur, dv_prev, dv_cur = [], [], [], []
            for g in range(N_KV):
                kk, vv = _kv_window(blk, g)
                q4 = _stack_heads(blk.cols, g).astype(BF16)
                do4 = _stack_heads(lambda c0, c1: dy_ref[blk.r, c0:c1], g).astype(BF16)
                lse4 = jnp.concatenate([lse_ref[blk.r, g * REP + rr:g * REP + rr + 1] for rr in range(REP)], axis=0)
                sc = jnp.where(valid, _nt(q4, kk) * ATTN_SCALE, NEG)
                pr = jnp.exp(sc - lse4)
                dp = _nt(do4, vv)
                delta = jnp.sum(pr * dp, axis=-1, keepdims=True)
                ds = ((pr * (dp - delta)) * ATTN_SCALE).astype(BF16)
                sunk = jnp.exp(_sink_column(sink_ref, g) - lse4) * delta
                dq4 = _nn(ds, kk)
                for rr in range(REP):
                    out["dsink"].append(-jnp.sum(sunk[rr * BLK:(rr + 1) * BLK], axis=0, keepdims=True))
                    out["dq"].append(dq4[rr * BLK:(rr + 1) * BLK])
                dkk = _tn(ds, q4)
                dvv = _tn(pr.astype(BF16), do4)
                dk_prev.append(dkk[:BLK]); dk_cur.append(dkk[BLK:])
                dv_prev.append(dvv[:BLK]); dv_cur.append(dvv[BLK:])
            out["dk_prev"], out["dk_cur"] = jnp.concatenate(dk_prev, axis=1), jnp.concatenate(dk_cur, axis=1)
            out["dv_prev"], out["dv_cur"] = jnp.concatenate(dv_prev, axis=1), jnp.concatenate(dv_cur, axis=1)
            c = _mix_common(blk, vn_ref, wcat_ref, bexp_ref, pwbd_ref)
            dyg = dy_ref[blk.r, 512:768]
            du = dyg * c["f"]
            df = dyg * c["u"]
            out["dzu"] = du * _gelu_grad(c["tu"])
            dfb = df.astype(BF16)
            for h in range(4):
                out["dbt"].append(jnp.sum(df[:, h * HD:(h + 1) * HD], axis=1, keepdims=True))
            out["dws"] = jnp.where(c["tril"], _nt(dfb, c["vn_bd"]), 0.0)
            dvn_bd = _tn(c["wcat"], dfb)
            dvn = functools.reduce(lambda a, b: a + b, [
                jnp.where(c["head"] == h, dvn_bd[h * BLK:(h + 1) * BLK], 0.0) for h in range(4)])
            dgv, out["dvn"] = _rms_bwd(dvn, c["xh"], c["rv"], vn_ref[...])
            out["dzv"] = dgv * _gelu_grad(c["tv"])
            dyp = dy_ref[blk.r, 768:1024]
            out["dps"] = jnp.sum(dyp * c["pout"], axis=0, keepdims=True)
            dout = (dyp * ps_ref[...]).astype(BF16)
            out["dpw"] = _tn(c["diff"], dout)
            out["ddiff"] = _nt(dout, c["pwbd"])
            out["dd"] = out["ddiff"] / c["cnt"]
            return out

        def write_previous_tile(dd_next, dk_next, dv_next):
            if MIX_NB > 1:
                dz_ref[0:TILE - BLK, :] = carry_ref[0:TILE - BLK, :].astype(BF16)
            rs = _window_sums(jnp.concatenate([ddc_ref[...], dd_next], axis=0), False)
            dz_ref[last, 0:O_K] = carry_ref[last, 0:O_K].astype(BF16)
            dz_ref[last, O_K:O_V] = (carry_ref[last, O_K:O_V] + dk_next).astype(BF16)
            dz_ref[last, O_V:O_U] = (carry_ref[last, O_V:O_U] + dv_next).astype(BF16)
            dz_ref[last, O_U:O_P] = carry_ref[last, O_U:O_P].astype(BF16)
            dz_ref[last, O_P:INW] = (carry_ref[last, O_P:INW] + rs[:BLK, :]).astype(BF16)

        @pl.when(n < nt)
        def _():
            parts = [block_grads(j) for j in range(MIX_NB)]
            total = lambda key, i=None: functools.reduce(
                lambda a, b: a + b, [p[key] if i is None else p[key][i] for p in parts])
            for h in range(N_HEADS):
                dsink_ref[h:h + 1, :] += total("dsink", h) + jnp.zeros((1, 128), F32)
            for h in range(4):
                dbt_ref[:, h:h + 1] += total("dbt", h)
            dws_ref[...] += total("dws")
            dpw_ref[...] += total("dpw")
            dvn_ref[0:1, :] += total("dvn")
            dps_ref[0:1, :] += total("dps")
            write_previous_tile(parts[0]["dd"], parts[0]["dk_prev"], parts[0]["dv_prev"])
            for j, p in enumerate(parts):
                r = slice(j * BLK, (j + 1) * BLK)
                nxt = parts[j + 1] if j + 1 < MIX_NB else None
                for h in range(N_HEADS):
                    carry_ref[r, h * HD:(h + 1) * HD] = p["dq"][h]
                carry_ref[r, O_U:O_G] = p["dzu"]
                carry_ref[r, O_G:O_P] = p["dzv"]
                if nxt is None:
                    carry_ref[r, O_K:O_V] = p["dk_cur"]
                    carry_ref[r, O_V:O_U] = p["dv_cur"]
                    carry_ref[r, O_P:INW] = -p["ddiff"]
                    ddc_ref[...] = p["dd"]
                else:
                    rs = _window_sums(jnp.concatenate([p["dd"], nxt["dd"]], axis=0), False)
                    carry_ref[r, O_K:O_V] = p["dk_cur"] + nxt["dk_prev"]
                    carry_ref[r, O_V:O_U] = p["dv_cur"] + nxt["dv_prev"]
                    carry_ref[r, O_P:INW] = rs[:BLK, :] - p["ddiff"]

        @pl.when(n == nt)
        def _():
            none = jnp.zeros((BLK, BLK), F32)
            write_previous_tile(jnp.zeros((BLK, 256), F32), none, none)

    cur = lambda n: jnp.minimum(n, nt - 1)
    done = lambda n: jnp.maximum(n - 1, 0)
    halo = lambda n: jnp.maximum(MIX_NB * jnp.minimum(n, nt - 1) - 1, 0)
    return _host(
        comm, body, name=name, grid=(nt + 1,),
        in_specs=[pl.BlockSpec(memory_space=pltpu.SMEM),
                  pl.BlockSpec((TILE, INW), lambda n: (cur(n), 0)),
                  pl.BlockSpec((BLK, 256), lambda n: (halo(n), 2)),
                  pl.BlockSpec((BLK, 256), lambda n: (halo(n), 5)),
                  pl.BlockSpec((TILE, D), lambda n: (cur(n), 0)),
                  pl.BlockSpec((TILE, 128), lambda n: (cur(n), 0)),
                  _fixspec(1, 256), _fixspec(BLK, 4 * BLK), _fixspec(BLK, 256), _fixspec(256, 256), _fixspec(1, 256)],
        out_specs=[pl.BlockSpec((TILE, INW), lambda n: (done(n), 0)),
                   _fixspec(8, 128), _fixspec(8, 256), _fixspec(BLK, 4 * BLK), _fixspec(BLK, 128),
                   _fixspec(256, 256), _fixspec(8, 256)],
        out_shape=[jax.ShapeDtypeStruct((s, INW), BF16), jax.ShapeDtypeStruct((8, 128), F32),
                   jax.ShapeDtypeStruct((8, 256), F32), jax.ShapeDtypeStruct((BLK, 4 * BLK), F32),
                   jax.ShapeDtypeStruct((BLK, 128), F32), jax.ShapeDtypeStruct((256, 256), F32),
                   jax.ShapeDtypeStruct((8, 256), F32)],
        scratch_shapes=[pltpu.VMEM((TILE, INW), F32), pltpu.VMEM((BLK, 256), F32)],
        args=(sinks, z, z, z, dy, lse, vnorm, wcat, bexp, pwbd, pscale))


def _position():
    x, y, c = lax.axis_index("x"), lax.axis_index("y"), lax.axis_index("c")
    return x, y, c


class _GatherTask:
    def __init__(self, srcs):
        self.inputs = list(srcs)
        ng = len(srcs)
        self.out_shape = [jax.ShapeDtypeStruct((a.shape[0], N_DEV) + a.shape[1:], a.dtype) for a in srcs]
        self.scratch = [pltpu.SemaphoreType.DMA((ng, 7)), pltpu.SemaphoreType.DMA((ng, 7)),
                        pltpu.SemaphoreType.DMA((ng,))]

    def _plan(self, src, dst, sems):
        send_sems, recv_sems, local_sems = sems
        ng = len(src)
        x, y, c = _position()
        me, sibling = (x, y, c), (x, y, 1 - c)
        chips = [(1 - x, y), (x, 1 - y), (1 - x, 1 - y)]

        def slot(pos):
            return 4 * pos[0] + 2 * pos[1] + pos[2]

        def copy(gi, k, block, to, from_src=False):
            rows = dst[gi].at[:, slot(block)]
            return pltpu.make_async_remote_copy(
                src_ref=src[gi] if from_src else rows, dst_ref=rows,
                send_sem=send_sems.at[gi, k], recv_sem=recv_sems.at[gi, k],
                device_id=to, device_id_type=MESH)

        make = functools.partial
        mine = [make(pltpu.make_async_copy, src[gi], dst[gi].at[:, slot(me)], local_sems.at[gi]) for gi in range(ng)]
        first = []
        for gi in range(ng):
            first.append(make(copy, gi, 0, me, sibling, True))
            first += [make(copy, gi, 1 + j, me, (*chip, c), True) for j, chip in enumerate(chips)]
        passed = [make(copy, gi, 4 + j, (*chip, c), sibling) for j, chip in enumerate(chips) for gi in range(ng)]
        arrive_ici = [make(copy, gi, 1 + j, (*chip, c), me) for j, chip in enumerate(chips) for gi in range(ng)]
        arrive_d2d = [make(copy, gi, 0, sibling, me) for gi in range(ng)]
        arrive_d2d += [make(copy, gi, 4 + j, (*chip, 1 - c), me) for j, chip in enumerate(chips) for gi in range(ng)]
        return mine, first, passed, arrive_ici, arrive_d2d

    def start(self, src, dst, sems):
        mine, first, _, _, _ = self._plan(src, dst, sems)
        for cp in mine + first:
            cp().start()

    def mid(self, src, dst, sems):
        _, _, passed, arrive_ici, _ = self._plan(src, dst, sems)
        for arrived, fw in zip(arrive_ici, passed):
            arrived().wait_recv()
            fw().start()

    def finish(self, src, dst, sems):
        mine, first, passed, _, arrive_d2d = self._plan(src, dst, sems)
        for cp in arrive_d2d:
            cp().wait_recv()
        for cp in first + passed:
            cp().wait_send()
        for cp in mine:
            cp().wait()


class _SiblingTask:
    def __init__(self, g5s):
        self.inputs = list(g5s)
        ng = len(g5s)
        self.out_shape = [jax.ShapeDtypeStruct((a.shape[0], 4) + a.shape[3:], a.dtype) for a in g5s]
        self.scratch = [pltpu.SemaphoreType.DMA((ng,)), pltpu.SemaphoreType.DMA((ng,))]

    def _plan(self, src, dst, sems):
        send_sems, recv_sems = sems
        x, y, c = _position()
        return [functools.partial(
            pltpu.make_async_remote_copy,
            src_ref=src[gi].at[:, :, 1 - c], dst_ref=dst[gi],
            send_sem=send_sems.at[gi], recv_sem=recv_sems.at[gi],
            device_id=(x, y, 1 - c), device_id_type=MESH) for gi in range(len(src))]

    def start(self, src, dst, sems):
        for cp in self._plan(src, dst, sems):
            cp().start()

    def mid(self, src, dst, sems):
        pass

    def finish(self, src, dst, sems):
        for cp in self._plan(src, dst, sems):
            cp().wait()


class _ChipTask(_SiblingTask):
    def __init__(self, sbs):
        self.inputs = list(sbs)
        ng = len(sbs)
        self.out_shape = [jax.ShapeDtypeStruct(a.shape, a.dtype) for a in sbs]
        self.scratch = [pltpu.SemaphoreType.DMA((ng, 3)), pltpu.SemaphoreType.DMA((ng, 3))]

    def _plan(self, src, dst, sems):
        send_sems, recv_sems = sems
        x, y, c = _position()
        jme = 2 * x + y
        chips = [(1 - x, y), (x, 1 - y), (1 - x, 1 - y)]
        return [functools.partial(
            pltpu.make_async_remote_copy,
            src_ref=src[gi].at[:, 2 * chip[0] + chip[1]], dst_ref=dst[gi].at[:, jme],
            send_sem=send_sems.at[gi, k], recv_sem=recv_sems.at[gi, k],
            device_id=(*chip, c), device_id_type=MESH) for k, chip in enumerate(chips) for gi in range(len(src))]


def _alone(task, name):
    n_in, n_out = len(task.inputs), len(task.out_shape)

    def body(*refs):
        parts = (refs[:n_in], refs[n_in:n_in + n_out], refs[n_in + n_out:])
        task.start(*parts)
        task.mid(*parts)
        task.finish(*parts)

    any_spec = pl.BlockSpec(memory_space=pl.ANY)
    return _call(body, name=name, in_specs=[any_spec] * n_in, out_specs=[any_spec] * n_out,
                 out_shape=task.out_shape, scratch_shapes=task.scratch)(*task.inputs)


def _core_sum(ids, g5, r1, name):
    n, _, _, rows, _ = g5.shape

    def body(ids_ref, g_ref, r_ref, sb_ref, own_ref):
        j = pl.program_id(2)
        t = g_ref[...] + r_ref[...]
        sb_ref[...] = t.astype(BF16)

        @pl.when(j == ids_ref[1])
        def _():
            own_ref[...] = t

    grid_spec = pltpu.PrefetchScalarGridSpec(
        num_scalar_prefetch=1, grid=(n, 1, 4),
        in_specs=[pl.BlockSpec((None, None, None, rows, D), lambda i, t, j, ids: (i, j, ids[0], t, 0)),
                  pl.BlockSpec((None, None, rows, D), lambda i, t, j, ids: (i, j, t, 0))],
        out_specs=[pl.BlockSpec((None, None, rows, D), lambda i, t, j, ids: (i, j, t, 0)),
                   pl.BlockSpec((None, rows, D), lambda i, t, j, ids: (i, t, 0))])
    return _call(
        body, name=name, grid_spec=grid_spec,
        out_shape=[jax.ShapeDtypeStruct((n, 4, rows, D), BF16), jax.ShapeDtypeStruct((n, rows, D), F32)],
        compiler_params=_params(("arbitrary", "arbitrary", "arbitrary")),
    )(ids, g5, r1)


def _chip_sum(others, own, r2, name):
    n, rows, _ = own.shape

    def body(oth_ref, own_ref, r0_ref, r1_ref, r2_ref, out_ref):
        out_ref[...] = ((own_ref[...] + r0_ref[...].astype(F32)) + r1_ref[...].astype(F32)) \
            + r2_ref[...].astype(F32)

    def rspec(k):
        return pl.BlockSpec((None, None, rows, D), lambda i, oth, k=k: (i, oth[k], 0, 0))

    grid_spec = pltpu.PrefetchScalarGridSpec(
        num_scalar_prefetch=1, grid=(n,),
        in_specs=[pl.BlockSpec((None, rows, D), lambda i, oth: (i, 0, 0)), rspec(0), rspec(1), rspec(2)],
        out_specs=pl.BlockSpec((None, rows, D), lambda i, oth: (i, 0, 0)))
    return _call(
        body, name=name, grid_spec=grid_spec,
        out_shape=jax.ShapeDtypeStruct((n, rows, D), F32),
        compiler_params=_params(("arbitrary",)),
    )(others, own, r2, r2, r2)


def _adam_math(w, g, m, v):
    m = ADAM_B1 * m + (1.0 - ADAM_B1) * g
    v = ADAM_B2 * v + (1.0 - ADAM_B2) * (g * g)
    m_hat = m / (1.0 - ADAM_B1 ** ADAM_STEP)
    v_hat = v / (1.0 - ADAM_B2 ** ADAM_STEP)
    delta = -ADAM_LR * (m_hat / (jnp.sqrt(v_hat) + ADAM_EPS) + ADAM_WD * w)
    return delta, m, v


def _adamw(w, g, m, v, name):
    shape = w.shape
    c = shape[-1]
    r = w.size // c
    rb = max(d for d in range(8, min(r, 512) + 1, 8) if r % d == 0)

    def body(w_ref, g_ref, m_ref, v_ref, d_ref, mo_ref, vo_ref):
        d_ref[...], mo_ref[...], vo_ref[...] = _adam_math(w_ref[...], g_ref[...], m_ref[...], v_ref[...])

    spec = _rowspec(rb, c)
    outs = _call(
        body, name=name, grid=(r // rb,),
        in_specs=[spec] * 4, out_specs=[spec] * 3,
        out_shape=[jax.ShapeDtypeStruct((r, c), F32)] * 3,
        compiler_params=_params(("arbitrary",)),
    )(*[t.reshape(r, c) for t in (w, g, m, v)])
    return [o.reshape(shape) for o in outs]


def _adamw_small(parts, w, m, v, name):
    def body(p_ref, w_ref, m_ref, v_ref, g_ref, d_ref, mo_ref, vo_ref):
        g = p_ref[0]
        for dev in range(1, N_DEV):
            g = g + p_ref[dev]
        g_ref[...] = g
        d_ref[...], mo_ref[...], vo_ref[...] = _adam_math(w_ref[...], g, m_ref[...], v_ref[...])

    return _call(
        body, name=name,
        out_shape=[jax.ShapeDtypeStruct(w.shape, F32)] * 4,
        compiler_params=_params(),
    )(parts, w, m, v)


SMALL = ["ffn1_norm", "mix_norm", "attn_sinks", "gmlp_v_norm", "gmlp_w_s", "gmlp_b", "pool_w", "pool_scale",
         "ffn2_norm", "final_norm"]


def _piece_rows(size):
    return -(-size // 1024) * 8


def _pack_small(arrs, extra=None):
    pieces = []
    for a in list(arrs) + [jnp.zeros((1,), F32) if extra is None else extra]:
        f = a.reshape(-1)
        pieces.append(jnp.pad(f, (0, _piece_rows(f.shape[0]) * 128 - f.shape[0])).reshape(-1, 128))
    return jnp.concatenate(pieces, axis=0)


def _unpack_small(packed, like):
    out, off = [], 0
    for a in like:
        rows = _piece_rows(a.size)
        out.append(packed[off:off + rows].reshape(-1)[:a.size].reshape(a.shape))
        off += rows
    return out, packed[off, 0]


def kernel(x, ffn1_norm, ffn1_w_gate, ffn1_w_up, ffn1_w_down, mix_norm, w_in, attn_sinks, gmlp_v_norm, gmlp_w_s, gmlp_b, pool_w, pool_scale, w_out, ffn2_norm, ffn2_w_gate, ffn2_w_up, ffn2_w_down, final_norm, loss_target, m_ffn1_norm, m_ffn1_w_gate, m_ffn1_w_up, m_ffn1_w_down, m_mix_norm, m_w_in, m_attn_sinks, m_gmlp_v_norm, m_gmlp_w_s, m_gmlp_b, m_pool_w, m_pool_scale, m_w_out, m_ffn2_norm, m_ffn2_w_gate, m_ffn2_w_up, m_ffn2_w_down, m_final_norm, v_ffn1_norm, v_ffn1_w_gate, v_ffn1_w_up, v_ffn1_w_down, v_mix_norm, v_w_in, v_attn_sinks, v_gmlp_v_norm, v_gmlp_w_s, v_gmlp_b, v_pool_w, v_pool_scale, v_w_out, v_ffn2_norm, v_ffn2_w_gate, v_ffn2_w_up, v_ffn2_w_down, v_final_norm):
    s = x.shape[1]
    xi, yi, ci = _position()
    ids = jnp.stack([ci, 2 * xi + yi]).astype(jnp.int32)
    jme = 2 * xi + yi
    others = jnp.stack([k + (k >= jme).astype(jnp.int32) for k in range(3)]).astype(jnp.int32)
    t = lambda a: jnp.swapaxes(a, -1, -2)
    row = lambda a: a.reshape(1, -1)
    full = lambda a: a.reshape(a.shape[0], -1, D)

    loc_f1 = [jnp.stack([t(ffn1_w_gate[l]), t(ffn1_w_up[l]), ffn1_w_down[l]]).astype(BF16) for l in range(DEPTH)]
    loc_f2 = [jnp.stack([t(ffn2_w_gate[l]), t(ffn2_w_up[l]), ffn2_w_down[l]]).astype(BF16) for l in range(DEPTH)]
    loc_in = [t(w_in[l])[None].astype(BF16) for l in range(DEPTH)]
    loc_out = [w_out[l][None].astype(BF16) for l in range(DEPTH)]

    (wf1,) = _alone(_GatherTask([loc_f1[0]]), "gather_first")
    wf1 = full(wf1)
    xc = x.reshape(s, D)
    saved = []
    for l in range(DEPTH):
        x0 = xc
        if l == 0:
            (x1, *act1), ((wf2, win, wout),) = _ffn_fwd(
                x0, row(ffn1_norm[l]), wf1, 0, f"ffn1_fwd_{l}", comm=[_GatherTask([loc_f2[0], loc_in[0], loc_out[0]])])
        else:
            (x1, *act1), ((wf2,),) = _ffn_fwd(
                x0, row(ffn1_norm[l]), wf1, 0, f"ffn1_fwd_{l}", comm=[_GatherTask([loc_f2[1]])])
        wf2, win, wout = full(wf2), full(win), full(wout)
        z, hmix = _mixin_fwd(x1, row(mix_norm[l]), win, 0, f"mixin_fwd_{l}")
        wcat = jnp.concatenate([gmlp_w_s[l][h] for h in range(4)], axis=1)
        bexp = jnp.repeat(t(gmlp_b[l]), HD, axis=1)
        pwbd = jnp.zeros((256, 256), F32)
        for g in range(4):
            pwbd = pwbd.at[g * HD:(g + 1) * HD, g * HD:(g + 1) * HD].set(pool_w[l][g])
        mixp = (attn_sinks[l], row(gmlp_v_norm[l]), wcat, bexp, pwbd, row(pool_scale[l]))
        y, lse = _mix_fwd(z, *mixp, f"mix_fwd_{l}")
        keep = (x0, act1, wf1, x1, z, hmix, mixp, y, lse, win, wout)
        if l == 0:
            (xc, *act2, x2), ((wf1, win, wout),) = _ffn_fwd(
                x1, row(ffn2_norm[l]), wf2, 0, f"ffn2_fwd_{l}", mixer=(y, wout),
                comm=[_GatherTask([loc_f1[1], loc_in[1], loc_out[1]])])
            wf1 = full(wf1)
        else:
            dx, *act2, x2, loss_part, d_final = _ffn_fwd(
                x1, row(ffn2_norm[l]), wf2, 0, f"ffn2_fwd_{l}", mixer=(y, wout),
                loss=(row(final_norm), loss_target.reshape(s, D)))
        saved.append(keep + (x2, wf2, act2))

    def five(g):
        return g.reshape(g.shape[0], 4, 2, g.shape[1] // N_DEV, D)

    def core_sums(g5s, r1s, tag):
        res = [_core_sum(ids, g5, r1, f"core_sum_{tag}_{i}") for i, (g5, r1) in enumerate(zip(g5s, r1s))]
        return [sb for sb, _ in res], [own for _, own in res]

    def chip_sums(owns, r2s, tag):
        return [_chip_sum(others, own, r2, f"chip_sum_{tag}_{i}") for i, (own, r2) in enumerate(zip(owns, r2s))]

    def mix_small(l, dsink, dvn, dws, dbt, dpw, dps):
        return {("attn_sinks", l): dsink[:, 0], ("gmlp_v_norm", l): dvn[0],
                ("gmlp_w_s", l): jnp.stack([dws[:, h * BLK:(h + 1) * BLK] for h in range(4)]),
                ("gmlp_b", l): t(dbt[:, :4]),
                ("pool_w", l): jnp.stack([dpw[g * HD:(g + 1) * HD, g * HD:(g + 1) * HD] for g in range(4)]),
                ("pool_scale", l): dps[0]}

    small = {}
    red = {}
    gate_up = dict(cols=2 * FF, slab_rows=FF)
    x0, (p11, p21, hid1), wf1, x1, z, hmix, mixp, y, lse, win, wout, x2, wf2, (p12, p22, hid2) = saved[1]
    dx, dab, h, dyb, dg, dymix, dxb = _ffn_bwd(x2, row(ffn2_norm[1]), dx, p12, p22, wf2, 0, "ffn2_bwd_1", w_out=wout)
    small[("ffn2_norm", 1)] = dg[0]
    g = _wgrad(dab, h, None, 3, 0, "wgrad_gate_up2_1", **gate_up)
    g = _wgrad(hid2, dyb, g, 3, 2, "wgrad_down2_1")
    a5 = [five(g)]
    g_out, (a_r1,) = _wgrad(y, dxb, None, 1, 0, "wgrad_out_1", comm=[_SiblingTask(a5)])
    a_sb, a_own = core_sums(a5, a_r1, "a")
    (dz, dsink, dvn, dws, dbt, dpw, dps), (a_r2,) = _mix_bwd(z, dymix, lse, *mixp, "mix_bwd_1", comm=[_ChipTask(a_sb)])
    (red[("f2", 1)],) = chip_sums(a_own, a_r2, "a")
    small.update(mix_small(1, dsink, dvn, dws, dbt, dpw, dps))
    g_in = _wgrad(dz, hmix, None, 1, 0, "wgrad_in_1")
    b5 = [five(g_out), five(g_in)]
    (dx, dab, h, dyb, dg, dgm), (b_r1,) = _ffn_bwd(x0, row(ffn1_norm[1]), dx, p11, p21, wf1, 0, "ffn1_bwd_1",
                                                   comm=[_SiblingTask(b5)], mixin=(row(mix_norm[1]), dz, win, x1))
    small[("ffn1_norm", 1)], small[("mix_norm", 1)] = dg[0], dgm[0]
    b_sb, b_own = core_sums(b5, b_r1, "b")
    g_gu, (b_r2,) = _wgrad(dab, h, None, 2, 0, "wgrad_gate_up1_1", comm=[_ChipTask(b_sb)], **gate_up)
    red[("out", 1)], red[("in", 1)] = chip_sums(b_own, b_r2, "b")
    g_down = _wgrad(hid1, dyb, None, 1, 0, "wgrad_down1_1")
    c5 = [five(g_gu), five(g_down)]
    x0, (p11, p21, hid1), wf1, x1, z, hmix, mixp, y, lse, win, wout, x2, wf2, (p12, p22, hid2) = saved[0]
    (dx, dab, h, dyb, dg, dymix, dxb), (c_r1,) = _ffn_bwd(x2, row(ffn2_norm[0]), dx, p12, p22, wf2, 0, "ffn2_bwd_0",
                                                          comm=[_SiblingTask(c5)], w_out=wout)
    small[("ffn2_norm", 0)] = dg[0]
    c_sb, c_own = core_sums(c5, c_r1, "c")
    g_gu, (c_r2a,) = _wgrad(dab, h, None, 2, 0, "wgrad_gate_up2_0", comm=[_ChipTask(c_sb[0:1])], **gate_up)
    d5a = [five(g_gu)]
    g_down, (c_r2b, d_r1a) = _wgrad(hid2, dyb, None, 1, 0, "wgrad_down2_0",
                                    comm=[_ChipTask(c_sb[1:2]), _SiblingTask(d5a)])
    red[("f1", 1)] = jnp.concatenate(chip_sums(c_own, c_r2a + c_r2b, "c"), axis=0)
    da_sb, da_own = core_sums(d5a, d_r1a, "da")
    d5b = [five(g_down)]
    (dz, dsink, dvn, dws, dbt, dpw, dps), (d_r2a, d_r1b) = _mix_bwd(
        z, dymix, lse, *mixp, "mix_bwd_0", comm=[_ChipTask(da_sb), _SiblingTask(d5b)])
    small.update(mix_small(0, dsink, dvn, dws, dbt, dpw, dps))
    db_sb, db_own = core_sums(d5b, d_r1b, "db")
    dx, dab, h, dyb, dg, dgm = _ffn_bwd(x0, row(ffn1_norm[0]), dx, p11, p21, wf1, 0, "ffn1_bwd_0",
                                        mixin=(row(mix_norm[0]), dz, win, x1))
    small[("ffn1_norm", 0)], small[("mix_norm", 0)] = dg[0], dgm[0]
    grad_x = dx.reshape(1, s, D)

    part = [d_final[0] if nm == "final_norm" else jnp.stack([small[(nm, l)] for l in range(DEPTH)]) for nm in SMALL]
    packed = _pack_small(part, loss_part[0, 0])
    g_gate, ((gathered,),) = _wgrad(dab, h, None, 1, 0, "wgrad_gate1_0", cols=FF, comm=[_GatherTask([packed[None]])])
    f5 = [five(g_gate)]
    g_up, (d_r2b, f_r1) = _wgrad(dab, h, None, 1, 0, "wgrad_up1_0", col0=FF, cols=FF,
                                 comm=[_ChipTask(db_sb), _SiblingTask(f5)])
    red[("f2", 0)] = jnp.concatenate(chip_sums(da_own + db_own, d_r2a + d_r2b, "d"), axis=0)
    f_sb, f_own = core_sums(f5, f_r1, "f")
    u5 = [five(g_up)]
    g_down, (f_r2, u_r1) = _wgrad(hid1, dyb, None, 1, 0, "wgrad_down1_0", comm=[_ChipTask(f_sb), _SiblingTask(u5)])
    u_sb, u_own = core_sums(u5, u_r1, "u")
    w5 = [five(g_down)]
    g_in, (u_r2, w_r1) = _wgrad(dz, hmix, None, 1, 0, "wgrad_in_0", comm=[_ChipTask(u_sb), _SiblingTask(w5)])
    w_sb, w_own = core_sums(w5, w_r1, "w")
    i5 = [five(g_in)]
    g_out, (w_r2, i_r1) = _wgrad(y, dxb, None, 1, 0, "wgrad_out_0", comm=[_ChipTask(w_sb), _SiblingTask(i5)])
    red[("f1", 0)] = jnp.concatenate(chip_sums(f_own + u_own + w_own, f_r2 + u_r2 + w_r2, "f"), axis=0)
    i_sb, i_own = core_sums(i5, i_r1, "i")
    o5 = [five(g_out)]
    o_r1 = _alone(_SiblingTask(o5), "reduce_sibling_last")
    o_sb, o_own = core_sums(o5, o_r1, "o")
    e_r2 = _alone(_ChipTask(o_sb + i_sb), "reduce_chips_last")
    red[("out", 0)], red[("in", 0)] = chip_sums(o_own + i_own, e_r2, "e")

    grads = {}
    red_rows = {}
    for k, nm in enumerate(["w_gate", "w_up", "w_down"]):
        for f in ("f1", "f2"):
            red_rows[f"ffn{f[1]}_{nm}"] = jnp.stack([red[(f, l)][k] for l in range(DEPTH)])
    red_rows["w_in"] = jnp.concatenate([red[("in", l)] for l in range(DEPTH)], axis=0)
    red_rows["w_out"] = jnp.concatenate([red[("out", l)] for l in range(DEPTH)], axis=0)
    transposed = ("ffn1_w_gate", "ffn1_w_up", "ffn2_w_gate", "ffn2_w_up", "w_in")

    small_w = dict(ffn1_norm=ffn1_norm, mix_norm=mix_norm, attn_sinks=attn_sinks, gmlp_v_norm=gmlp_v_norm,
                   gmlp_w_s=gmlp_w_s, gmlp_b=gmlp_b, pool_w=pool_w, pool_scale=pool_scale, ffn2_norm=ffn2_norm,
                   final_norm=final_norm)
    small_m = dict(ffn1_norm=m_ffn1_norm, mix_norm=m_mix_norm, attn_sinks=m_attn_sinks, gmlp_v_norm=m_gmlp_v_norm,
                   gmlp_w_s=m_gmlp_w_s, gmlp_b=m_gmlp_b, pool_w=m_pool_w, pool_scale=m_pool_scale,
                   ffn2_norm=m_ffn2_norm, final_norm=m_final_norm)
    small_v = dict(ffn1_norm=v_ffn1_norm, mix_norm=v_mix_norm, attn_sinks=v_attn_sinks, gmlp_v_norm=v_gmlp_v_norm,
                   gmlp_w_s=v_gmlp_w_s, gmlp_b=v_gmlp_b, pool_w=v_pool_w, pool_scale=v_pool_scale,
                   ffn2_norm=v_ffn2_norm, final_norm=v_final_norm)
    sg, sd, sm, sv = _adamw_small(gathered[0], _pack_small([small_w[nm] for nm in SMALL]),
                                  _pack_small([small_m[nm] for nm in SMALL]),
                                  _pack_small([small_v[nm] for nm in SMALL]), "adamw_small")
    like = [small_w[nm] for nm in SMALL]
    sg_l, loss = _unpack_small(sg, like)
    sd_l, _ = _unpack_small(sd, like)
    sm_l, _ = _unpack_small(sm, like)
    sv_l, _ = _unpack_small(sv, like)
    deltas, new_m, new_v = {}, {}, {}
    for i, nm in enumerate(SMALL):
        grads[nm], deltas[nm], new_m[nm], new_v[nm] = sg_l[i], sd_l[i], sm_l[i], sv_l[i]

    big_w = dict(ffn1_w_gate=ffn1_w_gate, ffn1_w_up=ffn1_w_up, ffn1_w_down=ffn1_w_down, w_in=w_in, w_out=w_out,
                 ffn2_w_gate=ffn2_w_gate, ffn2_w_up=ffn2_w_up, ffn2_w_down=ffn2_w_down)
    big_m = dict(ffn1_w_gate=m_ffn1_w_gate, ffn1_w_up=m_ffn1_w_up, ffn1_w_down=m_ffn1_w_down, w_in=m_w_in,
                 w_out=m_w_out, ffn2_w_gate=m_ffn2_w_gate, ffn2_w_up=m_ffn2_w_up, ffn2_w_down=m_ffn2_w_down)
    big_v = dict(ffn1_w_gate=v_ffn1_w_gate, ffn1_w_up=v_ffn1_w_up, ffn1_w_down=v_ffn1_w_down, w_in=v_w_in,
                 w_out=v_w_out, ffn2_w_gate=v_ffn2_w_gate, ffn2_w_up=v_ffn2_w_up, ffn2_w_down=v_ffn2_w_down)
    for nm in big_w:
        view = t if nm in transposed else (lambda a: a)
        res = _adamw(view(big_w[nm]), red_rows[nm], view(big_m[nm]), view(big_v[nm]), f"adamw_{nm}")
        grads[nm] = view(red_rows[nm])
        deltas[nm], new_m[nm], new_v[nm] = [view(r) for r in res]

    order = ["ffn1_norm", "ffn1_w_gate", "ffn1_w_up", "ffn1_w_down", "mix_norm", "w_in", "attn_sinks", "gmlp_v_norm",
             "gmlp_w_s", "gmlp_b", "pool_w", "pool_scale", "w_out", "ffn2_norm", "ffn2_w_gate", "ffn2_w_up",
             "ffn2_w_down", "final_norm"]
    return (loss, grad_x, *[grads[n] for n in order], *[deltas[n] for n in order],
            *[new_m[n] for n in order], *[new_v[n] for n in order])
```

```python
import functools
import math

import jax
import jax.numpy as jnp
from jax import lax
from jax.experimental import pallas as pl
from jax.experimental.pallas import tpu as pltpu

F32 = jnp.float32
BF16 = jnp.bfloat16
MESH = pl.DeviceIdType.MESH

D = 1024
FF = 2816
INW = 1536
N_DEV = 8
DEPTH = 2
BLK = 128
HD = 64
N_HEADS = 8
N_KV = 2
REP = 4
ATTN_SCALE = HD ** -0.5
POOL_WINDOWS = (2, 4, 8, 16)
EPS = 1e-6
NEG = -1e30
FC = 256
GELU_C0 = math.sqrt(2.0 / math.pi)
GELU_C1 = 0.044715

ADAM_LR = 0.001
ADAM_B1 = 0.9
ADAM_B2 = 0.999
ADAM_EPS = 1e-08
ADAM_WD = 0.01
ADAM_STEP = 10

VMEM_LIMIT = 60 * 1024 * 1024

O_K, O_V, O_U, O_G, O_P = 512, 640, 768, 1024, 1280


def _call(body, **kw):
    return pl.pallas_call(body, **kw)


def _params(sem=None, vmem=VMEM_LIMIT):
    return pltpu.CompilerParams(dimension_semantics=sem, vmem_limit_bytes=vmem)


def _host(comm, body, *, name, grid, in_specs, out_specs, out_shape, args, scratch_shapes=(), aliases=None):
    single = not isinstance(out_shape, (list, tuple))
    out_specs_l = [out_specs] if single else list(out_specs)
    out_shape_l = [out_shape] if single else list(out_shape)
    n_in, n_out, n_scr = len(in_specs), len(out_shape_l), len(scratch_shapes)
    steps = grid[0]
    any_spec = pl.BlockSpec(memory_space=pl.ANY)

    def wrapped(*refs):
        pos = 0

        def take(n):
            nonlocal pos
            part = refs[pos:pos + n]
            pos += n
            return part

        ins = take(n_in)
        cins = [take(len(t.inputs)) for t in comm]
        outs = take(n_out)
        couts = [take(len(t.out_shape)) for t in comm]
        scr = take(n_scr)
        cscr = [take(len(t.scratch)) for t in comm]
        i = pl.program_id(0)
        for k, t in enumerate(comm):
            pl.when(i == 0)(functools.partial(t.start, cins[k], couts[k], cscr[k]))
        body(*ins, *outs, *scr)
        for k, t in enumerate(comm):
            pl.when(i == (3 * steps) // 4)(functools.partial(t.mid, cins[k], couts[k], cscr[k]))
            pl.when(i == steps - 1)(functools.partial(t.finish, cins[k], couts[k], cscr[k]))

    c_args = [a for t in comm for a in t.inputs]
    c_shapes = [sh for t in comm for sh in t.out_shape]
    c_scr = [sc for t in comm for sc in t.scratch]
    res = _call(
        wrapped, name=name, grid=grid,
        in_specs=list(in_specs) + [any_spec] * len(c_args),
        out_specs=out_specs_l + [any_spec] * len(c_shapes),
        out_shape=out_shape_l + c_shapes,
        scratch_shapes=list(scratch_shapes) + c_scr,
        input_output_aliases=aliases or {},
        compiler_params=_params(("arbitrary",)),
    )(*args, *c_args)
    outs = res[0] if single else list(res[:n_out])
    if not comm:
        return outs
    c_outs, pos = [], n_out
    for t in comm:
        c_outs.append(list(res[pos:pos + len(t.out_shape)]))
        pos += len(t.out_shape)
    return outs, c_outs


def _nn(a, b):
    return lax.dot_general(a, b, (((1,), (0,)), ((), ())), preferred_element_type=F32)


def _nt(a, b):
    return lax.dot_general(a, b, (((1,), (1,)), ((), ())), preferred_element_type=F32)


def _tn(a, b):
    return lax.dot_general(a, b, (((0,), (0,)), ((), ())), preferred_element_type=F32)


def _gelu(x):
    x2 = x * x
    t = jnp.tanh(x * (GELU_C0 + (GELU_C0 * GELU_C1) * x2))
    hx = 0.5 * x
    return hx + hx * t, (hx, x2, t)


def _gelu_grad(parts):
    hx, x2, t = parts
    return (0.5 + 0.5 * t) + (hx * (1.0 - t * t)) * (GELU_C0 + (3.0 * GELU_C0 * GELU_C1) * x2)


def _rms(x):
    r = lax.rsqrt(jnp.mean(x * x, axis=-1, keepdims=True) + EPS)
    return x * r, r


def _rms_bwd(dy, xh, r, g):
    dg = jnp.sum(dy * xh, axis=0, keepdims=True)
    dxh = dy * g
    dx = r * (dxh - xh * jnp.mean(dxh * xh, axis=-1, keepdims=True))
    return dx, dg


def _wspec(rows, m):
    return pl.BlockSpec((None, rows, D), lambda i, m=m: (m, 0, 0), pipeline_mode=pl.Buffered(1))


def _rowspec(tm, cols):
    return pl.BlockSpec((tm, cols), lambda i: (i, 0))


def _fixspec(rows, cols):
    return pl.BlockSpec((rows, cols), lambda i: (0, 0))


def _ffn_fwd(x, gain, w352, mg, name, comm=(), mixer=None, loss=None):
    s = x.shape[0]
    tm = min(512, s)
    n_in = 5 + (2 if mixer is not None else 0) + (2 if loss is not None else 0)

    def body(*refs):
        x_ref, g_ref, wg_ref, wu_ref, wd_ref = refs[:5]
        more_in, outs = list(refs[5:n_in]), list(refs[n_in:])
        xo_ref, p1_ref, p2_ref, hid_ref = outs[:4]
        more_out = outs[4:]
        xt = x_ref[...]
        if mixer is not None:
            y_ref, wo_ref = more_in[:2]
            xt = xt + _nn(y_ref[...], wo_ref[...])
            more_out.pop(0)[...] = xt
        xh, _ = _rms(xt)
        h = (xh * g_ref[...]).astype(BF16)
        for c in range(FF // FC):
            sl = slice(c * FC, (c + 1) * FC)
            a = _nt(h, wg_ref[sl, :])
            b = _nt(h, wu_ref[sl, :])
            sig = 0.5 * jnp.tanh(0.5 * a) + 0.5
            sa = a * sig
            p1_ref[:, sl] = (b * (sig + sa * (1.0 - sig))).astype(BF16)
            p2_ref[:, sl] = sa.astype(BF16)
            hid_ref[:, sl] = (sa * b).astype(BF16)
        xo = xt + 0.5 * _nn(hid_ref[...], wd_ref[...])
        if loss is None:
            xo_ref[...] = xo
        else:
            gf_ref, t_ref = more_in[-2:]
            loss_ref, dgf_ref = more_out
            gf = gf_ref[...]
            xh, r = _rms(xo)
            err = xh * gf - t_ref[...]
            lp = 0.5 * jnp.sum(jnp.mean(err * err, axis=-1, keepdims=True), axis=0, keepdims=True)
            xo_ref[...], dgf = _rms_bwd(err * (1.0 / D), xh, r, gf)

            @pl.when(pl.program_id(0) == 0)
            def _():
                dgf_ref[...] = jnp.zeros_like(dgf_ref)
                loss_ref[...] = jnp.zeros_like(loss_ref)

            dgf_ref[0:1, :] += dgf
            loss_ref[0:1, :] += lp + jnp.zeros((1, 128), F32)

    act = jax.ShapeDtypeStruct((s, FF), BF16)
    tok = jax.ShapeDtypeStruct((s, D), F32)
    in_specs = [_rowspec(tm, D), _fixspec(1, D), _wspec(FF, mg), _wspec(FF, mg + 1), _wspec(FF, mg + 2)]
    out_specs = [_rowspec(tm, D), _rowspec(tm, FF), _rowspec(tm, FF), _rowspec(tm, FF)]
    out_shape = [tok, act, act, act]
    args = (x, gain, w352, w352, w352)
    if mixer is not None:
        in_specs += [_rowspec(tm, D), _wspec(D, 0)]
        out_specs += [_rowspec(tm, D)]
        out_shape += [tok]
        args += tuple(mixer)
    if loss is not None:
        in_specs += [_fixspec(1, D), _rowspec(tm, D)]
        out_specs += [_fixspec(8, 128), _fixspec(8, D)]
        out_shape += [jax.ShapeDtypeStruct((8, 128), F32), jax.ShapeDtypeStruct((8, D), F32)]
        args += tuple(loss)
    return _host(comm, body, name=name, grid=(s // tm,), in_specs=in_specs, out_specs=out_specs,
                 out_shape=out_shape, args=args)


def _ffn_bwd(x, gain, dy, p1, p2, w352, mg, name, comm=(), w_out=None, mixin=None):
    s = x.shape[0]
    tm = min(256, s)
    n_in = 8 + (1 if w_out is not None else 0) + (4 if mixin is not None else 0)

    def body(*refs):
        x_ref, g_ref, dy_ref, p1_ref, p2_ref, wg_ref, wu_ref, wd_ref = refs[:8]
        more_in, outs = list(refs[8:n_in]), list(refs[n_in:])
        dx_ref, dab_ref, h_ref, dyb_ref, dg_ref = outs[:5]
        more_out = outs[5:]
        i = pl.program_id(0)
        xt = x_ref[...]
        g = g_ref[...]
        xh, r = _rms(xt)
        h_ref[...] = (xh * g).astype(BF16)
        dyt = dy_ref[...]
        if mixin is not None:
            gm_ref, dz_ref, win_ref, x1_ref = more_in[-4:]
            dgm_ref = more_out[-1]
            xh1, r1 = _rms(x1_ref[...])
            dxm, dgm = _rms_bwd(_nn(dz_ref[...], win_ref[...]), xh1, r1, gm_ref[...])
            dyt = dyt + dxm

            @pl.when(i == 0)
            def _():
                dgm_ref[...] = jnp.zeros_like(dgm_ref)

            dgm_ref[0:1, :] += dgm
        dyb = (0.5 * dyt).astype(BF16)
        dyb_ref[...] = dyb
        for c in range(FF // FC):
            sl = slice(c * FC, (c + 1) * FC)
            dhid = _nt(dyb, wd_ref[sl, :])
            dab_ref[:, sl] = (dhid * p1_ref[:, sl].astype(F32)).astype(BF16)
            dab_ref[:, FF + c * FC:FF + (c + 1) * FC] = (dhid * p2_ref[:, sl].astype(F32)).astype(BF16)
        dh = _nn(dab_ref[:, :FF], wg_ref[...]) + _nn(dab_ref[:, FF:], wu_ref[...])
        dxn, dg = _rms_bwd(dh, xh, r, g)
        dx = dyt + dxn
        dx_ref[...] = dx
        if w_out is not None:
            dym_ref, dxb_ref = more_out[:2]
            dxb = dx.astype(BF16)
            dxb_ref[...] = dxb
            dym_ref[...] = _nt(dxb, more_in[0][...])

        @pl.when(i == 0)
        def _():
            dg_ref[...] = jnp.zeros_like(dg_ref)

        dg_ref[0:1, :] += dg

    tok = jax.ShapeDtypeStruct((s, D), BF16)
    tok32 = jax.ShapeDtypeStruct((s, D), F32)
    gain_grad = jax.ShapeDtypeStruct((8, D), F32)
    in_specs = [_rowspec(tm, D), _fixspec(1, D), _rowspec(tm, D), _rowspec(tm, FF), _rowspec(tm, FF),
                _wspec(FF, mg), _wspec(FF, mg + 1), _wspec(FF, mg + 2)]
    out_specs = [_rowspec(tm, D), _rowspec(tm, 2 * FF), _rowspec(tm, D), _rowspec(tm, D), _fixspec(8, D)]
    out_shape = [tok32, jax.ShapeDtypeStruct((s, 2 * FF), BF16), tok, tok, gain_grad]
    args = (x, gain, dy, p1, p2, w352, w352, w352)
    if w_out is not None:
        in_specs += [_wspec(D, 0)]
        out_specs += [_rowspec(tm, D), _rowspec(tm, D)]
        out_shape += [tok32, tok]
        args += (w_out,)
    if mixin is not None:
        in_specs += [_fixspec(1, D), _rowspec(tm, INW), _wspec(INW, 0), _rowspec(tm, D)]
        out_specs += [_fixspec(8, D)]
        out_shape += [gain_grad]
        args += tuple(mixin)
    return _host(comm, body, name=name, grid=(s // tm,), in_specs=in_specs, out_specs=out_specs,
                 out_shape=out_shape, args=args)


def _wgrad(a, b, g, n_slabs, m, name, comm=(), col0=0, cols=None, slab_rows=None):
    s = a.shape[0]
    cols = a.shape[1] if cols is None else cols
    slab_rows = cols if slab_rows is None else slab_rows
    mb = 256
    per_slab = slab_rows // mb

    def body(*refs):
        refs[-1][...] = _tn(refs[0][...], refs[1][...])

    in_specs = [pl.BlockSpec((s, mb), lambda i: (0, col0 // mb + i)),
                pl.BlockSpec((s, D), lambda i: (0, 0), pipeline_mode=pl.Buffered(1))]
    args = [a, b]
    aliases = {}
    if g is not None:
        in_specs.append(pl.BlockSpec(memory_space=pl.ANY))
        args.append(g)
        aliases = {2: 0}
    return _host(
        comm, body, name=name, grid=(cols // mb,),
        in_specs=in_specs,
        out_specs=pl.BlockSpec((None, mb, D), lambda i: (m + i // per_slab, i % per_slab, 0)),
        out_shape=jax.ShapeDtypeStruct((n_slabs, slab_rows, D), F32),
        aliases=aliases, args=args)


def _mixin_fwd(x, gain, w192, l, name):
    s = x.shape[0]
    tm = min(512, s)

    def body(x_ref, g_ref, w_ref, z_ref, h_ref):
        xh, _ = _rms(x_ref[...])
        h = (xh * g_ref[...]).astype(BF16)
        h_ref[...] = h
        z_ref[...] = _nt(h, w_ref[...])

    return _call(
        body, name=name, grid=(s // tm,),
        in_specs=[_rowspec(tm, D), _fixspec(1, D), _wspec(INW, l)],
        out_specs=[_rowspec(tm, INW), _rowspec(tm, D)],
        out_shape=[jax.ShapeDtypeStruct((s, INW), F32), jax.ShapeDtypeStruct((s, D), BF16)],
        compiler_params=_params(("arbitrary",)),
    )(x, gain, w192)


MIX_NB = 4
TILE = MIX_NB * BLK
GROUP_ROWS = REP * BLK


class _Block:
    def __init__(self, n, j, zc_ref, zkvp_ref, zpp_ref):
        self.zc, self.zkvp, self.zpp = zc_ref, zkvp_ref, zpp_ref
        self.first = j == 0
        self.r = slice(j * BLK, (j + 1) * BLK)
        self.rp = slice((j - 1) * BLK, j * BLK)
        self.index = n * MIX_NB + j
        self.lo = jnp.where(n > 0, 0, BLK) if self.first else 0
        self.has_prev = jnp.where(n > 0, 1.0, 0.0) if self.first else 1.0

    def cols(self, c0, c1):
        return self.zc[self.r, c0:c1]

    def prev_kv(self, c0, c1):
        return self.zkvp[:, c0:c1] if self.first else self.zc[self.rp, O_K + c0:O_K + c1]

    def prev_p(self):
        return self.zpp[...] * self.has_prev if self.first else self.zc[self.rp, O_P:INW]


def _band_mask(rows):
    row = lax.broadcasted_iota(jnp.int32, (rows, 2 * BLK), 0) & (BLK - 1)
    col = lax.broadcasted_iota(jnp.int32, (rows, 2 * BLK), 1)
    return (col > row) & (col <= row + BLK)


def _block_mask(band, blk):
    if not blk.first:
        return band
    return band & (lax.broadcasted_iota(jnp.int32, band.shape, 1) >= blk.lo)


def _lane_head(shape):
    return lax.broadcasted_iota(jnp.int32, shape, 1) // HD


def _lane_group_select(vals):
    grp = _lane_head(vals[0].shape)
    return jnp.where(grp == 0, vals[0], jnp.where(grp == 1, vals[1], jnp.where(grp == 2, vals[2], vals[3])))


def _pool_count(index):
    row = lax.broadcasted_iota(jnp.int32, (BLK, 256), 0)
    pos1 = (index * BLK + row + 1).astype(F32)
    wl = _lane_group_select([jnp.full((BLK, 256), float(w), F32) for w in POOL_WINDOWS])
    return jnp.minimum(pos1, wl)


def _window_sums(e, forward):
    tot = e.shape[0]
    lv = e
    out = []
    for sh in (1, 2, 4, 8):
        lv = lv + pltpu.roll(lv, sh if forward else tot - sh, 0)
        out.append(lv)
    return _lane_group_select(out)


def _stack_heads(get, g):
    return jnp.concatenate([get((g * REP + rr) * HD, (g * REP + rr + 1) * HD) for rr in range(REP)], axis=0)


def _sink_column(sink_ref, g):
    return jnp.concatenate([jnp.full((BLK, 1), sink_ref[g * REP + rr], F32) for rr in range(REP)], axis=0)


def _kv_window(blk, g):
    kk = jnp.concatenate([blk.prev_kv(g * HD, (g + 1) * HD),
                          blk.cols(O_K + g * HD, O_K + (g + 1) * HD)], axis=0).astype(BF16)
    vv = jnp.concatenate([blk.prev_kv(BLK + g * HD, BLK + (g + 1) * HD),
                          blk.cols(O_V + g * HD, O_V + (g + 1) * HD)], axis=0).astype(BF16)
    return kk, vv


def _mix_common(blk, vn_ref, wcat_ref, bexp_ref, pwbd_ref):
    u, tu = _gelu(blk.cols(O_U, O_G))
    gv, tv = _gelu(blk.cols(O_G, O_P))
    xh, rv = _rms(gv)
    vnb = (xh * vn_ref[...]).astype(BF16)
    head = _lane_head((BLK, 256))
    vn_bd = jnp.concatenate([jnp.where(head == h, vnb, jnp.zeros_like(vnb)) for h in range(4)], axis=0)
    row = lax.broadcasted_iota(jnp.int32, (BLK, 4 * BLK), 0)
    col = lax.broadcasted_iota(jnp.int32, (BLK, 4 * BLK), 1) & (BLK - 1)
    tril = col <= row
    wcat = jnp.where(tril, wcat_ref[...], 0.0).astype(BF16)
    f = _nn(wcat, vn_bd) + bexp_ref[...]
    p = blk.cols(O_P, INW)
    e = jnp.concatenate([blk.prev_p(), p], axis=0)
    cnt = _pool_count(blk.index)
    diff = (_window_sums(e, True)[BLK:, :] / cnt - p).astype(BF16)
    pwbd = pwbd_ref[...].astype(BF16)
    pout = _nn(diff, pwbd)
    return dict(u=u, tu=tu, tv=tv, xh=xh, rv=rv, vn_bd=vn_bd, wcat=wcat, f=f, cnt=cnt, diff=diff, pwbd=pwbd,
                pout=pout, tril=tril, head=head)


def _mix_fwd(z, sinks, vnorm, wcat, bexp, pwbd, pscale, name):
    s = z.shape[0]
    nt = s // TILE

    def body(sink_ref, zc_ref, zkvp_ref, zpp_ref, vn_ref, wcat_ref, bexp_ref, pwbd_ref, ps_ref, y_ref, lse_ref):
        n = pl.program_id(0)
        lse_ref[...] = jnp.zeros_like(lse_ref)
        band = _band_mask(BLK)
        for j in range(MIX_NB):
            blk = _Block(n, j, zc_ref, zkvp_ref, zpp_ref)
            valid = _block_mask(band, blk)
            for g in range(N_KV):
                kk, vv = _kv_window(blk, g)
                for rr in range(REP):
                    h = g * REP + rr
                    qh = (blk.cols(h * HD, (h + 1) * HD) * ATTN_SCALE).astype(BF16)
                    sc = jnp.where(valid, _nt(qh, kk), NEG)
                    sink = sink_ref[h]
                    m = jnp.maximum(jnp.max(sc, axis=-1, keepdims=True), sink)
                    ex = jnp.exp(sc - m)
                    den = jnp.sum(ex, axis=-1, keepdims=True) + jnp.exp(sink - m)
                    y_ref[blk.r, h * HD:(h + 1) * HD] = _nn((ex / den).astype(BF16), vv).astype(BF16)
                    lse_ref[blk.r, h:h + 1] = m + jnp.log(den)
            c = _mix_common(blk, vn_ref, wcat_ref, bexp_ref, pwbd_ref)
            y_ref[blk.r, 512:768] = (c["u"] * c["f"]).astype(BF16)
            y_ref[blk.r, 768:1024] = (c["pout"] * ps_ref[...]).astype(BF16)

    halo = lambda n: jnp.maximum(MIX_NB * n - 1, 0)
    return _call(
        body, name=name, grid=(nt,),
        in_specs=[pl.BlockSpec(memory_space=pltpu.SMEM),
                  pl.BlockSpec((TILE, INW), lambda n: (n, 0)),
                  pl.BlockSpec((BLK, 256), lambda n: (halo(n), 2)),
                  pl.BlockSpec((BLK, 256), lambda n: (halo(n), 5)),
                  _fixspec(1, 256), _fixspec(BLK, 4 * BLK), _fixspec(BLK, 256), _fixspec(256, 256), _fixspec(1, 256)],
        out_specs=[pl.BlockSpec((TILE, D), lambda n: (n, 0)), pl.BlockSpec((TILE, 128), lambda n: (n, 0))],
        out_shape=[jax.ShapeDtypeStruct((s, D), BF16), jax.ShapeDtypeStruct((s, 128), F32)],
        compiler_params=_params(("arbitrary",)),
    )(sinks, z, z, z, vnorm, wcat, bexp, pwbd, pscale)


def _mix_bwd(z, dy, lse, sinks, vnorm, wcat, bexp, pwbd, pscale, name, comm=()):
    s = z.shape[0]
    nt = s // TILE
    last = slice(TILE - BLK, TILE)

    def body(sink_ref, zc_ref, zkvp_ref, zpp_ref, dy_ref, lse_ref, vn_ref, wcat_ref, bexp_ref, pwbd_ref, ps_ref,
             dz_ref, dsink_ref, dvn_ref, dws_ref, dbt_ref, dpw_ref, dps_ref, carry_ref, ddc_ref):
        n = pl.program_id(0)

        @pl.when(n == 0)
        def _():
            carry_ref[...] = jnp.zeros_like(carry_ref)
            ddc_ref[...] = jnp.zeros_like(ddc_ref)
            dsink_ref[...] = jnp.zeros_like(dsink_ref)
            dvn_ref[...] = jnp.zeros_like(dvn_ref)
            dws_ref[...] = jnp.zeros_like(dws_ref)
            dbt_ref[...] = jnp.zeros_like(dbt_ref)
            dpw_ref[...] = jnp.zeros_like(dpw_ref)
            dps_ref[...] = jnp.zeros_like(dps_ref)

        def block_grads(j):
            blk = _Block(n, j, zc_ref, zkvp_ref, zpp_ref)
            valid = _block_mask(_band_mask(GROUP_ROWS), blk)
            out = dict(dq=[], dsink=[], dbt=[])
            dk_prev, dk_cur, dv_prev, dv_cur = [], [], [], []
            for g in range(N_KV):
                kk, vv = _kv_window(blk, g)
                q4 = _stack_heads(blk.cols, g).astype(BF16)
                do4 = _stack_heads(lambda c0, c1: dy_ref[blk.r, c0:c1], g).astype(BF16)
                lse4 = jnp.concatenate([lse_ref[blk.r, g * REP + rr:g * REP + rr + 1] for rr in range(REP)], axis=0)
                sc = jnp.where(valid, _nt(q4, kk) * ATTN_SCALE, NEG)
                pr = jnp.exp(sc - lse4)
                dp = _nt(do4, vv)
                delta = jnp.sum(pr * dp, axis=-1, keepdims=True)
                ds = ((pr * (dp - delta)) * ATTN_SCALE).astype(BF16)
                sunk = jnp.exp(_sink_column(sink_ref, g) - lse4) * delta
                dq4 = _nn(ds, kk)
                for rr in range(REP):
                    out["dsink"].append(-jnp.sum(sunk[rr * BLK:(rr + 1) * BLK], axis=0, keepdims=True))
                    out["dq"].append(dq4[rr * BLK:(rr + 1) * BLK])
                dkk = _tn(ds, q4)
                dvv = _tn(pr.astype(BF16), do4)
                dk_prev.append(dkk[:BLK]); dk_cur.append(dkk[BLK:])
                dv_prev.append(dvv[:BLK]); dv_cur.append(dvv[BLK:])
            out["dk_prev"], out["dk_cur"] = jnp.concatenate(dk_prev, axis=1), jnp.concatenate(dk_cur, axis=1)
            out["dv_prev"], out["dv_cur"] = jnp.concatenate(dv_prev, axis=1), jnp.concatenate(dv_cur, axis=1)
            c = _mix_common(blk, vn_ref, wcat_ref, bexp_ref, pwbd_ref)
            dyg = dy_ref[blk.r, 512:768]
            du = dyg * c["f"]
            df = dyg * c["u"]
            out["dzu"] = du * _gelu_grad(c["tu"])
            dfb = df.astype(BF16)
            for h in range(4):
                out["dbt"].append(jnp.sum(df[:, h * HD:(h + 1) * HD], axis=1, keepdims=True))
            out["dws"] = jnp.where(c["tril"], _nt(dfb, c["vn_bd"]), 0.0)
            dvn_bd = _tn(c["wcat"], dfb)
            dvn = functools.reduce(lambda a, b: a + b, [
                jnp.where(c["head"] == h, dvn_bd[h * BLK:(h + 1) * BLK], 0.0) for h in range(4)])
            dgv, out["dvn"] = _rms_bwd(dvn, c["xh"], c["rv"], vn_ref[...])
            out["dzv"] = dgv * _gelu_grad(c["tv"])
            dyp = dy_ref[blk.r, 768:1024]
            out["dps"] = jnp.sum(dyp * c["pout"], axis=0, keepdims=True)
            dout = (dyp * ps_ref[...]).astype(BF16)
            out["dpw"] = _tn(c["diff"], dout)
            out["ddiff"] = _nt(dout, c["pwbd"])
            out["dd"] = out["ddiff"] / c["cnt"]
            return out

        def write_previous_tile(dd_next, dk_next, dv_next):
            if MIX_NB > 1:
                dz_ref[0:TILE - BLK, :] = carry_ref[0:TILE - BLK, :].astype(BF16)
            rs = _window_sums(jnp.concatenate([ddc_ref[...], dd_next], axis=0), False)
            dz_ref[last, 0:O_K] = carry_ref[last, 0:O_K].astype(BF16)
            dz_ref[last, O_K:O_V] = (carry_ref[last, O_K:O_V] + dk_next).astype(BF16)
            dz_ref[last, O_V:O_U] = (carry_ref[last, O_V:O_U] + dv_next).astype(BF16)
            dz_ref[last, O_U:O_P] = carry_ref[last, O_U:O_P].astype(BF16)
            dz_ref[last, O_P:INW] = (carry_ref[last, O_P:INW] + rs[:BLK, :]).astype(BF16)

        @pl.when(n < nt)
        def _():
            parts = [block_grads(j) for j in range(MIX_NB)]
            total = lambda key, i=None: functools.reduce(
                lambda a, b: a + b, [p[key] if i is None else p[key][i] for p in parts])
            for h in range(N_HEADS):
                dsink_ref[h:h + 1, :] += total("dsink", h) + jnp.zeros((1, 128), F32)
            for h in range(4):
                dbt_ref[:, h:h + 1] += total("dbt", h)
            dws_ref[...] += total("dws")
            dpw_ref[...] += total("dpw")
            dvn_ref[0:1, :] += total("dvn")
            dps_ref[0:1, :] += total("dps")
            write_previous_tile(parts[0]["dd"], parts[0]["dk_prev"], parts[0]["dv_prev"])
            for j, p in enumerate(parts):
                r = slice(j * BLK, (j + 1) * BLK)
                nxt = parts[j + 1] if j + 1 < MIX_NB else None
                for h in range(N_HEADS):
                    carry_ref[r, h * HD:(h + 1) * HD] = p["dq"][h]
                carry_ref[r, O_U:O_G] = p["dzu"]
                carry_ref[r, O_G:O_P] = p["dzv"]
                if nxt is None:
                    carry_ref[r, O_K:O_V] = p["dk_cur"]
                    carry_ref[r, O_V:O_U] = p["dv_cur"]
                    carry_ref[r, O_P:INW] = -p["ddiff"]
                    ddc_ref[...] = p["dd"]
                else:
                    rs = _window_sums(jnp.concatenate([p["dd"], nxt["dd"]], axis=0), False)
                    carry_ref[r, O_K:O_V] = p["dk_cur"] + nxt["dk_prev"]
                    carry_ref[r, O_V:O_U] = p["dv_cur"] + nxt["dv_prev"]
                    carry_ref[r, O_P:INW] = rs[:BLK, :] - p["ddiff"]

        @pl.when(n == nt)
        def _():
            none = jnp.zeros((BLK, BLK), F32)
            write_previous_tile(jnp.zeros((BLK, 256), F32), none, none)

    cur = lambda n: jnp.minimum(n, nt - 1)
    done = lambda n: jnp.maximum(n - 1, 0)
    halo = lambda n: jnp.maximum(MIX_NB * jnp.minimum(n, nt - 1) - 1, 0)
    return _host(
        comm, body, name=name, grid=(nt + 1,),
        in_specs=[pl.BlockSpec(memory_space=pltpu.SMEM),
                  pl.BlockSpec((TILE, INW), lambda n: (cur(n), 0)),
                  pl.BlockSpec((BLK, 256), lambda n: (halo(n), 2)),
                  pl.BlockSpec((BLK, 256), lambda n: (halo(n), 5)),
                  pl.BlockSpec((TILE, D), lambda n: (cur(n), 0)),
                  pl.BlockSpec((TILE, 128), lambda n: (cur(n), 0)),
                  _fixspec(1, 256), _fixspec(BLK, 4 * BLK), _fixspec(BLK, 256), _fixspec(256, 256), _fixspec(1, 256)],
        out_specs=[pl.BlockSpec((TILE, INW), lambda n: (done(n), 0)),
                   _fixspec(8, 128), _fixspec(8, 256), _fixspec(BLK, 4 * BLK), _fixspec(BLK, 128),
                   _fixspec(256, 256), _fixspec(8, 256)],
        out_shape=[jax.ShapeDtypeStruct((s, INW), BF16), jax.ShapeDtypeStruct((8, 128), F32),
                   jax.ShapeDtypeStruct((8, 256), F32), jax.ShapeDtypeStruct((BLK, 4 * BLK), F32),
                   jax.ShapeDtypeStruct((BLK, 128), F32), jax.ShapeDtypeStruct((256, 256), F32),
                   jax.ShapeDtypeStruct((8, 256), F32)],
        scratch_shapes=[pltpu.VMEM((TILE, INW), F32), pltpu.VMEM((BLK, 256), F32)],
        args=(sinks, z, z, z, dy, lse, vnorm, wcat, bexp, pwbd, pscale))


def _position():
    x, y, c = lax.axis_index("x"), lax.axis_index("y"), lax.axis_index("c")
    return x, y, c


class _GatherTask:
    def __init__(self, srcs):
        self.inputs = list(srcs)
        ng = len(srcs)
        self.out_shape = [jax.ShapeDtypeStruct((a.shape[0], N_DEV) + a.shape[1:], a.dtype) for a in srcs]
        self.scratch = [pltpu.SemaphoreType.DMA((ng, 7)), pltpu.SemaphoreType.DMA((ng, 7)),
                        pltpu.SemaphoreType.DMA((ng,))]

    def _plan(self, src, dst, sems):
        send_sems, recv_sems, local_sems = sems
        ng = len(src)
        x, y, c = _position()
        me, sibling = (x, y, c), (x, y, 1 - c)
        chips = [(1 - x, y), (x, 1 - y), (1 - x, 1 - y)]

        def slot(pos):
            return 4 * pos[0] + 2 * pos[1] + pos[2]

        def copy(gi, k, block, to, from_src=False):
            rows = dst[gi].at[:, slot(block)]
            return pltpu.make_async_remote_copy(
                src_ref=src[gi] if from_src else rows, dst_ref=rows,
                send_sem=send_sems.at[gi, k], recv_sem=recv_sems.at[gi, k],
                device_id=to, device_id_type=MESH)

        make = functools.partial
        mine = [make(pltpu.make_async_copy, src[gi], dst[gi].at[:, slot(me)], local_sems.at[gi]) for gi in range(ng)]
        first = []
        for gi in range(ng):
            first.append(make(copy, gi, 0, me, sibling, True))
            first += [make(copy, gi, 1 + j, me, (*chip, c), True) for j, chip in enumerate(chips)]
        passed = [make(copy, gi, 4 + j, (*chip, c), sibling) for j, chip in enumerate(chips) for gi in range(ng)]
        arrive_ici = [make(copy, gi, 1 + j, (*chip, c), me) for j, chip in enumerate(chips) for gi in range(ng)]
        arrive_d2d = [make(copy, gi, 0, sibling, me) for gi in range(ng)]
        arrive_d2d += [make(copy, gi, 4 + j, (*chip, 1 - c), me) for j, chip in enumerate(chips) for gi in range(ng)]
        return mine, first, passed, arrive_ici, arrive_d2d

    def start(self, src, dst, sems):
        mine, first, _, _, _ = self._plan(src, dst, sems)
        for cp in mine + first:
            cp().start()

    def mid(self, src, dst, sems):
        _, _, passed, arrive_ici, _ = self._plan(src, dst, sems)
        for arrived, fw in zip(arrive_ici, passed):
            arrived().wait_recv()
            fw().start()

    def finish(self, src, dst, sems):
        mine, first, passed, _, arrive_d2d = self._plan(src, dst, sems)
        for cp in arrive_d2d:
            cp().wait_recv()
        for cp in first + passed:
            cp().wait_send()
        for cp in mine:
            cp().wait()


class _SiblingTask:
    def __init__(self, g5s):
        self.inputs = list(g5s)
        ng = len(g5s)
        self.out_shape = [jax.ShapeDtypeStruct((a.shape[0], 4) + a.shape[3:], a.dtype) for a in g5s]
        self.scratch = [pltpu.SemaphoreType.DMA((ng,)), pltpu.SemaphoreType.DMA((ng,))]

    def _plan(self, src, dst, sems):
        send_sems, recv_sems = sems
        x, y, c = _position()
        return [functools.partial(
            pltpu.make_async_remote_copy,
            src_ref=src[gi].at[:, :, 1 - c], dst_ref=dst[gi],
            send_sem=send_sems.at[gi], recv_sem=recv_sems.at[gi],
            device_id=(x, y, 1 - c), device_id_type=MESH) for gi in range(len(src))]

    def start(self, src, dst, sems):
        for cp in self._plan(src, dst, sems):
            cp().start()

    def mid(self, src, dst, sems):
        pass

    def finish(self, src, dst, sems):
        for cp in self._plan(src, dst, sems):
            cp().wait()


class _ChipTask(_SiblingTask):
    def __init__(self, sbs):
        self.inputs = list(sbs)
        ng = len(sbs)
        self.out_shape = [jax.ShapeDtypeStruct(a.shape, a.dtype) for a in sbs]
        self.scratch = [pltpu.SemaphoreType.DMA((ng, 3)), pltpu.SemaphoreType.DMA((ng, 3))]

    def _plan(self, src, dst, sems):
        send_sems, recv_sems = sems
        x, y, c = _position()
        jme = 2 * x + y
        chips = [(1 - x, y), (x, 1 - y), (1 - x, 1 - y)]
        return [functools.partial(
            pltpu.make_async_remote_copy,
            src_ref=src[gi].at[:, 2 * chip[0] + chip[1]], dst_ref=dst[gi].at[:, jme],
            send_sem=send_sems.at[gi, k], recv_sem=recv_sems.at[gi, k],
            device_id=(*chip, c), device_id_type=MESH) for k, chip in enumerate(chips) for gi in range(len(src))]


def _alone(task, name):
    n_in, n_out = len(task.inputs), len(task.out_shape)

    def body(*refs):
        parts = (refs[:n_in], refs[n_in:n_in + n_out], refs[n_in + n_out:])
        task.start(*parts)
        task.mid(*parts)
        task.finish(*parts)

    any_spec = pl.BlockSpec(memory_space=pl.ANY)
    return _call(body, name=name, in_specs=[any_spec] * n_in, out_specs=[any_spec] * n_out,
                 out_shape=task.out_shape, scratch_shapes=task.scratch)(*task.inputs)


def _core_sum(ids, g5, r1, name):
    n, _, _, rows, _ = g5.shape

    def body(ids_ref, g_ref, r_ref, sb_ref, own_ref):
        j = pl.program_id(2)
        t = g_ref[...] + r_ref[...]
        sb_ref[...] = t.astype(BF16)

        @pl.when(j == ids_ref[1])
        def _():
            own_ref[...] = t

    grid_spec = pltpu.PrefetchScalarGridSpec(
        num_scalar_prefetch=1, grid=(n, 1, 4),
        in_specs=[pl.BlockSpec((None, None, None, rows, D), lambda i, t, j, ids: (i, j, ids[0], t, 0)),
                  pl.BlockSpec((None, None, rows, D), lambda i, t, j, ids: (i, j, t, 0))],
        out_specs=[pl.BlockSpec((None, None, rows, D), lambda i, t, j, ids: (i, j, t, 0)),
                   pl.BlockSpec((None, rows, D), lambda i, t, j, ids: (i, t, 0))])
    return _call(
        body, name=name, grid_spec=grid_spec,
        out_shape=[jax.ShapeDtypeStruct((n, 4, rows, D), BF16), jax.ShapeDtypeStruct((n, rows, D), F32)],
        compiler_params=_params(("arbitrary", "arbitrary", "arbitrary")),
    )(ids, g5, r1)


def _chip_sum(others, own, r2, name):
    n, rows, _ = own.shape

    def body(oth_ref, own_ref, r0_ref, r1_ref, r2_ref, out_ref):
        out_ref[...] = ((own_ref[...] + r0_ref[...].astype(F32)) + r1_ref[...].astype(F32)) \
            + r2_ref[...].astype(F32)

    def rspec(k):
        return pl.BlockSpec((None, None, rows, D), lambda i, oth, k=k: (i, oth[k], 0, 0))

    grid_spec = pltpu.PrefetchScalarGridSpec(
        num_scalar_prefetch=1, grid=(n,),
        in_specs=[pl.BlockSpec((None, rows, D), lambda i, oth: (i, 0, 0)), rspec(0), rspec(1), rspec(2)],
        out_specs=pl.BlockSpec((None, rows, D), lambda i, oth: (i, 0, 0)))
    return _call(
        body, name=name, grid_spec=grid_spec,
        out_shape=jax.ShapeDtypeStruct((n, rows, D), F32),
        compiler_params=_params(("arbitrary",)),
    )(others, own, r2, r2, r2)


def _adam_math(w, g, m, v):
    m = ADAM_B1 * m + (1.0 - ADAM_B1) * g
    v = ADAM_B2 * v + (1.0 - ADAM_B2) * (g * g)
    m_hat = m / (1.0 - ADAM_B1 ** ADAM_STEP)
    v_hat = v / (1.0 - ADAM_B2 ** ADAM_STEP)
    delta = -ADAM_LR * (m_hat / (jnp.sqrt(v_hat) + ADAM_EPS) + ADAM_WD * w)
    return delta, m, v


def _adamw(w, m, v, gparts, name):
    _, r, c = w.shape

    def body(w_ref, m_ref, v_ref, *rest):
        g_refs, (g_ref, d_ref, mo_ref, vo_ref) = rest[:DEPTH], rest[DEPTH:]

        def run(src_ref):
            g = src_ref[...]
            g_ref[...] = g
            d_ref[...], mo_ref[...], vo_ref[...] = _adam_math(w_ref[...], g, m_ref[...], v_ref[...])

        for l in range(DEPTH):
            pl.when(pl.program_id(0) == l)(functools.partial(run, g_refs[l]))

    spec = pl.BlockSpec((None, r, c), lambda l: (l, 0, 0))
    return _call(
        body, name=name, grid=(DEPTH,),
        in_specs=[spec] * 3 + [pl.BlockSpec((None, r, c), lambda l, k=k: (k, 0, 0)) for _, k in gparts],
        out_specs=[spec] * 4,
        out_shape=[jax.ShapeDtypeStruct(w.shape, F32)] * 4,
        compiler_params=_params(("arbitrary",)),
    )(w, m, v, *[a for a, _ in gparts])


def _adamw_small(parts, w, m, v, name):
    def body(p_ref, w_ref, m_ref, v_ref, g_ref, d_ref, mo_ref, vo_ref):
        g = p_ref[0]
        for dev in range(1, N_DEV):
            g = g + p_ref[dev]
        g_ref[...] = g
        d_ref[...], mo_ref[...], vo_ref[...] = _adam_math(w_ref[...], g, m_ref[...], v_ref[...])

    return _call(
        body, name=name,
        out_shape=[jax.ShapeDtypeStruct(w.shape, F32)] * 4,
        compiler_params=_params(),
    )(parts, w, m, v)


SMALL = ["ffn1_norm", "mix_norm", "attn_sinks", "gmlp_v_norm", "gmlp_w_s", "gmlp_b", "pool_w", "pool_scale",
         "ffn2_norm", "final_norm"]


def _piece_rows(size):
    return -(-size // 1024) * 8


def _pack_small(arrs, extra=None):
    pieces = []
    for a in list(arrs) + [jnp.zeros((1,), F32) if extra is None else extra]:
        fill = _piece_rows(a.size) * 128 - a.size
        f = a.reshape(-1)
        pieces.append((jnp.pad(f, (0, fill)) if fill else f).reshape(-1, 128))
    return jnp.concatenate(pieces, axis=0)


def _unpack_small(packed, like):
    out, off = [], 0
    for a in like:
        piece = packed[off:off + -(-a.size // 128)]
        if a.size % 128:
            piece = piece.reshape(-1)[:a.size]
        out.append(piece.reshape(a.shape))
        off += _piece_rows(a.size)
    return out, packed[off, 0]


def kernel(x, ffn1_norm, ffn1_w_gate, ffn1_w_up, ffn1_w_down, mix_norm, w_in, attn_sinks, gmlp_v_norm, gmlp_w_s, gmlp_b, pool_w, pool_scale, w_out, ffn2_norm, ffn2_w_gate, ffn2_w_up, ffn2_w_down, final_norm, loss_target, m_ffn1_norm, m_ffn1_w_gate, m_ffn1_w_up, m_ffn1_w_down, m_mix_norm, m_w_in, m_attn_sinks, m_gmlp_v_norm, m_gmlp_w_s, m_gmlp_b, m_pool_w, m_pool_scale, m_w_out, m_ffn2_norm, m_ffn2_w_gate, m_ffn2_w_up, m_ffn2_w_down, m_final_norm, v_ffn1_norm, v_ffn1_w_gate, v_ffn1_w_up, v_ffn1_w_down, v_mix_norm, v_w_in, v_attn_sinks, v_gmlp_v_norm, v_gmlp_w_s, v_gmlp_b, v_pool_w, v_pool_scale, v_w_out, v_ffn2_norm, v_ffn2_w_gate, v_ffn2_w_up, v_ffn2_w_down, v_final_norm):
    s = x.shape[1]
    xi, yi, ci = _position()
    ids = jnp.stack([ci, 2 * xi + yi]).astype(jnp.int32)
    jme = 2 * xi + yi
    others = jnp.stack([k + (k >= jme).astype(jnp.int32) for k in range(3)]).astype(jnp.int32)
    t = lambda a: jnp.swapaxes(a, -1, -2)
    row = lambda a: a.reshape(1, -1)
    full = lambda a: a.reshape(a.shape[0], -1, D)

    loc_f1 = [jnp.stack([t(ffn1_w_gate[l]), t(ffn1_w_up[l]), ffn1_w_down[l]]).astype(BF16) for l in range(DEPTH)]
    loc_f2 = [jnp.stack([t(ffn2_w_gate[l]), t(ffn2_w_up[l]), ffn2_w_down[l]]).astype(BF16) for l in range(DEPTH)]
    loc_in = [t(w_in[l])[None].astype(BF16) for l in range(DEPTH)]
    loc_out = [w_out[l][None].astype(BF16) for l in range(DEPTH)]

    (wf1,) = _alone(_GatherTask([loc_f1[0]]), "gather_first")
    wf1 = full(wf1)
    xc = x.reshape(s, D)
    saved = []
    for l in range(DEPTH):
        x0 = xc
        if l == 0:
            (x1, *act1), ((wf2, win, wout),) = _ffn_fwd(
                x0, row(ffn1_norm[l]), wf1, 0, f"ffn1_fwd_{l}", comm=[_GatherTask([loc_f2[0], loc_in[0], loc_out[0]])])
        else:
            (x1, *act1), ((wf2,),) = _ffn_fwd(
                x0, row(ffn1_norm[l]), wf1, 0, f"ffn1_fwd_{l}", comm=[_GatherTask([loc_f2[1]])])
        wf2, win, wout = full(wf2), full(win), full(wout)
        z, hmix = _mixin_fwd(x1, row(mix_norm[l]), win, 0, f"mixin_fwd_{l}")
        wcat = jnp.concatenate([gmlp_w_s[l][h] for h in range(4)], axis=1)
        bexp = jnp.repeat(t(gmlp_b[l]), HD, axis=1)
        pwbd = jnp.zeros((256, 256), F32)
        for g in range(4):
            pwbd = pwbd.at[g * HD:(g + 1) * HD, g * HD:(g + 1) * HD].set(pool_w[l][g])
        mixp = (attn_sinks[l], row(gmlp_v_norm[l]), wcat, bexp, pwbd, row(pool_scale[l]))
        y, lse = _mix_fwd(z, *mixp, f"mix_fwd_{l}")
        keep = (x0, act1, wf1, x1, z, hmix, mixp, y, lse, win, wout)
        if l == 0:
            (xc, *act2, x2), ((wf1, win, wout),) = _ffn_fwd(
                x1, row(ffn2_norm[l]), wf2, 0, f"ffn2_fwd_{l}", mixer=(y, wout),
                comm=[_GatherTask([loc_f1[1], loc_in[1], loc_out[1]])])
            wf1 = full(wf1)
        else:
            dx, *act2, x2, loss_part, d_final = _ffn_fwd(
                x1, row(ffn2_norm[l]), wf2, 0, f"ffn2_fwd_{l}", mixer=(y, wout),
                loss=(row(final_norm), loss_target.reshape(s, D)))
        saved.append(keep + (x2, wf2, act2))

    def five(g):
        return g.reshape(g.shape[0], 4, 2, g.shape[1] // N_DEV, D)

    def core_sums(g5s, r1s, tag):
        res = [_core_sum(ids, g5, r1, f"core_sum_{tag}_{i}") for i, (g5, r1) in enumerate(zip(g5s, r1s))]
        return [sb for sb, _ in res], [own for _, own in res]

    def chip_sums(owns, r2s, tag):
        return [_chip_sum(others, own, r2, f"chip_sum_{tag}_{i}") for i, (own, r2) in enumerate(zip(owns, r2s))]

    def mix_small(l, dsink, dvn, dws, dbt, dpw, dps):
        return {("attn_sinks", l): dsink[:, 0], ("gmlp_v_norm", l): dvn[0],
                ("gmlp_w_s", l): jnp.stack([dws[:, h * BLK:(h + 1) * BLK] for h in range(4)]),
                ("gmlp_b", l): t(dbt[:, :4]),
                ("pool_w", l): jnp.stack([dpw[g * HD:(g + 1) * HD, g * HD:(g + 1) * HD] for g in range(4)]),
                ("pool_scale", l): dps[0]}

    small = {}
    gsrc = {}

    def reduced(l, names, arrays):
        slabs = [(a, k) for a in arrays for k in range(a.shape[0])]
        for nm, src in zip(names, slabs):
            gsrc[(nm, l)] = src

    ffn1_names = ["ffn1_w_gate", "ffn1_w_up", "ffn1_w_down"]
    ffn2_names = ["ffn2_w_gate", "ffn2_w_up", "ffn2_w_down"]
    gate_up = dict(cols=2 * FF, slab_rows=FF)
    x0, (p11, p21, hid1), wf1, x1, z, hmix, mixp, y, lse, win, wout, x2, wf2, (p12, p22, hid2) = saved[1]
    dx, dab, h, dyb, dg, dymix, dxb = _ffn_bwd(x2, row(ffn2_norm[1]), dx, p12, p22, wf2, 0, "ffn2_bwd_1", w_out=wout)
    small[("ffn2_norm", 1)] = dg[0]
    g = _wgrad(dab, h, None, 3, 0, "wgrad_gate_up2_1", **gate_up)
    g = _wgrad(hid2, dyb, g, 3, 2, "wgrad_down2_1")
    a5 = [five(g)]
    g_out, (a_r1,) = _wgrad(y, dxb, None, 1, 0, "wgrad_out_1", comm=[_SiblingTask(a5)])
    a_sb, a_own = core_sums(a5, a_r1, "a")
    (dz, dsink, dvn, dws, dbt, dpw, dps), (a_r2,) = _mix_bwd(z, dymix, lse, *mixp, "mix_bwd_1", comm=[_ChipTask(a_sb)])
    reduced(1, ffn2_names, chip_sums(a_own, a_r2, "a"))
    small.update(mix_small(1, dsink, dvn, dws, dbt, dpw, dps))
    g_in = _wgrad(dz, hmix, None, 1, 0, "wgrad_in_1")
    b5 = [five(g_out), five(g_in)]
    (dx, dab, h, dyb, dg, dgm), (b_r1,) = _ffn_bwd(x0, row(ffn1_norm[1]), dx, p11, p21, wf1, 0, "ffn1_bwd_1",
                                                   comm=[_SiblingTask(b5)], mixin=(row(mix_norm[1]), dz, win, x1))
    small[("ffn1_norm", 1)], small[("mix_norm", 1)] = dg[0], dgm[0]
    b_sb, b_own = core_sums(b5, b_r1, "b")
    g_gu, (b_r2,) = _wgrad(dab, h, None, 2, 0, "wgrad_gate_up1_1", comm=[_ChipTask(b_sb)], **gate_up)
    reduced(1, ["w_out", "w_in"], chip_sums(b_own, b_r2, "b"))
    g_down = _wgrad(hid1, dyb, None, 1, 0, "wgrad_down1_1")
    c5 = [five(g_gu), five(g_down)]
    x0, (p11, p21, hid1), wf1, x1, z, hmix, mixp, y, lse, win, wout, x2, wf2, (p12, p22, hid2) = saved[0]
    (dx, dab, h, dyb, dg, dymix, dxb), (c_r1,) = _ffn_bwd(x2, row(ffn2_norm[0]), dx, p12, p22, wf2, 0, "ffn2_bwd_0",
                                                          comm=[_SiblingTask(c5)], w_out=wout)
    small[("ffn2_norm", 0)] = dg[0]
    c_sb, c_own = core_sums(c5, c_r1, "c")
    g_gu, (c_r2a,) = _wgrad(dab, h, None, 2, 0, "wgrad_gate_up2_0", comm=[_ChipTask(c_sb[0:1])], **gate_up)
    d5a = [five(g_gu)]
    g_down, (c_r2b, d_r1a) = _wgrad(hid2, dyb, None, 1, 0, "wgrad_down2_0",
                                    comm=[_ChipTask(c_sb[1:2]), _SiblingTask(d5a)])
    reduced(1, ffn1_names, chip_sums(c_own, c_r2a + c_r2b, "c"))
    da_sb, da_own = core_sums(d5a, d_r1a, "da")
    d5b = [five(g_down)]
    (dz, dsink, dvn, dws, dbt, dpw, dps), (d_r2a, d_r1b) = _mix_bwd(
        z, dymix, lse, *mixp, "mix_bwd_0", comm=[_ChipTask(da_sb), _SiblingTask(d5b)])
    small.update(mix_small(0, dsink, dvn, dws, dbt, dpw, dps))
    db_sb, db_own = core_sums(d5b, d_r1b, "db")
    dx, dab, h, dyb, dg, dgm = _ffn_bwd(x0, row(ffn1_norm[0]), dx, p11, p21, wf1, 0, "ffn1_bwd_0",
                                        mixin=(row(mix_norm[0]), dz, win, x1))
    small[("ffn1_norm", 0)], small[("mix_norm", 0)] = dg[0], dgm[0]
    grad_x = dx.reshape(1, s, D)

    part = [d_final[0] if nm == "final_norm" else jnp.stack([small[(nm, l)] for l in range(DEPTH)]) for nm in SMALL]
    packed = _pack_small(part, loss_part[0, 0])
    g_gate, ((gathered,),) = _wgrad(dab, h, None, 1, 0, "wgrad_gate1_0", cols=FF, comm=[_GatherTask([packed[None]])])
    f5 = [five(g_gate)]
    g_up, (d_r2b, f_r1) = _wgrad(dab, h, None, 1, 0, "wgrad_up1_0", col0=FF, cols=FF,
                                 comm=[_ChipTask(db_sb), _SiblingTask(f5)])
    reduced(0, ffn2_names, chip_sums(da_own + db_own, d_r2a + d_r2b, "d"))
    f_sb, f_own = core_sums(f5, f_r1, "f")
    u5 = [five(g_up)]
    g_down, (f_r2, u_r1) = _wgrad(hid1, dyb, None, 1, 0, "wgrad_down1_0", comm=[_ChipTask(f_sb), _SiblingTask(u5)])
    u_sb, u_own = core_sums(u5, u_r1, "u")
    w5 = [five(g_down)]
    g_in, (u_r2, w_r1) = _wgrad(dz, hmix, None, 1, 0, "wgrad_in_0", comm=[_ChipTask(u_sb), _SiblingTask(w5)])
    w_sb, w_own = core_sums(w5, w_r1, "w")
    i5 = [five(g_in)]
    g_out, (w_r2, i_r1) = _wgrad(y, dxb, None, 1, 0, "wgrad_out_0", comm=[_ChipTask(w_sb), _SiblingTask(i5)])
    reduced(0, ffn1_names, chip_sums(f_own + u_own + w_own, f_r2 + u_r2 + w_r2, "f"))
    i_sb, i_own = core_sums(i5, i_r1, "i")
    o5 = [five(g_out)]
    o_r1 = _alone(_SiblingTask(o5), "reduce_sibling_last")
    o_sb, o_own = core_sums(o5, o_r1, "o")
    e_r2 = _alone(_ChipTask(o_sb + i_sb), "reduce_chips_last")
    reduced(0, ["w_out", "w_in"], chip_sums(o_own + i_own, e_r2, "e"))

    grads = {}
    transposed = ("ffn1_w_gate", "ffn1_w_up", "ffn2_w_gate", "ffn2_w_up", "w_in")

    small_w = dict(ffn1_norm=ffn1_norm, mix_norm=mix_norm, attn_sinks=attn_sinks, gmlp_v_norm=gmlp_v_norm,
                   gmlp_w_s=gmlp_w_s, gmlp_b=gmlp_b, pool_w=pool_w, pool_scale=pool_scale, ffn2_norm=ffn2_norm,
                   final_norm=final_norm)
    small_m = dict(ffn1_norm=m_ffn1_norm, mix_norm=m_mix_norm, attn_sinks=m_attn_sinks, gmlp_v_norm=m_gmlp_v_norm,
                   gmlp_w_s=m_gmlp_w_s, gmlp_b=m_gmlp_b, pool_w=m_pool_w, pool_scale=m_pool_scale,
                   ffn2_norm=m_ffn2_norm, final_norm=m_final_norm)
    small_v = dict(ffn1_norm=v_ffn1_norm, mix_norm=v_mix_norm, attn_sinks=v_attn_sinks, gmlp_v_norm=v_gmlp_v_norm,
                   gmlp_w_s=v_gmlp_w_s, gmlp_b=v_gmlp_b, pool_w=v_pool_w, pool_scale=v_pool_scale,
                   ffn2_norm=v_ffn2_norm, final_norm=v_final_norm)
    sg, sd, sm, sv = _adamw_small(gathered[0], _pack_small([small_w[nm] for nm in SMALL]),
                                  _pack_small([small_m[nm] for nm in SMALL]),
                                  _pack_small([small_v[nm] for nm in SMALL]), "adamw_small")
    like = [small_w[nm] for nm in SMALL]
    sg_l, loss = _unpack_small(sg, like)
    sd_l, _ = _unpack_small(sd, like)
    sm_l, _ = _unpack_small(sm, like)
    sv_l, _ = _unpack_small(sv, like)
    deltas, new_m, new_v = {}, {}, {}
    for i, nm in enumerate(SMALL):
        grads[nm], deltas[nm], new_m[nm], new_v[nm] = sg_l[i], sd_l[i], sm_l[i], sv_l[i]

    big_w = dict(ffn1_w_gate=ffn1_w_gate, ffn1_w_up=ffn1_w_up, ffn1_w_down=ffn1_w_down, w_in=w_in, w_out=w_out,
                 ffn2_w_gate=ffn2_w_gate, ffn2_w_up=ffn2_w_up, ffn2_w_down=ffn2_w_down)
    big_m = dict(ffn1_w_gate=m_ffn1_w_gate, ffn1_w_up=m_ffn1_w_up, ffn1_w_down=m_ffn1_w_down, w_in=m_w_in,
                 w_out=m_w_out, ffn2_w_gate=m_ffn2_w_gate, ffn2_w_up=m_ffn2_w_up, ffn2_w_down=m_ffn2_w_down)
    big_v = dict(ffn1_w_gate=v_ffn1_w_gate, ffn1_w_up=v_ffn1_w_up, ffn1_w_down=v_ffn1_w_down, w_in=v_w_in,
                 w_out=v_w_out, ffn2_w_gate=v_ffn2_w_gate, ffn2_w_up=v_ffn2_w_up, ffn2_w_down=v_ffn2_w_down)
    for nm in big_w:
        view = t if nm in transposed else (lambda a: a)
        res = _adamw(view(big_w[nm]), view(big_m[nm]), view(big_v[nm]), [gsrc[(nm, l)] for l in range(DEPTH)],
                     f"adamw_{nm}")
        grads[nm], deltas[nm], new_m[nm], new_v[nm] = [view(r) for r in res]

    order = ["ffn1_norm", "ffn1_w_gate", "ffn1_w_up", "ffn1_w_down", "mix_norm", "w_in", "attn_sinks", "gmlp_v_norm",
             "gmlp_w_s", "gmlp_b", "pool_w", "pool_scale", "w_out", "ffn2_norm", "ffn2_w_gate", "ffn2_w_up",
             "ffn2_w_down", "final_norm"]
    return (loss, grad_x, *[grads[n] for n in order], *[deltas[n] for n in order],
            *[new_m[n] for n in order], *[new_v[n] for n in order])
```

```python
import functools
import math

import jax
import jax.numpy as jnp
from jax import lax
from jax.experimental import pallas as pl
from jax.experimental.pallas import tpu as pltpu

F32 = jnp.float32
BF16 = jnp.bfloat16
MESH = pl.DeviceIdType.MESH

D = 1024
FF = 2816
INW = 1536
N_DEV = 8
DEPTH = 2
BLK = 128
HD = 64
N_HEADS = 8
N_KV = 2
REP = 4
ATTN_SCALE = HD ** -0.5
POOL_WINDOWS = (2, 4, 8, 16)
EPS = 1e-6
NEG = -1e30
FC = 256
GELU_C0 = math.sqrt(2.0 / math.pi)
GELU_C1 = 0.044715

ADAM_LR = 0.001
ADAM_B1 = 0.9
ADAM_B2 = 0.999
ADAM_EPS = 1e-08
ADAM_WD = 0.01
ADAM_STEP = 10

VMEM_LIMIT = 60 * 1024 * 1024

O_K, O_V, O_U, O_G, O_P = 512, 640, 768, 1024, 1280


def _call(body, **kw):
    return pl.pallas_call(body, **kw)


def _params(sem=None, vmem=VMEM_LIMIT):
    return pltpu.CompilerParams(dimension_semantics=sem, vmem_limit_bytes=vmem)


def _host(comm, body, *, name, grid, in_specs, out_specs, out_shape, args, scratch_shapes=(), aliases=None):
    single = not isinstance(out_shape, (list, tuple))
    out_specs_l = [out_specs] if single else list(out_specs)
    out_shape_l = [out_shape] if single else list(out_shape)
    n_in, n_out, n_scr = len(in_specs), len(out_shape_l), len(scratch_shapes)
    steps = grid[0]
    any_spec = pl.BlockSpec(memory_space=pl.ANY)

    def wrapped(*refs):
        pos = 0

        def take(n):
            nonlocal pos
            part = refs[pos:pos + n]
            pos += n
            return part

        ins = take(n_in)
        cins = [take(len(t.inputs)) for t in comm]
        outs = take(n_out)
        couts = [take(len(t.out_shape)) for t in comm]
        scr = take(n_scr)
        cscr = [take(len(t.scratch)) for t in comm]
        i = pl.program_id(0)
        for k, t in enumerate(comm):
            pl.when(i == 0)(functools.partial(t.start, cins[k], couts[k], cscr[k]))
        body(*ins, *outs, *scr)
        for k, t in enumerate(comm):
            pl.when(i == (3 * steps) // 4)(functools.partial(t.mid, cins[k], couts[k], cscr[k]))
            pl.when(i == steps - 1)(functools.partial(t.finish, cins[k], couts[k], cscr[k]))

    c_args = [a for t in comm for a in t.inputs]
    c_shapes = [sh for t in comm for sh in t.out_shape]
    c_scr = [sc for t in comm for sc in t.scratch]
    res = _call(
        wrapped, name=name, grid=grid,
        in_specs=list(in_specs) + [any_spec] * len(c_args),
        out_specs=out_specs_l + [any_spec] * len(c_shapes),
        out_shape=out_shape_l + c_shapes,
        scratch_shapes=list(scratch_shapes) + c_scr,
        input_output_aliases=aliases or {},
        compiler_params=_params(("arbitrary",)),
    )(*args, *c_args)
    outs = res[0] if single else list(res[:n_out])
    if not comm:
        return outs
    c_outs, pos = [], n_out
    for t in comm:
        c_outs.append(list(res[pos:pos + len(t.out_shape)]))
        pos += len(t.out_shape)
    return outs, c_outs


def _nn(a, b):
    return lax.dot_general(a, b, (((1,), (0,)), ((), ())), preferred_element_type=F32)


def _nt(a, b):
    return lax.dot_general(a, b, (((1,), (1,)), ((), ())), preferred_element_type=F32)


def _tn(a, b):
    return lax.dot_general(a, b, (((0,), (0,)), ((), ())), preferred_element_type=F32)


def _gelu(x):
    x2 = x * x
    t = jnp.tanh(x * (GELU_C0 + (GELU_C0 * GELU_C1) * x2))
    hx = 0.5 * x
    return hx + hx * t, (hx, x2, t)


def _gelu_grad(parts):
    hx, x2, t = parts
    return (0.5 + 0.5 * t) + (hx * (1.0 - t * t)) * (GELU_C0 + (3.0 * GELU_C0 * GELU_C1) * x2)


def _rms(x):
    r = lax.rsqrt(jnp.mean(x * x, axis=-1, keepdims=True) + EPS)
    return x * r, r


def _rms_bwd(dy, xh, r, g):
    dg = jnp.sum(dy * xh, axis=0, keepdims=True)
    dxh = dy * g
    dx = r * (dxh - xh * jnp.mean(dxh * xh, axis=-1, keepdims=True))
    return dx, dg


def _wspec(rows, m):
    return pl.BlockSpec((None, rows, D), lambda i, m=m: (m, 0, 0), pipeline_mode=pl.Buffered(1))


def _rowspec(tm, cols):
    return pl.BlockSpec((tm, cols), lambda i: (i, 0))


def _fixspec(rows, cols):
    return pl.BlockSpec((rows, cols), lambda i: (0, 0))


def _ffn_fwd(x, gain, w352, mg, name, comm=(), mixer=None, loss=None):
    s = x.shape[0]
    tm = min(512, s)
    n_in = 5 + (2 if mixer is not None else 0) + (2 if loss is not None else 0)

    def body(*refs):
        x_ref, g_ref, wg_ref, wu_ref, wd_ref = refs[:5]
        more_in, outs = list(refs[5:n_in]), list(refs[n_in:])
        xo_ref, p1_ref, p2_ref, hid_ref = outs[:4]
        more_out = outs[4:]
        xt = x_ref[...]
        if mixer is not None:
            y_ref, wo_ref = more_in[:2]
            xt = xt + _nn(y_ref[...], wo_ref[...])
            more_out.pop(0)[...] = xt
        xh, _ = _rms(xt)
        h = (xh * g_ref[...]).astype(BF16)
        for c in range(FF // FC):
            sl = slice(c * FC, (c + 1) * FC)
            a = _nt(h, wg_ref[sl, :])
            b = _nt(h, wu_ref[sl, :])
            sig = 0.5 * jnp.tanh(0.5 * a) + 0.5
            sa = a * sig
            p1_ref[:, sl] = (b * (sig + sa * (1.0 - sig))).astype(BF16)
            p2_ref[:, sl] = sa.astype(BF16)
            hid_ref[:, sl] = (sa * b).astype(BF16)
        xo = xt + 0.5 * _nn(hid_ref[...], wd_ref[...])
        if loss is None:
            xo_ref[...] = xo
        else:
            gf_ref, t_ref = more_in[-2:]
            loss_ref, dgf_ref = more_out
            gf = gf_ref[...]
            xh, r = _rms(xo)
            err = xh * gf - t_ref[...]
            lp = 0.5 * jnp.sum(jnp.mean(err * err, axis=-1, keepdims=True), axis=0, keepdims=True)
            xo_ref[...], dgf = _rms_bwd(err * (1.0 / D), xh, r, gf)

            @pl.when(pl.program_id(0) == 0)
            def _():
                dgf_ref[...] = jnp.zeros_like(dgf_ref)
                loss_ref[...] = jnp.zeros_like(loss_ref)

            dgf_ref[0:1, :] += dgf
            loss_ref[0:1, :] += lp + jnp.zeros((1, 128), F32)

    act = jax.ShapeDtypeStruct((s, FF), BF16)
    tok = jax.ShapeDtypeStruct((s, D), F32)
    in_specs = [_rowspec(tm, D), _fixspec(1, D), _wspec(FF, mg), _wspec(FF, mg + 1), _wspec(FF, mg + 2)]
    out_specs = [_rowspec(tm, D), _rowspec(tm, FF), _rowspec(tm, FF), _rowspec(tm, FF)]
    out_shape = [tok, act, act, act]
    args = (x, gain, w352, w352, w352)
    if mixer is not None:
        in_specs += [_rowspec(tm, D), _wspec(D, 0)]
        out_specs += [_rowspec(tm, D)]
        out_shape += [tok]
        args += tuple(mixer)
    if loss is not None:
        in_specs += [_fixspec(1, D), _rowspec(tm, D)]
        out_specs += [_fixspec(8, 128), _fixspec(8, D)]
        out_shape += [jax.ShapeDtypeStruct((8, 128), F32), jax.ShapeDtypeStruct((8, D), F32)]
        args += tuple(loss)
    return _host(comm, body, name=name, grid=(s // tm,), in_specs=in_specs, out_specs=out_specs,
                 out_shape=out_shape, args=args)


def _ffn_bwd(x, gain, dy, p1, p2, w352, mg, name, comm=(), w_out=None, mixin=None):
    s = x.shape[0]
    tm = min(256, s)
    n_in = 8 + (1 if w_out is not None else 0) + (4 if mixin is not None else 0)

    def body(*refs):
        x_ref, g_ref, dy_ref, p1_ref, p2_ref, wg_ref, wu_ref, wd_ref = refs[:8]
        more_in, outs = list(refs[8:n_in]), list(refs[n_in:])
        dx_ref, dab_ref, h_ref, dyb_ref, dg_ref = outs[:5]
        more_out = outs[5:]
        i = pl.program_id(0)
        xt = x_ref[...]
        g = g_ref[...]
        xh, r = _rms(xt)
        h_ref[...] = (xh * g).astype(BF16)
        dyt = dy_ref[...]
        if mixin is not None:
            gm_ref, dz_ref, win_ref, x1_ref = more_in[-4:]
            dgm_ref = more_out[-1]
            xh1, r1 = _rms(x1_ref[...])
            dxm, dgm = _rms_bwd(_nn(dz_ref[...], win_ref[...]), xh1, r1, gm_ref[...])
            dyt = dyt + dxm

            @pl.when(i == 0)
            def _():
                dgm_ref[...] = jnp.zeros_like(dgm_ref)

            dgm_ref[0:1, :] += dgm
        dyb = (0.5 * dyt).astype(BF16)
        dyb_ref[...] = dyb
        for c in range(FF // FC):
            sl = slice(c * FC, (c + 1) * FC)
            dhid = _nt(dyb, wd_ref[sl, :])
            dab_ref[:, sl] = (dhid * p1_ref[:, sl].astype(F32)).astype(BF16)
            dab_ref[:, FF + c * FC:FF + (c + 1) * FC] = (dhid * p2_ref[:, sl].astype(F32)).astype(BF16)
        dh = _nn(dab_ref[:, :FF], wg_ref[...]) + _nn(dab_ref[:, FF:], wu_ref[...])
        dxn, dg = _rms_bwd(dh, xh, r, g)
        dx = dyt + dxn
        dx_ref[...] = dx
        if w_out is not None:
            dym_ref, dxb_ref = more_out[:2]
            dxb = dx.astype(BF16)
            dxb_ref[...] = dxb
            dym_ref[...] = _nt(dxb, more_in[0][...])

        @pl.when(i == 0)
        def _():
            dg_ref[...] = jnp.zeros_like(dg_ref)

        dg_ref[0:1, :] += dg

    tok = jax.ShapeDtypeStruct((s, D), BF16)
    tok32 = jax.ShapeDtypeStruct((s, D), F32)
    gain_grad = jax.ShapeDtypeStruct((8, D), F32)
    in_specs = [_rowspec(tm, D), _fixspec(1, D), _rowspec(tm, D), _rowspec(tm, FF), _rowspec(tm, FF),
                _wspec(FF, mg), _wspec(FF, mg + 1), _wspec(FF, mg + 2)]
    out_specs = [_rowspec(tm, D), _rowspec(tm, 2 * FF), _rowspec(tm, D), _rowspec(tm, D), _fixspec(8, D)]
    out_shape = [tok32, jax.ShapeDtypeStruct((s, 2 * FF), BF16), tok, tok, gain_grad]
    args = (x, gain, dy, p1, p2, w352, w352, w352)
    if w_out is not None:
        in_specs += [_wspec(D, 0)]
        out_specs += [_rowspec(tm, D), _rowspec(tm, D)]
        out_shape += [tok32, tok]
        args += (w_out,)
    if mixin is not None:
        in_specs += [_fixspec(1, D), _rowspec(tm, INW), _wspec(INW, 0), _rowspec(tm, D)]
        out_specs += [_fixspec(8, D)]
        out_shape += [gain_grad]
        args += tuple(mixin)
    return _host(comm, body, name=name, grid=(s // tm,), in_specs=in_specs, out_specs=out_specs,
                 out_shape=out_shape, args=args)


def _wgrad(a, b, g, n_slabs, m, name, comm=(), col0=0, cols=None, slab_rows=None):
    s = a.shape[0]
    cols = a.shape[1] if cols is None else cols
    slab_rows = cols if slab_rows is None else slab_rows
    mb = 256
    per_slab = slab_rows // mb

    def body(*refs):
        refs[-1][...] = _tn(refs[0][...], refs[1][...])

    in_specs = [pl.BlockSpec((s, mb), lambda i: (0, col0 // mb + i)),
                pl.BlockSpec((s, D), lambda i: (0, 0), pipeline_mode=pl.Buffered(1))]
    args = [a, b]
    aliases = {}
    if g is not None:
        in_specs.append(pl.BlockSpec(memory_space=pl.ANY))
        args.append(g)
        aliases = {2: 0}
    return _host(
        comm, body, name=name, grid=(cols // mb,),
        in_specs=in_specs,
        out_specs=pl.BlockSpec((None, mb, D), lambda i: (m + i // per_slab, i % per_slab, 0)),
        out_shape=jax.ShapeDtypeStruct((n_slabs, slab_rows, D), F32),
        aliases=aliases, args=args)


def _mixin_fwd(x, gain, w192, l, name):
    s = x.shape[0]
    tm = min(512, s)

    def body(x_ref, g_ref, w_ref, z_ref, h_ref):
        xh, _ = _rms(x_ref[...])
        h = (xh * g_ref[...]).astype(BF16)
        h_ref[...] = h
        z_ref[...] = _nt(h, w_ref[...])

    return _call(
        body, name=name, grid=(s // tm,),
        in_specs=[_rowspec(tm, D), _fixspec(1, D), _wspec(INW, l)],
        out_specs=[_rowspec(tm, INW), _rowspec(tm, D)],
        out_shape=[jax.ShapeDtypeStruct((s, INW), F32), jax.ShapeDtypeStruct((s, D), BF16)],
        compiler_params=_params(("arbitrary",)),
    )(x, gain, w192)


MIX_NB = 4
TILE = MIX_NB * BLK
GROUP_ROWS = REP * BLK


class _Block:
    def __init__(self, n, j, zc_ref, zkvp_ref, zpp_ref):
        self.zc, self.zkvp, self.zpp = zc_ref, zkvp_ref, zpp_ref
        self.first = j == 0
        self.r = slice(j * BLK, (j + 1) * BLK)
        self.rp = slice((j - 1) * BLK, j * BLK)
        self.index = n * MIX_NB + j
        self.lo = jnp.where(n > 0, 0, BLK) if self.first else 0
        self.has_prev = jnp.where(n > 0, 1.0, 0.0) if self.first else 1.0

    def cols(self, c0, c1):
        return self.zc[self.r, c0:c1]

    def prev_kv(self, c0, c1):
        return self.zkvp[:, c0:c1] if self.first else self.zc[self.rp, O_K + c0:O_K + c1]

    def prev_p(self):
        return self.zpp[...] * self.has_prev if self.first else self.zc[self.rp, O_P:INW]


def _band_mask(rows):
    row = lax.broadcasted_iota(jnp.int32, (rows, 2 * BLK), 0) & (BLK - 1)
    col = lax.broadcasted_iota(jnp.int32, (rows, 2 * BLK), 1)
    return (col > row) & (col <= row + BLK)


def _block_mask(band, blk):
    if not blk.first:
        return band
    return band & (lax.broadcasted_iota(jnp.int32, band.shape, 1) >= blk.lo)


def _lane_head(shape):
    return lax.broadcasted_iota(jnp.int32, shape, 1) // HD


def _lane_group_select(vals):
    grp = _lane_head(vals[0].shape)
    return jnp.where(grp == 0, vals[0], jnp.where(grp == 1, vals[1], jnp.where(grp == 2, vals[2], vals[3])))


def _pool_count(index):
    row = lax.broadcasted_iota(jnp.int32, (BLK, 256), 0)
    pos1 = (index * BLK + row + 1).astype(F32)
    wl = _lane_group_select([jnp.full((BLK, 256), float(w), F32) for w in POOL_WINDOWS])
    return jnp.minimum(pos1, wl)


def _window_sums(e, forward):
    tot = e.shape[0]
    lv = e
    out = []
    for sh in (1, 2, 4, 8):
        lv = lv + pltpu.roll(lv, sh if forward else tot - sh, 0)
        out.append(lv)
    return _lane_group_select(out)


def _stack_heads(get, g):
    return jnp.concatenate([get((g * REP + rr) * HD, (g * REP + rr + 1) * HD) for rr in range(REP)], axis=0)


def _sink_column(sink_ref, g):
    return jnp.concatenate([jnp.full((BLK, 1), sink_ref[g * REP + rr], F32) for rr in range(REP)], axis=0)


def _kv_window(blk, g):
    kk = jnp.concatenate([blk.prev_kv(g * HD, (g + 1) * HD),
                          blk.cols(O_K + g * HD, O_K + (g + 1) * HD)], axis=0).astype(BF16)
    vv = jnp.concatenate([blk.prev_kv(BLK + g * HD, BLK + (g + 1) * HD),
                          blk.cols(O_V + g * HD, O_V + (g + 1) * HD)], axis=0).astype(BF16)
    return kk, vv


def _mix_common(blk, vn_ref, wcat_ref, bexp_ref, pwbd_ref):
    u, tu = _gelu(blk.cols(O_U, O_G))
    gv, tv = _gelu(blk.cols(O_G, O_P))
    xh, rv = _rms(gv)
    vnb = (xh * vn_ref[...]).astype(BF16)
    head = _lane_head((BLK, 256))
    vn_bd = jnp.concatenate([jnp.where(head == h, vnb, jnp.zeros_like(vnb)) for h in range(4)], axis=0)
    row = lax.broadcasted_iota(jnp.int32, (BLK, 4 * BLK), 0)
    col = lax.broadcasted_iota(jnp.int32, (BLK, 4 * BLK), 1) & (BLK - 1)
    tril = col <= row
    wcat = jnp.where(tril, wcat_ref[...], 0.0).astype(BF16)
    f = _nn(wcat, vn_bd) + bexp_ref[...]
    p = blk.cols(O_P, INW)
    e = jnp.concatenate([blk.prev_p(), p], axis=0)
    cnt = _pool_count(blk.index)
    diff = (_window_sums(e, True)[BLK:, :] / cnt - p).astype(BF16)
    pwbd = pwbd_ref[...].astype(BF16)
    pout = _nn(diff, pwbd)
    return dict(u=u, tu=tu, tv=tv, xh=xh, rv=rv, vn_bd=vn_bd, wcat=wcat, f=f, cnt=cnt, diff=diff, pwbd=pwbd,
                pout=pout, tril=tril, head=head)


def _mix_fwd(z, sinks, vnorm, wcat, bexp, pwbd, pscale, name):
    s = z.shape[0]
    nt = s // TILE

    def body(sink_ref, zc_ref, zkvp_ref, zpp_ref, vn_ref, wcat_ref, bexp_ref, pwbd_ref, ps_ref, y_ref, lse_ref):
        n = pl.program_id(0)
        lse_ref[...] = jnp.zeros_like(lse_ref)
        band = _band_mask(BLK)
        for j in range(MIX_NB):
            blk = _Block(n, j, zc_ref, zkvp_ref, zpp_ref)
            valid = _block_mask(band, blk)
            for g in range(N_KV):
                kk, vv = _kv_window(blk, g)
                for rr in range(REP):
                    h = g * REP + rr
                    qh = (blk.cols(h * HD, (h + 1) * HD) * ATTN_SCALE).astype(BF16)
                    sc = jnp.where(valid, _nt(qh, kk), NEG)
                    sink = sink_ref[h]
                    m = jnp.maximum(jnp.max(sc, axis=-1, keepdims=True), sink)
                    ex = jnp.exp(sc - m)
                    den = jnp.sum(ex, axis=-1, keepdims=True) + jnp.exp(sink - m)
                    y_ref[blk.r, h * HD:(h + 1) * HD] = _nn((ex / den).astype(BF16), vv).astype(BF16)
                    lse_ref[blk.r, h:h + 1] = m + jnp.log(den)
            c = _mix_common(blk, vn_ref, wcat_ref, bexp_ref, pwbd_ref)
            y_ref[blk.r, 512:768] = (c["u"] * c["f"]).astype(BF16)
            y_ref[blk.r, 768:1024] = (c["pout"] * ps_ref[...]).astype(BF16)

    halo = lambda n: jnp.maximum(MIX_NB * n - 1, 0)
    return _call(
        body, name=name, grid=(nt,),
        in_specs=[pl.BlockSpec(memory_space=pltpu.SMEM),
                  pl.BlockSpec((TILE, INW), lambda n: (n, 0)),
                  pl.BlockSpec((BLK, 256), lambda n: (halo(n), 2)),
                  pl.BlockSpec((BLK, 256), lambda n: (halo(n), 5)),
                  _fixspec(1, 256), _fixspec(BLK, 4 * BLK), _fixspec(BLK, 256), _fixspec(256, 256), _fixspec(1, 256)],
        out_specs=[pl.BlockSpec((TILE, D), lambda n: (n, 0)), pl.BlockSpec((TILE, 128), lambda n: (n, 0))],
        out_shape=[jax.ShapeDtypeStruct((s, D), BF16), jax.ShapeDtypeStruct((s, 128), F32)],
        compiler_params=_params(("arbitrary",)),
    )(sinks, z, z, z, vnorm, wcat, bexp, pwbd, pscale)


def _mix_bwd(z, dy, lse, sinks, vnorm, wcat, bexp, pwbd, pscale, name, comm=()):
    s = z.shape[0]
    nt = s // TILE
    last = slice(TILE - BLK, TILE)

    def body(sink_ref, zc_ref, zkvp_ref, zpp_ref, dy_ref, lse_ref, vn_ref, wcat_ref, bexp_ref, pwbd_ref, ps_ref,
             dz_ref, dsink_ref, dvn_ref, dws_ref, dbt_ref, dpw_ref, dps_ref, carry_ref, ddc_ref):
        n = pl.program_id(0)

        @pl.when(n == 0)
        def _():
            carry_ref[...] = jnp.zeros_like(carry_ref)
            ddc_ref[...] = jnp.zeros_like(ddc_ref)
            dsink_ref[...] = jnp.zeros_like(dsink_ref)
            dvn_ref[...] = jnp.zeros_like(dvn_ref)
            dws_ref[...] = jnp.zeros_like(dws_ref)
            dbt_ref[...] = jnp.zeros_like(dbt_ref)
            dpw_ref[...] = jnp.zeros_like(dpw_ref)
            dps_ref[...] = jnp.zeros_like(dps_ref)

        def block_grads(j):
            blk = _Block(n, j, zc_ref, zkvp_ref, zpp_ref)
            valid = _block_mask(_band_mask(GROUP_ROWS), blk)
            out = dict(dq=[], dsink=[], dbt=[])
            dk_prev, dk_cur, dv_prev, dv_cur = [], [], [], []
            for g in range(N_KV):
                kk, vv = _kv_window(blk, g)
                q4 = _stack_heads(blk.cols, g).astype(BF16)
                do4 = _stack_heads(lambda c0, c1: dy_ref[blk.r, c0:c1], g).astype(BF16)
                lse4 = jnp.concatenate([lse_ref[blk.r, g * REP + rr:g * REP + rr + 1] for rr in range(REP)], axis=0)
                sc = jnp.where(valid, _nt(q4, kk) * ATTN_SCALE, NEG)
                pr = jnp.exp(sc - lse4)
                dp = _nt(do4, vv)
                delta = jnp.sum(pr * dp, axis=-1, keepdims=True)
                ds = ((pr * (dp - delta)) * ATTN_SCALE).astype(BF16)
                sunk = jnp.exp(_sink_column(sink_ref, g) - lse4) * delta
                dq4 = _nn(ds, kk)
                for rr in range(REP):
                    out["dsink"].append(-jnp.sum(sunk[rr * BLK:(rr + 1) * BLK], axis=0, keepdims=True))
                    out["dq"].append(dq4[rr * BLK:(rr + 1) * BLK])
                dkk = _tn(ds, q4)
                dvv = _tn(pr.astype(BF16), do4)
                dk_prev.append(dkk[:BLK]); dk_cur.append(dkk[BLK:])
                dv_prev.append(dvv[:BLK]); dv_cur.append(dvv[BLK:])
            out["dk_prev"], out["dk_cur"] = jnp.concatenate(dk_prev, axis=1), jnp.concatenate(dk_cur, axis=1)
            out["dv_prev"], out["dv_cur"] = jnp.concatenate(dv_prev, axis=1), jnp.concatenate(dv_cur, axis=1)
            c = _mix_common(blk, vn_ref, wcat_ref, bexp_ref, pwbd_ref)
            dyg = dy_ref[blk.r, 512:768]
            du = dyg * c["f"]
            df = dyg * c["u"]
            out["dzu"] = du * _gelu_grad(c["tu"])
            dfb = df.astype(BF16)
            for h in range(4):
                out["dbt"].append(jnp.sum(df[:, h * HD:(h + 1) * HD], axis=1, keepdims=True))
            out["dws"] = jnp.where(c["tril"], _nt(dfb, c["vn_bd"]), 0.0)
            dvn_bd = _tn(c["wcat"], dfb)
            dvn = functools.reduce(lambda a, b: a + b, [
                jnp.where(c["head"] == h, dvn_bd[h * BLK:(h + 1) * BLK], 0.0) for h in range(4)])
            dgv, out["dvn"] = _rms_bwd(dvn, c["xh"], c["rv"], vn_ref[...])
            out["dzv"] = dgv * _gelu_grad(c["tv"])
            dyp = dy_ref[blk.r, 768:1024]
            out["dps"] = jnp.sum(dyp * c["pout"], axis=0, keepdims=True)
            dout = (dyp * ps_ref[...]).astype(BF16)
            out["dpw"] = _tn(c["diff"], dout)
            out["ddiff"] = _nt(dout, c["pwbd"])
            out["dd"] = out["ddiff"] / c["cnt"]
            return out

        def write_previous_tile(dd_next, dk_next, dv_next):
            if MIX_NB > 1:
                dz_ref[0:TILE - BLK, :] = carry_ref[0:TILE - BLK, :].astype(BF16)
            rs = _window_sums(jnp.concatenate([ddc_ref[...], dd_next], axis=0), False)
            dz_ref[last, 0:O_K] = carry_ref[last, 0:O_K].astype(BF16)
            dz_ref[last, O_K:O_V] = (carry_ref[last, O_K:O_V] + dk_next).astype(BF16)
            dz_ref[last, O_V:O_U] = (carry_ref[last, O_V:O_U] + dv_next).astype(BF16)
            dz_ref[last, O_U:O_P] = carry_ref[last, O_U:O_P].astype(BF16)
            dz_ref[last, O_P:INW] = (carry_ref[last, O_P:INW] + rs[:BLK, :]).astype(BF16)

        @pl.when(n < nt)
        def _():
            parts = [block_grads(j) for j in range(MIX_NB)]
            total = lambda key, i=None: functools.reduce(
                lambda a, b: a + b, [p[key] if i is None else p[key][i] for p in parts])
            for h in range(N_HEADS):
                dsink_ref[h:h + 1, :] += total("dsink", h) + jnp.zeros((1, 128), F32)
            for h in range(4):
                dbt_ref[:, h:h + 1] += total("dbt", h)
            dws_ref[...] += total("dws")
            dpw_ref[...] += total("dpw")
            dvn_ref[0:1, :] += total("dvn")
            dps_ref[0:1, :] += total("dps")
            write_previous_tile(parts[0]["dd"], parts[0]["dk_prev"], parts[0]["dv_prev"])
            for j, p in enumerate(parts):
                r = slice(j * BLK, (j + 1) * BLK)
                nxt = parts[j + 1] if j + 1 < MIX_NB else None
                for h in range(N_HEADS):
                    carry_ref[r, h * HD:(h + 1) * HD] = p["dq"][h]
                carry_ref[r, O_U:O_G] = p["dzu"]
                carry_ref[r, O_G:O_P] = p["dzv"]
                if nxt is None:
                    carry_ref[r, O_K:O_V] = p["dk_cur"]
                    carry_ref[r, O_V:O_U] = p["dv_cur"]
                    carry_ref[r, O_P:INW] = -p["ddiff"]
                    ddc_ref[...] = p["dd"]
                else:
                    rs = _window_sums(jnp.concatenate([p["dd"], nxt["dd"]], axis=0), False)
                    carry_ref[r, O_K:O_V] = p["dk_cur"] + nxt["dk_prev"]
                    carry_ref[r, O_V:O_U] = p["dv_cur"] + nxt["dv_prev"]
                    carry_ref[r, O_P:INW] = rs[:BLK, :] - p["ddiff"]

        @pl.when(n == nt)
        def _():
            none = jnp.zeros((BLK, BLK), F32)
            write_previous_tile(jnp.zeros((BLK, 256), F32), none, none)

    cur = lambda n: jnp.minimum(n, nt - 1)
    done = lambda n: jnp.maximum(n - 1, 0)
    halo = lambda n: jnp.maximum(MIX_NB * jnp.minimum(n, nt - 1) - 1, 0)
    return _host(
        comm, body, name=name, grid=(nt + 1,),
        in_specs=[pl.BlockSpec(memory_space=pltpu.SMEM),
                  pl.BlockSpec((TILE, INW), lambda n: (cur(n), 0)),
                  pl.BlockSpec((BLK, 256), lambda n: (halo(n), 2)),
                  pl.BlockSpec((BLK, 256), lambda n: (halo(n), 5)),
                  pl.BlockSpec((TILE, D), lambda n: (cur(n), 0)),
                  pl.BlockSpec((TILE, 128), lambda n: (cur(n), 0)),
                  _fixspec(1, 256), _fixspec(BLK, 4 * BLK), _fixspec(BLK, 256), _fixspec(256, 256), _fixspec(1, 256)],
        out_specs=[pl.BlockSpec((TILE, INW), lambda n: (done(n), 0)),
                   _fixspec(8, 128), _fixspec(8, 256), _fixspec(BLK, 4 * BLK), _fixspec(BLK, 128),
                   _fixspec(256, 256), _fixspec(8, 256)],
        out_shape=[jax.ShapeDtypeStruct((s, INW), BF16), jax.ShapeDtypeStruct((8, 128), F32),
                   jax.ShapeDtypeStruct((8, 256), F32), jax.ShapeDtypeStruct((BLK, 4 * BLK), F32),
                   jax.ShapeDtypeStruct((BLK, 128), F32), jax.ShapeDtypeStruct((256, 256), F32),
                   jax.ShapeDtypeStruct((8, 256), F32)],
        scratch_shapes=[pltpu.VMEM((TILE, INW), F32), pltpu.VMEM((BLK, 256), F32)],
        args=(sinks, z, z, z, dy, lse, vnorm, wcat, bexp, pwbd, pscale))


def _position():
    x, y, c = lax.axis_index("x"), lax.axis_index("y"), lax.axis_index("c")
    return x, y, c


class _GatherTask:
    def __init__(self, srcs):
        self.inputs = list(srcs)
        ng = len(srcs)
        self.out_shape = [jax.ShapeDtypeStruct((a.shape[0], N_DEV) + a.shape[1:], a.dtype) for a in srcs]
        self.scratch = [pltpu.SemaphoreType.DMA((ng, 7)), pltpu.SemaphoreType.DMA((ng, 7)),
                        pltpu.SemaphoreType.DMA((ng,))]

    def _plan(self, src, dst, sems):
        send_sems, recv_sems, local_sems = sems
        ng = len(src)
        x, y, c = _position()
        me, sibling = (x, y, c), (x, y, 1 - c)
        chips = [(1 - x, y), (x, 1 - y), (1 - x, 1 - y)]

        def slot(pos):
            return 4 * pos[0] + 2 * pos[1] + pos[2]

        def copy(gi, k, block, to, from_src=False):
            rows = dst[gi].at[:, slot(block)]
            return pltpu.make_async_remote_copy(
                src_ref=src[gi] if from_src else rows, dst_ref=rows,
                send_sem=send_sems.at[gi, k], recv_sem=recv_sems.at[gi, k],
                device_id=to, device_id_type=MESH)

        make = functools.partial
        mine = [make(pltpu.make_async_copy, src[gi], dst[gi].at[:, slot(me)], local_sems.at[gi]) for gi in range(ng)]
        first = []
        for gi in range(ng):
            first.append(make(copy, gi, 0, me, sibling, True))
            first += [make(copy, gi, 1 + j, me, (*chip, c), True) for j, chip in enumerate(chips)]
        passed = [make(copy, gi, 4 + j, (*chip, c), sibling) for j, chip in enumerate(chips) for gi in range(ng)]
        arrive_ici = [make(copy, gi, 1 + j, (*chip, c), me) for j, chip in enumerate(chips) for gi in range(ng)]
        arrive_d2d = [make(copy, gi, 0, sibling, me) for gi in range(ng)]
        arrive_d2d += [make(copy, gi, 4 + j, (*chip, 1 - c), me) for j, chip in enumerate(chips) for gi in range(ng)]
        return mine, first, passed, arrive_ici, arrive_d2d

    def start(self, src, dst, sems):
        mine, first, _, _, _ = self._plan(src, dst, sems)
        for cp in mine + first:
            cp().start()

    def mid(self, src, dst, sems):
        _, _, passed, arrive_ici, _ = self._plan(src, dst, sems)
        for arrived, fw in zip(arrive_ici, passed):
            arrived().wait_recv()
            fw().start()

    def finish(self, src, dst, sems):
        mine, first, passed, _, arrive_d2d = self._plan(src, dst, sems)
        for cp in arrive_d2d:
            cp().wait_recv()
        for cp in first + passed:
            cp().wait_send()
        for cp in mine:
            cp().wait()


class _SiblingTask:
    def __init__(self, g5s):
        self.inputs = list(g5s)
        ng = len(g5s)
        self.out_shape = [jax.ShapeDtypeStruct((a.shape[0], 4) + a.shape[3:], a.dtype) for a in g5s]
        self.scratch = [pltpu.SemaphoreType.DMA((ng,)), pltpu.SemaphoreType.DMA((ng,))]

    def _plan(self, src, dst, sems):
        send_sems, recv_sems = sems
        x, y, c = _position()
        return [functools.partial(
            pltpu.make_async_remote_copy,
            src_ref=src[gi].at[:, :, 1 - c], dst_ref=dst[gi],
            send_sem=send_sems.at[gi], recv_sem=recv_sems.at[gi],
            device_id=(x, y, 1 - c), device_id_type=MESH) for gi in range(len(src))]

    def start(self, src, dst, sems):
        for cp in self._plan(src, dst, sems):
            cp().start()

    def mid(self, src, dst, sems):
        pass

    def finish(self, src, dst, sems):
        for cp in self._plan(src, dst, sems):
            cp().wait()


class _ChipTask(_SiblingTask):
    def __init__(self, sbs):
        self.inputs = list(sbs)
        ng = len(sbs)
        self.out_shape = [jax.ShapeDtypeStruct(a.shape, a.dtype) for a in sbs]
        self.scratch = [pltpu.SemaphoreType.DMA((ng, 3)), pltpu.SemaphoreType.DMA((ng, 3))]

    def _plan(self, src, dst, sems):
        send_sems, recv_sems = sems
        x, y, c = _position()
        jme = 2 * x + y
        chips = [(1 - x, y), (x, 1 - y), (1 - x, 1 - y)]
        return [functools.partial(
            pltpu.make_async_remote_copy,
            src_ref=src[gi].at[:, 2 * chip[0] + chip[1]], dst_ref=dst[gi].at[:, jme],
            send_sem=send_sems.at[gi, k], recv_sem=recv_sems.at[gi, k],
            device_id=(*chip, c), device_id_type=MESH) for k, chip in enumerate(chips) for gi in range(len(src))]


def _alone(task, name):
    n_in, n_out = len(task.inputs), len(task.out_shape)

    def body(*refs):
        parts = (refs[:n_in], refs[n_in:n_in + n_out], refs[n_in + n_out:])
        task.start(*parts)
        task.mid(*parts)
        task.finish(*parts)

    any_spec = pl.BlockSpec(memory_space=pl.ANY)
    return _call(body, name=name, in_specs=[any_spec] * n_in, out_specs=[any_spec] * n_out,
                 out_shape=task.out_shape, scratch_shapes=task.scratch)(*task.inputs)


def _core_sum(ids, g5, r1, name):
    n, _, _, rows, _ = g5.shape

    def body(ids_ref, g_ref, r_ref, sb_ref, own_ref):
        j = pl.program_id(2)
        t = g_ref[...] + r_ref[...]
        sb_ref[...] = t.astype(BF16)

        @pl.when(j == ids_ref[1])
        def _():
            own_ref[...] = t

    grid_spec = pltpu.PrefetchScalarGridSpec(
        num_scalar_prefetch=1, grid=(n, 1, 4),
        in_specs=[pl.BlockSpec((None, None, None, rows, D), lambda i, t, j, ids: (i, j, ids[0], t, 0)),
                  pl.BlockSpec((None, None, rows, D), lambda i, t, j, ids: (i, j, t, 0))],
        out_specs=[pl.BlockSpec((None, None, rows, D), lambda i, t, j, ids: (i, j, t, 0)),
                   pl.BlockSpec((None, rows, D), lambda i, t, j, ids: (i, t, 0))])
    return _call(
        body, name=name, grid_spec=grid_spec,
        out_shape=[jax.ShapeDtypeStruct((n, 4, rows, D), BF16), jax.ShapeDtypeStruct((n, rows, D), F32)],
        compiler_params=_params(("arbitrary", "arbitrary", "arbitrary")),
    )(ids, g5, r1)


def _adam_math(w, g, m, v):
    m = ADAM_B1 * m + (1.0 - ADAM_B1) * g
    v = ADAM_B2 * v + (1.0 - ADAM_B2) * (g * g)
    m_hat = m / (1.0 - ADAM_B1 ** ADAM_STEP)
    v_hat = v / (1.0 - ADAM_B2 ** ADAM_STEP)
    delta = -ADAM_LR * (m_hat / (jnp.sqrt(v_hat) + ADAM_EPS) + ADAM_WD * w)
    return delta, m, v


def _adamw(others, w, m, v, gparts, name):
    _, r, c = w.shape

    def body(oth_ref, w_ref, m_ref, v_ref, *rest):
        srcs, (g_ref, d_ref, mo_ref, vo_ref) = rest[:4 * DEPTH], rest[4 * DEPTH:]

        def run(own_ref, r0_ref, r1_ref, r2_ref):
            g = ((own_ref[...] + r0_ref[...].astype(F32)) + r1_ref[...].astype(F32)) + r2_ref[...].astype(F32)
            g_ref[...] = g
            d_ref[...], mo_ref[...], vo_ref[...] = _adam_math(w_ref[...], g, m_ref[...], v_ref[...])

        for l in range(DEPTH):
            pl.when(pl.program_id(0) == l)(functools.partial(run, *srcs[4 * l:4 * l + 4]))

    spec = pl.BlockSpec((None, r, c), lambda l, oth: (l, 0, 0))
    g_specs, g_args = [], []
    for own, recv, k in gparts:
        g_specs.append(pl.BlockSpec((None, r, c), lambda l, oth, k=k: (k, 0, 0)))
        g_args.append(own)
        for j in range(3):
            g_specs.append(pl.BlockSpec((None, None, r, c), lambda l, oth, k=k, j=j: (k, oth[j], 0, 0)))
            g_args.append(recv)
    grid_spec = pltpu.PrefetchScalarGridSpec(
        num_scalar_prefetch=1, grid=(DEPTH,), in_specs=[spec] * 3 + g_specs, out_specs=[spec] * 4)
    return _call(
        body, name=name, grid_spec=grid_spec,
        out_shape=[jax.ShapeDtypeStruct(w.shape, F32)] * 4,
        compiler_params=_params(("arbitrary",)),
    )(others, w, m, v, *g_args)


def _adamw_small(parts, w, m, v, name):
    def body(p_ref, w_ref, m_ref, v_ref, g_ref, d_ref, mo_ref, vo_ref):
        g = p_ref[0]
        for dev in range(1, N_DEV):
            g = g + p_ref[dev]
        g_ref[...] = g
        d_ref[...], mo_ref[...], vo_ref[...] = _adam_math(w_ref[...], g, m_ref[...], v_ref[...])

    return _call(
        body, name=name,
        out_shape=[jax.ShapeDtypeStruct(w.shape, F32)] * 4,
        compiler_params=_params(),
    )(parts, w, m, v)


SMALL = ["ffn1_norm", "mix_norm", "attn_sinks", "gmlp_v_norm", "gmlp_w_s", "gmlp_b", "pool_w", "pool_scale",
         "ffn2_norm", "final_norm"]


def _piece_rows(size):
    return -(-size // 1024) * 8


def _pack_small(arrs, extra=None):
    pieces = []
    for a in list(arrs) + [jnp.zeros((1,), F32) if extra is None else extra]:
        fill = _piece_rows(a.size) * 128 - a.size
        f = a.reshape(-1)
        pieces.append((jnp.pad(f, (0, fill)) if fill else f).reshape(-1, 128))
    return jnp.concatenate(pieces, axis=0)


def _unpack_small(packed, like):
    out, off = [], 0
    for a in like:
        piece = packed[off:off + -(-a.size // 128)]
        if a.size % 128:
            piece = piece.reshape(-1)[:a.size]
        out.append(piece.reshape(a.shape))
        off += _piece_rows(a.size)
    return out, packed[off, 0]


def kernel(x, ffn1_norm, ffn1_w_gate, ffn1_w_up, ffn1_w_down, mix_norm, w_in, attn_sinks, gmlp_v_norm, gmlp_w_s, gmlp_b, pool_w, pool_scale, w_out, ffn2_norm, ffn2_w_gate, ffn2_w_up, ffn2_w_down, final_norm, loss_target, m_ffn1_norm, m_ffn1_w_gate, m_ffn1_w_up, m_ffn1_w_down, m_mix_norm, m_w_in, m_attn_sinks, m_gmlp_v_norm, m_gmlp_w_s, m_gmlp_b, m_pool_w, m_pool_scale, m_w_out, m_ffn2_norm, m_ffn2_w_gate, m_ffn2_w_up, m_ffn2_w_down, m_final_norm, v_ffn1_norm, v_ffn1_w_gate, v_ffn1_w_up, v_ffn1_w_down, v_mix_norm, v_w_in, v_attn_sinks, v_gmlp_v_norm, v_gmlp_w_s, v_gmlp_b, v_pool_w, v_pool_scale, v_w_out, v_ffn2_norm, v_ffn2_w_gate, v_ffn2_w_up, v_ffn2_w_down, v_final_norm):
    s = x.shape[1]
    xi, yi, ci = _position()
    ids = jnp.stack([ci, 2 * xi + yi]).astype(jnp.int32)
    jme = 2 * xi + yi
    others = jnp.stack([k + (k >= jme).astype(jnp.int32) for k in range(3)]).astype(jnp.int32)
    t = lambda a: jnp.swapaxes(a, -1, -2)
    row = lambda a: a.reshape(1, -1)
    full = lambda a: a.reshape(a.shape[0], -1, D)

    loc_f1 = [jnp.stack([t(ffn1_w_gate[l]), t(ffn1_w_up[l]), ffn1_w_down[l]]).astype(BF16) for l in range(DEPTH)]
    loc_f2 = [jnp.stack([t(ffn2_w_gate[l]), t(ffn2_w_up[l]), ffn2_w_down[l]]).astype(BF16) for l in range(DEPTH)]
    loc_in = [t(w_in[l])[None].astype(BF16) for l in range(DEPTH)]
    loc_out = [w_out[l][None].astype(BF16) for l in range(DEPTH)]

    (wf1,) = _alone(_GatherTask([loc_f1[0]]), "gather_first")
    wf1 = full(wf1)
    xc = x.reshape(s, D)
    saved = []
    for l in range(DEPTH):
        x0 = xc
        if l == 0:
            (x1, *act1), ((wf2, win, wout),) = _ffn_fwd(
                x0, row(ffn1_norm[l]), wf1, 0, f"ffn1_fwd_{l}", comm=[_GatherTask([loc_f2[0], loc_in[0], loc_out[0]])])
        else:
            (x1, *act1), ((wf2,),) = _ffn_fwd(
                x0, row(ffn1_norm[l]), wf1, 0, f"ffn1_fwd_{l}", comm=[_GatherTask([loc_f2[1]])])
        wf2, win, wout = full(wf2), full(win), full(wout)
        z, hmix = _mixin_fwd(x1, row(mix_norm[l]), win, 0, f"mixin_fwd_{l}")
        wcat = jnp.concatenate([gmlp_w_s[l][h] for h in range(4)], axis=1)
        bexp = jnp.repeat(t(gmlp_b[l]), HD, axis=1)
        pwbd = jnp.zeros((256, 256), F32)
        for g in range(4):
            pwbd = pwbd.at[g * HD:(g + 1) * HD, g * HD:(g + 1) * HD].set(pool_w[l][g])
        mixp = (attn_sinks[l], row(gmlp_v_norm[l]), wcat, bexp, pwbd, row(pool_scale[l]))
        y, lse = _mix_fwd(z, *mixp, f"mix_fwd_{l}")
        keep = (x0, act1, wf1, x1, z, hmix, mixp, y, lse, win, wout)
        if l == 0:
            (xc, *act2, x2), ((wf1, win, wout),) = _ffn_fwd(
                x1, row(ffn2_norm[l]), wf2, 0, f"ffn2_fwd_{l}", mixer=(y, wout),
                comm=[_GatherTask([loc_f1[1], loc_in[1], loc_out[1]])])
            wf1 = full(wf1)
        else:
            dx, *act2, x2, loss_part, d_final = _ffn_fwd(
                x1, row(ffn2_norm[l]), wf2, 0, f"ffn2_fwd_{l}", mixer=(y, wout),
                loss=(row(final_norm), loss_target.reshape(s, D)))
        saved.append(keep + (x2, wf2, act2))

    def five(g):
        return g.reshape(g.shape[0], 4, 2, g.shape[1] // N_DEV, D)

    def core_sums(g5s, r1s, tag):
        res = [_core_sum(ids, g5, r1, f"core_sum_{tag}_{i}") for i, (g5, r1) in enumerate(zip(g5s, r1s))]
        return [sb for sb, _ in res], [own for _, own in res]

    def mix_small(l, dsink, dvn, dws, dbt, dpw, dps):
        return {("attn_sinks", l): dsink[:, 0], ("gmlp_v_norm", l): dvn[0],
                ("gmlp_w_s", l): jnp.stack([dws[:, h * BLK:(h + 1) * BLK] for h in range(4)]),
                ("gmlp_b", l): t(dbt[:, :4]),
                ("pool_w", l): jnp.stack([dpw[g * HD:(g + 1) * HD, g * HD:(g + 1) * HD] for g in range(4)]),
                ("pool_scale", l): dps[0]}

    small = {}
    gsrc = {}

    def reduced(l, names, owns, recvs):
        slabs = [(own, recv, k) for own, recv in zip(owns, recvs) for k in range(own.shape[0])]
        for nm, src in zip(names, slabs):
            gsrc[(nm, l)] = src

    ffn1_names = ["ffn1_w_gate", "ffn1_w_up", "ffn1_w_down"]
    ffn2_names = ["ffn2_w_gate", "ffn2_w_up", "ffn2_w_down"]
    gate_up = dict(cols=2 * FF, slab_rows=FF)
    x0, (p11, p21, hid1), wf1, x1, z, hmix, mixp, y, lse, win, wout, x2, wf2, (p12, p22, hid2) = saved[1]
    dx, dab, h, dyb, dg, dymix, dxb = _ffn_bwd(x2, row(ffn2_norm[1]), dx, p12, p22, wf2, 0, "ffn2_bwd_1", w_out=wout)
    small[("ffn2_norm", 1)] = dg[0]
    g = _wgrad(dab, h, None, 3, 0, "wgrad_gate_up2_1", **gate_up)
    g = _wgrad(hid2, dyb, g, 3, 2, "wgrad_down2_1")
    a5 = [five(g)]
    g_out, (a_r1,) = _wgrad(y, dxb, None, 1, 0, "wgrad_out_1", comm=[_SiblingTask(a5)])
    a_sb, a_own = core_sums(a5, a_r1, "a")
    (dz, dsink, dvn, dws, dbt, dpw, dps), (a_r2,) = _mix_bwd(z, dymix, lse, *mixp, "mix_bwd_1", comm=[_ChipTask(a_sb)])
    reduced(1, ffn2_names, a_own, a_r2)
    small.update(mix_small(1, dsink, dvn, dws, dbt, dpw, dps))
    g_in = _wgrad(dz, hmix, None, 1, 0, "wgrad_in_1")
    b5 = [five(g_out), five(g_in)]
    (dx, dab, h, dyb, dg, dgm), (b_r1,) = _ffn_bwd(x0, row(ffn1_norm[1]), dx, p11, p21, wf1, 0, "ffn1_bwd_1",
                                                   comm=[_SiblingTask(b5)], mixin=(row(mix_norm[1]), dz, win, x1))
    small[("ffn1_norm", 1)], small[("mix_norm", 1)] = dg[0], dgm[0]
    b_sb, b_own = core_sums(b5, b_r1, "b")
    g_gu, (b_r2,) = _wgrad(dab, h, None, 2, 0, "wgrad_gate_up1_1", comm=[_ChipTask(b_sb)], **gate_up)
    reduced(1, ["w_out", "w_in"], b_own, b_r2)
    g_down = _wgrad(hid1, dyb, None, 1, 0, "wgrad_down1_1")
    c5 = [five(g_gu), five(g_down)]
    x0, (p11, p21, hid1), wf1, x1, z, hmix, mixp, y, lse, win, wout, x2, wf2, (p12, p22, hid2) = saved[0]
    (dx, dab, h, dyb, dg, dymix, dxb), (c_r1,) = _ffn_bwd(x2, row(ffn2_norm[0]), dx, p12, p22, wf2, 0, "ffn2_bwd_0",
                                                          comm=[_SiblingTask(c5)], w_out=wout)
    small[("ffn2_norm", 0)] = dg[0]
    c_sb, c_own = core_sums(c5, c_r1, "c")
    g_gu, (c_r2a,) = _wgrad(dab, h, None, 2, 0, "wgrad_gate_up2_0", comm=[_ChipTask(c_sb[0:1])], **gate_up)
    d5a = [five(g_gu)]
    g_down, (c_r2b, d_r1a) = _wgrad(hid2, dyb, None, 1, 0, "wgrad_down2_0",
                                    comm=[_ChipTask(c_sb[1:2]), _SiblingTask(d5a)])
    reduced(1, ffn1_names, c_own, c_r2a + c_r2b)
    da_sb, da_own = core_sums(d5a, d_r1a, "da")
    d5b = [five(g_down)]
    (dz, dsink, dvn, dws, dbt, dpw, dps), (d_r2a, d_r1b) = _mix_bwd(
        z, dymix, lse, *mixp, "mix_bwd_0", comm=[_ChipTask(da_sb), _SiblingTask(d5b)])
    small.update(mix_small(0, dsink, dvn, dws, dbt, dpw, dps))
    db_sb, db_own = core_sums(d5b, d_r1b, "db")
    dx, dab, h, dyb, dg, dgm = _ffn_bwd(x0, row(ffn1_norm[0]), dx, p11, p21, wf1, 0, "ffn1_bwd_0",
                                        mixin=(row(mix_norm[0]), dz, win, x1))
    small[("ffn1_norm", 0)], small[("mix_norm", 0)] = dg[0], dgm[0]
    grad_x = dx.reshape(1, s, D)

    part = [d_final[0] if nm == "final_norm" else jnp.stack([small[(nm, l)] for l in range(DEPTH)]) for nm in SMALL]
    packed = _pack_small(part, loss_part[0, 0])
    g_gate, ((gathered,),) = _wgrad(dab, h, None, 1, 0, "wgrad_gate1_0", cols=FF, comm=[_GatherTask([packed[None]])])
    f5 = [five(g_gate)]
    g_up, (d_r2b, f_r1) = _wgrad(dab, h, None, 1, 0, "wgrad_up1_0", col0=FF, cols=FF,
                                 comm=[_ChipTask(db_sb), _SiblingTask(f5)])
    reduced(0, ffn2_names, da_own + db_own, d_r2a + d_r2b)
    f_sb, f_own = core_sums(f5, f_r1, "f")
    u5 = [five(g_up)]
    g_down, (f_r2, u_r1) = _wgrad(hid1, dyb, None, 1, 0, "wgrad_down1_0", comm=[_ChipTask(f_sb), _SiblingTask(u5)])
    u_sb, u_own = core_sums(u5, u_r1, "u")
    w5 = [five(g_down)]
    g_in, (u_r2, w_r1) = _wgrad(dz, hmix, None, 1, 0, "wgrad_in_0", comm=[_ChipTask(u_sb), _SiblingTask(w5)])
    w_sb, w_own = core_sums(w5, w_r1, "w")
    i5 = [five(g_in)]
    g_out, (w_r2, i_r1) = _wgrad(y, dxb, None, 1, 0, "wgrad_out_0", comm=[_ChipTask(w_sb), _SiblingTask(i5)])
    reduced(0, ffn1_names, f_own + u_own + w_own, f_r2 + u_r2 + w_r2)
    i_sb, i_own = core_sums(i5, i_r1, "i")
    o5 = [five(g_out)]
    o_r1 = _alone(_SiblingTask(o5), "reduce_sibling_last")
    o_sb, o_own = core_sums(o5, o_r1, "o")
    e_r2 = _alone(_ChipTask(o_sb + i_sb), "reduce_chips_last")
    reduced(0, ["w_out", "w_in"], o_own + i_own, e_r2)

    grads = {}
    transposed = ("ffn1_w_gate", "ffn1_w_up", "ffn2_w_gate", "ffn2_w_up", "w_in")

    small_w = dict(ffn1_norm=ffn1_norm, mix_norm=mix_norm, attn_sinks=attn_sinks, gmlp_v_norm=gmlp_v_norm,
                   gmlp_w_s=gmlp_w_s, gmlp_b=gmlp_b, pool_w=pool_w, pool_scale=pool_scale, ffn2_norm=ffn2_norm,
                   final_norm=final_norm)
    small_m = dict(ffn1_norm=m_ffn1_norm, mix_norm=m_mix_norm, attn_sinks=m_attn_sinks, gmlp_v_norm=m_gmlp_v_norm,
                   gmlp_w_s=m_gmlp_w_s, gmlp_b=m_gmlp_b, pool_w=m_pool_w, pool_scale=m_pool_scale,
                   ffn2_norm=m_ffn2_norm, final_norm=m_final_norm)
    small_v = dict(ffn1_norm=v_ffn1_norm, mix_norm=v_mix_norm, attn_sinks=v_attn_sinks, gmlp_v_norm=v_gmlp_v_norm,
                   gmlp_w_s=v_gmlp_w_s, gmlp_b=v_gmlp_b, pool_w=v_pool_w, pool_scale=v_pool_scale,
                   ffn2_norm=v_ffn2_norm, final_norm=v_final_norm)
    sg, sd, sm, sv = _adamw_small(gathered[0], _pack_small([small_w[nm] for nm in SMALL]),
                                  _pack_small([small_m[nm] for nm in SMALL]),
                                  _pack_small([small_v[nm] for nm in SMALL]), "adamw_small")
    like = [small_w[nm] for nm in SMALL]
    sg_l, loss = _unpack_small(sg, like)
    sd_l, _ = _unpack_small(sd, like)
    sm_l, _ = _unpack_small(sm, like)
    sv_l, _ = _unpack_small(sv, like)
    deltas, new_m, new_v = {}, {}, {}
    for i, nm in enumerate(SMALL):
        grads[nm], deltas[nm], new_m[nm], new_v[nm] = sg_l[i], sd_l[i], sm_l[i], sv_l[i]

    big_w = dict(ffn1_w_gate=ffn1_w_gate, ffn1_w_up=ffn1_w_up, ffn1_w_down=ffn1_w_down, w_in=w_in, w_out=w_out,
                 ffn2_w_gate=ffn2_w_gate, ffn2_w_up=ffn2_w_up, ffn2_w_down=ffn2_w_down)
    big_m = dict(ffn1_w_gate=m_ffn1_w_gate, ffn1_w_up=m_ffn1_w_up, ffn1_w_down=m_ffn1_w_down, w_in=m_w_in,
                 w_out=m_w_out, ffn2_w_gate=m_ffn2_w_gate, ffn2_w_up=m_ffn2_w_up, ffn2_w_down=m_ffn2_w_down)
    big_v = dict(ffn1_w_gate=v_ffn1_w_gate, ffn1_w_up=v_ffn1_w_up, ffn1_w_down=v_ffn1_w_down, w_in=v_w_in,
                 w_out=v_w_out, ffn2_w_gate=v_ffn2_w_gate, ffn2_w_up=v_ffn2_w_up, ffn2_w_down=v_ffn2_w_down)
    for nm in big_w:
        view = t if nm in transposed else (lambda a: a)
        res = _adamw(others, view(big_w[nm]), view(big_m[nm]), view(big_v[nm]),
                     [gsrc[(nm, l)] for l in range(DEPTH)], f"adamw_{nm}")
        grads[nm], deltas[nm], new_m[nm], new_v[nm] = [view(r) for r in res]

    order = ["ffn1_norm", "ffn1_w_gate", "ffn1_w_up", "ffn1_w_down", "mix_norm", "w_in", "attn_sinks", "gmlp_v_norm",
             "gmlp_w_s", "gmlp_b", "pool_w", "pool_scale", "w_out", "ffn2_norm", "ffn2_w_gate", "ffn2_w_up",
             "ffn2_w_down", "final_norm"]
    return (loss, grad_x, *[grads[n] for n in order], *[deltas[n] for n in order],
            *[new_m[n] for n in order], *[new_v[n] for n in order])
```

```python
import functools
import math

import jax
import jax.numpy as jnp
from jax import lax
from jax.experimental import pallas as pl
from jax.experimental.pallas import tpu as pltpu

F32 = jnp.float32
BF16 = jnp.bfloat16
MESH = pl.DeviceIdType.MESH

D = 1024
FF = 2816
INW = 1536
N_DEV = 8
DEPTH = 2
BLK = 128
HD = 64
N_HEADS = 8
N_KV = 2
REP = 4
ATTN_SCALE = HD ** -0.5
POOL_WINDOWS = (2, 4, 8, 16)
EPS = 1e-6
NEG = -1e30
FC = 256
GELU_C0 = math.sqrt(2.0 / math.pi)
GELU_C1 = 0.044715

ADAM_LR = 0.001
ADAM_B1 = 0.9
ADAM_B2 = 0.999
ADAM_EPS = 1e-08
ADAM_WD = 0.01
ADAM_STEP = 10

VMEM_LIMIT = 60 * 1024 * 1024

O_K, O_V, O_U, O_G, O_P = 512, 640, 768, 1024, 1280


def _call(body, **kw):
    return pl.pallas_call(body, **kw)


def _params(sem=None, vmem=VMEM_LIMIT):
    return pltpu.CompilerParams(dimension_semantics=sem, vmem_limit_bytes=vmem)


def _host(comm, body, *, name, grid, in_specs, out_specs, out_shape, args, scratch_shapes=(), aliases=None):
    single = not isinstance(out_shape, (list, tuple))
    out_specs_l = [out_specs] if single else list(out_specs)
    out_shape_l = [out_shape] if single else list(out_shape)
    n_in, n_out, n_scr = len(in_specs), len(out_shape_l), len(scratch_shapes)
    steps = grid[0]
    any_spec = pl.BlockSpec(memory_space=pl.ANY)

    def wrapped(*refs):
        pos = 0

        def take(n):
            nonlocal pos
            part = refs[pos:pos + n]
            pos += n
            return part

        ins = take(n_in)
        cins = [take(len(t.inputs)) for t in comm]
        outs = take(n_out)
        couts = [take(len(t.out_shape)) for t in comm]
        scr = take(n_scr)
        cscr = [take(len(t.scratch)) for t in comm]
        i = pl.program_id(0)
        for k, t in enumerate(comm):
            pl.when(i == 0)(functools.partial(t.start, cins[k], couts[k], cscr[k]))
        body(*ins, *outs, *scr)
        for k, t in enumerate(comm):
            pl.when(i == (3 * steps) // 4)(functools.partial(t.mid, cins[k], couts[k], cscr[k]))
            pl.when(i == steps - 1)(functools.partial(t.finish, cins[k], couts[k], cscr[k]))

    c_args = [a for t in comm for a in t.inputs]
    c_shapes = [sh for t in comm for sh in t.out_shape]
    c_scr = [sc for t in comm for sc in t.scratch]
    res = _call(
        wrapped, name=name, grid=grid,
        in_specs=list(in_specs) + [any_spec] * len(c_args),
        out_specs=out_specs_l + [any_spec] * len(c_shapes),
        out_shape=out_shape_l + c_shapes,
        scratch_shapes=list(scratch_shapes) + c_scr,
        input_output_aliases=aliases or {},
        compiler_params=_params(("arbitrary",)),
    )(*args, *c_args)
    outs = res[0] if single else list(res[:n_out])
    if not comm:
        return outs
    c_outs, pos = [], n_out
    for t in comm:
        c_outs.append(list(res[pos:pos + len(t.out_shape)]))
        pos += len(t.out_shape)
    return outs, c_outs


def _nn(a, b):
    return lax.dot_general(a, b, (((1,), (0,)), ((), ())), preferred_element_type=F32)


def _nt(a, b):
    return lax.dot_general(a, b, (((1,), (1,)), ((), ())), preferred_element_type=F32)


def _tn(a, b):
    return lax.dot_general(a, b, (((0,), (0,)), ((), ())), preferred_element_type=F32)


def _gelu(x):
    x2 = x * x
    t = jnp.tanh(x * (GELU_C0 + (GELU_C0 * GELU_C1) * x2))
    hx = 0.5 * x
    return hx + hx * t, (hx, x2, t)


def _gelu_grad(parts):
    hx, x2, t = parts
    return (0.5 + 0.5 * t) + (hx * (1.0 - t * t)) * (GELU_C0 + (3.0 * GELU_C0 * GELU_C1) * x2)


def _rms(x):
    r = lax.rsqrt(jnp.mean(x * x, axis=-1, keepdims=True) + EPS)
    return x * r, r


def _rms_bwd(dy, xh, r, g):
    dg = jnp.sum(dy * xh, axis=0, keepdims=True)
    dxh = dy * g
    dx = r * (dxh - xh * jnp.mean(dxh * xh, axis=-1, keepdims=True))
    return dx, dg


def _wspec(rows, m):
    return pl.BlockSpec((None, rows, D), lambda i, m=m: (m, 0, 0), pipeline_mode=pl.Buffered(1))


def _rowspec(tm, cols):
    return pl.BlockSpec((tm, cols), lambda i: (i, 0))


def _fixspec(rows, cols):
    return pl.BlockSpec((rows, cols), lambda i: (0, 0))


def _ffn_fwd(x, gain, w352, mg, name, comm=(), mixer=None, loss=None):
    s = x.shape[0]
    tm = min(512, s)
    n_in = 5 + (2 if mixer is not None else 0) + (2 if loss is not None else 0)

    def body(*refs):
        x_ref, g_ref, wg_ref, wu_ref, wd_ref = refs[:5]
        more_in, outs = list(refs[5:n_in]), list(refs[n_in:])
        xo_ref, p1_ref, p2_ref, hid_ref = outs[:4]
        more_out = outs[4:]
        xt = x_ref[...]
        if mixer is not None:
            y_ref, wo_ref = more_in[:2]
            xt = xt + _nn(y_ref[...], wo_ref[...])
            more_out.pop(0)[...] = xt
        xh, _ = _rms(xt)
        h = (xh * g_ref[...]).astype(BF16)
        for c in range(FF // FC):
            sl = slice(c * FC, (c + 1) * FC)
            a = _nt(h, wg_ref[sl, :])
            b = _nt(h, wu_ref[sl, :])
            sig = 0.5 * jnp.tanh(0.5 * a) + 0.5
            sa = a * sig
            p1_ref[:, sl] = (b * (sig + sa * (1.0 - sig))).astype(BF16)
            p2_ref[:, sl] = sa.astype(BF16)
            hid_ref[:, sl] = (sa * b).astype(BF16)
        xo = xt + 0.5 * _nn(hid_ref[...], wd_ref[...])
        if loss is None:
            xo_ref[...] = xo
        else:
            gf_ref, t_ref = more_in[-2:]
            loss_ref, dgf_ref = more_out
            gf = gf_ref[...]
            xh, r = _rms(xo)
            err = xh * gf - t_ref[...]
            lp = 0.5 * jnp.sum(jnp.mean(err * err, axis=-1, keepdims=True), axis=0, keepdims=True)
            xo_ref[...], dgf = _rms_bwd(err * (1.0 / D), xh, r, gf)

            @pl.when(pl.program_id(0) == 0)
            def _():
                dgf_ref[...] = jnp.zeros_like(dgf_ref)
                loss_ref[...] = jnp.zeros_like(loss_ref)

            dgf_ref[0:1, :] += dgf
            loss_ref[0:1, :] += lp + jnp.zeros((1, 128), F32)

    act = jax.ShapeDtypeStruct((s, FF), BF16)
    tok = jax.ShapeDtypeStruct((s, D), F32)
    in_specs = [_rowspec(tm, D), _fixspec(1, D), _wspec(FF, mg), _wspec(FF, mg + 1), _wspec(FF, mg + 2)]
    out_specs = [_rowspec(tm, D), _rowspec(tm, FF), _rowspec(tm, FF), _rowspec(tm, FF)]
    out_shape = [tok, act, act, act]
    args = (x, gain, w352, w352, w352)
    if mixer is not None:
        in_specs += [_rowspec(tm, D), _wspec(D, 0)]
        out_specs += [_rowspec(tm, D)]
        out_shape += [tok]
        args += tuple(mixer)
    if loss is not None:
        in_specs += [_fixspec(1, D), _rowspec(tm, D)]
        out_specs += [_fixspec(8, 128), _fixspec(8, D)]
        out_shape += [jax.ShapeDtypeStruct((8, 128), F32), jax.ShapeDtypeStruct((8, D), F32)]
        args += tuple(loss)
    return _host(comm, body, name=name, grid=(s // tm,), in_specs=in_specs, out_specs=out_specs,
                 out_shape=out_shape, args=args)


def _ffn_bwd(x, gain, dy, p1, p2, w352, mg, name, comm=(), w_out=None, mixin=None):
    s = x.shape[0]
    tm = min(256, s)
    n_in = 8 + (1 if w_out is not None else 0) + (4 if mixin is not None else 0)

    def body(*refs):
        x_ref, g_ref, dy_ref, p1_ref, p2_ref, wg_ref, wu_ref, wd_ref = refs[:8]
        more_in, outs = list(refs[8:n_in]), list(refs[n_in:])
        dx_ref, dab_ref, h_ref, dyb_ref, dg_ref = outs[:5]
        more_out = outs[5:]
        i = pl.program_id(0)
        xt = x_ref[...]
        g = g_ref[...]
        xh, r = _rms(xt)
        h_ref[...] = (xh * g).astype(BF16)
        dyt = dy_ref[...]
        if mixin is not None:
            gm_ref, dz_ref, win_ref, x1_ref = more_in[-4:]
            dgm_ref = more_out[-1]
            xh1, r1 = _rms(x1_ref[...])
            dxm, dgm = _rms_bwd(_nn(dz_ref[...], win_ref[...]), xh1, r1, gm_ref[...])
            dyt = dyt + dxm

            @pl.when(i == 0)
            def _():
                dgm_ref[...] = jnp.zeros_like(dgm_ref)

            dgm_ref[0:1, :] += dgm
        dyb = (0.5 * dyt).astype(BF16)
        dyb_ref[...] = dyb
        for c in range(FF // FC):
            sl = slice(c * FC, (c + 1) * FC)
            dhid = _nt(dyb, wd_ref[sl, :])
            dab_ref[:, sl] = (dhid * p1_ref[:, sl].astype(F32)).astype(BF16)
            dab_ref[:, FF + c * FC:FF + (c + 1) * FC] = (dhid * p2_ref[:, sl].astype(F32)).astype(BF16)
        dh = _nn(dab_ref[:, :FF], wg_ref[...]) + _nn(dab_ref[:, FF:], wu_ref[...])
        dxn, dg = _rms_bwd(dh, xh, r, g)
        dx = dyt + dxn
        dx_ref[...] = dx
        if w_out is not None:
            dym_ref, dxb_ref = more_out[:2]
            dxb = dx.astype(BF16)
            dxb_ref[...] = dxb
            dym_ref[...] = _nt(dxb, more_in[0][...])

        @pl.when(i == 0)
        def _():
            dg_ref[...] = jnp.zeros_like(dg_ref)

        dg_ref[0:1, :] += dg

    tok = jax.ShapeDtypeStruct((s, D), BF16)
    tok32 = jax.ShapeDtypeStruct((s, D), F32)
    gain_grad = jax.ShapeDtypeStruct((8, D), F32)
    in_specs = [_rowspec(tm, D), _fixspec(1, D), _rowspec(tm, D), _rowspec(tm, FF), _rowspec(tm, FF),
                _wspec(FF, mg), _wspec(FF, mg + 1), _wspec(FF, mg + 2)]
    out_specs = [_rowspec(tm, D), _rowspec(tm, 2 * FF), _rowspec(tm, D), _rowspec(tm, D), _fixspec(8, D)]
    out_shape = [tok32, jax.ShapeDtypeStruct((s, 2 * FF), BF16), tok, tok, gain_grad]
    args = (x, gain, dy, p1, p2, w352, w352, w352)
    if w_out is not None:
        in_specs += [_wspec(D, 0)]
        out_specs += [_rowspec(tm, D), _rowspec(tm, D)]
        out_shape += [tok32, tok]
        args += (w_out,)
    if mixin is not None:
        in_specs += [_fixspec(1, D), _rowspec(tm, INW), _wspec(INW, 0), _rowspec(tm, D)]
        out_specs += [_fixspec(8, D)]
        out_shape += [gain_grad]
        args += tuple(mixin)
    return _host(comm, body, name=name, grid=(s // tm,), in_specs=in_specs, out_specs=out_specs,
                 out_shape=out_shape, args=args)


def _wgrad(a, b, g, n_slabs, m, name, comm=(), col0=0, cols=None, slab_rows=None):
    s = a.shape[0]
    cols = a.shape[1] if cols is None else cols
    slab_rows = cols if slab_rows is None else slab_rows
    mb = 256
    per_slab = slab_rows // mb

    def body(*refs):
        refs[-1][...] = _tn(refs[0][...], refs[1][...])

    in_specs = [pl.BlockSpec((s, mb), lambda i: (0, col0 // mb + i)),
                pl.BlockSpec((s, D), lambda i: (0, 0), pipeline_mode=pl.Buffered(1))]
    args = [a, b]
    aliases = {}
    if g is not None:
        in_specs.append(pl.BlockSpec(memory_space=pl.ANY))
        args.append(g)
        aliases = {2: 0}
    return _host(
        comm, body, name=name, grid=(cols // mb,),
        in_specs=in_specs,
        out_specs=pl.BlockSpec((None, mb, D), lambda i: (m + i // per_slab, i % per_slab, 0)),
        out_shape=jax.ShapeDtypeStruct((n_slabs, slab_rows, D), F32),
        aliases=aliases, args=args)


def _mixin_fwd(x, gain, w192, l, name):
    s = x.shape[0]
    tm = min(512, s)

    def body(x_ref, g_ref, w_ref, z_ref, h_ref):
        xh, _ = _rms(x_ref[...])
        h = (xh * g_ref[...]).astype(BF16)
        h_ref[...] = h
        z_ref[...] = _nt(h, w_ref[...])

    return _call(
        body, name=name, grid=(s // tm,),
        in_specs=[_rowspec(tm, D), _fixspec(1, D), _wspec(INW, l)],
        out_specs=[_rowspec(tm, INW), _rowspec(tm, D)],
        out_shape=[jax.ShapeDtypeStruct((s, INW), F32), jax.ShapeDtypeStruct((s, D), BF16)],
        compiler_params=_params(("arbitrary",)),
    )(x, gain, w192)


MIX_NB = 4
TILE = MIX_NB * BLK
GROUP_ROWS = REP * BLK


class _Block:
    def __init__(self, n, j, zc_ref, zkvp_ref, zpp_ref):
        self.zc, self.zkvp, self.zpp = zc_ref, zkvp_ref, zpp_ref
        self.first = j == 0
        self.r = slice(j * BLK, (j + 1) * BLK)
        self.rp = slice((j - 1) * BLK, j * BLK)
        self.index = n * MIX_NB + j
        self.lo = jnp.where(n > 0, 0, BLK) if self.first else 0
        self.has_prev = jnp.where(n > 0, 1.0, 0.0) if self.first else 1.0

    def cols(self, c0, c1):
        return self.zc[self.r, c0:c1]

    def prev_kv(self, c0, c1):
        return self.zkvp[:, c0:c1] if self.first else self.zc[self.rp, O_K + c0:O_K + c1]

    def prev_p(self):
        return self.zpp[...] * self.has_prev if self.first else self.zc[self.rp, O_P:INW]


def _band_mask(rows):
    row = lax.broadcasted_iota(jnp.int32, (rows, 2 * BLK), 0) & (BLK - 1)
    col = lax.broadcasted_iota(jnp.int32, (rows, 2 * BLK), 1)
    return (col > row) & (col <= row + BLK)


def _block_mask(band, blk):
    if not blk.first:
        return band
    return band & (lax.broadcasted_iota(jnp.int32, band.shape, 1) >= blk.lo)


def _lane_head(shape):
    return lax.broadcasted_iota(jnp.int32, shape, 1) // HD


def _lane_group_select(vals):
    grp = _lane_head(vals[0].shape)
    return jnp.where(grp == 0, vals[0], jnp.where(grp == 1, vals[1], jnp.where(grp == 2, vals[2], vals[3])))


def _pool_count(index):
    row = lax.broadcasted_iota(jnp.int32, (BLK, 256), 0)
    pos1 = (index * BLK + row + 1).astype(F32)
    wl = _lane_group_select([jnp.full((BLK, 256), float(w), F32) for w in POOL_WINDOWS])
    return jnp.minimum(pos1, wl)


def _window_sums(e, forward):
    tot = e.shape[0]
    lv = e
    out = []
    for sh in (1, 2, 4, 8):
        lv = lv + pltpu.roll(lv, sh if forward else tot - sh, 0)
        out.append(lv)
    return _lane_group_select(out)


def _stack_heads(get, g):
    return jnp.concatenate([get((g * REP + rr) * HD, (g * REP + rr + 1) * HD) for rr in range(REP)], axis=0)


def _sink_column(sink_ref, g):
    return jnp.concatenate([jnp.full((BLK, 1), sink_ref[g * REP + rr], F32) for rr in range(REP)], axis=0)


def _kv_window(blk, g):
    kk = jnp.concatenate([blk.prev_kv(g * HD, (g + 1) * HD),
                          blk.cols(O_K + g * HD, O_K + (g + 1) * HD)], axis=0).astype(BF16)
    vv = jnp.concatenate([blk.prev_kv(BLK + g * HD, BLK + (g + 1) * HD),
                          blk.cols(O_V + g * HD, O_V + (g + 1) * HD)], axis=0).astype(BF16)
    return kk, vv


def _mix_common(blk, vn_ref, wcat_ref, bexp_ref, pwbd_ref):
    u, tu = _gelu(blk.cols(O_U, O_G))
    gv, tv = _gelu(blk.cols(O_G, O_P))
    xh, rv = _rms(gv)
    vnb = (xh * vn_ref[...]).astype(BF16)
    head = _lane_head((BLK, 256))
    vn_bd = jnp.concatenate([jnp.where(head == h, vnb, jnp.zeros_like(vnb)) for h in range(4)], axis=0)
    row = lax.broadcasted_iota(jnp.int32, (BLK, 4 * BLK), 0)
    col = lax.broadcasted_iota(jnp.int32, (BLK, 4 * BLK), 1) & (BLK - 1)
    tril = col <= row
    wcat = jnp.where(tril, wcat_ref[...], 0.0).astype(BF16)
    f = _nn(wcat, vn_bd) + bexp_ref[...]
    p = blk.cols(O_P, INW)
    e = jnp.concatenate([blk.prev_p(), p], axis=0)
    cnt = _pool_count(blk.index)
    diff = (_window_sums(e, True)[BLK:, :] / cnt - p).astype(BF16)
    pwbd = pwbd_ref[...].astype(BF16)
    pout = _nn(diff, pwbd)
    return dict(u=u, tu=tu, tv=tv, xh=xh, rv=rv, vn_bd=vn_bd, wcat=wcat, f=f, cnt=cnt, diff=diff, pwbd=pwbd,
                pout=pout, tril=tril, head=head)


def _mix_fwd(z, sinks, vnorm, wcat, bexp, pwbd, pscale, name):
    s = z.shape[0]
    nt = s // TILE

    def body(sink_ref, zc_ref, zkvp_ref, zpp_ref, vn_ref, wcat_ref, bexp_ref, pwbd_ref, ps_ref, y_ref, lse_ref):
        n = pl.program_id(0)
        lse_ref[...] = jnp.zeros_like(lse_ref)
        band = _band_mask(BLK)
        for j in range(MIX_NB):
            blk = _Block(n, j, zc_ref, zkvp_ref, zpp_ref)
            valid = _block_mask(band, blk)
            for g in range(N_KV):
                kk, vv = _kv_window(blk, g)
                for rr in range(REP):
                    h = g * REP + rr
                    qh = (blk.cols(h * HD, (h + 1) * HD) * ATTN_SCALE).astype(BF16)
                    sc = jnp.where(valid, _nt(qh, kk), NEG)
                    sink = sink_ref[h]
                    m = jnp.maximum(jnp.max(sc, axis=-1, keepdims=True), sink)
                    ex = jnp.exp(sc - m)
                    den = jnp.sum(ex, axis=-1, keepdims=True) + jnp.exp(sink - m)
                    y_ref[blk.r, h * HD:(h + 1) * HD] = _nn((ex / den).astype(BF16), vv).astype(BF16)
                    lse_ref[blk.r, h:h + 1] = m + jnp.log(den)
            c = _mix_common(blk, vn_ref, wcat_ref, bexp_ref, pwbd_ref)
            y_ref[blk.r, 512:768] = (c["u"] * c["f"]).astype(BF16)
            y_ref[blk.r, 768:1024] = (c["pout"] * ps_ref[...]).astype(BF16)

    halo = lambda n: jnp.maximum(MIX_NB * n - 1, 0)
    return _call(
        body, name=name, grid=(nt,),
        in_specs=[pl.BlockSpec(memory_space=pltpu.SMEM),
                  pl.BlockSpec((TILE, INW), lambda n: (n, 0)),
                  pl.BlockSpec((BLK, 256), lambda n: (halo(n), 2)),
                  pl.BlockSpec((BLK, 256), lambda n: (halo(n), 5)),
                  _fixspec(1, 256), _fixspec(BLK, 4 * BLK), _fixspec(BLK, 256), _fixspec(256, 256), _fixspec(1, 256)],
        out_specs=[pl.BlockSpec((TILE, D), lambda n: (n, 0)), pl.BlockSpec((TILE, 128), lambda n: (n, 0))],
        out_shape=[jax.ShapeDtypeStruct((s, D), BF16), jax.ShapeDtypeStruct((s, 128), F32)],
        compiler_params=_params(("arbitrary",)),
    )(sinks, z, z, z, vnorm, wcat, bexp, pwbd, pscale)


def _mix_bwd(z, dy, lse, sinks, vnorm, wcat, bexp, pwbd, pscale, name, comm=()):
    s = z.shape[0]
    nt = s // TILE
    last = slice(TILE - BLK, TILE)

    def body(sink_ref, zc_ref, zkvp_ref, zpp_ref, dy_ref, lse_ref, vn_ref, wcat_ref, bexp_ref, pwbd_ref, ps_ref,
             dz_ref, dsink_ref, dvn_ref, dws_ref, dbt_ref, dpw_ref, dps_ref, carry_ref, ddc_ref):
        n = pl.program_id(0)

        @pl.when(n == 0)
        def _():
            carry_ref[...] = jnp.zeros_like(carry_ref)
            ddc_ref[...] = jnp.zeros_like(ddc_ref)
            dsink_ref[...] = jnp.zeros_like(dsink_ref)
            dvn_ref[...] = jnp.zeros_like(dvn_ref)
            dws_ref[...] = jnp.zeros_like(dws_ref)
            dbt_ref[...] = jnp.zeros_like(dbt_ref)
            dpw_ref[...] = jnp.zeros_like(dpw_ref)
            dps_ref[...] = jnp.zeros_like(dps_ref)

        def block_grads(j):
            blk = _Block(n, j, zc_ref, zkvp_ref, zpp_ref)
            valid = _block_mask(_band_mask(GROUP_ROWS), blk)
            out = dict(dq=[], dsink=[], dbt=[])
            dk_prev, dk_cur, dv_prev, dv_cur = [], [], [], []
            for g in range(N_KV):
                kk, vv = _kv_window(blk, g)
                q4 = _stack_heads(blk.cols, g).astype(BF16)
                do4 = _stack_heads(lambda c0, c1: dy_ref[blk.r, c0:c1], g).astype(BF16)
                lse4 = jnp.concatenate([lse_ref[blk.r, g * REP + rr:g * REP + rr + 1] for rr in range(REP)], axis=0)
                sc = jnp.where(valid, _nt(q4, kk) * ATTN_SCALE, NEG)
                pr = jnp.exp(sc - lse4)
                dp = _nt(do4, vv)
                delta = jnp.sum(pr * dp, axis=-1, keepdims=True)
                ds = ((pr * (dp - delta)) * ATTN_SCALE).astype(BF16)
                sunk = jnp.exp(_sink_column(sink_ref, g) - lse4) * delta
                dq4 = _nn(ds, kk)
                for rr in range(REP):
                    out["dsink"].append(-jnp.sum(sunk[rr * BLK:(rr + 1) * BLK], axis=0, keepdims=True))
                    out["dq"].append(dq4[rr * BLK:(rr + 1) * BLK])
                dkk = _tn(ds, q4)
                dvv = _tn(pr.astype(BF16), do4)
                dk_prev.append(dkk[:BLK]); dk_cur.append(dkk[BLK:])
                dv_prev.append(dvv[:BLK]); dv_cur.append(dvv[BLK:])
            out["dk_prev"], out["dk_cur"] = jnp.concatenate(dk_prev, axis=1), jnp.concatenate(dk_cur, axis=1)
            out["dv_prev"], out["dv_cur"] = jnp.concatenate(dv_prev, axis=1), jnp.concatenate(dv_cur, axis=1)
            c = _mix_common(blk, vn_ref, wcat_ref, bexp_ref, pwbd_ref)
            dyg = dy_ref[blk.r, 512:768]
            du = dyg * c["f"]
            df = dyg * c["u"]
            out["dzu"] = du * _gelu_grad(c["tu"])
            dfb = df.astype(BF16)
            for h in range(4):
                out["dbt"].append(jnp.sum(df[:, h * HD:(h + 1) * HD], axis=1, keepdims=True))
            out["dws"] = jnp.where(c["tril"], _nt(dfb, c["vn_bd"]), 0.0)
            dvn_bd = _tn(c["wcat"], dfb)
            dvn = functools.reduce(lambda a, b: a + b, [
                jnp.where(c["head"] == h, dvn_bd[h * BLK:(h + 1) * BLK], 0.0) for h in range(4)])
            dgv, out["dvn"] = _rms_bwd(dvn, c["xh"], c["rv"], vn_ref[...])
            out["dzv"] = dgv * _gelu_grad(c["tv"])
            dyp = dy_ref[blk.r, 768:1024]
            out["dps"] = jnp.sum(dyp * c["pout"], axis=0, keepdims=True)
            dout = (dyp * ps_ref[...]).astype(BF16)
            out["dpw"] = _tn(c["diff"], dout)
            out["ddiff"] = _nt(dout, c["pwbd"])
            out["dd"] = out["ddiff"] / c["cnt"]
            return out

        def write_previous_tile(dd_next, dk_next, dv_next):
            if MIX_NB > 1:
                dz_ref[0:TILE - BLK, :] = carry_ref[0:TILE - BLK, :].astype(BF16)
            rs = _window_sums(jnp.concatenate([ddc_ref[...], dd_next], axis=0), False)
            dz_ref[last, 0:O_K] = carry_ref[last, 0:O_K].astype(BF16)
            dz_ref[last, O_K:O_V] = (carry_ref[last, O_K:O_V] + dk_next).astype(BF16)
            dz_ref[last, O_V:O_U] = (carry_ref[last, O_V:O_U] + dv_next).astype(BF16)
            dz_ref[last, O_U:O_P] = carry_ref[last, O_U:O_P].astype(BF16)
            dz_ref[last, O_P:INW] = (carry_ref[last, O_P:INW] + rs[:BLK, :]).astype(BF16)

        @pl.when(n < nt)
        def _():
            parts = [block_grads(j) for j in range(MIX_NB)]
            total = lambda key, i=None: functools.reduce(
                lambda a, b: a + b, [p[key] if i is None else p[key][i] for p in parts])
            for h in range(N_HEADS):
                dsink_ref[h:h + 1, :] += total("dsink", h) + jnp.zeros((1, 128), F32)
            for h in range(4):
                dbt_ref[:, h:h + 1] += total("dbt", h)
            dws_ref[...] += total("dws")
            dpw_ref[...] += total("dpw")
            dvn_ref[0:1, :] += total("dvn")
            dps_ref[0:1, :] += total("dps")
            write_previous_tile(parts[0]["dd"], parts[0]["dk_prev"], parts[0]["dv_prev"])
            for j, p in enumerate(parts):
                r = slice(j * BLK, (j + 1) * BLK)
                nxt = parts[j + 1] if j + 1 < MIX_NB else None
                for h in range(N_HEADS):
                    carry_ref[r, h * HD:(h + 1) * HD] = p["dq"][h]
                carry_ref[r, O_U:O_G] = p["dzu"]
                carry_ref[r, O_G:O_P] = p["dzv"]
                if nxt is None:
                    carry_ref[r, O_K:O_V] = p["dk_cur"]
                    carry_ref[r, O_V:O_U] = p["dv_cur"]
                    carry_ref[r, O_P:INW] = -p["ddiff"]
                    ddc_ref[...] = p["dd"]
                else:
                    rs = _window_sums(jnp.concatenate([p["dd"], nxt["dd"]], axis=0), False)
                    carry_ref[r, O_K:O_V] = p["dk_cur"] + nxt["dk_prev"]
                    carry_ref[r, O_V:O_U] = p["dv_cur"] + nxt["dv_prev"]
                    carry_ref[r, O_P:INW] = rs[:BLK, :] - p["ddiff"]

        @pl.when(n == nt)
        def _():
            none = jnp.zeros((BLK, BLK), F32)
            write_previous_tile(jnp.zeros((BLK, 256), F32), none, none)

    cur = lambda n: jnp.minimum(n, nt - 1)
    done = lambda n: jnp.maximum(n - 1, 0)
    halo = lambda n: jnp.maximum(MIX_NB * jnp.minimum(n, nt - 1) - 1, 0)
    return _host(
        comm, body, name=name, grid=(nt + 1,),
        in_specs=[pl.BlockSpec(memory_space=pltpu.SMEM),
                  pl.BlockSpec((TILE, INW), lambda n: (cur(n), 0)),
                  pl.BlockSpec((BLK, 256), lambda n: (halo(n), 2)),
                  pl.BlockSpec((BLK, 256), lambda n: (halo(n), 5)),
                  pl.BlockSpec((TILE, D), lambda n: (cur(n), 0)),
                  pl.BlockSpec((TILE, 128), lambda n: (cur(n), 0)),
                  _fixspec(1, 256), _fixspec(BLK, 4 * BLK), _fixspec(BLK, 256), _fixspec(256, 256), _fixspec(1, 256)],
        out_specs=[pl.BlockSpec((TILE, INW), lambda n: (done(n), 0)),
                   _fixspec(8, 128), _fixspec(8, 256), _fixspec(BLK, 4 * BLK), _fixspec(BLK, 128),
                   _fixspec(256, 256), _fixspec(8, 256)],
        out_shape=[jax.ShapeDtypeStruct((s, INW), BF16), jax.ShapeDtypeStruct((8, 128), F32),
                   jax.ShapeDtypeStruct((8, 256), F32), jax.ShapeDtypeStruct((BLK, 4 * BLK), F32),
                   jax.ShapeDtypeStruct((BLK, 128), F32), jax.ShapeDtypeStruct((256, 256), F32),
                   jax.ShapeDtypeStruct((8, 256), F32)],
        scratch_shapes=[pltpu.VMEM((TILE, INW), F32), pltpu.VMEM((BLK, 256), F32)],
        args=(sinks, z, z, z, dy, lse, vnorm, wcat, bexp, pwbd, pscale))


def _position():
    x, y, c = lax.axis_index("x"), lax.axis_index("y"), lax.axis_index("c")
    return x, y, c


class _GatherTask:
    def __init__(self, srcs):
        self.inputs = list(srcs)
        ng = len(srcs)
        self.out_shape = [jax.ShapeDtypeStruct((a.shape[0], N_DEV) + a.shape[1:], a.dtype) for a in srcs]
        self.scratch = [pltpu.SemaphoreType.DMA((ng, 7)), pltpu.SemaphoreType.DMA((ng, 7)),
                        pltpu.SemaphoreType.DMA((ng,))]

    def _plan(self, src, dst, sems):
        send_sems, recv_sems, local_sems = sems
        ng = len(src)
        x, y, c = _position()
        me, sibling = (x, y, c), (x, y, 1 - c)
        chips = [(1 - x, y), (x, 1 - y), (1 - x, 1 - y)]

        def slot(pos):
            return 4 * pos[0] + 2 * pos[1] + pos[2]

        def copy(gi, k, block, to, from_src=False):
            rows = dst[gi].at[:, slot(block)]
            return pltpu.make_async_remote_copy(
                src_ref=src[gi] if from_src else rows, dst_ref=rows,
                send_sem=send_sems.at[gi, k], recv_sem=recv_sems.at[gi, k],
                device_id=to, device_id_type=MESH)

        make = functools.partial
        mine = [make(pltpu.make_async_copy, src[gi], dst[gi].at[:, slot(me)], local_sems.at[gi]) for gi in range(ng)]
        first = []
        for gi in range(ng):
            first.append(make(copy, gi, 0, me, sibling, True))
            first += [make(copy, gi, 1 + j, me, (*chip, c), True) for j, chip in enumerate(chips)]
        passed = [make(copy, gi, 4 + j, (*chip, c), sibling) for j, chip in enumerate(chips) for gi in range(ng)]
        arrive_ici = [make(copy, gi, 1 + j, (*chip, c), me) for j, chip in enumerate(chips) for gi in range(ng)]
        arrive_d2d = [make(copy, gi, 0, sibling, me) for gi in range(ng)]
        arrive_d2d += [make(copy, gi, 4 + j, (*chip, 1 - c), me) for j, chip in enumerate(chips) for gi in range(ng)]
        return mine, first, passed, arrive_ici, arrive_d2d

    def start(self, src, dst, sems):
        mine, first, _, _, _ = self._plan(src, dst, sems)
        for cp in mine + first:
            cp().start()

    def mid(self, src, dst, sems):
        _, _, passed, arrive_ici, _ = self._plan(src, dst, sems)
        for arrived, fw in zip(arrive_ici, passed):
            arrived().wait_recv()
            fw().start()

    def finish(self, src, dst, sems):
        mine, first, passed, _, arrive_d2d = self._plan(src, dst, sems)
        for cp in arrive_d2d:
            cp().wait_recv()
        for cp in first + passed:
            cp().wait_send()
        for cp in mine:
            cp().wait()


class _SiblingTask:
    def __init__(self, g5s):
        self.inputs = list(g5s)
        ng = len(g5s)
        self.out_shape = [jax.ShapeDtypeStruct((a.shape[0], 4) + a.shape[3:], a.dtype) for a in g5s]
        self.scratch = [pltpu.SemaphoreType.DMA((ng,)), pltpu.SemaphoreType.DMA((ng,))]

    def _plan(self, src, dst, sems):
        send_sems, recv_sems = sems
        x, y, c = _position()
        return [functools.partial(
            pltpu.make_async_remote_copy,
            src_ref=src[gi].at[:, :, 1 - c], dst_ref=dst[gi],
            send_sem=send_sems.at[gi], recv_sem=recv_sems.at[gi],
            device_id=(x, y, 1 - c), device_id_type=MESH) for gi in range(len(src))]

    def start(self, src, dst, sems):
        for cp in self._plan(src, dst, sems):
            cp().start()

    def mid(self, src, dst, sems):
        pass

    def finish(self, src, dst, sems):
        for cp in self._plan(src, dst, sems):
            cp().wait()


class _ChipTask(_SiblingTask):
    def __init__(self, sbs):
        self.inputs = list(sbs)
        ng = len(sbs)
        self.out_shape = [jax.ShapeDtypeStruct(a.shape, a.dtype) for a in sbs]
        self.scratch = [pltpu.SemaphoreType.DMA((ng, 3)), pltpu.SemaphoreType.DMA((ng, 3))]

    def _plan(self, src, dst, sems):
        send_sems, recv_sems = sems
        x, y, c = _position()
        jme = 2 * x + y
        chips = [(1 - x, y), (x, 1 - y), (1 - x, 1 - y)]
        return [functools.partial(
            pltpu.make_async_remote_copy,
            src_ref=src[gi].at[:, 2 * chip[0] + chip[1]], dst_ref=dst[gi].at[:, jme],
            send_sem=send_sems.at[gi, k], recv_sem=recv_sems.at[gi, k],
            device_id=(*chip, c), device_id_type=MESH) for k, chip in enumerate(chips) for gi in range(len(src))]


def _alone(task, name):
    n_in, n_out = len(task.inputs), len(task.out_shape)

    def body(*refs):
        parts = (refs[:n_in], refs[n_in:n_in + n_out], refs[n_in + n_out:])
        task.start(*parts)
        task.mid(*parts)
        task.finish(*parts)

    any_spec = pl.BlockSpec(memory_space=pl.ANY)
    return _call(body, name=name, in_specs=[any_spec] * n_in, out_specs=[any_spec] * n_out,
                 out_shape=task.out_shape, scratch_shapes=task.scratch)(*task.inputs)


def _core_sum(ids, g5, r1, name):
    n, _, _, rows, _ = g5.shape

    def body(ids_ref, g_ref, r_ref, sb_ref, own_ref):
        j = pl.program_id(2)
        t = g_ref[...] + r_ref[...]
        sb_ref[...] = t.astype(BF16)

        @pl.when(j == ids_ref[1])
        def _():
            own_ref[...] = t

    grid_spec = pltpu.PrefetchScalarGridSpec(
        num_scalar_prefetch=1, grid=(n, 1, 4),
        in_specs=[pl.BlockSpec((None, None, None, rows, D), lambda i, t, j, ids: (i, j, ids[0], t, 0)),
                  pl.BlockSpec((None, None, rows, D), lambda i, t, j, ids: (i, j, t, 0))],
        out_specs=[pl.BlockSpec((None, None, rows, D), lambda i, t, j, ids: (i, j, t, 0)),
                   pl.BlockSpec((None, rows, D), lambda i, t, j, ids: (i, t, 0))])
    return _call(
        body, name=name, grid_spec=grid_spec,
        out_shape=[jax.ShapeDtypeStruct((n, 4, rows, D), BF16), jax.ShapeDtypeStruct((n, rows, D), F32)],
        compiler_params=_params(("arbitrary", "arbitrary", "arbitrary")),
    )(ids, g5, r1)


def _adam_math(w, g, m, v):
    m = ADAM_B1 * m + (1.0 - ADAM_B1) * g
    v = ADAM_B2 * v + (1.0 - ADAM_B2) * (g * g)
    m_hat = m / (1.0 - ADAM_B1 ** ADAM_STEP)
    v_hat = v / (1.0 - ADAM_B2 ** ADAM_STEP)
    delta = -ADAM_LR * (m_hat / (jnp.sqrt(v_hat) + ADAM_EPS) + ADAM_WD * w)
    return delta, m, v


def _adamw(others, w, m, v, gparts, name):
    _, r, c = w.shape

    def body(oth_ref, w_ref, m_ref, v_ref, *rest):
        srcs, (g_ref, d_ref, mo_ref, vo_ref) = rest[:4 * DEPTH], rest[4 * DEPTH:]

        def run(own_ref, r0_ref, r1_ref, r2_ref):
            g = ((own_ref[...] + r0_ref[...].astype(F32)) + r1_ref[...].astype(F32)) + r2_ref[...].astype(F32)
            g_ref[...] = g
            d_ref[...], mo_ref[...], vo_ref[...] = _adam_math(w_ref[...], g, m_ref[...], v_ref[...])

        for l in range(DEPTH):
            pl.when(pl.program_id(0) == l)(functools.partial(run, *srcs[4 * l:4 * l + 4]))

    rb = r // 2 if r % 32 == 0 else r
    spec = pl.BlockSpec((None, rb, c), lambda l, i, oth: (l, i, 0))
    g_specs, g_args = [], []
    for own, recv, k in gparts:
        g_specs.append(pl.BlockSpec((None, rb, c), lambda l, i, oth, k=k: (k, i, 0)))
        g_args.append(own)
        for j in range(3):
            g_specs.append(pl.BlockSpec((None, None, rb, c), lambda l, i, oth, k=k, j=j: (k, oth[j], i, 0)))
            g_args.append(recv)
    grid_spec = pltpu.PrefetchScalarGridSpec(
        num_scalar_prefetch=1, grid=(DEPTH, r // rb), in_specs=[spec] * 3 + g_specs, out_specs=[spec] * 4)
    return _call(
        body, name=name, grid_spec=grid_spec,
        out_shape=[jax.ShapeDtypeStruct(w.shape, F32)] * 4,
        compiler_params=_params(("arbitrary", "arbitrary")),
    )(others, w, m, v, *g_args)


def _adamw_small(parts, w, m, v, name):
    def body(p_ref, w_ref, m_ref, v_ref, g_ref, d_ref, mo_ref, vo_ref):
        g = p_ref[0]
        for dev in range(1, N_DEV):
            g = g + p_ref[dev]
        g_ref[...] = g
        d_ref[...], mo_ref[...], vo_ref[...] = _adam_math(w_ref[...], g, m_ref[...], v_ref[...])

    return _call(
        body, name=name,
        out_shape=[jax.ShapeDtypeStruct(w.shape, F32)] * 4,
        compiler_params=_params(),
    )(parts, w, m, v)


SMALL = ["ffn1_norm", "mix_norm", "attn_sinks", "gmlp_v_norm", "gmlp_w_s", "gmlp_b", "pool_w", "pool_scale",
         "ffn2_norm", "final_norm"]


def _piece_rows(size):
    return -(-size // 1024) * 8


def _pack_small(arrs, extra=None):
    pieces = []
    for a in list(arrs) + [jnp.zeros((1,), F32) if extra is None else extra]:
        fill = _piece_rows(a.size) * 128 - a.size
        f = a.reshape(-1)
        pieces.append((jnp.pad(f, (0, fill)) if fill else f).reshape(-1, 128))
    return jnp.concatenate(pieces, axis=0)


def _unpack_small(packed, like):
    out, off = [], 0
    for a in like:
        piece = packed[off:off + -(-a.size // 128)]
        if a.size % 128:
            piece = piece.reshape(-1)[:a.size]
        out.append(piece.reshape(a.shape))
        off += _piece_rows(a.size)
    return out, packed[off, 0]


def kernel(x, ffn1_norm, ffn1_w_gate, ffn1_w_up, ffn1_w_down, mix_norm, w_in, attn_sinks, gmlp_v_norm, gmlp_w_s, gmlp_b, pool_w, pool_scale, w_out, ffn2_norm, ffn2_w_gate, ffn2_w_up, ffn2_w_down, final_norm, loss_target, m_ffn1_norm, m_ffn1_w_gate, m_ffn1_w_up, m_ffn1_w_down, m_mix_norm, m_w_in, m_attn_sinks, m_gmlp_v_norm, m_gmlp_w_s, m_gmlp_b, m_pool_w, m_pool_scale, m_w_out, m_ffn2_norm, m_ffn2_w_gate, m_ffn2_w_up, m_ffn2_w_down, m_final_norm, v_ffn1_norm, v_ffn1_w_gate, v_ffn1_w_up, v_ffn1_w_down, v_mix_norm, v_w_in, v_attn_sinks, v_gmlp_v_norm, v_gmlp_w_s, v_gmlp_b, v_pool_w, v_pool_scale, v_w_out, v_ffn2_norm, v_ffn2_w_gate, v_ffn2_w_up, v_ffn2_w_down, v_final_norm):
    s = x.shape[1]
    xi, yi, ci = _position()
    ids = jnp.stack([ci, 2 * xi + yi]).astype(jnp.int32)
    jme = 2 * xi + yi
    others = jnp.stack([k + (k >= jme).astype(jnp.int32) for k in range(3)]).astype(jnp.int32)
    t = lambda a: jnp.swapaxes(a, -1, -2)
    row = lambda a: a.reshape(1, -1)
    full = lambda a: a.reshape(a.shape[0], -1, D)

    loc_f1 = [jnp.stack([t(ffn1_w_gate[l]), t(ffn1_w_up[l]), ffn1_w_down[l]]).astype(BF16) for l in range(DEPTH)]
    loc_f2 = [jnp.stack([t(ffn2_w_gate[l]), t(ffn2_w_up[l]), ffn2_w_down[l]]).astype(BF16) for l in range(DEPTH)]
    loc_in = [t(w_in[l])[None].astype(BF16) for l in range(DEPTH)]
    loc_out = [w_out[l][None].astype(BF16) for l in range(DEPTH)]

    (wf1,) = _alone(_GatherTask([loc_f1[0]]), "gather_first")
    wf1 = full(wf1)
    xc = x.reshape(s, D)
    saved = []
    for l in range(DEPTH):
        x0 = xc
        if l == 0:
            (x1, *act1), ((wf2, win, wout),) = _ffn_fwd(
                x0, row(ffn1_norm[l]), wf1, 0, f"ffn1_fwd_{l}", comm=[_GatherTask([loc_f2[0], loc_in[0], loc_out[0]])])
        else:
            (x1, *act1), ((wf2,),) = _ffn_fwd(
                x0, row(ffn1_norm[l]), wf1, 0, f"ffn1_fwd_{l}", comm=[_GatherTask([loc_f2[1]])])
        wf2, win, wout = full(wf2), full(win), full(wout)
        z, hmix = _mixin_fwd(x1, row(mix_norm[l]), win, 0, f"mixin_fwd_{l}")
        wcat = jnp.concatenate([gmlp_w_s[l][h] for h in range(4)], axis=1)
        bexp = jnp.repeat(t(gmlp_b[l]), HD, axis=1)
        pwbd = jnp.zeros((256, 256), F32)
        for g in range(4):
            pwbd = pwbd.at[g * HD:(g + 1) * HD, g * HD:(g + 1) * HD].set(pool_w[l][g])
        mixp = (attn_sinks[l], row(gmlp_v_norm[l]), wcat, bexp, pwbd, row(pool_scale[l]))
        y, lse = _mix_fwd(z, *mixp, f"mix_fwd_{l}")
        keep = (x0, act1, wf1, x1, z, hmix, mixp, y, lse, win, wout)
        if l == 0:
            (xc, *act2, x2), ((wf1, win, wout),) = _ffn_fwd(
                x1, row(ffn2_norm[l]), wf2, 0, f"ffn2_fwd_{l}", mixer=(y, wout),
                comm=[_GatherTask([loc_f1[1], loc_in[1], loc_out[1]])])
            wf1 = full(wf1)
        else:
            dx, *act2, x2, loss_part, d_final = _ffn_fwd(
                x1, row(ffn2_norm[l]), wf2, 0, f"ffn2_fwd_{l}", mixer=(y, wout),
                loss=(row(final_norm), loss_target.reshape(s, D)))
        saved.append(keep + (x2, wf2, act2))

    def five(g):
        return g.reshape(g.shape[0], 4, 2, g.shape[1] // N_DEV, D)

    def core_sums(g5s, r1s, tag):
        res = [_core_sum(ids, g5, r1, f"core_sum_{tag}_{i}") for i, (g5, r1) in enumerate(zip(g5s, r1s))]
        return [sb for sb, _ in res], [own for _, own in res]

    def mix_small(l, dsink, dvn, dws, dbt, dpw, dps):
        return {("attn_sinks", l): dsink[:, 0], ("gmlp_v_norm", l): dvn[0],
                ("gmlp_w_s", l): jnp.stack([dws[:, h * BLK:(h + 1) * BLK] for h in range(4)]),
                ("gmlp_b", l): t(dbt[:, :4]),
                ("pool_w", l): jnp.stack([dpw[g * HD:(g + 1) * HD, g * HD:(g + 1) * HD] for g in range(4)]),
                ("pool_scale", l): dps[0]}

    small = {}
    gsrc = {}

    def reduced(l, names, owns, recvs):
        slabs = [(own, recv, k) for own, recv in zip(owns, recvs) for k in range(own.shape[0])]
        for nm, src in zip(names, slabs):
            gsrc[(nm, l)] = src

    ffn1_names = ["ffn1_w_gate", "ffn1_w_up", "ffn1_w_down"]
    ffn2_names = ["ffn2_w_gate", "ffn2_w_up", "ffn2_w_down"]
    gate_up = dict(cols=2 * FF, slab_rows=FF)
    x0, (p11, p21, hid1), wf1, x1, z, hmix, mixp, y, lse, win, wout, x2, wf2, (p12, p22, hid2) = saved[1]
    dx, dab, h, dyb, dg, dymix, dxb = _ffn_bwd(x2, row(ffn2_norm[1]), dx, p12, p22, wf2, 0, "ffn2_bwd_1", w_out=wout)
    small[("ffn2_norm", 1)] = dg[0]
    g = _wgrad(dab, h, None, 3, 0, "wgrad_gate_up2_1", **gate_up)
    g = _wgrad(hid2, dyb, g, 3, 2, "wgrad_down2_1")
    a5 = [five(g)]
    g_out, (a_r1,) = _wgrad(y, dxb, None, 1, 0, "wgrad_out_1", comm=[_SiblingTask(a5)])
    a_sb, a_own = core_sums(a5, a_r1, "a")
    (dz, dsink, dvn, dws, dbt, dpw, dps), (a_r2,) = _mix_bwd(z, dymix, lse, *mixp, "mix_bwd_1", comm=[_ChipTask(a_sb)])
    reduced(1, ffn2_names, a_own, a_r2)
    small.update(mix_small(1, dsink, dvn, dws, dbt, dpw, dps))
    g_in = _wgrad(dz, hmix, None, 1, 0, "wgrad_in_1")
    b5 = [five(g_out), five(g_in)]
    (dx, dab, h, dyb, dg, dgm), (b_r1,) = _ffn_bwd(x0, row(ffn1_norm[1]), dx, p11, p21, wf1, 0, "ffn1_bwd_1",
                                                   comm=[_SiblingTask(b5)], mixin=(row(mix_norm[1]), dz, win, x1))
    small[("ffn1_norm", 1)], small[("mix_norm", 1)] = dg[0], dgm[0]
    b_sb, b_own = core_sums(b5, b_r1, "b")
    g_gu, (b_r2,) = _wgrad(dab, h, None, 2, 0, "wgrad_gate_up1_1", comm=[_ChipTask(b_sb)], **gate_up)
    reduced(1, ["w_out", "w_in"], b_own, b_r2)
    g_down = _wgrad(hid1, dyb, None, 1, 0, "wgrad_down1_1")
    c5 = [five(g_gu), five(g_down)]
    x0, (p11, p21, hid1), wf1, x1, z, hmix, mixp, y, lse, win, wout, x2, wf2, (p12, p22, hid2) = saved[0]
    (dx, dab, h, dyb, dg, dymix, dxb), (c_r1,) = _ffn_bwd(x2, row(ffn2_norm[0]), dx, p12, p22, wf2, 0, "ffn2_bwd_0",
                                                          comm=[_SiblingTask(c5)], w_out=wout)
    small[("ffn2_norm", 0)] = dg[0]
    c_sb, c_own = core_sums(c5, c_r1, "c")
    g_gu, (c_r2a,) = _wgrad(dab, h, None, 2, 0, "wgrad_gate_up2_0", comm=[_ChipTask(c_sb[0:1])], **gate_up)
    d5a = [five(g_gu)]
    g_down, (c_r2b, d_r1a) = _wgrad(hid2, dyb, None, 1, 0, "wgrad_down2_0",
                                    comm=[_ChipTask(c_sb[1:2]), _SiblingTask(d5a)])
    reduced(1, ffn1_names, c_own, c_r2a + c_r2b)
    da_sb, da_own = core_sums(d5a, d_r1a, "da")
    d5b = [five(g_down)]
    (dz, dsink, dvn, dws, dbt, dpw, dps), (d_r2a, d_r1b) = _mix_bwd(
        z, dymix, lse, *mixp, "mix_bwd_0", comm=[_ChipTask(da_sb), _SiblingTask(d5b)])
    small.update(mix_small(0, dsink, dvn, dws, dbt, dpw, dps))
    db_sb, db_own = core_sums(d5b, d_r1b, "db")
    dx, dab, h, dyb, dg, dgm = _ffn_bwd(x0, row(ffn1_norm[0]), dx, p11, p21, wf1, 0, "ffn1_bwd_0",
                                        mixin=(row(mix_norm[0]), dz, win, x1))
    small[("ffn1_norm", 0)], small[("mix_norm", 0)] = dg[0], dgm[0]
    grad_x = dx.reshape(1, s, D)

    part = [d_final[0] if nm == "final_norm" else jnp.stack([small[(nm, l)] for l in range(DEPTH)]) for nm in SMALL]
    packed = _pack_small(part, loss_part[0, 0])
    g_gate, ((gathered,),) = _wgrad(dab, h, None, 1, 0, "wgrad_gate1_0", cols=FF, comm=[_GatherTask([packed[None]])])
    f5 = [five(g_gate)]
    g_up, (d_r2b, f_r1) = _wgrad(dab, h, None, 1, 0, "wgrad_up1_0", col0=FF, cols=FF,
                                 comm=[_ChipTask(db_sb), _SiblingTask(f5)])
    reduced(0, ffn2_names, da_own + db_own, d_r2a + d_r2b)
    f_sb, f_own = core_sums(f5, f_r1, "f")
    u5 = [five(g_up)]
    g_down, (f_r2, u_r1) = _wgrad(hid1, dyb, None, 1, 0, "wgrad_down1_0", comm=[_ChipTask(f_sb), _SiblingTask(u5)])
    u_sb, u_own = core_sums(u5, u_r1, "u")
    w5 = [five(g_down)]
    g_in, (u_r2, w_r1) = _wgrad(dz, hmix, None, 1, 0, "wgrad_in_0", comm=[_ChipTask(u_sb), _SiblingTask(w5)])
    w_sb, w_own = core_sums(w5, w_r1, "w")
    i5 = [five(g_in)]
    g_out, (w_r2, i_r1) = _wgrad(y, dxb, None, 1, 0, "wgrad_out_0", comm=[_ChipTask(w_sb), _SiblingTask(i5)])
    reduced(0, ffn1_names, f_own + u_own + w_own, f_r2 + u_r2 + w_r2)
    i_sb, i_own = core_sums(i5, i_r1, "i")
    o5 = [five(g_out)]
    o_r1 = _alone(_SiblingTask(o5), "reduce_sibling_last")
    o_sb, o_own = core_sums(o5, o_r1, "o")
    e_r2 = _alone(_ChipTask(o_sb + i_sb), "reduce_chips_last")
    reduced(0, ["w_out", "w_in"], o_own + i_own, e_r2)

    grads = {}
    transposed = ("ffn1_w_gate", "ffn1_w_up", "ffn2_w_gate", "ffn2_w_up", "w_in")

    small_w = dict(ffn1_norm=ffn1_norm, mix_norm=mix_norm, attn_sinks=attn_sinks, gmlp_v_norm=gmlp_v_norm,
                   gmlp_w_s=gmlp_w_s, gmlp_b=gmlp_b, pool_w=pool_w, pool_scale=pool_scale, ffn2_norm=ffn2_norm,
                   final_norm=final_norm)
    small_m = dict(ffn1_norm=m_ffn1_norm, mix_norm=m_mix_norm, attn_sinks=m_attn_sinks, gmlp_v_norm=m_gmlp_v_norm,
                   gmlp_w_s=m_gmlp_w_s, gmlp_b=m_gmlp_b, pool_w=m_pool_w, pool_scale=m_pool_scale,
                   ffn2_norm=m_ffn2_norm, final_norm=m_final_norm)
    small_v = dict(ffn1_norm=v_ffn1_norm, mix_norm=v_mix_norm, attn_sinks=v_attn_sinks, gmlp_v_norm=v_gmlp_v_norm,
                   gmlp_w_s=v_gmlp_w_s, gmlp_b=v_gmlp_b, pool_w=v_pool_w, pool_scale=v_pool_scale,
                   ffn2_norm=v_ffn2_norm, final_norm=v_final_norm)
    sg, sd, sm, sv = _adamw_small(gathered[0], _pack_small([small_w[nm] for nm in SMALL]),
                                  _pack_small([small_m[nm] for nm in SMALL]),
                                  _pack_small([small_v[nm] for nm in SMALL]), "adamw_small")
    like = [small_w[nm] for nm in SMALL]
    sg_l, loss = _unpack_small(sg, like)
    sd_l, _ = _unpack_small(sd, like)
    sm_l, _ = _unpack_small(sm, like)
    sv_l, _ = _unpack_small(sv, like)
    deltas, new_m, new_v = {}, {}, {}
    for i, nm in enumerate(SMALL):
        grads[nm], deltas[nm], new_m[nm], new_v[nm] = sg_l[i], sd_l[i], sm_l[i], sv_l[i]

    big_w = dict(ffn1_w_gate=ffn1_w_gate, ffn1_w_up=ffn1_w_up, ffn1_w_down=ffn1_w_down, w_in=w_in, w_out=w_out,
                 ffn2_w_gate=ffn2_w_gate, ffn2_w_up=ffn2_w_up, ffn2_w_down=ffn2_w_down)
    big_m = dict(ffn1_w_gate=m_ffn1_w_gate, ffn1_w_up=m_ffn1_w_up, ffn1_w_down=m_ffn1_w_down, w_in=m_w_in,
                 w_out=m_w_out, ffn2_w_gate=m_ffn2_w_gate, ffn2_w_up=m_ffn2_w_up, ffn2_w_down=m_ffn2_w_down)
    big_v = dict(ffn1_w_gate=v_ffn1_w_gate, ffn1_w_up=v_ffn1_w_up, ffn1_w_down=v_ffn1_w_down, w_in=v_w_in,
                 w_out=v_w_out, ffn2_w_gate=v_ffn2_w_gate, ffn2_w_up=v_ffn2_w_up, ffn2_w_down=v_ffn2_w_down)
    for nm in big_w:
        view = t if nm in transposed else (lambda a: a)
        res = _adamw(others, view(big_w[nm]), view(big_m[nm]), view(big_v[nm]),
                     [gsrc[(nm, l)] for l in range(DEPTH)], f"adamw_{nm}")
        grads[nm], deltas[nm], new_m[nm], new_v[nm] = [view(r) for r in res]

    order = ["ffn1_norm", "ffn1_w_gate", "ffn1_w_up", "ffn1_w_down", "mix_norm", "w_in", "attn_sinks", "gmlp_v_norm",
             "gmlp_w_s", "gmlp_b", "pool_w", "pool_scale", "w_out", "ffn2_norm", "ffn2_w_gate", "ffn2_w_up",
             "ffn2_w_down", "final_norm"]
    return (loss, grad_x, *[grads[n] for n in order], *[deltas[n] for n in order],
            *[new_m[n] for n in order], *[new_v[n] for n in order])
```

```python
import functools
import math

import jax
import jax.numpy as jnp
from jax import lax
from jax.experimental import pallas as pl
from jax.experimental.pallas import tpu as pltpu

F32 = jnp.float32
BF16 = jnp.bfloat16
MESH = pl.DeviceIdType.MESH

D = 1024
FF = 2816
INW = 1536
N_DEV = 8
DEPTH = 2
BLK = 128
HD = 64
N_HEADS = 8
N_KV = 2
REP = 4
ATTN_SCALE = HD ** -0.5
POOL_WINDOWS = (2, 4, 8, 16)
EPS = 1e-6
NEG = -1e30
FC = 256
GELU_C0 = math.sqrt(2.0 / math.pi)
GELU_C1 = 0.044715

ADAM_LR = 0.001
ADAM_B1 = 0.9
ADAM_B2 = 0.999
ADAM_EPS = 1e-08
ADAM_WD = 0.01
ADAM_STEP = 10

VMEM_LIMIT = 60 * 1024 * 1024

O_K, O_V, O_U, O_G, O_P = 512, 640, 768, 1024, 1280


def _call(body, **kw):
    return pl.pallas_call(body, **kw)


def _params(sem=None, vmem=VMEM_LIMIT):
    return pltpu.CompilerParams(dimension_semantics=sem, vmem_limit_bytes=vmem)


def _host(comm, body, *, name, grid, in_specs, out_specs, out_shape, args, scratch_shapes=(), aliases=None):
    single = not isinstance(out_shape, (list, tuple))
    out_specs_l = [out_specs] if single else list(out_specs)
    out_shape_l = [out_shape] if single else list(out_shape)
    n_in, n_out, n_scr = len(in_specs), len(out_shape_l), len(scratch_shapes)
    steps = grid[0]
    any_spec = pl.BlockSpec(memory_space=pl.ANY)

    def wrapped(*refs):
        pos = 0

        def take(n):
            nonlocal pos
            part = refs[pos:pos + n]
            pos += n
            return part

        ins = take(n_in)
        cins = [take(len(t.inputs)) for t in comm]
        outs = take(n_out)
        couts = [take(len(t.out_shape)) for t in comm]
        scr = take(n_scr)
        cscr = [take(len(t.scratch)) for t in comm]
        i = pl.program_id(0)
        for k, t in enumerate(comm):
            pl.when(i == 0)(functools.partial(t.start, cins[k], couts[k], cscr[k]))
        body(*ins, *outs, *scr)
        for k, t in enumerate(comm):
            pl.when(i == (3 * steps) // 4)(functools.partial(t.mid, cins[k], couts[k], cscr[k]))
            pl.when(i == steps - 1)(functools.partial(t.finish, cins[k], couts[k], cscr[k]))

    c_args = [a for t in comm for a in t.inputs]
    c_shapes = [sh for t in comm for sh in t.out_shape]
    c_scr = [sc for t in comm for sc in t.scratch]
    res = _call(
        wrapped, name=name, grid=grid,
        in_specs=list(in_specs) + [any_spec] * len(c_args),
        out_specs=out_specs_l + [any_spec] * len(c_shapes),
        out_shape=out_shape_l + c_shapes,
        scratch_shapes=list(scratch_shapes) + c_scr,
        input_output_aliases=aliases or {},
        compiler_params=_params(("arbitrary",)),
    )(*args, *c_args)
    outs = res[0] if single else list(res[:n_out])
    if not comm:
        return outs
    c_outs, pos = [], n_out
    for t in comm:
        c_outs.append(list(res[pos:pos + len(t.out_shape)]))
        pos += len(t.out_shape)
    return outs, c_outs


def _nn(a, b):
    return lax.dot_general(a, b, (((1,), (0,)), ((), ())), preferred_element_type=F32)


def _nt(a, b):
    return lax.dot_general(a, b, (((1,), (1,)), ((), ())), preferred_element_type=F32)


def _tn(a, b):
    return lax.dot_general(a, b, (((0,), (0,)), ((), ())), preferred_element_type=F32)


def _gelu(x):
    x2 = x * x
    t = jnp.tanh(x * (GELU_C0 + (GELU_C0 * GELU_C1) * x2))
    hx = 0.5 * x
    return hx + hx * t, (hx, x2, t)


def _gelu_grad(parts):
    hx, x2, t = parts
    return (0.5 + 0.5 * t) + (hx * (1.0 - t * t)) * (GELU_C0 + (3.0 * GELU_C0 * GELU_C1) * x2)


def _rms(x):
    r = lax.rsqrt(jnp.mean(x * x, axis=-1, keepdims=True) + EPS)
    return x * r, r


def _rms_bwd(dy, xh, r, g):
    dg = jnp.sum(dy * xh, axis=0, keepdims=True)
    dxh = dy * g
    dx = r * (dxh - xh * jnp.mean(dxh * xh, axis=-1, keepdims=True))
    return dx, dg


def _wspec(rows, m):
    return pl.BlockSpec((None, rows, D), lambda i, m=m: (m, 0, 0), pipeline_mode=pl.Buffered(1))


def _rowspec(tm, cols):
    return pl.BlockSpec((tm, cols), lambda i: (i, 0))


def _fixspec(rows, cols):
    return pl.BlockSpec((rows, cols), lambda i: (0, 0))


def _ffn_fwd(x, gain, w352, mg, name, comm=(), mixer=None, loss=None):
    s = x.shape[0]
    tm = min(512, s)
    n_in = 5 + (2 if mixer is not None else 0) + (2 if loss is not None else 0)

    def body(*refs):
        x_ref, g_ref, wg_ref, wu_ref, wd_ref = refs[:5]
        more_in, outs = list(refs[5:n_in]), list(refs[n_in:])
        xo_ref, p1_ref, p2_ref, hid_ref = outs[:4]
        more_out = outs[4:]
        xt = x_ref[...]
        if mixer is not None:
            y_ref, wo_ref = more_in[:2]
            xt = xt + _nn(y_ref[...], wo_ref[...])
            more_out.pop(0)[...] = xt
        xh, _ = _rms(xt)
        h = (xh * g_ref[...]).astype(BF16)
        for c in range(FF // FC):
            sl = slice(c * FC, (c + 1) * FC)
            a = _nt(h, wg_ref[sl, :])
            b = _nt(h, wu_ref[sl, :])
            sig = 0.5 * jnp.tanh(0.5 * a) + 0.5
            sa = a * sig
            p1_ref[:, sl] = (b * (sig + sa * (1.0 - sig))).astype(BF16)
            p2_ref[:, sl] = sa.astype(BF16)
            hid_ref[:, sl] = (sa * b).astype(BF16)
        xo = xt + 0.5 * _nn(hid_ref[...], wd_ref[...])
        if loss is None:
            xo_ref[...] = xo
        else:
            gf_ref, t_ref = more_in[-2:]
            loss_ref, dgf_ref = more_out
            gf = gf_ref[...]
            xh, r = _rms(xo)
            err = xh * gf - t_ref[...]
            lp = 0.5 * jnp.sum(jnp.mean(err * err, axis=-1, keepdims=True), axis=0, keepdims=True)
            xo_ref[...], dgf = _rms_bwd(err * (1.0 / D), xh, r, gf)

            @pl.when(pl.program_id(0) == 0)
            def _():
                dgf_ref[...] = jnp.zeros_like(dgf_ref)
                loss_ref[...] = jnp.zeros_like(loss_ref)

            dgf_ref[0:1, :] += dgf
            loss_ref[0:1, :] += lp + jnp.zeros((1, 128), F32)

    act = jax.ShapeDtypeStruct((s, FF), BF16)
    tok = jax.ShapeDtypeStruct((s, D), F32)
    in_specs = [_rowspec(tm, D), _fixspec(1, D), _wspec(FF, mg), _wspec(FF, mg + 1), _wspec(FF, mg + 2)]
    out_specs = [_rowspec(tm, D), _rowspec(tm, FF), _rowspec(tm, FF), _rowspec(tm, FF)]
    out_shape = [tok, act, act, act]
    args = (x, gain, w352, w352, w352)
    if mixer is not None:
        in_specs += [_rowspec(tm, D), _wspec(D, 0)]
        out_specs += [_rowspec(tm, D)]
        out_shape += [tok]
        args += tuple(mixer)
    if loss is not None:
        in_specs += [_fixspec(1, D), _rowspec(tm, D)]
        out_specs += [_fixspec(8, 128), _fixspec(8, D)]
        out_shape += [jax.ShapeDtypeStruct((8, 128), F32), jax.ShapeDtypeStruct((8, D), F32)]
        args += tuple(loss)
    return _host(comm, body, name=name, grid=(s // tm,), in_specs=in_specs, out_specs=out_specs,
                 out_shape=out_shape, args=args)


def _ffn_bwd(x, gain, dy, p1, p2, w352, mg, name, comm=(), w_out=None, mixin=None):
    s = x.shape[0]
    tm = min(256, s)
    n_in = 8 + (1 if w_out is not None else 0) + (4 if mixin is not None else 0)

    def body(*refs):
        x_ref, g_ref, dy_ref, p1_ref, p2_ref, wg_ref, wu_ref, wd_ref = refs[:8]
        more_in, outs = list(refs[8:n_in]), list(refs[n_in:])
        dx_ref, dab_ref, h_ref, dyb_ref, dg_ref = outs[:5]
        more_out = outs[5:]
        i = pl.program_id(0)
        xt = x_ref[...]
        g = g_ref[...]
        xh, r = _rms(xt)
        h_ref[...] = (xh * g).astype(BF16)
        dyt = dy_ref[...]
        if mixin is not None:
            gm_ref, dz_ref, win_ref, x1_ref = more_in[-4:]
            dgm_ref = more_out[-1]
            xh1, r1 = _rms(x1_ref[...])
            dxm, dgm = _rms_bwd(_nn(dz_ref[...], win_ref[...]), xh1, r1, gm_ref[...])
            dyt = dyt + dxm

            @pl.when(i == 0)
            def _():
                dgm_ref[...] = jnp.zeros_like(dgm_ref)

            dgm_ref[0:1, :] += dgm
        dyb = (0.5 * dyt).astype(BF16)
        dyb_ref[...] = dyb
        for c in range(FF // FC):
            sl = slice(c * FC, (c + 1) * FC)
            dhid = _nt(dyb, wd_ref[sl, :])
            dab_ref[:, sl] = (dhid * p1_ref[:, sl].astype(F32)).astype(BF16)
            dab_ref[:, FF + c * FC:FF + (c + 1) * FC] = (dhid * p2_ref[:, sl].astype(F32)).astype(BF16)
        dh = _nn(dab_ref[:, :FF], wg_ref[...]) + _nn(dab_ref[:, FF:], wu_ref[...])
        dxn, dg = _rms_bwd(dh, xh, r, g)
        dx = dyt + dxn
        dx_ref[...] = dx
        if w_out is not None:
            dym_ref, dxb_ref = more_out[:2]
            dxb = dx.astype(BF16)
            dxb_ref[...] = dxb
            dym_ref[...] = _nt(dxb, more_in[0][...])

        @pl.when(i == 0)
        def _():
            dg_ref[...] = jnp.zeros_like(dg_ref)

        dg_ref[0:1, :] += dg

    tok = jax.ShapeDtypeStruct((s, D), BF16)
    tok32 = jax.ShapeDtypeStruct((s, D), F32)
    gain_grad = jax.ShapeDtypeStruct((8, D), F32)
    in_specs = [_rowspec(tm, D), _fixspec(1, D), _rowspec(tm, D), _rowspec(tm, FF), _rowspec(tm, FF),
                _wspec(FF, mg), _wspec(FF, mg + 1), _wspec(FF, mg + 2)]
    out_specs = [_rowspec(tm, D), _rowspec(tm, 2 * FF), _rowspec(tm, D), _rowspec(tm, D), _fixspec(8, D)]
    out_shape = [tok32, jax.ShapeDtypeStruct((s, 2 * FF), BF16), tok, tok, gain_grad]
    args = (x, gain, dy, p1, p2, w352, w352, w352)
    if w_out is not None:
        in_specs += [_wspec(D, 0)]
        out_specs += [_rowspec(tm, D), _rowspec(tm, D)]
        out_shape += [tok32, tok]
        args += (w_out,)
    if mixin is not None:
        in_specs += [_fixspec(1, D), _rowspec(tm, INW), _wspec(INW, 0), _rowspec(tm, D)]
        out_specs += [_fixspec(8, D)]
        out_shape += [gain_grad]
        args += tuple(mixin)
    return _host(comm, body, name=name, grid=(s // tm,), in_specs=in_specs, out_specs=out_specs,
                 out_shape=out_shape, args=args)


def _wgrad(a, b, g, n_slabs, m, name, comm=(), col0=0, cols=None, slab_rows=None):
    s = a.shape[0]
    cols = a.shape[1] if cols is None else cols
    slab_rows = cols if slab_rows is None else slab_rows
    mb = 256
    per_slab = slab_rows // mb

    def body(*refs):
        refs[-1][...] = _tn(refs[0][...], refs[1][...])

    in_specs = [pl.BlockSpec((s, mb), lambda i: (0, col0 // mb + i)),
                pl.BlockSpec((s, D), lambda i: (0, 0), pipeline_mode=pl.Buffered(1))]
    args = [a, b]
    aliases = {}
    if g is not None:
        in_specs.append(pl.BlockSpec(memory_space=pl.ANY))
        args.append(g)
        aliases = {2: 0}
    return _host(
        comm, body, name=name, grid=(cols // mb,),
        in_specs=in_specs,
        out_specs=pl.BlockSpec((None, mb, D), lambda i: (m + i // per_slab, i % per_slab, 0)),
        out_shape=jax.ShapeDtypeStruct((n_slabs, slab_rows, D), F32),
        aliases=aliases, args=args)


def _mixin_fwd(x, gain, w192, l, name):
    s = x.shape[0]
    tm = min(512, s)

    def body(x_ref, g_ref, w_ref, z_ref, h_ref):
        xh, _ = _rms(x_ref[...])
        h = (xh * g_ref[...]).astype(BF16)
        h_ref[...] = h
        z_ref[...] = _nt(h, w_ref[...])

    return _call(
        body, name=name, grid=(s // tm,),
        in_specs=[_rowspec(tm, D), _fixspec(1, D), _wspec(INW, l)],
        out_specs=[_rowspec(tm, INW), _rowspec(tm, D)],
        out_shape=[jax.ShapeDtypeStruct((s, INW), F32), jax.ShapeDtypeStruct((s, D), BF16)],
        compiler_params=_params(("arbitrary",)),
    )(x, gain, w192)


MIX_NB = 4
TILE = MIX_NB * BLK
GROUP_ROWS = REP * BLK


class _Block:
    def __init__(self, n, j, zc_ref, zkvp_ref, zpp_ref):
        self.zc, self.zkvp, self.zpp = zc_ref, zkvp_ref, zpp_ref
        self.first = j == 0
        self.r = slice(j * BLK, (j + 1) * BLK)
        self.rp = slice((j - 1) * BLK, j * BLK)
        self.index = n * MIX_NB + j
        self.lo = jnp.where(n > 0, 0, BLK) if self.first else 0
        self.has_prev = jnp.where(n > 0, 1.0, 0.0) if self.first else 1.0

    def cols(self, c0, c1):
        return self.zc[self.r, c0:c1]

    def prev_kv(self, c0, c1):
        return self.zkvp[:, c0:c1] if self.first else self.zc[self.rp, O_K + c0:O_K + c1]

    def prev_p(self):
        return self.zpp[...] * self.has_prev if self.first else self.zc[self.rp, O_P:INW]


def _band_mask(rows):
    row = lax.broadcasted_iota(jnp.int32, (rows, 2 * BLK), 0) & (BLK - 1)
    col = lax.broadcasted_iota(jnp.int32, (rows, 2 * BLK), 1)
    return (col > row) & (col <= row + BLK)


def _block_mask(band, blk):
    if not blk.first:
        return band
    return band & (lax.broadcasted_iota(jnp.int32, band.shape, 1) >= blk.lo)


def _lane_head(shape):
    return lax.broadcasted_iota(jnp.int32, shape, 1) // HD


def _lane_group_select(vals):
    grp = _lane_head(vals[0].shape)
    return jnp.where(grp == 0, vals[0], jnp.where(grp == 1, vals[1], jnp.where(grp == 2, vals[2], vals[3])))


def _pool_count(index):
    row = lax.broadcasted_iota(jnp.int32, (BLK, 256), 0)
    pos1 = (index * BLK + row + 1).astype(F32)
    wl = _lane_group_select([jnp.full((BLK, 256), float(w), F32) for w in POOL_WINDOWS])
    return jnp.minimum(pos1, wl)


def _window_sums(e, forward):
    tot = e.shape[0]
    lv = e
    out = []
    for sh in (1, 2, 4, 8):
        lv = lv + pltpu.roll(lv, sh if forward else tot - sh, 0)
        out.append(lv)
    return _lane_group_select(out)


def _stack_heads(get, g):
    return jnp.concatenate([get((g * REP + rr) * HD, (g * REP + rr + 1) * HD) for rr in range(REP)], axis=0)


def _sink_column(sink_ref, g):
    return jnp.concatenate([jnp.full((BLK, 1), sink_ref[g * REP + rr], F32) for rr in range(REP)], axis=0)


def _kv_window(blk, g):
    kk = jnp.concatenate([blk.prev_kv(g * HD, (g + 1) * HD),
                          blk.cols(O_K + g * HD, O_K + (g + 1) * HD)], axis=0).astype(BF16)
    vv = jnp.concatenate([blk.prev_kv(BLK + g * HD, BLK + (g + 1) * HD),
                          blk.cols(O_V + g * HD, O_V + (g + 1) * HD)], axis=0).astype(BF16)
    return kk, vv


def _mix_common(blk, vn_ref, wcat_ref, bexp_ref, pwbd_ref):
    u, tu = _gelu(blk.cols(O_U, O_G))
    gv, tv = _gelu(blk.cols(O_G, O_P))
    xh, rv = _rms(gv)
    vnb = (xh * vn_ref[...]).astype(BF16)
    head = _lane_head((BLK, 256))
    vn_bd = jnp.concatenate([jnp.where(head == h, vnb, jnp.zeros_like(vnb)) for h in range(4)], axis=0)
    row = lax.broadcasted_iota(jnp.int32, (BLK, 4 * BLK), 0)
    col = lax.broadcasted_iota(jnp.int32, (BLK, 4 * BLK), 1) & (BLK - 1)
    tril = col <= row
    wcat = jnp.where(tril, wcat_ref[...], 0.0).astype(BF16)
    f = _nn(wcat, vn_bd) + bexp_ref[...]
    p = blk.cols(O_P, INW)
    e = jnp.concatenate([blk.prev_p(), p], axis=0)
    cnt = _pool_count(blk.index)
    diff = (_window_sums(e, True)[BLK:, :] / cnt - p).astype(BF16)
    pwbd = pwbd_ref[...].astype(BF16)
    pout = _nn(diff, pwbd)
    return dict(u=u, tu=tu, tv=tv, xh=xh, rv=rv, vn_bd=vn_bd, wcat=wcat, f=f, cnt=cnt, diff=diff, pwbd=pwbd,
                pout=pout, tril=tril, head=head)


def _mix_fwd(z, sinks, vnorm, wcat, bexp, pwbd, pscale, name):
    s = z.shape[0]
    nt = s // TILE

    def body(sink_ref, zc_ref, zkvp_ref, zpp_ref, vn_ref, wcat_ref, bexp_ref, pwbd_ref, ps_ref, y_ref, lse_ref):
        n = pl.program_id(0)
        lse_ref[...] = jnp.zeros_like(lse_ref)
        band = _band_mask(BLK)
        for j in range(MIX_NB):
            blk = _Block(n, j, zc_ref, zkvp_ref, zpp_ref)
            valid = _block_mask(band, blk)
            for g in range(N_KV):
                kk, vv = _kv_window(blk, g)
                for rr in range(REP):
                    h = g * REP + rr
                    qh = (blk.cols(h * HD, (h + 1) * HD) * ATTN_SCALE).astype(BF16)
                    sc = jnp.where(valid, _nt(qh, kk), NEG)
                    sink = sink_ref[h]
                    m = jnp.maximum(jnp.max(sc, axis=-1, keepdims=True), sink)
                    ex = jnp.exp(sc - m)
                    den = jnp.sum(ex, axis=-1, keepdims=True) + jnp.exp(sink - m)
                    y_ref[blk.r, h * HD:(h + 1) * HD] = _nn((ex / den).astype(BF16), vv).astype(BF16)
                    lse_ref[blk.r, h:h + 1] = m + jnp.log(den)
            c = _mix_common(blk, vn_ref, wcat_ref, bexp_ref, pwbd_ref)
            y_ref[blk.r, 512:768] = (c["u"] * c["f"]).astype(BF16)
            y_ref[blk.r, 768:1024] = (c["pout"] * ps_ref[...]).astype(BF16)

    halo = lambda n: jnp.maximum(MIX_NB * n - 1, 0)
    return _call(
        body, name=name, grid=(nt,),
        in_specs=[pl.BlockSpec(memory_space=pltpu.SMEM),
                  pl.BlockSpec((TILE, INW), lambda n: (n, 0)),
                  pl.BlockSpec((BLK, 256), lambda n: (halo(n), 2)),
                  pl.BlockSpec((BLK, 256), lambda n: (halo(n), 5)),
                  _fixspec(1, 256), _fixspec(BLK, 4 * BLK), _fixspec(BLK, 256), _fixspec(256, 256), _fixspec(1, 256)],
        out_specs=[pl.BlockSpec((TILE, D), lambda n: (n, 0)), pl.BlockSpec((TILE, 128), lambda n: (n, 0))],
        out_shape=[jax.ShapeDtypeStruct((s, D), BF16), jax.ShapeDtypeStruct((s, 128), F32)],
        compiler_params=_params(("arbitrary",)),
    )(sinks, z, z, z, vnorm, wcat, bexp, pwbd, pscale)


def _mix_bwd(z, dy, lse, sinks, vnorm, wcat, bexp, pwbd, pscale, name, comm=()):
    s = z.shape[0]
    nt = s // TILE
    last = slice(TILE - BLK, TILE)

    def body(sink_ref, zc_ref, zkvp_ref, zpp_ref, dy_ref, lse_ref, vn_ref, wcat_ref, bexp_ref, pwbd_ref, ps_ref,
             dz_ref, dsink_ref, dvn_ref, dws_ref, dbt_ref, dpw_ref, dps_ref, carry_ref, ddc_ref):
        n = pl.program_id(0)

        @pl.when(n == 0)
        def _():
            carry_ref[...] = jnp.zeros_like(carry_ref)
            ddc_ref[...] = jnp.zeros_like(ddc_ref)
            dsink_ref[...] = jnp.zeros_like(dsink_ref)
            dvn_ref[...] = jnp.zeros_like(dvn_ref)
            dws_ref[...] = jnp.zeros_like(dws_ref)
            dbt_ref[...] = jnp.zeros_like(dbt_ref)
            dpw_ref[...] = jnp.zeros_like(dpw_ref)
            dps_ref[...] = jnp.zeros_like(dps_ref)

        def block_grads(j):
            blk = _Block(n, j, zc_ref, zkvp_ref, zpp_ref)
            valid = _block_mask(_band_mask(GROUP_ROWS), blk)
            out = dict(dq=[], dsink=[], dbt=[])
            dk_prev, dk_cur, dv_prev, dv_cur = [], [], [], []
            for g in range(N_KV):
                kk, vv = _kv_window(blk, g)
                q4 = _stack_heads(blk.cols, g).astype(BF16)
                do4 = _stack_heads(lambda c0, c1: dy_ref[blk.r, c0:c1], g).astype(BF16)
                lse4 = jnp.concatenate([lse_ref[blk.r, g * REP + rr:g * REP + rr + 1] for rr in range(REP)], axis=0)
                sc = jnp.where(valid, _nt(q4, kk) * ATTN_SCALE, NEG)
                pr = jnp.exp(sc - lse4)
                dp = _nt(do4, vv)
                delta = jnp.sum(pr * dp, axis=-1, keepdims=True)
                ds = ((pr * (dp - delta)) * ATTN_SCALE).astype(BF16)
                sunk = jnp.exp(_sink_column(sink_ref, g) - lse4) * delta
                dq4 = _nn(ds, kk)
                for rr in range(REP):
                    out["dsink"].append(-jnp.sum(sunk[rr * BLK:(rr + 1) * BLK], axis=0, keepdims=True))
                    out["dq"].append(dq4[rr * BLK:(rr + 1) * BLK])
                dkk = _tn(ds, q4)
                dvv = _tn(pr.astype(BF16), do4)
                dk_prev.append(dkk[:BLK]); dk_cur.append(dkk[BLK:])
                dv_prev.append(dvv[:BLK]); dv_cur.append(dvv[BLK:])
            out["dk_prev"], out["dk_cur"] = jnp.concatenate(dk_prev, axis=1), jnp.concatenate(dk_cur, axis=1)
            out["dv_prev"], out["dv_cur"] = jnp.concatenate(dv_prev, axis=1), jnp.concatenate(dv_cur, axis=1)
            c = _mix_common(blk, vn_ref, wcat_ref, bexp_ref, pwbd_ref)
            dyg = dy_ref[blk.r, 512:768]
            du = dyg * c["f"]
            df = dyg * c["u"]
            out["dzu"] = du * _gelu_grad(c["tu"])
            dfb = df.astype(BF16)
            for h in range(4):
                out["dbt"].append(jnp.sum(df[:, h * HD:(h + 1) * HD], axis=1, keepdims=True))
            out["dws"] = jnp.where(c["tril"], _nt(dfb, c["vn_bd"]), 0.0)
            dvn_bd = _tn(c["wcat"], dfb)
            dvn = functools.reduce(lambda a, b: a + b, [
                jnp.where(c["head"] == h, dvn_bd[h * BLK:(h + 1) * BLK], 0.0) for h in range(4)])
            dgv, out["dvn"] = _rms_bwd(dvn, c["xh"], c["rv"], vn_ref[...])
            out["dzv"] = dgv * _gelu_grad(c["tv"])
            dyp = dy_ref[blk.r, 768:1024]
            out["dps"] = jnp.sum(dyp * c["pout"], axis=0, keepdims=True)
            dout = (dyp * ps_ref[...]).astype(BF16)
            out["dpw"] = _tn(c["diff"], dout)
            out["ddiff"] = _nt(dout, c["pwbd"])
            out["dd"] = out["ddiff"] / c["cnt"]
            return out

        def write_previous_tile(dd_next, dk_next, dv_next):
            if MIX_NB > 1:
                dz_ref[0:TILE - BLK, :] = carry_ref[0:TILE - BLK, :].astype(BF16)
            rs = _window_sums(jnp.concatenate([ddc_ref[...], dd_next], axis=0), False)
            dz_ref[last, 0:O_K] = carry_ref[last, 0:O_K].astype(BF16)
            dz_ref[last, O_K:O_V] = (carry_ref[last, O_K:O_V] + dk_next).astype(BF16)
            dz_ref[last, O_V:O_U] = (carry_ref[last, O_V:O_U] + dv_next).astype(BF16)
            dz_ref[last, O_U:O_P] = carry_ref[last, O_U:O_P].astype(BF16)
            dz_ref[last, O_P:INW] = (carry_ref[last, O_P:INW] + rs[:BLK, :]).astype(BF16)

        @pl.when(n < nt)
        def _():
            parts = [block_grads(j) for j in range(MIX_NB)]
            total = lambda key, i=None: functools.reduce(
                lambda a, b: a + b, [p[key] if i is None else p[key][i] for p in parts])
            for h in range(N_HEADS):
                dsink_ref[h:h + 1, :] += total("dsink", h) + jnp.zeros((1, 128), F32)
            for h in range(4):
                dbt_ref[:, h:h + 1] += total("dbt", h)
            dws_ref[...] += total("dws")
            dpw_ref[...] += total("dpw")
            dvn_ref[0:1, :] += total("dvn")
            dps_ref[0:1, :] += total("dps")
            write_previous_tile(parts[0]["dd"], parts[0]["dk_prev"], parts[0]["dv_prev"])
            for j, p in enumerate(parts):
                r = slice(j * BLK, (j + 1) * BLK)
                nxt = parts[j + 1] if j + 1 < MIX_NB else None
                for h in range(N_HEADS):
                    carry_ref[r, h * HD:(h + 1) * HD] = p["dq"][h]
                carry_ref[r, O_U:O_G] = p["dzu"]
                carry_ref[r, O_G:O_P] = p["dzv"]
                if nxt is None:
                    carry_ref[r, O_K:O_V] = p["dk_cur"]
                    carry_ref[r, O_V:O_U] = p["dv_cur"]
                    carry_ref[r, O_P:INW] = -p["ddiff"]
                    ddc_ref[...] = p["dd"]
                else:
                    rs = _window_sums(jnp.concatenate([p["dd"], nxt["dd"]], axis=0), False)
                    carry_ref[r, O_K:O_V] = p["dk_cur"] + nxt["dk_prev"]
                    carry_ref[r, O_V:O_U] = p["dv_cur"] + nxt["dv_prev"]
                    carry_ref[r, O_P:INW] = rs[:BLK, :] - p["ddiff"]

        @pl.when(n == nt)
        def _():
            none = jnp.zeros((BLK, BLK), F32)
            write_previous_tile(jnp.zeros((BLK, 256), F32), none, none)

    cur = lambda n: jnp.minimum(n, nt - 1)
    done = lambda n: jnp.maximum(n - 1, 0)
    halo = lambda n: jnp.maximum(MIX_NB * jnp.minimum(n, nt - 1) - 1, 0)
    return _host(
        comm, body, name=name, grid=(nt + 1,),
        in_specs=[pl.BlockSpec(memory_space=pltpu.SMEM),
                  pl.BlockSpec((TILE, INW), lambda n: (cur(n), 0)),
                  pl.BlockSpec((BLK, 256), lambda n: (halo(n), 2)),
                  pl.BlockSpec((BLK, 256), lambda n: (halo(n), 5)),
                  pl.BlockSpec((TILE, D), lambda n: (cur(n), 0)),
                  pl.BlockSpec((TILE, 128), lambda n: (cur(n), 0)),
                  _fixspec(1, 256), _fixspec(BLK, 4 * BLK), _fixspec(BLK, 256), _fixspec(256, 256), _fixspec(1, 256)],
        out_specs=[pl.BlockSpec((TILE, INW), lambda n: (done(n), 0)),
                   _fixspec(8, 128), _fixspec(8, 256), _fixspec(BLK, 4 * BLK), _fixspec(BLK, 128),
                   _fixspec(256, 256), _fixspec(8, 256)],
        out_shape=[jax.ShapeDtypeStruct((s, INW), BF16), jax.ShapeDtypeStruct((8, 128), F32),
                   jax.ShapeDtypeStruct((8, 256), F32), jax.ShapeDtypeStruct((BLK, 4 * BLK), F32),
                   jax.ShapeDtypeStruct((BLK, 128), F32), jax.ShapeDtypeStruct((256, 256), F32),
                   jax.ShapeDtypeStruct((8, 256), F32)],
        scratch_shapes=[pltpu.VMEM((TILE, INW), F32), pltpu.VMEM((BLK, 256), F32)],
        args=(sinks, z, z, z, dy, lse, vnorm, wcat, bexp, pwbd, pscale))


def _position():
    x, y, c = lax.axis_index("x"), lax.axis_index("y"), lax.axis_index("c")
    return x, y, c


class _GatherTask:
    def __init__(self, srcs):
        self.inputs = list(srcs)
        ng = len(srcs)
        self.out_shape = [jax.ShapeDtypeStruct((a.shape[0], N_DEV) + a.shape[1:], a.dtype) for a in srcs]
        self.scratch = [pltpu.SemaphoreType.DMA((ng, 7 * self.HALVES)), pltpu.SemaphoreType.DMA((ng, 7 * self.HALVES)),
                        pltpu.SemaphoreType.DMA((ng,))]

    HALVES = 2

    def _plan(self, src, dst, sems):
        send_sems, recv_sems, local_sems = sems
        ng = len(src)
        x, y, c = _position()
        me, sibling = (x, y, c), (x, y, 1 - c)
        chips = [(1 - x, y), (x, 1 - y), (1 - x, 1 - y)]
        halves = range(self.HALVES)

        def slot(pos):
            return 4 * pos[0] + 2 * pos[1] + pos[2]

        def copy(gi, k, hf, block, to, from_src=False):
            part = pl.ds(hf * (src[gi].shape[1] // self.HALVES), src[gi].shape[1] // self.HALVES)
            rows = dst[gi].at[:, slot(block), part]
            return pltpu.make_async_remote_copy(
                src_ref=src[gi].at[:, part] if from_src else rows, dst_ref=rows,
                send_sem=send_sems.at[gi, k * self.HALVES + hf], recv_sem=recv_sems.at[gi, k * self.HALVES + hf],
                device_id=to, device_id_type=MESH)

        make = functools.partial
        mine = [make(pltpu.make_async_copy, src[gi], dst[gi].at[:, slot(me)], local_sems.at[gi]) for gi in range(ng)]
        first = []
        for hf in halves:
            for gi in range(ng):
                first.append(make(copy, gi, 0, hf, me, sibling, True))
                first += [make(copy, gi, 1 + j, hf, me, (*chip, c), True) for j, chip in enumerate(chips)]
        order = [(hf, j, chip, gi) for hf in halves for j, chip in enumerate(chips) for gi in range(ng)]
        passed = [make(copy, gi, 4 + j, hf, (*chip, c), sibling) for hf, j, chip, gi in order]
        arrive_ici = [make(copy, gi, 1 + j, hf, (*chip, c), me) for hf, j, chip, gi in order]
        arrive_d2d = [make(copy, gi, 0, hf, sibling, me) for hf in halves for gi in range(ng)]
        arrive_d2d += [make(copy, gi, 4 + j, hf, (*chip, 1 - c), me) for hf, j, chip, gi in order]
        return mine, first, passed, arrive_ici, arrive_d2d

    def start(self, src, dst, sems):
        mine, first, _, _, _ = self._plan(src, dst, sems)
        for cp in mine + first:
            cp().start()

    def mid(self, src, dst, sems):
        _, _, passed, arrive_ici, _ = self._plan(src, dst, sems)
        for arrived, fw in zip(arrive_ici, passed):
            arrived().wait_recv()
            fw().start()

    def finish(self, src, dst, sems):
        mine, first, passed, _, arrive_d2d = self._plan(src, dst, sems)
        for cp in arrive_d2d:
            cp().wait_recv()
        for cp in first + passed:
            cp().wait_send()
        for cp in mine:
            cp().wait()


class _SiblingTask:
    def __init__(self, g5s):
        self.inputs = list(g5s)
        ng = len(g5s)
        self.out_shape = [jax.ShapeDtypeStruct((a.shape[0], 4) + a.shape[3:], a.dtype) for a in g5s]
        self.scratch = [pltpu.SemaphoreType.DMA((ng,)), pltpu.SemaphoreType.DMA((ng,))]

    def _plan(self, src, dst, sems):
        send_sems, recv_sems = sems
        x, y, c = _position()
        return [functools.partial(
            pltpu.make_async_remote_copy,
            src_ref=src[gi].at[:, :, 1 - c], dst_ref=dst[gi],
            send_sem=send_sems.at[gi], recv_sem=recv_sems.at[gi],
            device_id=(x, y, 1 - c), device_id_type=MESH) for gi in range(len(src))]

    def start(self, src, dst, sems):
        for cp in self._plan(src, dst, sems):
            cp().start()

    def mid(self, src, dst, sems):
        pass

    def finish(self, src, dst, sems):
        for cp in self._plan(src, dst, sems):
            cp().wait()


class _ChipTask(_SiblingTask):
    def __init__(self, sbs):
        self.inputs = list(sbs)
        ng = len(sbs)
        self.out_shape = [jax.ShapeDtypeStruct(a.shape, a.dtype) for a in sbs]
        self.scratch = [pltpu.SemaphoreType.DMA((ng, 3)), pltpu.SemaphoreType.DMA((ng, 3))]

    def _plan(self, src, dst, sems):
        send_sems, recv_sems = sems
        x, y, c = _position()
        jme = 2 * x + y
        chips = [(1 - x, y), (x, 1 - y), (1 - x, 1 - y)]
        return [functools.partial(
            pltpu.make_async_remote_copy,
            src_ref=src[gi].at[:, 2 * chip[0] + chip[1]], dst_ref=dst[gi].at[:, jme],
            send_sem=send_sems.at[gi, k], recv_sem=recv_sems.at[gi, k],
            device_id=(*chip, c), device_id_type=MESH) for k, chip in enumerate(chips) for gi in range(len(src))]


def _alone(task, name):
    n_in, n_out = len(task.inputs), len(task.out_shape)

    def body(*refs):
        parts = (refs[:n_in], refs[n_in:n_in + n_out], refs[n_in + n_out:])
        task.start(*parts)
        task.mid(*parts)
        task.finish(*parts)

    any_spec = pl.BlockSpec(memory_space=pl.ANY)
    return _call(body, name=name, in_specs=[any_spec] * n_in, out_specs=[any_spec] * n_out,
                 out_shape=task.out_shape, scratch_shapes=task.scratch)(*task.inputs)


def _core_sum(ids, g5, r1, name):
    n, _, _, rows, _ = g5.shape

    def body(ids_ref, g_ref, r_ref, sb_ref, own_ref):
        j = pl.program_id(2)
        t = g_ref[...] + r_ref[...]
        sb_ref[...] = t.astype(BF16)

        @pl.when(j == ids_ref[1])
        def _():
            own_ref[...] = t

    grid_spec = pltpu.PrefetchScalarGridSpec(
        num_scalar_prefetch=1, grid=(n, 1, 4),
        in_specs=[pl.BlockSpec((None, None, None, rows, D), lambda i, t, j, ids: (i, j, ids[0], t, 0)),
                  pl.BlockSpec((None, None, rows, D), lambda i, t, j, ids: (i, j, t, 0))],
        out_specs=[pl.BlockSpec((None, None, rows, D), lambda i, t, j, ids: (i, j, t, 0)),
                   pl.BlockSpec((None, rows, D), lambda i, t, j, ids: (i, t, 0))])
    return _call(
        body, name=name, grid_spec=grid_spec,
        out_shape=[jax.ShapeDtypeStruct((n, 4, rows, D), BF16), jax.ShapeDtypeStruct((n, rows, D), F32)],
        compiler_params=_params(("arbitrary", "arbitrary", "arbitrary")),
    )(ids, g5, r1)


def _adam_math(w, g, m, v):
    m = ADAM_B1 * m + (1.0 - ADAM_B1) * g
    v = ADAM_B2 * v + (1.0 - ADAM_B2) * (g * g)
    m_hat = m / (1.0 - ADAM_B1 ** ADAM_STEP)
    v_hat = v / (1.0 - ADAM_B2 ** ADAM_STEP)
    delta = -ADAM_LR * (m_hat / (jnp.sqrt(v_hat) + ADAM_EPS) + ADAM_WD * w)
    return delta, m, v


def _adamw(others, w, m, v, gparts, name):
    _, r, c = w.shape

    def body(oth_ref, w_ref, m_ref, v_ref, *rest):
        srcs, (g_ref, d_ref, mo_ref, vo_ref) = rest[:4 * DEPTH], rest[4 * DEPTH:]

        def run(own_ref, r0_ref, r1_ref, r2_ref):
            g = ((own_ref[...] + r0_ref[...].astype(F32)) + r1_ref[...].astype(F32)) + r2_ref[...].astype(F32)
            g_ref[...] = g
            d_ref[...], mo_ref[...], vo_ref[...] = _adam_math(w_ref[...], g, m_ref[...], v_ref[...])

        for l in range(DEPTH):
            pl.when(pl.program_id(0) == l)(functools.partial(run, *srcs[4 * l:4 * l + 4]))

    spec = pl.BlockSpec((None, r, c), lambda l, oth: (l, 0, 0))
    g_specs, g_args = [], []
    for own, recv, k in gparts:
        g_specs.append(pl.BlockSpec((None, r, c), lambda l, oth, k=k: (k, 0, 0)))
        g_args.append(own)
        for j in range(3):
            g_specs.append(pl.BlockSpec((None, None, r, c), lambda l, oth, k=k, j=j: (k, oth[j], 0, 0)))
            g_args.append(recv)
    grid_spec = pltpu.PrefetchScalarGridSpec(
        num_scalar_prefetch=1, grid=(DEPTH,), in_specs=[spec] * 3 + g_specs, out_specs=[spec] * 4)
    return _call(
        body, name=name, grid_spec=grid_spec,
        out_shape=[jax.ShapeDtypeStruct(w.shape, F32)] * 4,
        compiler_params=_params(("arbitrary",)),
    )(others, w, m, v, *g_args)


def _adamw_small(parts, w, m, v, name):
    def body(p_ref, w_ref, m_ref, v_ref, g_ref, d_ref, mo_ref, vo_ref):
        g = p_ref[0]
        for dev in range(1, N_DEV):
            g = g + p_ref[dev]
        g_ref[...] = g
        d_ref[...], mo_ref[...], vo_ref[...] = _adam_math(w_ref[...], g, m_ref[...], v_ref[...])

    return _call(
        body, name=name,
        out_shape=[jax.ShapeDtypeStruct(w.shape, F32)] * 4,
        compiler_params=_params(),
    )(parts, w, m, v)


SMALL = ["ffn1_norm", "mix_norm", "attn_sinks", "gmlp_v_norm", "gmlp_w_s", "gmlp_b", "pool_w", "pool_scale",
         "ffn2_norm", "final_norm"]


def _piece_rows(size):
    return -(-size // 1024) * 8


def _pack_small(arrs, extra=None):
    pieces = []
    for a in list(arrs) + [jnp.zeros((1,), F32) if extra is None else extra]:
        fill = _piece_rows(a.size) * 128 - a.size
        f = a.reshape(-1)
        pieces.append((jnp.pad(f, (0, fill)) if fill else f).reshape(-1, 128))
    return jnp.concatenate(pieces, axis=0)


def _unpack_small(packed, like):
    out, off = [], 0
    for a in like:
        piece = packed[off:off + -(-a.size // 128)]
        if a.size % 128:
            piece = piece.reshape(-1)[:a.size]
        out.append(piece.reshape(a.shape))
        off += _piece_rows(a.size)
    return out, packed[off, 0]


def kernel(x, ffn1_norm, ffn1_w_gate, ffn1_w_up, ffn1_w_down, mix_norm, w_in, attn_sinks, gmlp_v_norm, gmlp_w_s, gmlp_b, pool_w, pool_scale, w_out, ffn2_norm, ffn2_w_gate, ffn2_w_up, ffn2_w_down, final_norm, loss_target, m_ffn1_norm, m_ffn1_w_gate, m_ffn1_w_up, m_ffn1_w_down, m_mix_norm, m_w_in, m_attn_sinks, m_gmlp_v_norm, m_gmlp_w_s, m_gmlp_b, m_pool_w, m_pool_scale, m_w_out, m_ffn2_norm, m_ffn2_w_gate, m_ffn2_w_up, m_ffn2_w_down, m_final_norm, v_ffn1_norm, v_ffn1_w_gate, v_ffn1_w_up, v_ffn1_w_down, v_mix_norm, v_w_in, v_attn_sinks, v_gmlp_v_norm, v_gmlp_w_s, v_gmlp_b, v_pool_w, v_pool_scale, v_w_out, v_ffn2_norm, v_ffn2_w_gate, v_ffn2_w_up, v_ffn2_w_down, v_final_norm):
    s = x.shape[1]
    xi, yi, ci = _position()
    ids = jnp.stack([ci, 2 * xi + yi]).astype(jnp.int32)
    jme = 2 * xi + yi
    others = jnp.stack([k + (k >= jme).astype(jnp.int32) for k in range(3)]).astype(jnp.int32)
    t = lambda a: jnp.swapaxes(a, -1, -2)
    row = lambda a: a.reshape(1, -1)
    full = lambda a: a.reshape(a.shape[0], -1, D)

    loc_f1 = [jnp.stack([t(ffn1_w_gate[l]), t(ffn1_w_up[l]), ffn1_w_down[l]]).astype(BF16) for l in range(DEPTH)]
    loc_f2 = [jnp.stack([t(ffn2_w_gate[l]), t(ffn2_w_up[l]), ffn2_w_down[l]]).astype(BF16) for l in range(DEPTH)]
    loc_in = [t(w_in[l])[None].astype(BF16) for l in range(DEPTH)]
    loc_out = [w_out[l][None].astype(BF16) for l in range(DEPTH)]

    (wf1,) = _alone(_GatherTask([loc_f1[0]]), "gather_first")
    wf1 = full(wf1)
    xc = x.reshape(s, D)
    saved = []
    for l in range(DEPTH):
        x0 = xc
        if l == 0:
            (x1, *act1), ((wf2, win, wout),) = _ffn_fwd(
                x0, row(ffn1_norm[l]), wf1, 0, f"ffn1_fwd_{l}", comm=[_GatherTask([loc_f2[0], loc_in[0], loc_out[0]])])
        else:
            (x1, *act1), ((wf2,),) = _ffn_fwd(
                x0, row(ffn1_norm[l]), wf1, 0, f"ffn1_fwd_{l}", comm=[_GatherTask([loc_f2[1]])])
        wf2, win, wout = full(wf2), full(win), full(wout)
        z, hmix = _mixin_fwd(x1, row(mix_norm[l]), win, 0, f"mixin_fwd_{l}")
        wcat = jnp.concatenate([gmlp_w_s[l][h] for h in range(4)], axis=1)
        bexp = jnp.repeat(t(gmlp_b[l]), HD, axis=1)
        pwbd = jnp.zeros((256, 256), F32)
        for g in range(4):
            pwbd = pwbd.at[g * HD:(g + 1) * HD, g * HD:(g + 1) * HD].set(pool_w[l][g])
        mixp = (attn_sinks[l], row(gmlp_v_norm[l]), wcat, bexp, pwbd, row(pool_scale[l]))
        y, lse = _mix_fwd(z, *mixp, f"mix_fwd_{l}")
        keep = (x0, act1, wf1, x1, z, hmix, mixp, y, lse, win, wout)
        if l == 0:
            (xc, *act2, x2), ((wf1, win, wout),) = _ffn_fwd(
                x1, row(ffn2_norm[l]), wf2, 0, f"ffn2_fwd_{l}", mixer=(y, wout),
                comm=[_GatherTask([loc_f1[1], loc_in[1], loc_out[1]])])
            wf1 = full(wf1)
        else:
            dx, *act2, x2, loss_part, d_final = _ffn_fwd(
                x1, row(ffn2_norm[l]), wf2, 0, f"ffn2_fwd_{l}", mixer=(y, wout),
                loss=(row(final_norm), loss_target.reshape(s, D)))
        saved.append(keep + (x2, wf2, act2))

    def five(g):
        return g.reshape(g.shape[0], 4, 2, g.shape[1] // N_DEV, D)

    def core_sums(g5s, r1s, tag):
        res = [_core_sum(ids, g5, r1, f"core_sum_{tag}_{i}") for i, (g5, r1) in enumerate(zip(g5s, r1s))]
        return [sb for sb, _ in res], [own for _, own in res]

    def mix_small(l, dsink, dvn, dws, dbt, dpw, dps):
        return {("attn_sinks", l): dsink[:, 0], ("gmlp_v_norm", l): dvn[0],
                ("gmlp_w_s", l): jnp.stack([dws[:, h * BLK:(h + 1) * BLK] for h in range(4)]),
                ("gmlp_b", l): t(dbt[:, :4]),
                ("pool_w", l): jnp.stack([dpw[g * HD:(g + 1) * HD, g * HD:(g + 1) * HD] for g in range(4)]),
                ("pool_scale", l): dps[0]}

    small = {}
    gsrc = {}

    def reduced(l, names, owns, recvs):
        slabs = [(own, recv, k) for own, recv in zip(owns, recvs) for k in range(own.shape[0])]
        for nm, src in zip(names, slabs):
            gsrc[(nm, l)] = src

    ffn1_names = ["ffn1_w_gate", "ffn1_w_up", "ffn1_w_down"]
    ffn2_names = ["ffn2_w_gate", "ffn2_w_up", "ffn2_w_down"]
    gate_up = dict(cols=2 * FF, slab_rows=FF)
    x0, (p11, p21, hid1), wf1, x1, z, hmix, mixp, y, lse, win, wout, x2, wf2, (p12, p22, hid2) = saved[1]
    dx, dab, h, dyb, dg, dymix, dxb = _ffn_bwd(x2, row(ffn2_norm[1]), dx, p12, p22, wf2, 0, "ffn2_bwd_1", w_out=wout)
    small[("ffn2_norm", 1)] = dg[0]
    g = _wgrad(dab, h, None, 3, 0, "wgrad_gate_up2_1", **gate_up)
    g = _wgrad(hid2, dyb, g, 3, 2, "wgrad_down2_1")
    a5 = [five(g)]
    g_out, (a_r1,) = _wgrad(y, dxb, None, 1, 0, "wgrad_out_1", comm=[_SiblingTask(a5)])
    a_sb, a_own = core_sums(a5, a_r1, "a")
    (dz, dsink, dvn, dws, dbt, dpw, dps), (a_r2,) = _mix_bwd(z, dymix, lse, *mixp, "mix_bwd_1", comm=[_ChipTask(a_sb)])
    reduced(1, ffn2_names, a_own, a_r2)
    small.update(mix_small(1, dsink, dvn, dws, dbt, dpw, dps))
    g_in = _wgrad(dz, hmix, None, 1, 0, "wgrad_in_1")
    b5 = [five(g_out), five(g_in)]
    (dx, dab, h, dyb, dg, dgm), (b_r1,) = _ffn_bwd(x0, row(ffn1_norm[1]), dx, p11, p21, wf1, 0, "ffn1_bwd_1",
                                                   comm=[_SiblingTask(b5)], mixin=(row(mix_norm[1]), dz, win, x1))
    small[("ffn1_norm", 1)], small[("mix_norm", 1)] = dg[0], dgm[0]
    b_sb, b_own = core_sums(b5, b_r1, "b")
    g_gu, (b_r2,) = _wgrad(dab, h, None, 2, 0, "wgrad_gate_up1_1", comm=[_ChipTask(b_sb)], **gate_up)
    reduced(1, ["w_out", "w_in"], b_own, b_r2)
    g_down = _wgrad(hid1, dyb, None, 1, 0, "wgrad_down1_1")
    c5 = [five(g_gu), five(g_down)]
    x0, (p11, p21, hid1), wf1, x1, z, hmix, mixp, y, lse, win, wout, x2, wf2, (p12, p22, hid2) = saved[0]
    (dx, dab, h, dyb, dg, dymix, dxb), (c_r1,) = _ffn_bwd(x2, row(ffn2_norm[0]), dx, p12, p22, wf2, 0, "ffn2_bwd_0",
                                                          comm=[_SiblingTask(c5)], w_out=wout)
    small[("ffn2_norm", 0)] = dg[0]
    c_sb, c_own = core_sums(c5, c_r1, "c")
    g_gu, (c_r2a,) = _wgrad(dab, h, None, 2, 0, "wgrad_gate_up2_0", comm=[_ChipTask(c_sb[0:1])], **gate_up)
    d5a = [five(g_gu)]
    g_down, (c_r2b, d_r1a) = _wgrad(hid2, dyb, None, 1, 0, "wgrad_down2_0",
                                    comm=[_ChipTask(c_sb[1:2]), _SiblingTask(d5a)])
    reduced(1, ffn1_names, c_own, c_r2a + c_r2b)
    da_sb, da_own = core_sums(d5a, d_r1a, "da")
    d5b = [five(g_down)]
    (dz, dsink, dvn, dws, dbt, dpw, dps), (d_r2a, d_r1b) = _mix_bwd(
        z, dymix, lse, *mixp, "mix_bwd_0", comm=[_ChipTask(da_sb), _SiblingTask(d5b)])
    small.update(mix_small(0, dsink, dvn, dws, dbt, dpw, dps))
    db_sb, db_own = core_sums(d5b, d_r1b, "db")
    dx, dab, h, dyb, dg, dgm = _ffn_bwd(x0, row(ffn1_norm[0]), dx, p11, p21, wf1, 0, "ffn1_bwd_0",
                                        mixin=(row(mix_norm[0]), dz, win, x1))
    small[("ffn1_norm", 0)], small[("mix_norm", 0)] = dg[0], dgm[0]
    grad_x = dx.reshape(1, s, D)

    part = [d_final[0] if nm == "final_norm" else jnp.stack([small[(nm, l)] for l in range(DEPTH)]) for nm in SMALL]
    packed = _pack_small(part, loss_part[0, 0])
    g_gate, ((gathered,),) = _wgrad(dab, h, None, 1, 0, "wgrad_gate1_0", cols=FF, comm=[_GatherTask([packed[None]])])
    f5 = [five(g_gate)]
    g_up, (d_r2b, f_r1) = _wgrad(dab, h, None, 1, 0, "wgrad_up1_0", col0=FF, cols=FF,
                                 comm=[_ChipTask(db_sb), _SiblingTask(f5)])
    reduced(0, ffn2_names, da_own + db_own, d_r2a + d_r2b)
    f_sb, f_own = core_sums(f5, f_r1, "f")
    u5 = [five(g_up)]
    g_down, (f_r2, u_r1) = _wgrad(hid1, dyb, None, 1, 0, "wgrad_down1_0", comm=[_ChipTask(f_sb), _SiblingTask(u5)])
    u_sb, u_own = core_sums(u5, u_r1, "u")
    w5 = [five(g_down)]
    g_in, (u_r2, w_r1) = _wgrad(dz, hmix, None, 1, 0, "wgrad_in_0", comm=[_ChipTask(u_sb), _SiblingTask(w5)])
    w_sb, w_own = core_sums(w5, w_r1, "w")
    i5 = [five(g_in)]
    g_out, (w_r2, i_r1) = _wgrad(y, dxb, None, 1, 0, "wgrad_out_0", comm=[_ChipTask(w_sb), _SiblingTask(i5)])
    reduced(0, ffn1_names, f_own + u_own + w_own, f_r2 + u_r2 + w_r2)
    i_sb, i_own = core_sums(i5, i_r1, "i")
    o5 = [five(g_out)]
    o_r1 = _alone(_SiblingTask(o5), "reduce_sibling_last")
    o_sb, o_own = core_sums(o5, o_r1, "o")
    e_r2 = _alone(_ChipTask(o_sb + i_sb), "reduce_chips_last")
    reduced(0, ["w_out", "w_in"], o_own + i_own, e_r2)

    grads = {}
    transposed = ("ffn1_w_gate", "ffn1_w_up", "ffn2_w_gate", "ffn2_w_up", "w_in")

    small_w = dict(ffn1_norm=ffn1_norm, mix_norm=mix_norm, attn_sinks=attn_sinks, gmlp_v_norm=gmlp_v_norm,
                   gmlp_w_s=gmlp_w_s, gmlp_b=gmlp_b, pool_w=pool_w, pool_scale=pool_scale, ffn2_norm=ffn2_norm,
                   final_norm=final_norm)
    small_m = dict(ffn1_norm=m_ffn1_norm, mix_norm=m_mix_norm, attn_sinks=m_attn_sinks, gmlp_v_norm=m_gmlp_v_norm,
                   gmlp_w_s=m_gmlp_w_s, gmlp_b=m_gmlp_b, pool_w=m_pool_w, pool_scale=m_pool_scale,
                   ffn2_norm=m_ffn2_norm, final_norm=m_final_norm)
    small_v = dict(ffn1_norm=v_ffn1_norm, mix_norm=v_mix_norm, attn_sinks=v_attn_sinks, gmlp_v_norm=v_gmlp_v_norm,
                   gmlp_w_s=v_gmlp_w_s, gmlp_b=v_gmlp_b, pool_w=v_pool_w, pool_scale=v_pool_scale,
                   ffn2_norm=v_ffn2_norm, final_norm=v_final_norm)
    sg, sd, sm, sv = _adamw_small(gathered[0], _pack_small([small_w[nm] for nm in SMALL]),
                                  _pack_small([small_m[nm] for nm in SMALL]),
                                  _pack_small([small_v[nm] for nm in SMALL]), "adamw_small")
    like = [small_w[nm] for nm in SMALL]
    sg_l, loss = _unpack_small(sg, like)
    sd_l, _ = _unpack_small(sd, like)
    sm_l, _ = _unpack_small(sm, like)
    sv_l, _ = _unpack_small(sv, like)
    deltas, new_m, new_v = {}, {}, {}
    for i, nm in enumerate(SMALL):
        grads[nm], deltas[nm], new_m[nm], new_v[nm] = sg_l[i], sd_l[i], sm_l[i], sv_l[i]

    big_w = dict(ffn1_w_gate=ffn1_w_gate, ffn1_w_up=ffn1_w_up, ffn1_w_down=ffn1_w_down, w_in=w_in, w_out=w_out,
                 ffn2_w_gate=ffn2_w_gate, ffn2_w_up=ffn2_w_up, ffn2_w_down=ffn2_w_down)
    big_m = dict(ffn1_w_gate=m_ffn1_w_gate, ffn1_w_up=m_ffn1_w_up, ffn1_w_down=m_ffn1_w_down, w_in=m_w_in,
                 w_out=m_w_out, ffn2_w_gate=m_ffn2_w_gate, ffn2_w_up=m_ffn2_w_up, ffn2_w_down=m_ffn2_w_down)
    big_v = dict(ffn1_w_gate=v_ffn1_w_gate, ffn1_w_up=v_ffn1_w_up, ffn1_w_down=v_ffn1_w_down, w_in=v_w_in,
                 w_out=v_w_out, ffn2_w_gate=v_ffn2_w_gate, ffn2_w_up=v_ffn2_w_up, ffn2_w_down=v_ffn2_w_down)
    for nm in big_w:
        view = t if nm in transposed else (lambda a: a)
        res = _adamw(others, view(big_w[nm]), view(big_m[nm]), view(big_v[nm]),
                     [gsrc[(nm, l)] for l in range(DEPTH)], f"adamw_{nm}")
        grads[nm], deltas[nm], new_m[nm], new_v[nm] = [view(r) for r in res]

    order = ["ffn1_norm", "ffn1_w_gate", "ffn1_w_up", "ffn1_w_down", "mix_norm", "w_in", "attn_sinks", "gmlp_v_norm",
             "gmlp_w_s", "gmlp_b", "pool_w", "pool_scale", "w_out", "ffn2_norm", "ffn2_w_gate", "ffn2_w_up",
             "ffn2_w_down", "final_norm"]
    return (loss, grad_x, *[grads[n] for n in order], *[deltas[n] for n in order],
            *[new_m[n] for n in order], *[new_v[n] for n in order])
```
